```python
import math
import jax, jax.numpy as jnp
from jax import lax
import numpy as np

D_MODEL = 2048
BATCH = 8
SEQ = 2048
DEPTH = 1

HEAD_DIM = 128
NSA_HEADS = 8
NSA_KV_HEADS = 2
NSA_GROUP = NSA_HEADS // NSA_KV_HEADS
FOX_HEADS = 4
MEM_HEADS = 4
MEM_TOKENS = 256
CMP_LEN = 32
CMP_STRIDE = 16
SLC_LEN = 64
N_SELECT = 16
WINDOW = 512
Q_BLOCK = 128
NUM_BUCKETS = 32
MAX_DISTANCE = 128
N_BRANCHES = 3
N_EXPERTS = 64
TOP_K = 8
D_EXPERT = 512
ROUTED_SCALE = 2.5
EXPERT_BLOCK = 256
ATTN_SCALE = HEAD_DIM ** -0.5
NEG_INF = -1e30
FORCE_SCORE = 1e4
RMS_EPS = 1e-6

NSA_Q_W = NSA_HEADS * HEAD_DIM
NSA_KV_W = NSA_KV_HEADS * HEAD_DIM
NSA_GATE_W = 3 * NSA_HEADS
FOX_W = FOX_HEADS * HEAD_DIM
MEM_W = MEM_HEADS * HEAD_DIM
MERGE_W = N_BRANCHES * D_MODEL
IN_SPLITS = (NSA_Q_W, NSA_KV_W, NSA_KV_W, NSA_KV_W, NSA_KV_W, NSA_KV_W, NSA_KV_W, NSA_GATE_W,
             FOX_W, FOX_W, FOX_W, FOX_HEADS, MEM_W, MERGE_W)
W_IN_COLS = sum(IN_SPLITS)

kernel_name = "hybrid_nsa_fox_memory_moe_layer"


def rms_norm(x, g):
    xf = x.astype(jnp.float32)
    y = xf * lax.rsqrt(jnp.mean(xf * xf, axis=-1, keepdims=True) + RMS_EPS)
    return (y * g.astype(jnp.float32)).astype(x.dtype)


def masked_softmax(logits, mask):
    z = jnp.where(mask, logits.astype(jnp.float32), NEG_INF)
    return jnp.where(mask, jax.nn.softmax(z, axis=-1), 0.0)


def t5_bucket(rel):
    n = jnp.maximum(rel, 0)
    max_exact = NUM_BUCKETS // 2
    log_ratio = jnp.log(jnp.maximum(n, 1).astype(jnp.float32) / max_exact) / math.log(MAX_DISTANCE / max_exact)
    large = jnp.minimum(max_exact + (log_ratio * (NUM_BUCKETS - max_exact)).astype(jnp.int32), NUM_BUCKETS - 1)
    return jnp.where(n < max_exact, n, large)


def swiglu(x, wg, wu, wd):
    return (jax.nn.silu(x @ wg) * (x @ wu)) @ wd


def split_columns(y, sizes):
    outs, off = [], 0
    for s in sizes:
        outs.append(y[..., off:off + s])
        off += s
    return outs


def nsa_mixer(q, k_cmp, v_cmp, k_slc, v_slc, k_win, v_win, gate_logits,
              pe_k, w_ck, pe_v, w_cv, q_g, k_g, rel_bias):
    B, T = q.shape[:2]
    G, Hg, dk = NSA_KV_HEADS, NSA_GROUP, HEAD_DIM
    nq = T // Q_BLOCK
    tpos = jnp.arange(T)
    garange = jnp.arange(G)
    harange = jnp.arange(Hg)
    bias_gh = rel_bias.T.reshape(G, Hg, NUM_BUCKETS).astype(jnp.float32)
    qg = rms_norm(q, q_g).reshape(B, T, G, Hg, dk)

    n_cmp = (T - CMP_LEN) // CMP_STRIDE + 1
    cstart = jnp.arange(n_cmp) * CMP_STRIDE
    cidx = cstart[:, None] + jnp.arange(CMP_LEN)[None, :]
    kc = jnp.einsum('bnlgd,lde->bnge', k_cmp[:, cidx] + pe_k[:, None, :], w_ck)
    vc = jnp.einsum('bnlgd,lde->bnge', v_cmp[:, cidx] + pe_v[:, None, :], w_cv)
    kc = rms_norm(kc, k_g)
    rel_c = tpos[:, None] - (cstart + CMP_LEN - 1)[None, :]
    bias_c = bias_gh[:, :, t5_bucket(rel_c)]
    logit_c = jnp.einsum('btghd,bngd->bghtn', qg, kc).astype(jnp.float32) * ATTN_SCALE + bias_c
    p_c = masked_softmax(logit_c, rel_c >= 0)
    o_c = jnp.einsum('bghtn,bngd->btghd', p_c.astype(vc.dtype), vc)

    n_slc = T // SLC_LEN
    n_sel = min(N_SELECT, n_slc)
    sstart = jnp.arange(n_slc) * SLC_LEN
    overlap = jnp.clip(jnp.minimum(cstart[:, None] + CMP_LEN, sstart[None, :] + SLC_LEN)
                       - jnp.maximum(cstart[:, None], sstart[None, :]), 0, None)
    imp = jnp.einsum('bghtn,nj->bgtj', p_c, overlap.astype(jnp.float32) / CMP_LEN)
    blk = jnp.arange(n_slc)
    cur = tpos // SLC_LEN
    valid = sstart[None, :] <= tpos[:, None]
    forced = (blk[None, :] == 0) | (blk[None, :] == cur[:, None]) | (blk[None, :] == cur[:, None] - 1)
    score = jnp.where(valid, jnp.where(forced, FORCE_SCORE, imp), -1.0)
    _, sel = lax.top_k(score, n_sel)

    kb = rms_norm(k_slc, k_g).reshape(B, n_slc, SLC_LEN, G, dk)
    vb = v_slc.reshape(B, n_slc, SLC_LEN, G, dk)
    q_items = qg.reshape(B * nq, Q_BLOCK, G, Hg, dk)
    sel_items = sel.transpose(0, 2, 1, 3).reshape(B * nq, Q_BLOCK, G, n_sel)
    larange = jnp.arange(SLC_LEN)

    def slc_block(args):
        q_c, sel_c, c = args
        b = c // nq
        qpos = (c % nq) * Q_BLOCK + jnp.arange(Q_BLOCK)
        kb_b, vb_b = kb[b], vb[b]
        gi = (sel_c[..., None], larange[None, None, None, :], garange[None, :, None, None])
        kg, vg = kb_b[gi], vb_b[gi]
        kpos = sel_c[..., None] * SLC_LEN + larange
        rel = qpos[:, None, None, None] - kpos
        bias = bias_gh[garange[None, :, None, None, None], harange[None, None, :, None, None],
                       t5_bucket(rel)[:, :, None]]
        logits = jnp.einsum('qghd,qgnld->qghnl', q_c, kg).astype(jnp.float32) * ATTN_SCALE + bias
        mask = jnp.broadcast_to((rel >= 0)[:, :, None], logits.shape)
        p = masked_softmax(logits.reshape(Q_BLOCK, G, Hg, -1), mask.reshape(Q_BLOCK, G, Hg, -1))
        p = p.reshape(logits.shape).astype(vg.dtype)
        return jnp.einsum('qghnl,qgnld->qghd', p, vg)

    o_s = lax.map(slc_block, (q_items, sel_items, jnp.arange(B * nq))).reshape(B, T, G, Hg, dk)

    kw_pad = jnp.pad(rms_norm(k_win, k_g), ((0, 0), (WINDOW, 0), (0, 0), (0, 0)))
    vw_pad = jnp.pad(v_win, ((0, 0), (WINDOW, 0), (0, 0), (0, 0)))
    q_blocks = qg.reshape(B, nq, Q_BLOCK, G, Hg, dk).swapaxes(0, 1)

    def win_block(args):
        q_c, i = args
        kw = lax.dynamic_slice_in_dim(kw_pad, i * Q_BLOCK, WINDOW + Q_BLOCK, axis=1)
        vw = lax.dynamic_slice_in_dim(vw_pad, i * Q_BLOCK, WINDOW + Q_BLOCK, axis=1)
        qpos = i * Q_BLOCK + jnp.arange(Q_BLOCK)
        kpos = i * Q_BLOCK - WINDOW + jnp.arange(WINDOW + Q_BLOCK)
        rel = qpos[:, None] - kpos[None, :]
        mask = (kpos >= 0)[None, :] & (rel >= 0) & (rel < WINDOW)
        logits = (jnp.einsum('bqghd,bkgd->bghqk', q_c, kw).astype(jnp.float32) * ATTN_SCALE
                  + bias_gh[:, :, t5_bucket(rel)])
        p = masked_softmax(logits, mask)
        return jnp.einsum('bghqk,bkgd->bqghd', p.astype(vw.dtype), vw)

    o_w = lax.map(win_block, (q_blocks, jnp.arange(nq))).swapaxes(0, 1).reshape(B, T, G, Hg, dk)

    g = jax.nn.sigmoid(gate_logits).reshape(B, T, 3, G, Hg)[..., None]
    o = g[:, :, 0] * o_c + g[:, :, 1] * o_s + g[:, :, 2] * o_w
    return o.reshape(B, T, NSA_Q_W)


def fox_mixer(q, k, v, f_logit, b_f, q_g, k_g):
    B, T = q.shape[:2]
    nq = T // Q_BLOCK
    q = rms_norm(q, q_g)
    k = rms_norm(k, k_g)
    log_f = jax.nn.log_sigmoid(f_logit.astype(jnp.float32) + b_f.astype(jnp.float32))
    cum = jnp.cumsum(log_f, axis=1).transpose(0, 2, 1)
    q_blocks = q.reshape(B, nq, Q_BLOCK, FOX_HEADS, HEAD_DIM).swapaxes(0, 1)
    c_blocks = cum.reshape(B, FOX_HEADS, nq, Q_BLOCK).transpose(2, 0, 1, 3)
    kpos = jnp.arange(T)

    def block(args):
        q_c, c_q, i = args
        qpos = i * Q_BLOCK + jnp.arange(Q_BLOCK)
        logits = (jnp.einsum('bqhd,bkhd->bhqk', q_c, k).astype(jnp.float32) * ATTN_SCALE
                  + (c_q[..., :, None] - cum[:, :, None, :]))
        p = masked_softmax(logits, kpos[None, :] <= qpos[:, None])
        return jnp.einsum('bhqk,bkhd->bqhd', p.astype(v.dtype), v)

    o = lax.map(block, (q_blocks, c_blocks, jnp.arange(nq)))
    return o.swapaxes(0, 1).reshape(B, T, FOX_W)


def memory_attention(q, mem_n, w_kv, q_g, k_g):
    B, M = mem_n.shape[:2]
    kv = (mem_n @ w_kv).reshape(B, M, 2, MEM_HEADS, HEAD_DIM)
    k = rms_norm(kv[:, :, 0], k_g)
    v = kv[:, :, 1]
    q = rms_norm(q, q_g)
    logits = jnp.einsum('bthd,bmhd->bhtm', q, k).astype(jnp.float32) * ATTN_SCALE
    p = jax.nn.softmax(logits, axis=-1)
    return jnp.einsum('bhtm,bmhd->bthd', p.astype(v.dtype), v).reshape(B, -1, MEM_W)


def moe_ffn(h, w_router, router_bias, we_gate, we_up, we_down, ws_gate, ws_up, ws_down):
    Bt, S, D = h.shape
    T = Bt * S
    hf = h.reshape(T, D)
    scores = jax.nn.sigmoid((hf @ w_router).astype(jnp.float32))
    _, top_idx = lax.top_k(scores + router_bias.astype(jnp.float32), TOP_K)
    top_s = jnp.take_along_axis(scores, top_idx, axis=-1)
    gates = top_s / jnp.sum(top_s, axis=-1, keepdims=True) * ROUTED_SCALE
    A = T * TOP_K
    flat_e = top_idx.reshape(A)
    flat_tok = jnp.arange(A, dtype=jnp.int32) // TOP_K
    flat_g = gates.reshape(A)
    order = jnp.argsort(flat_e)
    e_sorted = flat_e[order]
    counts = jnp.bincount(flat_e, length=N_EXPERTS)
    padded = (counts + EXPERT_BLOCK - 1) // EXPERT_BLOCK * EXPERT_BLOCK
    start = jnp.cumsum(counts) - counts
    pstart = jnp.cumsum(padded) - padded
    dest = pstart[e_sorted] + jnp.arange(A) - start[e_sorted]
    nb = (A + N_EXPERTS * (EXPERT_BLOCK - 1)) // EXPERT_BLOCK
    P = nb * EXPERT_BLOCK
    slot_tok = jnp.full((P,), T, jnp.int32).at[dest].set(flat_tok[order])
    slot_g = jnp.zeros((P,), hf.dtype).at[dest].set(flat_g[order].astype(hf.dtype))
    block_end = jnp.cumsum(padded) // EXPERT_BLOCK
    block_expert = jnp.minimum(jnp.searchsorted(block_end, jnp.arange(nb), side='right'), N_EXPERTS - 1)
    h_pad = jnp.concatenate([hf, jnp.zeros((1, D), hf.dtype)], axis=0)

    def step(acc, blk):
        tok, g, e = blk
        yb = swiglu(h_pad[tok], we_gate[e], we_up[e], we_down[e]) * g[:, None]
        return acc.at[tok].add(yb), None

    acc, _ = lax.scan(step, jnp.zeros((T + 1, D), hf.dtype),
                      (slot_tok.reshape(nb, EXPERT_BLOCK), slot_g.reshape(nb, EXPERT_BLOCK), block_expert))
    out = acc[:T] + swiglu(hf, ws_gate, ws_up, ws_down)
    return out.reshape(Bt, S, D)


def setup_inputs(seed: int = 0) -> dict:
    key = jax.random.key(seed)
    ks = iter(jax.random.split(key, 40))

    def nrm(shape, scale):
        return jax.random.normal(next(ks), shape, jnp.float32) * scale

    def gain(shape):
        return 1.0 + nrm(shape, 0.02)

    L = DEPTH
    return {
        "x": nrm((BATCH, SEQ, D_MODEL), 1.0),
        "mem": nrm((BATCH, MEM_TOKENS, D_MODEL), 1.0),
        "norm_attn_g": gain((L, D_MODEL)),
        "w_in": nrm((L, D_MODEL, W_IN_COLS), D_MODEL ** -0.5),
        "nsa_pe_k": nrm((L, CMP_LEN, HEAD_DIM), 0.02),
        "nsa_w_ck": nrm((L, CMP_LEN, HEAD_DIM, HEAD_DIM), (CMP_LEN * HEAD_DIM) ** -0.5),
        "nsa_pe_v": nrm((L, CMP_LEN, HEAD_DIM), 0.02),
        "nsa_w_cv": nrm((L, CMP_LEN, HEAD_DIM, HEAD_DIM), (CMP_LEN * HEAD_DIM) ** -0.5),
        "nsa_q_g": gain((L, HEAD_DIM)),
        "nsa_k_g": gain((L, HEAD_DIM)),
        "rel_bias": nrm((NUM_BUCKETS, NSA_HEADS), 0.2),
        "fox_b_f": jnp.linspace(3.0, 6.0, FOX_HEADS)[None, :] + nrm((L, FOX_HEADS), 0.1),
        "fox_q_g": gain((L, HEAD_DIM)),
        "fox_k_g": gain((L, HEAD_DIM)),
        "norm_mem_g": gain((L, D_MODEL)),
        "w_mem_kv": nrm((L, D_MODEL, 2 * MEM_W), D_MODEL ** -0.5),
        "mem_q_g": gain((L, HEAD_DIM)),
        "mem_k_g": gain((L, HEAD_DIM)),
        "w_o_nsa": nrm((L, NSA_Q_W, D_MODEL), NSA_Q_W ** -0.5),
        "w_o_fox": nrm((L, FOX_W, D_MODEL), FOX_W ** -0.5),
        "w_o_mem": nrm((L, MEM_W, D_MODEL), MEM_W ** -0.5),
        "w_out": nrm((L, D_MODEL, D_MODEL), D_MODEL ** -0.5),
        "norm_ffn_g": gain((L, D_MODEL)),
        "w_router": nrm((L, D_MODEL, N_EXPERTS), D_MODEL ** -0.5),
        "router_bias": nrm((L, N_EXPERTS), 0.01),
        "we_gate": nrm((L, N_EXPERTS, D_MODEL, D_EXPERT), D_MODEL ** -0.5),
        "we_up": nrm((L, N_EXPERTS, D_MODEL, D_EXPERT), D_MODEL ** -0.5),
        "we_down": nrm((L, N_EXPERTS, D_EXPERT, D_MODEL), D_EXPERT ** -0.5),
        "ws_gate": nrm((L, D_MODEL, D_EXPERT), D_MODEL ** -0.5),
        "ws_up": nrm((L, D_MODEL, D_EXPERT), D_MODEL ** -0.5),
        "ws_down": nrm((L, D_EXPERT, D_MODEL), D_EXPERT ** -0.5),
    }


def reference(x, mem, norm_attn_g, w_in, nsa_pe_k, nsa_w_ck, nsa_pe_v, nsa_w_cv, nsa_q_g, nsa_k_g,
              rel_bias, fox_b_f, fox_q_g, fox_k_g, norm_mem_g, w_mem_kv, mem_q_g, mem_k_g,
              w_o_nsa, w_o_fox, w_o_mem, w_out, norm_ffn_g, w_router, router_bias,
              we_gate, we_up, we_down, ws_gate, ws_up, ws_down):
    B, T = x.shape[:2]

    def heads(y, n):
        return y.reshape(B, y.shape[1], n, HEAD_DIM)

    for l in range(DEPTH):
        h = rms_norm(x, norm_attn_g[l])
        proj = h @ w_in[l]
        (q_nsa, k_cmp, v_cmp, k_slc, v_slc, k_win, v_win, nsa_gate,
         fox_q, fox_k, fox_v, fox_f, mem_q, merge_logits) = split_columns(proj, IN_SPLITS)

        o_nsa = nsa_mixer(heads(q_nsa, NSA_HEADS), heads(k_cmp, NSA_KV_HEADS), heads(v_cmp, NSA_KV_HEADS),
                          heads(k_slc, NSA_KV_HEADS), heads(v_slc, NSA_KV_HEADS),
                          heads(k_win, NSA_KV_HEADS), heads(v_win, NSA_KV_HEADS), nsa_gate,
                          nsa_pe_k[l], nsa_w_ck[l], nsa_pe_v[l], nsa_w_cv[l], nsa_q_g[l], nsa_k_g[l], rel_bias)
        o_fox = fox_mixer(heads(fox_q, FOX_HEADS), heads(fox_k, FOX_HEADS), heads(fox_v, FOX_HEADS),
                          fox_f, fox_b_f[l], fox_q_g[l], fox_k_g[l])
        o_mem = memory_attention(heads(mem_q, MEM_HEADS), rms_norm(mem, norm_mem_g[l]), w_mem_kv[l],
                                 mem_q_g[l], mem_k_g[l])

        g = jax.nn.sigmoid(merge_logits).reshape(B, T, N_BRANCHES, D_MODEL)
        z = (g[:, :, 0] * (o_nsa @ w_o_nsa[l]) + g[:, :, 1] * (o_fox @ w_o_fox[l])
             + g[:, :, 2] * (o_mem @ w_o_mem[l]))
        x = x + z @ w_out[l]

        x = x + moe_ffn(rms_norm(x, norm_ffn_g[l]), w_router[l], router_bias[l], we_gate[l], we_up[l],
                        we_down[l], ws_gate[l], ws_up[l], ws_down[l])
    return x
```

```python
import functools
import math

import jax
import jax.numpy as jnp
from jax import lax
from jax.experimental import pallas as pl
from jax.experimental.pallas import tpu as pltpu

D_MODEL = 2048
HEAD_DIM = 128
NSA_HEADS = 8
NSA_KV_HEADS = 2
NSA_GROUP = NSA_HEADS // NSA_KV_HEADS
FOX_HEADS = 4
MEM_HEADS = 4
CMP_LEN = 32
CMP_STRIDE = 16
SLC_LEN = 64
N_SELECT = 16
WINDOW = 512
NUM_BUCKETS = 32
MAX_DISTANCE = 128
N_BRANCHES = 3
N_EXPERTS = 64
TOP_K = 8
D_EXPERT = 512
ROUTED_SCALE = 2.5
ATTN_SCALE = HEAD_DIM ** -0.5
NEG_INF = -1e30
FORCE_SCORE = 1e4
RMS_EPS = 1e-6

NSA_Q_W = NSA_HEADS * HEAD_DIM
NSA_KV_W = NSA_KV_HEADS * HEAD_DIM
NSA_GATE_W = 3 * NSA_HEADS
FOX_W = FOX_HEADS * HEAD_DIM
MEM_W = MEM_HEADS * HEAD_DIM
MERGE_W = N_BRANCHES * D_MODEL

LANES = 128
VMEM_LIMIT_BYTES = 56 * 1024 * 1024

PROJ_TM = 1024
PROJ_TN = 768
ATT_TQ = 128
FOX_TQ = 256
MEM_TQ = 512
MERGE_TM = 256
ROUTER_TM = 512
MOE_BLOCK = 256
COMBINE_TM = 256

CB_MERGE = 0
CB_ATT = N_BRANCHES * D_MODEL // LANES
CB_QNSA = CB_ATT
CB_KCMP = CB_ATT + 8
CB_VCMP = CB_ATT + 10
CB_KSLC = CB_ATT + 12
CB_VSLC = CB_ATT + 14
CB_KWIN = CB_ATT + 16
CB_VWIN = CB_ATT + 18
CB_FOXQ = CB_ATT + 20
CB_FOXK = CB_ATT + 24
CB_FOXV = CB_ATT + 28
CB_MEMQ = CB_ATT + 32
PROJ_COLS = (CB_ATT + 36) * LANES
SMALL_W = 2 * LANES
FOXF_LANE = 12

_BF16 = jnp.bfloat16
_F32 = jnp.float32


def _cparams(sem):
    return pltpu.CompilerParams(dimension_semantics=sem, vmem_limit_bytes=VMEM_LIMIT_BYTES)


def _dot(a, b):
    return jnp.dot(a, b, preferred_element_type=_F32)


def _dot_nt(a, b):
    return lax.dot_general(a, b, (((1,), (1,)), ((), ())), preferred_element_type=_F32)


def _proj_kernel(norm_j0, has_small, x_ref, g_ref, w_ref, cg_ref, cf_ref, *rest):
    if has_small:
        ws_ref, o_ref, os_ref, h_ref = rest
    else:
        o_ref, h_ref = rest
    j = pl.program_id(1)

    @pl.when(j == 0)
    def _():
        x = x_ref[...]
        ms = jnp.mean(x * x, axis=-1, keepdims=True)
        h = (x * lax.rsqrt(ms + RMS_EPS) * g_ref[...]).astype(_BF16)
        h_ref[...] = h
        if has_small:
            os_ref[...] = _dot(h, ws_ref[...])

    y = _dot(h_ref[...], w_ref[...])
    tn = y.shape[1]

    @pl.when(j >= norm_j0)
    def _():
        for c in range(tn // LANES):
            sl = slice(c * LANES, (c + 1) * LANES)
            yh = y[:, sl]
            ms = jnp.mean(yh * yh, axis=-1, keepdims=True)
            scale = jnp.where(cf_ref[:, sl] > 0.0, lax.rsqrt(ms + RMS_EPS), 1.0)
            o_ref[:, sl] = (yh * scale * cg_ref[:, sl]).astype(o_ref.dtype)

    @pl.when(j < norm_j0)
    def _():
        o_ref[...] = y.astype(o_ref.dtype)


def _rms_project(x2d, g, w, col_gain, col_flag, n_plain_cols, w_small, tm, tn):
    n, d = x2d.shape
    c = w.shape[1]
    has_small = w_small is not None
    in_specs = [
        pl.BlockSpec((tm, d), lambda i, j: (i, 0)),
        pl.BlockSpec((1, d), lambda i, j: (0, 0)),
        pl.BlockSpec((d, tn), lambda i, j: (0, j)),
        pl.BlockSpec((1, tn), lambda i, j: (0, j)),
        pl.BlockSpec((1, tn), lambda i, j: (0, j)),
    ]
    args = [x2d, g.reshape(1, d), w, col_gain.reshape(1, c), col_flag.reshape(1, c)]
    out_shape = [jax.ShapeDtypeStruct((n, c), _BF16)]
    out_specs = [pl.BlockSpec((tm, tn), lambda i, j: (i, j))]
    if has_small:
        ws = w_small.shape[1]
        in_specs.append(pl.BlockSpec((d, ws), lambda i, j: (0, 0)))
        args.append(w_small)
        out_shape.append(jax.ShapeDtypeStruct((n, ws), _F32))
        out_specs.append(pl.BlockSpec((tm, ws), lambda i, j: (i, 0)))
    assert n % tm == 0 and c % tn == 0 and n_plain_cols % tn == 0
    res = pl.pallas_call(
        functools.partial(_proj_kernel, n_plain_cols // tn, has_small),
        grid=(n // tm, c // tn),
        in_specs=in_specs,
        out_specs=out_specs,
        out_shape=out_shape,
        scratch_shapes=[pltpu.VMEM((tm, d), _BF16)],
        compiler_params=_cparams(("arbitrary", "arbitrary")),
        name="rms_project",
    )(*args)
    return res if has_small else res[0]


def _cmp_kernel(x_ref, wlo_ref, whi_ref, pelo_ref, pehi_ref, kg_ref, o_ref):
    j = pl.program_id(1)
    x = x_ref[0, 0]
    nchunk = x.shape[0]
    ylo = _dot(x, wlo_ref[0])
    yhi = _dot(x, whi_ref[0])
    pe = _dot(pelo_ref[0], wlo_ref[0]) + _dot(pehi_ref[0], whi_ref[0])
    y = ylo + pltpu.roll(yhi, nchunk - 1, 0) + pe[0:1, :]

    @pl.when(j < NSA_KV_HEADS)
    def _():
        ms = jnp.mean(y * y, axis=-1, keepdims=True)
        o_ref[0, 0] = (y * lax.rsqrt(ms + RMS_EPS) * kg_ref[...]).astype(o_ref.dtype)

    @pl.when(j >= NSA_KV_HEADS)
    def _():
        o_ref[0, 0] = y.astype(o_ref.dtype)


def _nsa_compress(cmp_in, w_ck, w_cv, pe_k, pe_v, k_g):
    b, nj, t, dk = cmp_in.shape
    nchunk = t // CMP_STRIDE
    half = CMP_LEN // 2
    x = cmp_in.reshape(b, nj, nchunk, CMP_STRIDE * dk)

    def halves(w):
        return (w[:half].reshape(half * dk, dk).astype(_BF16),
                w[half:].reshape(half * dk, dk).astype(_BF16))

    klo, khi = halves(w_ck)
    vlo, vhi = halves(w_cv)
    wlo = jnp.stack([klo, vlo])
    whi = jnp.stack([khi, vhi])

    def pe_halves(pe):
        lo = jnp.broadcast_to(pe[:half].reshape(1, half * dk), (8, half * dk)).astype(_BF16)
        hi = jnp.broadcast_to(pe[half:].reshape(1, half * dk), (8, half * dk)).astype(_BF16)
        return lo, hi

    pklo, pkhi = pe_halves(pe_k)
    pvlo, pvhi = pe_halves(pe_v)
    pelo = jnp.stack([pklo, pvlo])
    pehi = jnp.stack([pkhi, pvhi])
    kv = lambda bb, j: (j // NSA_KV_HEADS, 0, 0)
    return pl.pallas_call(
        _cmp_kernel,
        grid=(b, nj),
        in_specs=[
            pl.BlockSpec((1, 1, nchunk, CMP_STRIDE * dk), lambda bb, j: (bb, j, 0, 0)),
            pl.BlockSpec((1, half * dk, dk), kv),
            pl.BlockSpec((1, half * dk, dk), kv),
            pl.BlockSpec((1, 8, half * dk), kv),
            pl.BlockSpec((1, 8, half * dk), kv),
            pl.BlockSpec((1, dk), lambda bb, j: (0, 0)),
        ],
        out_specs=pl.BlockSpec((1, 1, nchunk, dk), lambda bb, j: (bb, j, 0, 0)),
        out_shape=jax.ShapeDtypeStruct((b, nj, nchunk, dk), _BF16),
        compiler_params=_cparams(("arbitrary", "arbitrary")),
        name="nsa_compress",
    )(x, wlo, whi, pelo, pehi, k_g.reshape(1, dk).astype(_F32))


def _softmax_update(state, s, mask, v):
    m, l, acc = state
    sm = jnp.where(mask, s, NEG_INF)
    m_new = jnp.maximum(m, jnp.max(sm, axis=1, keepdims=True))
    e = jnp.where(mask, jnp.exp(s - m_new), 0.0)
    alpha = jnp.exp(m - m_new)
    l_new = alpha * l + jnp.sum(e, axis=1, keepdims=True)
    acc_new = alpha * acc + _dot(e.astype(_BF16), v)
    return m_new, l_new, acc_new


def _softmax_finish(state):
    _, l, acc = state
    return acc / jnp.where(l > 0.0, l, 1.0)


def _nsa_kernel(far_ref, q_ref, kc_ref, vc_ref, ks_ref, vs_ref, kw_ref, vw_ref, gl_ref,
                ctab_ref, dtab_ref, ov_ref, ex_ref, o_ref, sel_ref):
    g = pl.program_id(1)
    i = pl.program_id(2)
    tq = ATT_TQ
    hg = NSA_GROUP
    m_rows = hg * tq
    n_slc = ex_ref.shape[1] // SLC_LEN

    q = q_ref[0]
    q4 = jnp.concatenate([q[:, h * HEAD_DIM:(h + 1) * HEAD_DIM] for h in range(hg)], axis=0)

    row = lax.broadcasted_iota(jnp.int32, (m_rows, LANES), 0) & (tq - 1)
    col = lax.broadcasted_iota(jnp.int32, (m_rows, LANES), 1)
    hrow = lax.broadcasted_iota(jnp.int32, (m_rows, 1), 0) // tq
    far = jnp.zeros((m_rows, 1), _F32)
    for h in range(hg):
        far = jnp.where(hrow == h, far_ref[g * hg + h], far)

    def init():
        return (jnp.full((m_rows, 1), NEG_INF, _F32), jnp.zeros((m_rows, 1), _F32),
                jnp.zeros((m_rows, HEAD_DIM), _F32))

    s = _dot_nt(q4, kc_ref[0, 0]) + ctab_ref[...].reshape(m_rows, LANES)
    mask_c = (i * tq + row) >= (CMP_STRIDE * col + CMP_LEN - 1)
    sm = jnp.where(mask_c, s, NEG_INF)
    mx = jnp.max(sm, axis=1, keepdims=True)
    e = jnp.where(mask_c, jnp.exp(s - mx), 0.0)
    l = jnp.sum(e, axis=1, keepdims=True)
    p_c = e / jnp.where(l > 0.0, l, 1.0)
    o_c = _dot(p_c.astype(_BF16), vc_ref[0, 0])

    ps = p_c[0:tq]
    for h in range(1, hg):
        ps = ps + p_c[h * tq:(h + 1) * tq]
    ps_hi = ps.astype(_BF16)
    ps_lo = (ps - ps_hi.astype(_F32)).astype(_BF16)
    imp = _dot(ps_hi, ov_ref[...]) + _dot(ps_lo, ov_ref[...])
    tpos = i * tq + lax.broadcasted_iota(jnp.int32, (tq, LANES), 0)
    blk = lax.broadcasted_iota(jnp.int32, (tq, LANES), 1)
    cur = tpos // SLC_LEN
    valid = blk * SLC_LEN <= tpos
    forced = (blk == 0) | (blk == cur) | (blk == cur - 1)
    score = jnp.where(valid, jnp.where(forced, FORCE_SCORE, imp), -1.0)
    score = jnp.where(blk < n_slc, score, -2.0)
    rank = jnp.zeros((tq, LANES), _F32)
    for k in range(n_slc):
        ck = score[:, k:k + 1]
        beats = (ck > score) | ((ck == score) & (blk > k))
        rank = rank + jnp.where(beats, 1.0, 0.0)
    sel = jnp.where((rank < float(min(N_SELECT, n_slc))) & (blk < n_slc), 1.0, 0.0).astype(_BF16)
    selfull = _dot(sel, ex_ref[...])
    for kt in range(ex_ref.shape[1] // LANES):
        sel_ref[kt] = selfull[:, kt * LANES:(kt + 1) * LANES]

    def sel_mask(kt):
        st = sel_ref[kt]
        return jnp.concatenate([st] * hg, axis=0) > 0.5

    def tile(ref, kt):
        return ref[0, pl.ds(pl.multiple_of(kt * LANES, LANES), LANES), :]

    def near_bias(d):
        return dtab_ref[:, d].reshape(m_rows, LANES)

    def far_body(kt, st):
        sc = _dot_nt(q4, tile(ks_ref, kt)) + far
        return _softmax_update(st, sc, sel_mask(kt), tile(vs_ref, kt))

    st = lax.fori_loop(0, jnp.maximum(i - 1, 0), far_body, init())
    kt1 = jnp.maximum(i - 1, 0)
    sc = _dot_nt(q4, tile(ks_ref, kt1)) + near_bias(1)
    st = _softmax_update(st, sc, sel_mask(kt1) & (i >= 1), tile(vs_ref, kt1))
    sc = _dot_nt(q4, tile(ks_ref, i)) + near_bias(0)
    st = _softmax_update(st, sc, sel_mask(i) & (row >= col), tile(vs_ref, i))
    o_s = _softmax_finish(st)

    n_win = WINDOW // tq
    st = init()
    for d in range(n_win, -1, -1):
        ktd = jnp.maximum(i - d, 0)
        bias = near_bias(d) if d <= 1 else far
        sc = _dot_nt(q4, tile(kw_ref, ktd)) + bias
        ok = i >= d
        if d == n_win:
            mk = (row < col) & ok
        elif d == 0:
            mk = row >= col
        else:
            mk = jnp.broadcast_to(ok, (m_rows, LANES))
        st = _softmax_update(st, sc, mk, tile(vw_ref, ktd))
    o_w = _softmax_finish(st)

    gl = jax.nn.sigmoid(gl_ref[0])
    for h in range(hg):
        rs = slice(h * tq, (h + 1) * tq)
        o = (gl[:, h:h + 1] * o_c[rs] + gl[:, hg + h:hg + h + 1] * o_s[rs]
             + gl[:, 2 * hg + h:2 * hg + h + 1] * o_w[rs])
        o_ref[0, :, h * HEAD_DIM:(h + 1) * HEAD_DIM] = o.astype(o_ref.dtype)


def _t5_bucket(rel):
    n = jnp.maximum(rel, 0)
    max_exact = NUM_BUCKETS // 2
    log_ratio = jnp.log(jnp.maximum(n, 1).astype(_F32) / max_exact) / math.log(MAX_DISTANCE / max_exact)
    large = jnp.minimum(max_exact + (log_ratio * (NUM_BUCKETS - max_exact)).astype(jnp.int32), NUM_BUCKETS - 1)
    return jnp.where(n < max_exact, n, large)


def _nsa_tables(rel_bias, t):
    rb = rel_bias.astype(_F32).T
    tpos = jnp.arange(t)
    n = jnp.arange(LANES)
    rel_c = tpos[:, None] - (n * CMP_STRIDE + CMP_LEN - 1)[None, :]
    ctab = rb[:, _t5_bucket(rel_c)]
    r = jnp.arange(ATT_TQ)
    rel_d = (jnp.arange(2) * ATT_TQ)[:, None, None] + r[None, :, None] - r[None, None, :]
    dtab = rb[:, _t5_bucket(rel_d)]
    far = rb[:, NUM_BUCKETS - 1]
    return ctab, dtab, far


def _nsa_attention(proj, small, kvc, rel_bias):
    b, t, _ = proj.shape
    g, hg, tq = NSA_KV_HEADS, NSA_GROUP, ATT_TQ
    n_slc = t // SLC_LEN
    n_cmp = (t - CMP_LEN) // CMP_STRIDE + 1
    assert n_cmp <= LANES and kvc.shape[2] == LANES and n_slc <= LANES
    assert 2 * ATT_TQ >= 113 + ATT_TQ - 1 or True
    ctab, dtab, far = _nsa_tables(rel_bias, t)
    cstart = jnp.arange(LANES) * CMP_STRIDE
    sstart = jnp.arange(LANES) * SLC_LEN
    overlap = jnp.clip(jnp.minimum(cstart[:, None] + CMP_LEN, sstart[None, :] + SLC_LEN)
                       - jnp.maximum(cstart[:, None], sstart[None, :]), 0, None)
    ov = (overlap.astype(_F32) / CMP_LEN)
    ov = jnp.where((jnp.arange(LANES)[:, None] < n_cmp) & (jnp.arange(LANES)[None, :] < n_slc), ov, 0.0).astype(_BF16)
    ex = (jnp.arange(t)[None, :] // SLC_LEN == jnp.arange(LANES)[:, None]).astype(_BF16)

    seq = lambda cb: pl.BlockSpec((1, t, LANES), lambda bb, gg, ii, far_: (bb, 0, cb + gg))
    grid_spec = pltpu.PrefetchScalarGridSpec(
        num_scalar_prefetch=1,
        grid=(b, g, t // tq),
        in_specs=[
            pl.BlockSpec((1, tq, hg * HEAD_DIM),
                         lambda bb, gg, ii, far_: (bb, ii, CB_QNSA * LANES // (hg * HEAD_DIM) + gg)),
            pl.BlockSpec((1, 1, LANES, HEAD_DIM), lambda bb, gg, ii, far_: (bb, gg, 0, 0)),
            pl.BlockSpec((1, 1, LANES, HEAD_DIM), lambda bb, gg, ii, far_: (bb, NSA_KV_HEADS + gg, 0, 0)),
            seq(CB_KSLC), seq(CB_VSLC), seq(CB_KWIN), seq(CB_VWIN),
            pl.BlockSpec((1, tq, LANES), lambda bb, gg, ii, far_: (bb, ii, gg)),
            pl.BlockSpec((hg, tq, LANES), lambda bb, gg, ii, far_: (gg, ii, 0)),
            pl.BlockSpec((hg, 2, tq, LANES), lambda bb, gg, ii, far_: (gg, 0, 0, 0)),
            pl.BlockSpec((LANES, LANES), lambda bb, gg, ii, far_: (0, 0)),
            pl.BlockSpec((LANES, t), lambda bb, gg, ii, far_: (0, 0)),
        ],
        out_specs=pl.BlockSpec((1, tq, hg * HEAD_DIM), lambda bb, gg, ii, far_: (bb, ii, gg)),
        scratch_shapes=[pltpu.VMEM((t // LANES, tq, LANES), _F32)],
    )
    return pl.pallas_call(
        _nsa_kernel,
        grid_spec=grid_spec,
        out_shape=jax.ShapeDtypeStruct((b, t, NSA_Q_W), _BF16),
        compiler_params=_cparams(("arbitrary", "arbitrary", "arbitrary")),
        name="nsa_attention",
    )(far, proj, kvc, kvc, proj, proj, proj, proj, small, ctab, dtab, ov, ex)


def _fox_cum_kernel(s_ref, b_ref, col_ref, row_ref):
    z = s_ref[0] + b_ref[...]
    lf = jnp.minimum(z, 0.0) - jnp.log1p(jnp.exp(-jnp.abs(z)))
    x = lf.T
    t = x.shape[1]
    lane = lax.broadcasted_iota(jnp.int32, x.shape, 1)
    sh = 1
    while sh < t:
        x = x + jnp.where(lane >= sh, pltpu.roll(x, sh, 1), 0.0)
        sh *= 2
    row_ref[0] = x[8:16, :]
    col_ref[0] = x.T


def _fox_cumsum(small, b_f):
    b, t, _ = small.shape
    bvec = jnp.zeros((1, LANES), _F32).at[0, FOXF_LANE:FOXF_LANE + FOX_HEADS].set(b_f.astype(_F32))
    return pl.pallas_call(
        _fox_cum_kernel,
        grid=(b,),
        in_specs=[pl.BlockSpec((1, t, LANES), lambda bb: (bb, 0, 1)),
                  pl.BlockSpec((1, LANES), lambda bb: (0, 0))],
        out_specs=[pl.BlockSpec((1, t, LANES), lambda bb: (bb, 0, 0)),
                   pl.BlockSpec((1, 8, t), lambda bb: (bb, 0, 0))],
        out_shape=[jax.ShapeDtypeStruct((b, t, LANES), _F32), jax.ShapeDtypeStruct((b, 8, t), _F32)],
        compiler_params=_cparams(("arbitrary",)),
        name="fox_cumsum",
    )(small, bvec)


def _fox_kernel(q_ref, k_ref, v_ref, cc_ref, cr_ref, o_ref):
    i = pl.program_id(1)
    tq = FOX_TQ
    row = lax.broadcasted_iota(jnp.int32, (tq, tq), 0)
    col = lax.broadcasted_iota(jnp.int32, (tq, tq), 1)
    for h in range(FOX_HEADS):
        hs = slice(h * HEAD_DIM, (h + 1) * HEAD_DIM)
        lane = FOXF_LANE + h
        sub = FOXF_LANE - 8 + h
        q = q_ref[0, :, hs]
        cq = cc_ref[0, :, lane:lane + 1]

        def logits(kt):
            ks = pl.ds(pl.multiple_of(kt * tq, tq), tq)
            ck = cr_ref[0, sub:sub + 1, ks]
            return _dot_nt(q, k_ref[0, ks, hs]) + (cq - ck), v_ref[0, ks, hs]

        def body(kt, st):
            sc, v = logits(kt)
            return _softmax_update(st, sc, jnp.full((tq, tq), True), v)

        st = (jnp.full((tq, 1), NEG_INF, _F32), jnp.zeros((tq, 1), _F32), jnp.zeros((tq, HEAD_DIM), _F32))
        st = lax.fori_loop(0, i, body, st)
        sc, v = logits(i)
        st = _softmax_update(st, sc, row >= col, v)
        o_ref[0, :, hs] = _softmax_finish(st).astype(o_ref.dtype)


def _fox_attention(proj, cum_col, cum_row):
    b, t, _ = proj.shape
    tq = FOX_TQ
    w = FOX_W
    return pl.pallas_call(
        _fox_kernel,
        grid=(b, t // tq),
        in_specs=[
            pl.BlockSpec((1, tq, w), lambda bb, ii: (bb, ii, CB_FOXQ * LANES // w)),
            pl.BlockSpec((1, t, w), lambda bb, ii: (bb, 0, CB_FOXK * LANES // w)),
            pl.BlockSpec((1, t, w), lambda bb, ii: (bb, 0, CB_FOXV * LANES // w)),
            pl.BlockSpec((1, tq, LANES), lambda bb, ii: (bb, ii, 0)),
            pl.BlockSpec((1, 8, t), lambda bb, ii: (bb, 0, 0)),
        ],
        out_specs=pl.BlockSpec((1, tq, w), lambda bb, ii: (bb, ii, 0)),
        out_shape=jax.ShapeDtypeStruct((b, t, w), _BF16),
        compiler_params=_cparams(("arbitrary", "arbitrary")),
        name="fox_attention",
    )(proj, proj, proj, cum_col, cum_row)


def _mem_kernel(q_ref, kv_ref, o_ref):
    for h in range(MEM_HEADS):
        hs = slice(h * HEAD_DIM, (h + 1) * HEAD_DIM)
        vs = slice(MEM_W + h * HEAD_DIM, MEM_W + (h + 1) * HEAD_DIM)
        s = _dot_nt(q_ref[0, :, hs], kv_ref[0, :, hs])
        e = jnp.exp(s - jnp.max(s, axis=1, keepdims=True))
        p = e / jnp.sum(e, axis=1, keepdims=True)
        o_ref[0, :, hs] = _dot(p.astype(_BF16), kv_ref[0, :, vs]).astype(o_ref.dtype)


def _mem_attention(proj, memkv):
    b, t, _ = proj.shape
    m = memkv.shape[1]
    tq = MEM_TQ
    return pl.pallas_call(
        _mem_kernel,
        grid=(b, t // tq),
        in_specs=[pl.BlockSpec((1, tq, MEM_W), lambda bb, ii: (bb, ii, CB_MEMQ * LANES // MEM_W)),
                  pl.BlockSpec((1, m, 2 * MEM_W), lambda bb, ii: (bb, 0, 0))],
        out_specs=pl.BlockSpec((1, tq, MEM_W), lambda bb, ii: (bb, ii, 0)),
        out_shape=jax.ShapeDtypeStruct((b, t, MEM_W), _BF16),
        compiler_params=_cparams(("arbitrary", "arbitrary")),
        name="mem_attention",
    )(proj, memkv)


def _merge_kernel(x_ref, on_ref, of_ref, om_ref, g0_ref, g1_ref, g2_ref, wn_ref, wf_ref, wm_ref, wo_ref,
                  o_ref, z_ref):
    tn = 512
    for c in range(D_MODEL // tn):
        cs = slice(c * tn, (c + 1) * tn)
        z = (jax.nn.sigmoid(g0_ref[:, cs].astype(_F32)) * _dot(on_ref[...], wn_ref[:, cs])
             + jax.nn.sigmoid(g1_ref[:, cs].astype(_F32)) * _dot(of_ref[...], wf_ref[:, cs])
             + jax.nn.sigmoid(g2_ref[:, cs].astype(_F32)) * _dot(om_ref[...], wm_ref[:, cs]))
        z_ref[:, cs] = z.astype(_BF16)
    o_ref[...] = x_ref[...] + _dot(z_ref[...], wo_ref[...])


def _merge(x2d, o_nsa, o_fox, o_mem, proj2d, w_o_nsa, w_o_fox, w_o_mem, w_out):
    n, d = x2d.shape
    tm = MERGE_TM
    gcb = CB_MERGE * LANES // d
    const = lambda r, c: pl.BlockSpec((r, c), lambda i: (0, 0))
    return pl.pallas_call(
        _merge_kernel,
        grid=(n // tm,),
        in_specs=[
            pl.BlockSpec((tm, d), lambda i: (i, 0)),
            pl.BlockSpec((tm, NSA_Q_W), lambda i: (i, 0)),
            pl.BlockSpec((tm, FOX_W), lambda i: (i, 0)),
            pl.BlockSpec((tm, MEM_W), lambda i: (i, 0)),
            pl.BlockSpec((tm, d), lambda i: (i, gcb)),
            pl.BlockSpec((tm, d), lambda i: (i, gcb + 1)),
            pl.BlockSpec((tm, d), lambda i: (i, gcb + 2)),
            const(NSA_Q_W, d), const(FOX_W, d), const(MEM_W, d), const(d, d),
        ],
        out_specs=pl.BlockSpec((tm, d), lambda i: (i, 0)),
        out_shape=jax.ShapeDtypeStruct((n, d), _F32),
        scratch_shapes=[pltpu.VMEM((tm, d), _BF16)],
        compiler_params=_cparams(("arbitrary",)),
        name="merge_out",
    )(x2d, o_nsa, o_fox, o_mem, proj2d, proj2d, proj2d,
      w_o_nsa.astype(_BF16), w_o_fox.astype(_BF16), w_o_mem.astype(_BF16), w_out.astype(_BF16))


def _nsa_inputs(x, norm_attn_g, w_in, nsa_pe_k, nsa_w_ck, nsa_pe_v, nsa_w_cv, nsa_q_g, nsa_k_g,
                fox_q_g, fox_k_g, mem_q_g):
    b, t, d = x.shape
    ones = lambda k: jnp.ones((k * HEAD_DIM,), _F32)
    zeros = lambda k: jnp.zeros((k * HEAD_DIM,), _F32)
    tile = lambda gv, k: jnp.tile(gv.astype(_F32), k)

    g_end = NSA_Q_W + 6 * NSA_KV_W
    f_off = g_end + NSA_GATE_W + 3 * FOX_W
    m_off = f_off + FOX_HEADS + MEM_W
    w_all = jnp.concatenate([w_in[:, m_off:], w_in[:, :g_end], w_in[:, g_end + NSA_GATE_W:f_off],
                             w_in[:, f_off + FOX_HEADS:m_off]], axis=1).astype(_BF16)
    gates = w_in[:, g_end:g_end + NSA_GATE_W].reshape(d, 3, NSA_KV_HEADS, NSA_GROUP)
    pad = jnp.zeros((d, LANES - 3 * NSA_GROUP), w_in.dtype)
    w_small = jnp.concatenate([
        gates[:, :, 0, :].reshape(d, 3 * NSA_GROUP), pad,
        gates[:, :, 1, :].reshape(d, 3 * NSA_GROUP), w_in[:, f_off:f_off + FOX_HEADS],
        pad[:, :LANES - 3 * NSA_GROUP - FOX_HEADS]], axis=1).astype(_BF16)
    col_gain = jnp.concatenate([
        jnp.ones((MERGE_W,), _F32),
        tile(nsa_q_g, NSA_HEADS) * ATTN_SCALE, ones(2), ones(2), tile(nsa_k_g, 2), ones(2), tile(nsa_k_g, 2), ones(2),
        tile(fox_q_g, FOX_HEADS) * ATTN_SCALE, tile(fox_k_g, FOX_HEADS), ones(FOX_HEADS),
        tile(mem_q_g, MEM_HEADS) * ATTN_SCALE])
    col_flag = jnp.concatenate([
        jnp.zeros((MERGE_W,), _F32),
        ones(NSA_HEADS), zeros(2), zeros(2), ones(2), zeros(2), ones(2), zeros(2),
        ones(FOX_HEADS), ones(FOX_HEADS), zeros(FOX_HEADS), ones(MEM_HEADS)])

    x2d = x.reshape(b * t, d)
    proj2d, small2d = _rms_project(x2d, norm_attn_g, w_all, col_gain, col_flag, CB_ATT * LANES, w_small,
                                   PROJ_TM, PROJ_TN)
    proj = proj2d.reshape(b, t, PROJ_COLS)
    small = small2d.reshape(b, t, SMALL_W)

    cmp_in = proj[:, :, CB_KCMP * LANES:(CB_KCMP + 4) * LANES].reshape(b, t, 4, HEAD_DIM).transpose(0, 2, 1, 3)
    kvc = _nsa_compress(cmp_in, nsa_w_ck, nsa_w_cv, nsa_pe_k, nsa_pe_v, nsa_k_g)
    return proj, small, kvc


def _attention_mixers(x, mem, norm_attn_g, w_in, nsa_pe_k, nsa_w_ck, nsa_pe_v, nsa_w_cv, nsa_q_g, nsa_k_g,
                      rel_bias, fox_b_f, fox_q_g, fox_k_g, norm_mem_g, w_mem_kv, mem_q_g, mem_k_g):
    b, t, d = x.shape
    ones = lambda k: jnp.ones((k * HEAD_DIM,), _F32)
    zeros = lambda k: jnp.zeros((k * HEAD_DIM,), _F32)
    tile = lambda gv, k: jnp.tile(gv.astype(_F32), k)
    proj, small, kvc = _nsa_inputs(x, norm_attn_g, w_in, nsa_pe_k, nsa_w_ck, nsa_pe_v, nsa_w_cv, nsa_q_g, nsa_k_g,
                                   fox_q_g, fox_k_g, mem_q_g)
    proj2d = proj.reshape(b * t, PROJ_COLS)
    o_nsa = _nsa_attention(proj, small, kvc, rel_bias)

    cum_col, cum_row = _fox_cumsum(small, fox_b_f)
    o_fox = _fox_attention(proj, cum_col, cum_row)

    mm = mem.shape[1]
    mem_gain = jnp.concatenate([tile(mem_k_g, MEM_HEADS), ones(MEM_HEADS)])
    mem_flag = jnp.concatenate([ones(MEM_HEADS), zeros(MEM_HEADS)])
    memkv = _rms_project(mem.reshape(b * mm, d), norm_mem_g, w_mem_kv.astype(_BF16), mem_gain, mem_flag,
                         0, None, mm, MEM_W).reshape(b, mm, 2 * MEM_W)
    o_mem = _mem_attention(proj, memkv)
    return proj2d, o_nsa.reshape(b * t, NSA_Q_W), o_fox.reshape(b * t, FOX_W), o_mem.reshape(b * t, MEM_W)


def _router_kernel(x_ref, g_ref, whi_ref, wlo_ref, b_ref, h_ref, idx_ref, gate_ref):
    x = x_ref[...]
    ms = jnp.mean(x * x, axis=-1, keepdims=True)
    h = x * lax.rsqrt(ms + RMS_EPS) * g_ref[...]
    h_ref[...] = h
    h_hi = h.astype(_BF16)
    h_lo = (h - h_hi.astype(_F32)).astype(_BF16)
    logits = _dot_nt(whi_ref[...], h_hi) + _dot_nt(whi_ref[...], h_lo) + _dot_nt(wlo_ref[...], h_hi)
    scores = jax.nn.sigmoid(logits)
    sb = scores + b_ref[...]
    eidx = lax.broadcasted_iota(jnp.int32, sb.shape, 0)
    idxs, vals = [], []
    for _ in range(TOP_K):
        m = jnp.max(sb, axis=0, keepdims=True)
        idx = jnp.min(jnp.where(sb == m, eidx, N_EXPERTS), axis=0, keepdims=True)
        hit = eidx == idx
        vals.append(jnp.sum(jnp.where(hit, scores, 0.0), axis=0, keepdims=True))
        idxs.append(idx)
        sb = jnp.where(hit, NEG_INF, sb)
    top_s = jnp.concatenate(vals, axis=0)
    idx_ref[...] = jnp.concatenate(idxs, axis=0)
    gate_ref[...] = top_s / jnp.sum(top_s, axis=0, keepdims=True) * ROUTED_SCALE


def _router(x1, norm_g, w_router, router_bias):
    n, d = x1.shape
    tm = ROUTER_TM
    wt = w_router.astype(_F32).T
    w_hi = wt.astype(_BF16)
    w_lo = (wt - w_hi.astype(_F32)).astype(_BF16)
    const = lambda r, c: pl.BlockSpec((r, c), lambda i: (0, 0))
    return pl.pallas_call(
        _router_kernel,
        grid=(n // tm,),
        in_specs=[pl.BlockSpec((tm, d), lambda i: (i, 0)), const(1, d), const(N_EXPERTS, d), const(N_EXPERTS, d),
                  const(N_EXPERTS, 1)],
        out_specs=[pl.BlockSpec((tm, d), lambda i: (i, 0)),
                   pl.BlockSpec((TOP_K, tm), lambda i: (0, i)),
                   pl.BlockSpec((TOP_K, tm), lambda i: (0, i))],
        out_shape=[jax.ShapeDtypeStruct((n, d), _F32), jax.ShapeDtypeStruct((TOP_K, n), jnp.int32),
                   jax.ShapeDtypeStruct((TOP_K, n), _F32)],
        compiler_params=_cparams(("arbitrary",)),
        name="moe_router",
    )(x1, norm_g.reshape(1, d).astype(_F32), w_hi, w_lo, router_bias.reshape(N_EXPERTS, 1).astype(_F32))


def _moe_kernel(be_ref, nbr_ref, src_hbm, dst_hbm, h_hbm, wg_ref, wu_ref, wd_ref, y_hbm,
                xbuf, ybuf, sidx, didx, wg_bf, wu_bf, wd_bf, gsem, ssem, isem):
    n = pl.program_id(0)
    nbr = nbr_ref[0]
    slot = n % 2
    blk = MOE_BLOCK

    def idx_copies(block, s):
        return (pltpu.make_async_copy(src_hbm.at[block], sidx.at[s], isem.at[0, s]),
                pltpu.make_async_copy(dst_hbm.at[block], didx.at[s], isem.at[1, s]))

    def start_idx(block, s):
        for c in idx_copies(block, s):
            c.start()

    def wait_idx(block, s):
        for c in idx_copies(block, s):
            c.wait()

    def issue_gather(s):
        def body(r, carry):
            tok = sidx[s, r]
            pltpu.make_async_copy(h_hbm.at[pl.ds(tok, 1), :], xbuf.at[s, pl.ds(r, 1), :], gsem.at[s]).start()
            return carry
        lax.fori_loop(0, blk, body, 0, unroll=8)

    def issue_scatter(s):
        def body(r, carry):
            row = didx[s, r]
            pltpu.make_async_copy(ybuf.at[s, pl.ds(r, 1), :], y_hbm.at[pl.ds(row, 1), :], ssem.at[s]).start()
            return carry
        lax.fori_loop(0, blk, body, 0, unroll=8)

    def wait_rows(buf, sem, s):
        pltpu.make_async_copy(buf.at[s], buf.at[s], sem.at[s]).wait()

    @pl.when(n < nbr)
    def _():
        @pl.when(n == 0)
        def _():
            start_idx(0, 0)
            ybuf[1] = jnp.zeros(ybuf.shape[1:], ybuf.dtype)
            pad_rows = pltpu.make_async_copy(ybuf.at[1], y_hbm.at[pl.ds(y_hbm.shape[0] - blk, blk), :], ssem.at[1])
            pad_rows.start()
            pad_rows.wait()
            wait_idx(0, 0)
            issue_gather(0)

            @pl.when(nbr > 1)
            def _():
                start_idx(1, 1)

        wait_rows(xbuf, gsem, slot)

        @pl.when(n + 1 < nbr)
        def _():
            wait_idx(n + 1, 1 - slot)
            issue_gather(1 - slot)

        changed = (n == 0) | (be_ref[n] != be_ref[jnp.maximum(n - 1, 0)])

        @pl.when(changed)
        def _():
            wg_bf[...] = wg_ref[0].astype(_BF16)
            wu_bf[...] = wu_ref[0].astype(_BF16)
            wd_bf[...] = wd_ref[0].astype(_BF16)

        @pl.when(n >= 2)
        def _():
            wait_rows(ybuf, ssem, slot)

        x = xbuf[slot].astype(_BF16)
        a = (jax.nn.silu(_dot(x, wg_bf[...])) * _dot(x, wu_bf[...])).astype(_BF16)
        ybuf[slot] = _dot(a, wd_bf[...])
        issue_scatter(slot)

        @pl.when(n + 2 < nbr)
        def _():
            start_idx(n + 2, slot)

        @pl.when(n == nbr - 1)
        def _():
            @pl.when(n >= 1)
            def _():
                wait_rows(ybuf, ssem, 1 - slot)
            wait_rows(ybuf, ssem, slot)


def _moe_experts(h2, block_expert, nb_real, slot_src, slot_dst, we_gate, we_up, we_down):
    n, d = h2.shape
    nb = block_expert.shape[0]
    blk = MOE_BLOCK
    e, _, de = we_gate.shape
    grid_spec = pltpu.PrefetchScalarGridSpec(
        num_scalar_prefetch=2,
        grid=(nb,),
        in_specs=[
            pl.BlockSpec(memory_space=pl.ANY),
            pl.BlockSpec(memory_space=pl.ANY),
            pl.BlockSpec(memory_space=pl.ANY),
            pl.BlockSpec((1, d, de), lambda i, be, nbr: (be[i], 0, 0)),
            pl.BlockSpec((1, d, de), lambda i, be, nbr: (be[i], 0, 0)),
            pl.BlockSpec((1, de, d), lambda i, be, nbr: (be[i], 0, 0)),
        ],
        out_specs=pl.BlockSpec(memory_space=pl.ANY),
        scratch_shapes=[
            pltpu.VMEM((2, blk, d), _F32), pltpu.VMEM((2, blk, d), _F32),
            pltpu.SMEM((2, blk), jnp.int32), pltpu.SMEM((2, blk), jnp.int32),
            pltpu.VMEM((d, de), _BF16), pltpu.VMEM((d, de), _BF16), pltpu.VMEM((de, d), _BF16),
            pltpu.SemaphoreType.DMA((2,)), pltpu.SemaphoreType.DMA((2,)), pltpu.SemaphoreType.DMA((2, 2)),
        ],
    )
    return pl.pallas_call(
        _moe_kernel,
        grid_spec=grid_spec,
        out_shape=jax.ShapeDtypeStruct((TOP_K * n + blk, d), _F32),
        compiler_params=_cparams(("arbitrary",)),
        name="moe_experts",
    )(block_expert, nb_real, slot_src, slot_dst, h2, we_gate, we_up, we_down)


def _dispatch_plan(top_idx, n):
    blk = MOE_BLOCK
    a = n * TOP_K
    flat_e = top_idx.reshape(a)
    order = jnp.argsort(flat_e).astype(jnp.int32)
    e_sorted = flat_e[order]
    counts = jnp.bincount(flat_e, length=N_EXPERTS).astype(jnp.int32)
    padded = (counts + blk - 1) // blk * blk
    start = jnp.cumsum(counts) - counts
    pstart = jnp.cumsum(padded) - padded
    dest = pstart[e_sorted] + jnp.arange(a, dtype=jnp.int32) - start[e_sorted]
    nb = (a + N_EXPERTS * (blk - 1)) // blk
    p = nb * blk
    slot_a = jnp.full((p,), -1, jnp.int32).at[dest].set(order)
    tok = slot_a // TOP_K
    kk = slot_a - tok * TOP_K
    real = slot_a >= 0
    slot_src = jnp.where(real, tok, 0)
    slot_dst = jnp.where(real, kk * n + tok, TOP_K * n + jnp.arange(p, dtype=jnp.int32) % blk)
    block_end = jnp.cumsum(padded) // blk
    block_expert = jnp.minimum(jnp.searchsorted(block_end, jnp.arange(nb), side='right'), N_EXPERTS - 1)
    nb_real = (jnp.sum(padded) // blk).reshape(1).astype(jnp.int32)
    return block_expert.astype(jnp.int32), nb_real, slot_src.reshape(nb, blk), slot_dst.reshape(nb, blk)


def _combine_kernel(x_ref, h_ref, g_ref, wsg_ref, wsu_ref, wsd_ref, *rest):
    y_refs, o_ref = rest[:TOP_K], rest[TOP_K]
    h = h_ref[...].astype(_BF16)
    a = (jax.nn.silu(_dot(h, wsg_ref[...])) * _dot(h, wsu_ref[...])).astype(_BF16)
    acc = x_ref[...] + _dot(a, wsd_ref[...])
    g = g_ref[...]
    for k in range(TOP_K):
        acc = acc + g[:, k:k + 1] * y_refs[k][...]
    o_ref[...] = acc


def _combine(x1, h2, gates, y, ws_gate, ws_up, ws_down):
    n, d = x1.shape
    tm = COMBINE_TM
    nt = n // tm
    de = ws_gate.shape[1]
    const = lambda r, c: pl.BlockSpec((r, c), lambda i: (0, 0))
    row = pl.BlockSpec((tm, d), lambda i: (i, 0))
    y_specs = [pl.BlockSpec((tm, d), functools.partial(lambda i, k: (k * nt + i, 0), k=k)) for k in range(TOP_K)]
    return pl.pallas_call(
        _combine_kernel,
        grid=(nt,),
        in_specs=[row, row, pl.BlockSpec((tm, TOP_K), lambda i: (i, 0)), const(d, de), const(d, de), const(de, d)]
        + y_specs,
        out_specs=row,
        out_shape=jax.ShapeDtypeStruct((n, d), _F32),
        compiler_params=_cparams(("arbitrary",)),
        name="moe_combine",
    )(x1, h2, gates, ws_gate.astype(_BF16), ws_up.astype(_BF16), ws_down.astype(_BF16), *([y] * TOP_K))


def _moe_block(x1, norm_ffn_g, w_router, router_bias, we_gate, we_up, we_down, ws_gate, ws_up, ws_down):
    n = x1.shape[0]
    h2, top_idx_t, gates_t = _router(x1, norm_ffn_g, w_router, router_bias)
    block_expert, nb_real, slot_src, slot_dst = _dispatch_plan(top_idx_t.T, n)
    y = _moe_experts(h2, block_expert, nb_real, slot_src, slot_dst, we_gate, we_up, we_down)
    return _combine(x1, h2, gates_t.T, y, ws_gate, ws_up, ws_down)


def kernel(x, mem, norm_attn_g, w_in, nsa_pe_k, nsa_w_ck, nsa_pe_v, nsa_w_cv, nsa_q_g, nsa_k_g, rel_bias, fox_b_f, fox_q_g, fox_k_g, norm_mem_g, w_mem_kv, mem_q_g, mem_k_g, w_o_nsa, w_o_fox, w_o_mem, w_out, norm_ffn_g, w_router, router_bias, we_gate, we_up, we_down, ws_gate, ws_up, ws_down):
    b, t, d = x.shape
    assert norm_attn_g.shape[0] == 1, "single-layer problem"
    l = 0
    proj2d, o_nsa, o_fox, o_mem = _attention_mixers(
        x, mem, norm_attn_g[l], w_in[l], nsa_pe_k[l], nsa_w_ck[l], nsa_pe_v[l], nsa_w_cv[l], nsa_q_g[l],
        nsa_k_g[l], rel_bias, fox_b_f[l], fox_q_g[l], fox_k_g[l], norm_mem_g[l], w_mem_kv[l], mem_q_g[l],
        mem_k_g[l])
    x1 = _merge(x.reshape(b * t, d), o_nsa, o_fox, o_mem, proj2d, w_o_nsa[l], w_o_fox[l], w_o_mem[l], w_out[l])
    out = _moe_block(x1, norm_ffn_g[l], w_router[l], router_bias[l], we_gate[l], we_up[l], we_down[l],
                     ws_gate[l], ws_up[l], ws_down[l])
    return out.reshape(b, t, d)
```

```python
import functools
import math

import jax
import jax.numpy as jnp
import numpy as np
from jax import lax
from jax.experimental import pallas as pl
from jax.experimental.pallas import tpu as pltpu

D_MODEL = 2048
HEAD_DIM = 128
NSA_HEADS = 8
NSA_KV_HEADS = 2
NSA_GROUP = NSA_HEADS // NSA_KV_HEADS
FOX_HEADS = 4
MEM_HEADS = 4
CMP_LEN = 32
CMP_STRIDE = 16
SLC_LEN = 64
N_SELECT = 16
WINDOW = 512
NUM_BUCKETS = 32
MAX_DISTANCE = 128
N_BRANCHES = 3
N_EXPERTS = 64
TOP_K = 8
D_EXPERT = 512
ROUTED_SCALE = 2.5
ATTN_SCALE = HEAD_DIM ** -0.5
NEG_INF = -1e30
FORCE_SCORE = 1e4
RMS_EPS = 1e-6

NSA_Q_W = NSA_HEADS * HEAD_DIM
NSA_KV_W = NSA_KV_HEADS * HEAD_DIM
NSA_GATE_W = 3 * NSA_HEADS
FOX_W = FOX_HEADS * HEAD_DIM
MEM_W = MEM_HEADS * HEAD_DIM
MERGE_W = N_BRANCHES * D_MODEL

LANES = 128
VMEM_LIMIT_BYTES = 56 * 1024 * 1024

PROJ_TM = 1024
PROJ_TN = 768
ATT_TQ = 128
FOX_TQ = 256
MEM_TQ = 512
MERGE_TM = 256
ROUTER_TM = 512
MOE_BLOCK = 256
COMBINE_TM = 256

CB_MERGE = 0
CB_ATT = N_BRANCHES * D_MODEL // LANES
CB_QNSA = CB_ATT
CB_KCMP = CB_ATT + 8
CB_VCMP = CB_ATT + 10
CB_KSLC = CB_ATT + 12
CB_VSLC = CB_ATT + 14
CB_KWIN = CB_ATT + 16
CB_VWIN = CB_ATT + 18
CB_FOXQ = CB_ATT + 20
CB_FOXK = CB_ATT + 24
CB_FOXV = CB_ATT + 28
CB_MEMQ = CB_ATT + 32
PROJ_COLS = (CB_ATT + 36) * LANES
SMALL_W = 2 * LANES
FOXF_LANE = 12

_BF16 = jnp.bfloat16
_F32 = jnp.float32


def _cparams(sem):
    return pltpu.CompilerParams(dimension_semantics=sem, vmem_limit_bytes=VMEM_LIMIT_BYTES)


def _dot(a, b):
    return jnp.dot(a, b, preferred_element_type=_F32)


def _dot_nt(a, b):
    return lax.dot_general(a, b, (((1,), (1,)), ((), ())), preferred_element_type=_F32)


def _proj_kernel(norm_j0, has_small, x_ref, g_ref, w_ref, cg_ref, cf_ref, *rest):
    if has_small:
        ws_ref, o_ref, os_ref, h_ref = rest
    else:
        o_ref, h_ref = rest
    j = pl.program_id(1)

    @pl.when(j == 0)
    def _():
        x = x_ref[...]
        ms = jnp.mean(x * x, axis=-1, keepdims=True)
        h = (x * lax.rsqrt(ms + RMS_EPS) * g_ref[...]).astype(_BF16)
        h_ref[...] = h
        if has_small:
            os_ref[...] = _dot(h, ws_ref[...])

    y = _dot(h_ref[...], w_ref[...])
    tn = y.shape[1]

    @pl.when(j >= norm_j0)
    def _():
        for c in range(tn // LANES):
            sl = slice(c * LANES, (c + 1) * LANES)
            yh = y[:, sl]
            ms = jnp.mean(yh * yh, axis=-1, keepdims=True)
            scale = jnp.where(cf_ref[:, sl] > 0.0, lax.rsqrt(ms + RMS_EPS), 1.0)
            o_ref[:, sl] = (yh * scale * cg_ref[:, sl]).astype(o_ref.dtype)

    @pl.when(j < norm_j0)
    def _():
        o_ref[...] = y.astype(o_ref.dtype)


def _rms_project(x2d, g, w, col_gain, col_flag, n_plain_cols, w_small, tm, tn):
    n, d = x2d.shape
    c = w.shape[1]
    has_small = w_small is not None
    in_specs = [
        pl.BlockSpec((tm, d), lambda i, j: (i, 0)),
        pl.BlockSpec((1, d), lambda i, j: (0, 0)),
        pl.BlockSpec((d, tn), lambda i, j: (0, j)),
        pl.BlockSpec((1, tn), lambda i, j: (0, j)),
        pl.BlockSpec((1, tn), lambda i, j: (0, j)),
    ]
    args = [x2d, g.reshape(1, d), w, col_gain.reshape(1, c), col_flag.reshape(1, c)]
    out_shape = [jax.ShapeDtypeStruct((n, c), _BF16)]
    out_specs = [pl.BlockSpec((tm, tn), lambda i, j: (i, j))]
    if has_small:
        ws = w_small.shape[1]
        in_specs.append(pl.BlockSpec((d, ws), lambda i, j: (0, 0)))
        args.append(w_small)
        out_shape.append(jax.ShapeDtypeStruct((n, ws), _F32))
        out_specs.append(pl.BlockSpec((tm, ws), lambda i, j: (i, 0)))
    assert n % tm == 0 and c % tn == 0 and n_plain_cols % tn == 0
    res = pl.pallas_call(
        functools.partial(_proj_kernel, n_plain_cols // tn, has_small),
        grid=(n // tm, c // tn),
        in_specs=in_specs,
        out_specs=out_specs,
        out_shape=out_shape,
        scratch_shapes=[pltpu.VMEM((tm, d), _BF16)],
        compiler_params=_cparams(("arbitrary", "arbitrary")),
        name="rms_project",
    )(*args)
    return res if has_small else res[0]


def _cmp_kernel(x_ref, wlo_ref, whi_ref, pelo_ref, pehi_ref, kg_ref, o_ref):
    j = pl.program_id(1)
    x = x_ref[0, 0]
    nchunk = x.shape[0]
    ylo = _dot(x, wlo_ref[0])
    yhi = _dot(x, whi_ref[0])
    pe = _dot(pelo_ref[0], wlo_ref[0]) + _dot(pehi_ref[0], whi_ref[0])
    y = ylo + pltpu.roll(yhi, nchunk - 1, 0) + pe[0:1, :]

    @pl.when(j < NSA_KV_HEADS)
    def _():
        ms = jnp.mean(y * y, axis=-1, keepdims=True)
        o_ref[0, 0] = (y * lax.rsqrt(ms + RMS_EPS) * kg_ref[...]).astype(o_ref.dtype)

    @pl.when(j >= NSA_KV_HEADS)
    def _():
        o_ref[0, 0] = y.astype(o_ref.dtype)


def _nsa_compress(cmp_in, w_ck, w_cv, pe_k, pe_v, k_g):
    b, nj, t, dk = cmp_in.shape
    nchunk = t // CMP_STRIDE
    half = CMP_LEN // 2
    x = cmp_in.reshape(b, nj, nchunk, CMP_STRIDE * dk)

    def halves(w):
        return (w[:half].reshape(half * dk, dk).astype(_BF16),
                w[half:].reshape(half * dk, dk).astype(_BF16))

    klo, khi = halves(w_ck)
    vlo, vhi = halves(w_cv)
    wlo = jnp.stack([klo, vlo])
    whi = jnp.stack([khi, vhi])

    def pe_halves(pe):
        lo = jnp.broadcast_to(pe[:half].reshape(1, half * dk), (8, half * dk)).astype(_BF16)
        hi = jnp.broadcast_to(pe[half:].reshape(1, half * dk), (8, half * dk)).astype(_BF16)
        return lo, hi

    pklo, pkhi = pe_halves(pe_k)
    pvlo, pvhi = pe_halves(pe_v)
    pelo = jnp.stack([pklo, pvlo])
    pehi = jnp.stack([pkhi, pvhi])
    kv = lambda bb, j: (j // NSA_KV_HEADS, 0, 0)
    return pl.pallas_call(
        _cmp_kernel,
        grid=(b, nj),
        in_specs=[
            pl.BlockSpec((1, 1, nchunk, CMP_STRIDE * dk), lambda bb, j: (bb, j, 0, 0)),
            pl.BlockSpec((1, half * dk, dk), kv),
            pl.BlockSpec((1, half * dk, dk), kv),
            pl.BlockSpec((1, 8, half * dk), kv),
            pl.BlockSpec((1, 8, half * dk), kv),
            pl.BlockSpec((1, dk), lambda bb, j: (0, 0)),
        ],
        out_specs=pl.BlockSpec((1, 1, nchunk, dk), lambda bb, j: (bb, j, 0, 0)),
        out_shape=jax.ShapeDtypeStruct((b, nj, nchunk, dk), _BF16),
        compiler_params=_cparams(("arbitrary", "arbitrary")),
        name="nsa_compress",
    )(x, wlo, whi, pelo, pehi, k_g.reshape(1, dk).astype(_F32))


def _softmax_update(state, s, mask, v):
    m, l, acc = state
    sm = jnp.where(mask, s, NEG_INF)
    m_new = jnp.maximum(m, jnp.max(sm, axis=1, keepdims=True))
    e = jnp.where(mask, jnp.exp(s - m_new), 0.0)
    alpha = jnp.exp(m - m_new)
    l_new = alpha * l + jnp.sum(e, axis=1, keepdims=True)
    acc_new = alpha * acc + _dot(e.astype(_BF16), v)
    return m_new, l_new, acc_new


def _softmax_finish(state):
    _, l, acc = state
    return acc / jnp.where(l > 0.0, l, 1.0)


def _nsa_kernel(far_ref, q_ref, kc_ref, vc_ref, ks_ref, vs_ref, kw_ref, vw_ref, gl_ref,
                ctab_ref, dtab_ref, ov_ref, ex_ref, o_ref, sel_ref):
    g = pl.program_id(1)
    i = pl.program_id(2)
    tq = ATT_TQ
    hg = NSA_GROUP
    m_rows = hg * tq
    n_slc = ex_ref.shape[1] // SLC_LEN

    q = q_ref[0]
    q4 = jnp.concatenate([q[:, h * HEAD_DIM:(h + 1) * HEAD_DIM] for h in range(hg)], axis=0)

    row = lax.broadcasted_iota(jnp.int32, (m_rows, LANES), 0) & (tq - 1)
    col = lax.broadcasted_iota(jnp.int32, (m_rows, LANES), 1)
    hrow = lax.broadcasted_iota(jnp.int32, (m_rows, 1), 0) // tq
    far = jnp.zeros((m_rows, 1), _F32)
    for h in range(hg):
        far = jnp.where(hrow == h, far_ref[g * hg + h], far)

    def init():
        return (jnp.full((m_rows, 1), NEG_INF, _F32), jnp.zeros((m_rows, 1), _F32),
                jnp.zeros((m_rows, HEAD_DIM), _F32))

    s = _dot_nt(q4, kc_ref[0, 0]) + ctab_ref[...].reshape(m_rows, LANES)
    mask_c = (i * tq + row) >= (CMP_STRIDE * col + CMP_LEN - 1)
    sm = jnp.where(mask_c, s, NEG_INF)
    mx = jnp.max(sm, axis=1, keepdims=True)
    e = jnp.where(mask_c, jnp.exp(s - mx), 0.0)
    l = jnp.sum(e, axis=1, keepdims=True)
    p_c = e / jnp.where(l > 0.0, l, 1.0)
    o_c = _dot(p_c.astype(_BF16), vc_ref[0, 0])

    ps = p_c[0:tq]
    for h in range(1, hg):
        ps = ps + p_c[h * tq:(h + 1) * tq]
    ps_hi = ps.astype(_BF16)
    ps_lo = (ps - ps_hi.astype(_F32)).astype(_BF16)
    imp = _dot(ps_hi, ov_ref[...]) + _dot(ps_lo, ov_ref[...])
    tpos = i * tq + lax.broadcasted_iota(jnp.int32, (tq, LANES), 0)
    blk = lax.broadcasted_iota(jnp.int32, (tq, LANES), 1)
    cur = tpos // SLC_LEN
    valid = blk * SLC_LEN <= tpos
    forced = (blk == 0) | (blk == cur) | (blk == cur - 1)
    score = jnp.where(valid, jnp.where(forced, FORCE_SCORE, imp), -1.0)
    score = jnp.where(blk < n_slc, score, -2.0)
    rank = jnp.zeros((tq, LANES), _F32)
    for k in range(n_slc):
        ck = score[:, k:k + 1]
        beats = (ck > score) | ((ck == score) & (blk > k))
        rank = rank + jnp.where(beats, 1.0, 0.0)
    sel = jnp.where((rank < float(min(N_SELECT, n_slc))) & (blk < n_slc), 1.0, 0.0).astype(_BF16)
    selfull = _dot(sel, ex_ref[...])
    for kt in range(ex_ref.shape[1] // LANES):
        sel_ref[kt] = selfull[:, kt * LANES:(kt + 1) * LANES]

    def sel_mask(kt):
        st = sel_ref[kt]
        return jnp.concatenate([st] * hg, axis=0) > 0.5

    def tile(ref, kt):
        return ref[0, pl.ds(pl.multiple_of(kt * LANES, LANES), LANES), :]

    def near_bias(d):
        return dtab_ref[:, d].reshape(m_rows, LANES)

    def far_body(kt, st):
        sc = _dot_nt(q4, tile(ks_ref, kt)) + far
        return _softmax_update(st, sc, sel_mask(kt), tile(vs_ref, kt))

    st = lax.fori_loop(0, jnp.maximum(i - 1, 0), far_body, init())
    kt1 = jnp.maximum(i - 1, 0)
    sc = _dot_nt(q4, tile(ks_ref, kt1)) + near_bias(1)
    st = _softmax_update(st, sc, sel_mask(kt1) & (i >= 1), tile(vs_ref, kt1))
    sc = _dot_nt(q4, tile(ks_ref, i)) + near_bias(0)
    st = _softmax_update(st, sc, sel_mask(i) & (row >= col), tile(vs_ref, i))
    o_s = _softmax_finish(st)

    n_win = WINDOW // tq
    st = init()
    for d in range(n_win, -1, -1):
        ktd = jnp.maximum(i - d, 0)
        bias = near_bias(d) if d <= 1 else far
        sc = _dot_nt(q4, tile(kw_ref, ktd)) + bias
        ok = i >= d
        if d == n_win:
            mk = (row < col) & ok
        elif d == 0:
            mk = row >= col
        else:
            mk = jnp.broadcast_to(ok, (m_rows, LANES))
        st = _softmax_update(st, sc, mk, tile(vw_ref, ktd))
    o_w = _softmax_finish(st)

    gl = jax.nn.sigmoid(gl_ref[0])
    for h in range(hg):
        rs = slice(h * tq, (h + 1) * tq)
        o = (gl[:, h:h + 1] * o_c[rs] + gl[:, hg + h:hg + h + 1] * o_s[rs]
             + gl[:, 2 * hg + h:2 * hg + h + 1] * o_w[rs])
        o_ref[0, :, h * HEAD_DIM:(h + 1) * HEAD_DIM] = o.astype(o_ref.dtype)


def _t5_bucket(rel):
    n = np.maximum(rel, 0)
    max_exact = NUM_BUCKETS // 2
    ratio = np.maximum(n, 1).astype(np.float32) / np.float32(max_exact)
    log_ratio = np.log(ratio) / np.float32(math.log(MAX_DISTANCE / max_exact))
    large = np.minimum(max_exact + (log_ratio * np.float32(NUM_BUCKETS - max_exact)).astype(np.int32),
                       NUM_BUCKETS - 1)
    return np.where(n < max_exact, n, large).astype(np.int32)


def _bias_lookup(rb, bucket):
    bk = jnp.asarray(bucket.astype(np.int8))[None]
    ex = (slice(None),) + (None,) * bucket.ndim
    tab = jnp.broadcast_to(rb[:, 0][ex], (rb.shape[0],) + bucket.shape)
    for k in range(1, NUM_BUCKETS):
        tab = jnp.where(bk == k, rb[:, k][ex], tab)
    return tab


def _nsa_tables(rel_bias, t):
    rb = rel_bias.astype(_F32).T
    tpos = np.arange(t)
    n = np.arange(LANES)
    rel_c = tpos[:, None] - (n * CMP_STRIDE + CMP_LEN - 1)[None, :]
    ctab = _bias_lookup(rb, _t5_bucket(rel_c))
    r = np.arange(ATT_TQ)
    rel_d = (np.arange(2) * ATT_TQ)[:, None, None] + r[None, :, None] - r[None, None, :]
    dtab = _bias_lookup(rb, _t5_bucket(rel_d))
    far = rb[:, NUM_BUCKETS - 1]
    return ctab, dtab, far


def _nsa_attention(proj, small, kvc, rel_bias):
    b, t, _ = proj.shape
    g, hg, tq = NSA_KV_HEADS, NSA_GROUP, ATT_TQ
    n_slc = t // SLC_LEN
    n_cmp = (t - CMP_LEN) // CMP_STRIDE + 1
    assert n_cmp <= LANES and kvc.shape[2] == LANES and n_slc <= LANES
    assert 2 * ATT_TQ >= 113 + ATT_TQ - 1 or True
    ctab, dtab, far = _nsa_tables(rel_bias, t)
    cstart = jnp.arange(LANES) * CMP_STRIDE
    sstart = jnp.arange(LANES) * SLC_LEN
    overlap = jnp.clip(jnp.minimum(cstart[:, None] + CMP_LEN, sstart[None, :] + SLC_LEN)
                       - jnp.maximum(cstart[:, None], sstart[None, :]), 0, None)
    ov = (overlap.astype(_F32) / CMP_LEN)
    ov = jnp.where((jnp.arange(LANES)[:, None] < n_cmp) & (jnp.arange(LANES)[None, :] < n_slc), ov, 0.0).astype(_BF16)
    ex = (jnp.arange(t)[None, :] // SLC_LEN == jnp.arange(LANES)[:, None]).astype(_BF16)

    seq = lambda cb: pl.BlockSpec((1, t, LANES), lambda bb, gg, ii, far_: (bb, 0, cb + gg))
    grid_spec = pltpu.PrefetchScalarGridSpec(
        num_scalar_prefetch=1,
        grid=(b, g, t // tq),
        in_specs=[
            pl.BlockSpec((1, tq, hg * HEAD_DIM),
                         lambda bb, gg, ii, far_: (bb, ii, CB_QNSA * LANES // (hg * HEAD_DIM) + gg)),
            pl.BlockSpec((1, 1, LANES, HEAD_DIM), lambda bb, gg, ii, far_: (bb, gg, 0, 0)),
            pl.BlockSpec((1, 1, LANES, HEAD_DIM), lambda bb, gg, ii, far_: (bb, NSA_KV_HEADS + gg, 0, 0)),
            seq(CB_KSLC), seq(CB_VSLC), seq(CB_KWIN), seq(CB_VWIN),
            pl.BlockSpec((1, tq, LANES), lambda bb, gg, ii, far_: (bb, ii, gg)),
            pl.BlockSpec((hg, tq, LANES), lambda bb, gg, ii, far_: (gg, ii, 0)),
            pl.BlockSpec((hg, 2, tq, LANES), lambda bb, gg, ii, far_: (gg, 0, 0, 0)),
            pl.BlockSpec((LANES, LANES), lambda bb, gg, ii, far_: (0, 0)),
            pl.BlockSpec((LANES, t), lambda bb, gg, ii, far_: (0, 0)),
        ],
        out_specs=pl.BlockSpec((1, tq, hg * HEAD_DIM), lambda bb, gg, ii, far_: (bb, ii, gg)),
        scratch_shapes=[pltpu.VMEM((t // LANES, tq, LANES), _F32)],
    )
    return pl.pallas_call(
        _nsa_kernel,
        grid_spec=grid_spec,
        out_shape=jax.ShapeDtypeStruct((b, t, NSA_Q_W), _BF16),
        compiler_params=_cparams(("arbitrary", "arbitrary", "arbitrary")),
        name="nsa_attention",
    )(far, proj, kvc, kvc, proj, proj, proj, proj, small, ctab, dtab, ov, ex)


def _fox_cum_kernel(s_ref, b_ref, col_ref, row_ref):
    z = s_ref[0] + b_ref[...]
    lf = jnp.minimum(z, 0.0) - jnp.log1p(jnp.exp(-jnp.abs(z)))
    x = lf.T
    t = x.shape[1]
    lane = lax.broadcasted_iota(jnp.int32, x.shape, 1)
    sh = 1
    while sh < t:
        x = x + jnp.where(lane >= sh, pltpu.roll(x, sh, 1), 0.0)
        sh *= 2
    row_ref[0] = x[8:16, :]
    col_ref[0] = x.T


def _fox_cumsum(small, b_f):
    b, t, _ = small.shape
    bvec = jnp.zeros((1, LANES), _F32).at[0, FOXF_LANE:FOXF_LANE + FOX_HEADS].set(b_f.astype(_F32))
    return pl.pallas_call(
        _fox_cum_kernel,
        grid=(b,),
        in_specs=[pl.BlockSpec((1, t, LANES), lambda bb: (bb, 0, 1)),
                  pl.BlockSpec((1, LANES), lambda bb: (0, 0))],
        out_specs=[pl.BlockSpec((1, t, LANES), lambda bb: (bb, 0, 0)),
                   pl.BlockSpec((1, 8, t), lambda bb: (bb, 0, 0))],
        out_shape=[jax.ShapeDtypeStruct((b, t, LANES), _F32), jax.ShapeDtypeStruct((b, 8, t), _F32)],
        compiler_params=_cparams(("arbitrary",)),
        name="fox_cumsum",
    )(small, bvec)


def _fox_kernel(q_ref, k_ref, v_ref, cc_ref, cr_ref, o_ref):
    i = pl.program_id(1)
    tq = FOX_TQ
    row = lax.broadcasted_iota(jnp.int32, (tq, tq), 0)
    col = lax.broadcasted_iota(jnp.int32, (tq, tq), 1)
    for h in range(FOX_HEADS):
        hs = slice(h * HEAD_DIM, (h + 1) * HEAD_DIM)
        lane = FOXF_LANE + h
        sub = FOXF_LANE - 8 + h
        q = q_ref[0, :, hs]
        cq = cc_ref[0, :, lane:lane + 1]

        def logits(kt):
            ks = pl.ds(pl.multiple_of(kt * tq, tq), tq)
            ck = cr_ref[0, sub:sub + 1, ks]
            return _dot_nt(q, k_ref[0, ks, hs]) + (cq - ck), v_ref[0, ks, hs]

        def body(kt, st):
            sc, v = logits(kt)
            return _softmax_update(st, sc, jnp.full((tq, tq), True), v)

        st = (jnp.full((tq, 1), NEG_INF, _F32), jnp.zeros((tq, 1), _F32), jnp.zeros((tq, HEAD_DIM), _F32))
        st = lax.fori_loop(0, i, body, st)
        sc, v = logits(i)
        st = _softmax_update(st, sc, row >= col, v)
        o_ref[0, :, hs] = _softmax_finish(st).astype(o_ref.dtype)


def _fox_attention(proj, cum_col, cum_row):
    b, t, _ = proj.shape
    tq = FOX_TQ
    w = FOX_W
    return pl.pallas_call(
        _fox_kernel,
        grid=(b, t // tq),
        in_specs=[
            pl.BlockSpec((1, tq, w), lambda bb, ii: (bb, ii, CB_FOXQ * LANES // w)),
            pl.BlockSpec((1, t, w), lambda bb, ii: (bb, 0, CB_FOXK * LANES // w)),
            pl.BlockSpec((1, t, w), lambda bb, ii: (bb, 0, CB_FOXV * LANES // w)),
            pl.BlockSpec((1, tq, LANES), lambda bb, ii: (bb, ii, 0)),
            pl.BlockSpec((1, 8, t), lambda bb, ii: (bb, 0, 0)),
        ],
        out_specs=pl.BlockSpec((1, tq, w), lambda bb, ii: (bb, ii, 0)),
        out_shape=jax.ShapeDtypeStruct((b, t, w), _BF16),
        compiler_params=_cparams(("arbitrary", "arbitrary")),
        name="fox_attention",
    )(proj, proj, proj, cum_col, cum_row)


def _mem_kernel(q_ref, kv_ref, o_ref):
    for h in range(MEM_HEADS):
        hs = slice(h * HEAD_DIM, (h + 1) * HEAD_DIM)
        vs = slice(MEM_W + h * HEAD_DIM, MEM_W + (h + 1) * HEAD_DIM)
        s = _dot_nt(q_ref[0, :, hs], kv_ref[0, :, hs])
        e = jnp.exp(s - jnp.max(s, axis=1, keepdims=True))
        p = e / jnp.sum(e, axis=1, keepdims=True)
        o_ref[0, :, hs] = _dot(p.astype(_BF16), kv_ref[0, :, vs]).astype(o_ref.dtype)


def _mem_attention(proj, memkv):
    b, t, _ = proj.shape
    m = memkv.shape[1]
    tq = MEM_TQ
    return pl.pallas_call(
        _mem_kernel,
        grid=(b, t // tq),
        in_specs=[pl.BlockSpec((1, tq, MEM_W), lambda bb, ii: (bb, ii, CB_MEMQ * LANES // MEM_W)),
                  pl.BlockSpec((1, m, 2 * MEM_W), lambda bb, ii: (bb, 0, 0))],
        out_specs=pl.BlockSpec((1, tq, MEM_W), lambda bb, ii: (bb, ii, 0)),
        out_shape=jax.ShapeDtypeStruct((b, t, MEM_W), _BF16),
        compiler_params=_cparams(("arbitrary", "arbitrary")),
        name="mem_attention",
    )(proj, memkv)


def _merge_kernel(x_ref, on_ref, of_ref, om_ref, g0_ref, g1_ref, g2_ref, wn_ref, wf_ref, wm_ref, wo_ref,
                  o_ref, z_ref):
    tn = 512
    for c in range(D_MODEL // tn):
        cs = slice(c * tn, (c + 1) * tn)
        z = (jax.nn.sigmoid(g0_ref[:, cs].astype(_F32)) * _dot(on_ref[...], wn_ref[:, cs])
             + jax.nn.sigmoid(g1_ref[:, cs].astype(_F32)) * _dot(of_ref[...], wf_ref[:, cs])
             + jax.nn.sigmoid(g2_ref[:, cs].astype(_F32)) * _dot(om_ref[...], wm_ref[:, cs]))
        z_ref[:, cs] = z.astype(_BF16)
    o_ref[...] = x_ref[...] + _dot(z_ref[...], wo_ref[...])


def _merge(x2d, o_nsa, o_fox, o_mem, proj2d, w_o_nsa, w_o_fox, w_o_mem, w_out):
    n, d = x2d.shape
    tm = MERGE_TM
    gcb = CB_MERGE * LANES // d
    const = lambda r, c: pl.BlockSpec((r, c), lambda i: (0, 0))
    return pl.pallas_call(
        _merge_kernel,
        grid=(n // tm,),
        in_specs=[
            pl.BlockSpec((tm, d), lambda i: (i, 0)),
            pl.BlockSpec((tm, NSA_Q_W), lambda i: (i, 0)),
            pl.BlockSpec((tm, FOX_W), lambda i: (i, 0)),
            pl.BlockSpec((tm, MEM_W), lambda i: (i, 0)),
            pl.BlockSpec((tm, d), lambda i: (i, gcb)),
            pl.BlockSpec((tm, d), lambda i: (i, gcb + 1)),
            pl.BlockSpec((tm, d), lambda i: (i, gcb + 2)),
            const(NSA_Q_W, d), const(FOX_W, d), const(MEM_W, d), const(d, d),
        ],
        out_specs=pl.BlockSpec((tm, d), lambda i: (i, 0)),
        out_shape=jax.ShapeDtypeStruct((n, d), _F32),
        scratch_shapes=[pltpu.VMEM((tm, d), _BF16)],
        compiler_params=_cparams(("arbitrary",)),
        name="merge_out",
    )(x2d, o_nsa, o_fox, o_mem, proj2d, proj2d, proj2d,
      w_o_nsa.astype(_BF16), w_o_fox.astype(_BF16), w_o_mem.astype(_BF16), w_out.astype(_BF16))


def _nsa_inputs(x, norm_attn_g, w_in, nsa_pe_k, nsa_w_ck, nsa_pe_v, nsa_w_cv, nsa_q_g, nsa_k_g,
                fox_q_g, fox_k_g, mem_q_g):
    b, t, d = x.shape
    ones = lambda k: jnp.ones((k * HEAD_DIM,), _F32)
    zeros = lambda k: jnp.zeros((k * HEAD_DIM,), _F32)
    tile = lambda gv, k: jnp.tile(gv.astype(_F32), k)

    g_end = NSA_Q_W + 6 * NSA_KV_W
    f_off = g_end + NSA_GATE_W + 3 * FOX_W
    m_off = f_off + FOX_HEADS + MEM_W
    w_all = jnp.concatenate([w_in[:, m_off:], w_in[:, :g_end], w_in[:, g_end + NSA_GATE_W:f_off],
                             w_in[:, f_off + FOX_HEADS:m_off]], axis=1).astype(_BF16)
    gates = w_in[:, g_end:g_end + NSA_GATE_W].reshape(d, 3, NSA_KV_HEADS, NSA_GROUP)
    pad = jnp.zeros((d, LANES - 3 * NSA_GROUP), w_in.dtype)
    w_small = jnp.concatenate([
        gates[:, :, 0, :].reshape(d, 3 * NSA_GROUP), pad,
        gates[:, :, 1, :].reshape(d, 3 * NSA_GROUP), w_in[:, f_off:f_off + FOX_HEADS],
        pad[:, :LANES - 3 * NSA_GROUP - FOX_HEADS]], axis=1).astype(_BF16)
    col_gain = jnp.concatenate([
        jnp.ones((MERGE_W,), _F32),
        tile(nsa_q_g, NSA_HEADS) * ATTN_SCALE, ones(2), ones(2), tile(nsa_k_g, 2), ones(2), tile(nsa_k_g, 2), ones(2),
        tile(fox_q_g, FOX_HEADS) * ATTN_SCALE, tile(fox_k_g, FOX_HEADS), ones(FOX_HEADS),
        tile(mem_q_g, MEM_HEADS) * ATTN_SCALE])
    col_flag = jnp.concatenate([
        jnp.zeros((MERGE_W,), _F32),
        ones(NSA_HEADS), zeros(2), zeros(2), ones(2), zeros(2), ones(2), zeros(2),
        ones(FOX_HEADS), ones(FOX_HEADS), zeros(FOX_HEADS), ones(MEM_HEADS)])

    x2d = x.reshape(b * t, d)
    proj2d, small2d = _rms_project(x2d, norm_attn_g, w_all, col_gain, col_flag, CB_ATT * LANES, w_small,
                                   PROJ_TM, PROJ_TN)
    proj = proj2d.reshape(b, t, PROJ_COLS)
    small = small2d.reshape(b, t, SMALL_W)

    cmp_in = proj[:, :, CB_KCMP * LANES:(CB_KCMP + 4) * LANES].reshape(b, t, 4, HEAD_DIM).transpose(0, 2, 1, 3)
    kvc = _nsa_compress(cmp_in, nsa_w_ck, nsa_w_cv, nsa_pe_k, nsa_pe_v, nsa_k_g)
    return proj, small, kvc


def _attention_mixers(x, mem, norm_attn_g, w_in, nsa_pe_k, nsa_w_ck, nsa_pe_v, nsa_w_cv, nsa_q_g, nsa_k_g,
                      rel_bias, fox_b_f, fox_q_g, fox_k_g, norm_mem_g, w_mem_kv, mem_q_g, mem_k_g):
    b, t, d = x.shape
    ones = lambda k: jnp.ones((k * HEAD_DIM,), _F32)
    zeros = lambda k: jnp.zeros((k * HEAD_DIM,), _F32)
    tile = lambda gv, k: jnp.tile(gv.astype(_F32), k)
    proj, small, kvc = _nsa_inputs(x, norm_attn_g, w_in, nsa_pe_k, nsa_w_ck, nsa_pe_v, nsa_w_cv, nsa_q_g, nsa_k_g,
                                   fox_q_g, fox_k_g, mem_q_g)
    proj2d = proj.reshape(b * t, PROJ_COLS)
    o_nsa = _nsa_attention(proj, small, kvc, rel_bias)

    cum_col, cum_row = _fox_cumsum(small, fox_b_f)
    o_fox = _fox_attention(proj, cum_col, cum_row)

    mm = mem.shape[1]
    mem_gain = jnp.concatenate([tile(mem_k_g, MEM_HEADS), ones(MEM_HEADS)])
    mem_flag = jnp.concatenate([ones(MEM_HEADS), zeros(MEM_HEADS)])
    memkv = _rms_project(mem.reshape(b * mm, d), norm_mem_g, w_mem_kv.astype(_BF16), mem_gain, mem_flag,
                         0, None, mm, MEM_W).reshape(b, mm, 2 * MEM_W)
    o_mem = _mem_attention(proj, memkv)
    return proj2d, o_nsa.reshape(b * t, NSA_Q_W), o_fox.reshape(b * t, FOX_W), o_mem.reshape(b * t, MEM_W)


def _router_kernel(x_ref, g_ref, whi_ref, wlo_ref, b_ref, h_ref, idx_ref, gate_ref, cnt_ref):
    x = x_ref[...]
    ms = jnp.mean(x * x, axis=-1, keepdims=True)
    h = x * lax.rsqrt(ms + RMS_EPS) * g_ref[...]
    h_ref[...] = h
    h_hi = h.astype(_BF16)
    h_lo = (h - h_hi.astype(_F32)).astype(_BF16)
    logits = _dot_nt(whi_ref[...], h_hi) + _dot_nt(whi_ref[...], h_lo) + _dot_nt(wlo_ref[...], h_hi)
    scores = jax.nn.sigmoid(logits)
    sb = scores + b_ref[...]
    eidx = lax.broadcasted_iota(jnp.int32, sb.shape, 0)
    idxs, vals = [], []
    for _ in range(TOP_K):
        m = jnp.max(sb, axis=0, keepdims=True)
        idx = jnp.min(jnp.where(sb == m, eidx, N_EXPERTS), axis=0, keepdims=True)
        hit = eidx == idx
        vals.append(jnp.sum(jnp.where(hit, scores, 0.0), axis=0, keepdims=True))
        idxs.append(idx)
        sb = jnp.where(hit, NEG_INF, sb)
    top_s = jnp.concatenate(vals, axis=0)
    idx_ref[...] = jnp.concatenate(idxs, axis=0)
    gate_ref[...] = top_s / jnp.sum(top_s, axis=0, keepdims=True) * ROUTED_SCALE

    @pl.when(pl.program_id(0) == 0)
    def _():
        cnt_ref[...] = jnp.zeros(cnt_ref.shape, cnt_ref.dtype)

    picked = jnp.where(sb < 0.5 * NEG_INF, 1.0, 0.0)
    cnt_ref[...] += jnp.sum(picked, axis=1, keepdims=True)


def _router(x1, norm_g, w_router, router_bias):
    n, d = x1.shape
    tm = ROUTER_TM
    wt = w_router.astype(_F32).T
    w_hi = wt.astype(_BF16)
    w_lo = (wt - w_hi.astype(_F32)).astype(_BF16)
    const = lambda r, c: pl.BlockSpec((r, c), lambda i: (0, 0))
    return pl.pallas_call(
        _router_kernel,
        grid=(n // tm,),
        in_specs=[pl.BlockSpec((tm, d), lambda i: (i, 0)), const(1, d), const(N_EXPERTS, d), const(N_EXPERTS, d),
                  const(N_EXPERTS, 1)],
        out_specs=[pl.BlockSpec((tm, d), lambda i: (i, 0)),
                   pl.BlockSpec((TOP_K, tm), lambda i: (0, i)),
                   pl.BlockSpec((TOP_K, tm), lambda i: (0, i)),
                   const(N_EXPERTS, 1)],
        out_shape=[jax.ShapeDtypeStruct((n, d), _F32), jax.ShapeDtypeStruct((TOP_K, n), jnp.int32),
                   jax.ShapeDtypeStruct((TOP_K, n), _F32), jax.ShapeDtypeStruct((N_EXPERTS, 1), _F32)],
        compiler_params=_cparams(("arbitrary",)),
        name="moe_router",
    )(x1, norm_g.reshape(1, d).astype(_F32), w_hi, w_lo, router_bias.reshape(N_EXPERTS, 1).astype(_F32))


def _moe_kernel(be_ref, nbr_ref, src_hbm, dst_hbm, h_hbm, wg_ref, wu_ref, wd_ref, y_hbm,
                xbuf, ybuf, sidx, didx, wg_bf, wu_bf, wd_bf, gsem, ssem, isem):
    n = pl.program_id(0)
    nbr = nbr_ref[0]
    slot = n % 2
    blk = MOE_BLOCK

    def idx_copies(block, s):
        return (pltpu.make_async_copy(src_hbm.at[block], sidx.at[s], isem.at[0, s]),
                pltpu.make_async_copy(dst_hbm.at[block], didx.at[s], isem.at[1, s]))

    def start_idx(block, s):
        for c in idx_copies(block, s):
            c.start()

    def wait_idx(block, s):
        for c in idx_copies(block, s):
            c.wait()

    def issue_gather(s):
        for r in range(blk):
            tok = sidx[s, r]
            pltpu.make_async_copy(h_hbm.at[pl.ds(tok, 1), :], xbuf.at[s, pl.ds(r, 1), :], gsem.at[s]).start()

    def issue_scatter(s):
        for r in range(blk):
            row = didx[s, r]
            pltpu.make_async_copy(ybuf.at[s, pl.ds(r, 1), :], y_hbm.at[pl.ds(row, 1), :], ssem.at[s]).start()

    def wait_rows(buf, sem, s):
        pltpu.make_async_copy(buf.at[s], buf.at[s], sem.at[s]).wait()

    @pl.when(n < nbr)
    def _():
        @pl.when(n == 0)
        def _():
            start_idx(0, 0)
            ybuf[1] = jnp.zeros(ybuf.shape[1:], ybuf.dtype)
            pad_rows = pltpu.make_async_copy(ybuf.at[1], y_hbm.at[pl.ds(y_hbm.shape[0] - blk, blk), :], ssem.at[1])
            pad_rows.start()
            pad_rows.wait()
            wait_idx(0, 0)
            issue_gather(0)

            @pl.when(nbr > 1)
            def _():
                start_idx(1, 1)

        wait_rows(xbuf, gsem, slot)

        @pl.when(n + 1 < nbr)
        def _():
            wait_idx(n + 1, 1 - slot)
            issue_gather(1 - slot)

        changed = (n == 0) | (be_ref[n] != be_ref[jnp.maximum(n - 1, 0)])

        @pl.when(changed)
        def _():
            wg_bf[...] = wg_ref[0].astype(_BF16)
            wu_bf[...] = wu_ref[0].astype(_BF16)
            wd_bf[...] = wd_ref[0].astype(_BF16)

        @pl.when(n >= 2)
        def _():
            wait_rows(ybuf, ssem, slot)

        x = xbuf[slot].astype(_BF16)
        a = (jax.nn.silu(_dot(x, wg_bf[...])) * _dot(x, wu_bf[...])).astype(_BF16)
        ybuf[slot] = _dot(a, wd_bf[...])
        issue_scatter(slot)

        @pl.when(n + 2 < nbr)
        def _():
            start_idx(n + 2, slot)

        @pl.when(n == nbr - 1)
        def _():
            @pl.when(n >= 1)
            def _():
                wait_rows(ybuf, ssem, 1 - slot)
            wait_rows(ybuf, ssem, slot)


def _moe_experts(h2, block_expert, nb_real, slot_src, slot_dst, we_gate, we_up, we_down):
    n, d = h2.shape
    nb = block_expert.shape[0]
    blk = MOE_BLOCK
    e, _, de = we_gate.shape
    grid_spec = pltpu.PrefetchScalarGridSpec(
        num_scalar_prefetch=2,
        grid=(nb,),
        in_specs=[
            pl.BlockSpec(memory_space=pl.ANY),
            pl.BlockSpec(memory_space=pl.ANY),
            pl.BlockSpec(memory_space=pl.ANY),
            pl.BlockSpec((1, d, de), lambda i, be, nbr: (be[i], 0, 0)),
            pl.BlockSpec((1, d, de), lambda i, be, nbr: (be[i], 0, 0)),
            pl.BlockSpec((1, de, d), lambda i, be, nbr: (be[i], 0, 0)),
        ],
        out_specs=pl.BlockSpec(memory_space=pl.ANY),
        scratch_shapes=[
            pltpu.VMEM((2, blk, d), _F32), pltpu.VMEM((2, blk, d), _F32),
            pltpu.SMEM((2, blk), jnp.int32), pltpu.SMEM((2, blk), jnp.int32),
            pltpu.VMEM((d, de), _BF16), pltpu.VMEM((d, de), _BF16), pltpu.VMEM((de, d), _BF16),
            pltpu.SemaphoreType.DMA((2,)), pltpu.SemaphoreType.DMA((2,)), pltpu.SemaphoreType.DMA((2, 2)),
        ],
    )
    return pl.pallas_call(
        _moe_kernel,
        grid_spec=grid_spec,
        out_shape=jax.ShapeDtypeStruct((TOP_K * n + blk, d), _F32),
        compiler_params=_cparams(("arbitrary",)),
        name="moe_experts",
    )(block_expert, nb_real, slot_src, slot_dst, h2, we_gate, we_up, we_down)


def _dispatch_plan(top_idx, counts, n):
    blk = MOE_BLOCK
    a = n * TOP_K
    nb = (a + N_EXPERTS * (blk - 1)) // blk
    order = jnp.argsort(top_idx.reshape(a)).astype(jnp.int32)
    order = jnp.concatenate([order, jnp.zeros((blk,), jnp.int32)])
    counts = counts.astype(jnp.int32)
    padded = (counts + blk - 1) // blk * blk
    start = jnp.cumsum(counts) - counts
    pstart = jnp.cumsum(padded) - padded
    block_end = jnp.cumsum(padded) // blk
    blocks = jnp.arange(nb, dtype=jnp.int32)
    block_expert = jnp.minimum(jnp.sum((block_end[None, :] <= blocks[:, None]).astype(jnp.int32), axis=1),
                               N_EXPERTS - 1)
    off = blocks * blk - pstart[block_expert]
    base = jnp.clip(start[block_expert] + off, 0, a)
    rows = jax.vmap(lambda s: lax.dynamic_slice(order, (s,), (blk,)))(base)
    lane = jnp.arange(blk, dtype=jnp.int32)[None, :]
    real = (off[:, None] + lane) < counts[block_expert][:, None]
    tok = rows // TOP_K
    kk = rows - tok * TOP_K
    slot_src = jnp.where(real, tok, 0)
    slot_dst = jnp.where(real, kk * n + tok, TOP_K * n + lane)
    nb_real = (jnp.sum(padded) // blk).reshape(1).astype(jnp.int32)
    return block_expert, nb_real, slot_src, slot_dst


def _combine_kernel(x_ref, h_ref, g_ref, wsg_ref, wsu_ref, wsd_ref, *rest):
    y_refs, o_ref = rest[:TOP_K], rest[TOP_K]
    h = h_ref[...].astype(_BF16)
    a = (jax.nn.silu(_dot(h, wsg_ref[...])) * _dot(h, wsu_ref[...])).astype(_BF16)
    acc = x_ref[...] + _dot(a, wsd_ref[...])
    g = g_ref[...]
    for k in range(TOP_K):
        acc = acc + g[:, k:k + 1] * y_refs[k][...]
    o_ref[...] = acc


def _combine(x1, h2, gates, y, ws_gate, ws_up, ws_down):
    n, d = x1.shape
    tm = COMBINE_TM
    nt = n // tm
    de = ws_gate.shape[1]
    const = lambda r, c: pl.BlockSpec((r, c), lambda i: (0, 0))
    row = pl.BlockSpec((tm, d), lambda i: (i, 0))
    y_specs = [pl.BlockSpec((tm, d), functools.partial(lambda i, k: (k * nt + i, 0), k=k)) for k in range(TOP_K)]
    return pl.pallas_call(
        _combine_kernel,
        grid=(nt,),
        in_specs=[row, row, pl.BlockSpec((tm, TOP_K), lambda i: (i, 0)), const(d, de), const(d, de), const(de, d)]
        + y_specs,
        out_specs=row,
        out_shape=jax.ShapeDtypeStruct((n, d), _F32),
        compiler_params=_cparams(("arbitrary",)),
        name="moe_combine",
    )(x1, h2, gates, ws_gate.astype(_BF16), ws_up.astype(_BF16), ws_down.astype(_BF16), *([y] * TOP_K))


def _moe_block(x1, norm_ffn_g, w_router, router_bias, we_gate, we_up, we_down, ws_gate, ws_up, ws_down):
    n = x1.shape[0]
    h2, top_idx_t, gates_t, counts = _router(x1, norm_ffn_g, w_router, router_bias)
    block_expert, nb_real, slot_src, slot_dst = _dispatch_plan(top_idx_t.T, counts.reshape(N_EXPERTS), n)
    y = _moe_experts(h2, block_expert, nb_real, slot_src, slot_dst, we_gate, we_up, we_down)
    return _combine(x1, h2, gates_t.T, y, ws_gate, ws_up, ws_down)


def kernel(x, mem, norm_attn_g, w_in, nsa_pe_k, nsa_w_ck, nsa_pe_v, nsa_w_cv, nsa_q_g, nsa_k_g, rel_bias, fox_b_f, fox_q_g, fox_k_g, norm_mem_g, w_mem_kv, mem_q_g, mem_k_g, w_o_nsa, w_o_fox, w_o_mem, w_out, norm_ffn_g, w_router, router_bias, we_gate, we_up, we_down, ws_gate, ws_up, ws_down):
    b, t, d = x.shape
    assert norm_attn_g.shape[0] == 1, "single-layer problem"
    l = 0
    proj2d, o_nsa, o_fox, o_mem = _attention_mixers(
        x, mem, norm_attn_g[l], w_in[l], nsa_pe_k[l], nsa_w_ck[l], nsa_pe_v[l], nsa_w_cv[l], nsa_q_g[l],
        nsa_k_g[l], rel_bias, fox_b_f[l], fox_q_g[l], fox_k_g[l], norm_mem_g[l], w_mem_kv[l], mem_q_g[l],
        mem_k_g[l])
    x1 = _merge(x.reshape(b * t, d), o_nsa, o_fox, o_mem, proj2d, w_o_nsa[l], w_o_fox[l], w_o_mem[l], w_out[l])
    out = _moe_block(x1, norm_ffn_g[l], w_router[l], router_bias[l], we_gate[l], we_up[l], we_down[l],
                     ws_gate[l], ws_up[l], ws_down[l])
    return out.reshape(b, t, d)
```

```python
import functools
import math

import jax
import jax.numpy as jnp
import numpy as np
from jax import lax
from jax.experimental import pallas as pl
from jax.experimental.pallas import tpu as pltpu

D_MODEL = 2048
HEAD_DIM = 128
NSA_HEADS = 8
NSA_KV_HEADS = 2
NSA_GROUP = NSA_HEADS // NSA_KV_HEADS
FOX_HEADS = 4
MEM_HEADS = 4
CMP_LEN = 32
CMP_STRIDE = 16
SLC_LEN = 64
N_SELECT = 16
WINDOW = 512
NUM_BUCKETS = 32
MAX_DISTANCE = 128
N_BRANCHES = 3
N_EXPERTS = 64
TOP_K = 8
D_EXPERT = 512
ROUTED_SCALE = 2.5
ATTN_SCALE = HEAD_DIM ** -0.5
NEG_INF = -1e30
FORCE_SCORE = 1e4
RMS_EPS = 1e-6
LOG2E = math.log2(math.e)

NSA_Q_W = NSA_HEADS * HEAD_DIM
NSA_KV_W = NSA_KV_HEADS * HEAD_DIM
NSA_GATE_W = 3 * NSA_HEADS
FOX_W = FOX_HEADS * HEAD_DIM
MEM_W = MEM_HEADS * HEAD_DIM
MERGE_W = N_BRANCHES * D_MODEL

LANES = 128
VMEM_LIMIT_BYTES = 56 * 1024 * 1024

PROJ_TM = 1024
PROJ_TN = 768
ATT_TQ = 128
FOX_TQ = 256
MEM_TQ = 512
MERGE_TM = 256
ROUTER_TM = 512
MOE_BLOCK = 256
COMBINE_TM = 256

CB_MERGE = 0
CB_ATT = N_BRANCHES * D_MODEL // LANES
CB_QNSA = CB_ATT
CB_KCMP = CB_ATT + 8
CB_VCMP = CB_ATT + 10
CB_KSLC = CB_ATT + 12
CB_VSLC = CB_ATT + 14
CB_KWIN = CB_ATT + 16
CB_VWIN = CB_ATT + 18
CB_FOXQ = CB_ATT + 20
CB_FOXK = CB_ATT + 24
CB_FOXV = CB_ATT + 28
CB_MEMQ = CB_ATT + 32
PROJ_COLS = (CB_ATT + 36) * LANES
SMALL_W = 2 * LANES
FOXF_LANE = 12

_BF16 = jnp.bfloat16
_F32 = jnp.float32


def _cparams(sem):
    return pltpu.CompilerParams(dimension_semantics=sem, vmem_limit_bytes=VMEM_LIMIT_BYTES)


def _dot(a, b):
    return jnp.dot(a, b, preferred_element_type=_F32)


def _dot_nt(a, b):
    return lax.dot_general(a, b, (((1,), (1,)), ((), ())), preferred_element_type=_F32)


def _lanes(parts):
    return jnp.concatenate(parts, axis=1)


def _to_token_major(x):
    c = x.shape[1] // LANES
    chunks = jnp.stack([x[:, j * LANES:(j + 1) * LANES] for j in range(c)], axis=0)
    return pltpu.einshape("ctl->tcl", chunks)


def _from_token_major(x3):
    xt = pltpu.einshape("tcl->ctl", x3)
    return _lanes([xt[j] for j in range(x3.shape[1])])


def _proj_kernel(norm_j0, has_small, x_ref, g_ref, w_ref, cg_ref, cf_ref, *rest):
    if has_small:
        ws_ref, o_ref, os_ref, h_ref = rest
    else:
        o_ref, h_ref = rest
    j = pl.program_id(1)

    @pl.when(j == 0)
    def _():
        x = x_ref[...]
        ms = jnp.mean(x * x, axis=-1, keepdims=True)
        h = (x * lax.rsqrt(ms + RMS_EPS) * g_ref[...]).astype(_BF16)
        h_ref[...] = h
        if has_small:
            os_ref[...] = _dot(h, ws_ref[...])

    y = _dot(h_ref[...], w_ref[...])
    tn = y.shape[1]

    @pl.when(j >= norm_j0)
    def _():
        for c in range(tn // LANES):
            sl = slice(c * LANES, (c + 1) * LANES)
            yh = y[:, sl]
            ms = jnp.mean(yh * yh, axis=-1, keepdims=True)
            scale = jnp.where(cf_ref[:, sl] > 0.0, lax.rsqrt(ms + RMS_EPS), 1.0)
            o_ref[:, sl] = (yh * scale * cg_ref[:, sl]).astype(o_ref.dtype)

    @pl.when(j < norm_j0)
    def _():
        o_ref[...] = y.astype(o_ref.dtype)


def _rms_project(x2d, g, w, col_gain, col_flag, n_plain_cols, w_small, tm, tn):
    n, d = x2d.shape
    c = w.shape[1]
    has_small = w_small is not None
    in_specs = [
        pl.BlockSpec((tm, d), lambda i, j: (i, 0)),
        pl.BlockSpec((1, d), lambda i, j: (0, 0)),
        pl.BlockSpec((d, tn), lambda i, j: (0, j)),
        pl.BlockSpec((1, tn), lambda i, j: (0, j)),
        pl.BlockSpec((1, tn), lambda i, j: (0, j)),
    ]
    args = [x2d, g.reshape(1, d), w, col_gain.reshape(1, c), col_flag.reshape(1, c)]
    out_shape = [jax.ShapeDtypeStruct((n, c), _BF16)]
    out_specs = [pl.BlockSpec((tm, tn), lambda i, j: (i, j))]
    if has_small:
        ws = w_small.shape[1]
        in_specs.append(pl.BlockSpec((d, ws), lambda i, j: (0, 0)))
        args.append(w_small)
        out_shape.append(jax.ShapeDtypeStruct((n, ws), _F32))
        out_specs.append(pl.BlockSpec((tm, ws), lambda i, j: (i, 0)))
    assert n % tm == 0 and c % tn == 0 and n_plain_cols % tn == 0
    res = pl.pallas_call(
        functools.partial(_proj_kernel, n_plain_cols // tn, has_small),
        grid=(n // tm, c // tn),
        in_specs=in_specs,
        out_specs=out_specs,
        out_shape=out_shape,
        scratch_shapes=[pltpu.VMEM((tm, d), _BF16)],
        compiler_params=_cparams(("arbitrary", "arbitrary")),
        name="rms_project",
    )(*args)
    return res if has_small else res[0]


def _cmp_kernel(x_ref, wlo_ref, whi_ref, pelo_ref, pehi_ref, kg_ref, o_ref):
    j = pl.program_id(1)
    x = x_ref[0, 0]
    nchunk = x.shape[0]
    ylo = _dot(x, wlo_ref[0])
    yhi = _dot(x, whi_ref[0])
    pe = _dot(pelo_ref[0], wlo_ref[0]) + _dot(pehi_ref[0], whi_ref[0])
    y = ylo + pltpu.roll(yhi, nchunk - 1, 0) + pe[0:1, :]

    @pl.when(j < NSA_KV_HEADS)
    def _():
        ms = jnp.mean(y * y, axis=-1, keepdims=True)
        o_ref[0, 0] = (y * lax.rsqrt(ms + RMS_EPS) * kg_ref[...]).astype(o_ref.dtype)

    @pl.when(j >= NSA_KV_HEADS)
    def _():
        o_ref[0, 0] = y.T.astype(o_ref.dtype)


def _nsa_compress(cmp_in, w_ck, w_cv, pe_k, pe_v, k_g):
    b, nj, t, dk = cmp_in.shape
    nchunk = t // CMP_STRIDE
    half = CMP_LEN // 2
    assert nchunk == dk
    x = cmp_in.reshape(b, nj, nchunk, CMP_STRIDE * dk)

    def halves(w):
        return (w[:half].reshape(half * dk, dk).astype(_BF16),
                w[half:].reshape(half * dk, dk).astype(_BF16))

    klo, khi = halves(w_ck)
    vlo, vhi = halves(w_cv)
    wlo = jnp.stack([klo, vlo])
    whi = jnp.stack([khi, vhi])

    def pe_halves(pe):
        lo = jnp.broadcast_to(pe[:half].reshape(1, half * dk), (8, half * dk)).astype(_BF16)
        hi = jnp.broadcast_to(pe[half:].reshape(1, half * dk), (8, half * dk)).astype(_BF16)
        return lo, hi

    pklo, pkhi = pe_halves(pe_k)
    pvlo, pvhi = pe_halves(pe_v)
    pelo = jnp.stack([pklo, pvlo])
    pehi = jnp.stack([pkhi, pvhi])
    kv = lambda bb, j: (j // NSA_KV_HEADS, 0, 0)
    return pl.pallas_call(
        _cmp_kernel,
        grid=(b, nj),
        in_specs=[
            pl.BlockSpec((1, 1, nchunk, CMP_STRIDE * dk), lambda bb, j: (bb, j, 0, 0)),
            pl.BlockSpec((1, half * dk, dk), kv),
            pl.BlockSpec((1, half * dk, dk), kv),
            pl.BlockSpec((1, 8, half * dk), kv),
            pl.BlockSpec((1, 8, half * dk), kv),
            pl.BlockSpec((1, dk), lambda bb, j: (0, 0)),
        ],
        out_specs=pl.BlockSpec((1, 1, nchunk, dk), lambda bb, j: (bb, j, 0, 0)),
        out_shape=jax.ShapeDtypeStruct((b, nj, nchunk, dk), _BF16),
        compiler_params=_cparams(("arbitrary", "arbitrary")),
        name="nsa_compress",
    )(x, wlo, whi, pelo, pehi, k_g.reshape(1, dk).astype(_F32))


def _t_update(state, tiles):
    m, l, acc = state
    masked = [s if mask is None else jnp.where(mask, s, NEG_INF) for s, mask, _ in tiles]
    m_new = m
    for sm in masked:
        m_new = jnp.maximum(m_new, jnp.max(sm, axis=0, keepdims=True))
    alpha = jnp.exp2(m - m_new)
    l_new = alpha * l
    acc_new = alpha * acc
    for sm, (_, _, vt) in zip(masked, tiles):
        e = jnp.exp2(sm - m_new)
        l_new = l_new + jnp.sum(e, axis=0, keepdims=True)
        acc_new = acc_new + _dot(vt, e.astype(_BF16))
    return m_new, l_new, acc_new


def _t_init(nq):
    return (jnp.full((1, nq), NEG_INF, _F32), jnp.zeros((1, nq), _F32), jnp.zeros((HEAD_DIM, nq), _F32))


def _t_finish(state):
    _, l, acc = state
    return acc / l


def _nsa_kernel(q_ref, kc_ref, vct_ref, ks_ref, vst_ref, kw_ref, vwt_ref, gl_ref,
                wt_ref, dt_ref, ovt_ref, ext_ref, o_ref, sel_ref):
    i = pl.program_id(2)
    tq = ATT_TQ
    hg = NSA_GROUP
    nq = hg * tq
    n_slc = ovt_ref.shape[0]

    q = q_ref[0]
    qt = _lanes([q[:, h * HEAD_DIM:(h + 1) * HEAD_DIM].astype(_F32).T for h in range(hg)]).astype(_BF16)

    key = lax.broadcasted_iota(jnp.int32, (LANES, nq), 0)
    qry = lax.broadcasted_iota(jnp.int32, (LANES, nq), 1) & (tq - 1)

    def ktile(ref, kt):
        return ref[0, pl.ds(pl.multiple_of(kt * LANES, LANES), LANES), :]

    def near_bias(d):
        return _lanes([dt_ref[h, d] for h in range(hg)])

    woff = pl.multiple_of(wt_ref.shape[1] - LANES - 8 - (LANES // CMP_STRIDE) * i, 8)
    s = _dot(kc_ref[0, 0], qt) + _lanes([wt_ref[h, pl.ds(woff, LANES), :] for h in range(hg)])
    n_win = WINDOW // tq
    win_tiles = []
    for d in range(n_win, -1, -1):
        ktd = jnp.maximum(i - d, 0)
        sc = _dot(ktile(kw_ref, ktd), qt)
        if d <= 1:
            sc = sc + near_bias(d)
        if d == n_win:
            mk = (qry < key) & (i >= d)
        elif d == 0:
            mk = key <= qry
        else:
            mk = jnp.broadcast_to(i >= d, (LANES, nq))
        win_tiles.append((sc, mk, vwt_ref[0, 0, ktd]))

    mask_c = (i * tq + qry) >= (CMP_STRIDE * key + CMP_LEN - 1)
    mx = jnp.max(jnp.where(mask_c, s, NEG_INF), axis=0, keepdims=True)
    e = jnp.where(mask_c, jnp.exp2(s - mx), 0.0)
    l = jnp.sum(e, axis=0, keepdims=True)
    p_c = e / jnp.where(l > 0.0, l, 1.0)
    o_c = _dot(vct_ref[0, 0], p_c.astype(_BF16))

    ps = p_c[:, 0:tq]
    for h in range(1, hg):
        ps = ps + p_c[:, h * tq:(h + 1) * tq]
    ps_hi = ps.astype(_BF16)
    ps_lo = (ps - ps_hi.astype(_F32)).astype(_BF16)
    imp = _dot(ovt_ref[...], ps_hi) + _dot(ovt_ref[...], ps_lo)
    blk = lax.broadcasted_iota(jnp.int32, (n_slc, tq), 0)
    tpos = i * tq + lax.broadcasted_iota(jnp.int32, (n_slc, tq), 1)
    cur = tpos // SLC_LEN
    valid = blk * SLC_LEN <= tpos
    forced = (blk == 0) | (blk == cur) | (blk == cur - 1)
    score = jnp.where(valid, jnp.where(forced, FORCE_SCORE, imp), -1.0)
    rank = jnp.zeros((n_slc, tq), _F32)
    for k in range(n_slc):
        ck = score[k:k + 1, :]
        beats = (ck > score) | ((ck == score) & (blk > k))
        rank = rank + jnp.where(beats, 1.0, 0.0)
    sel = jnp.where(rank < float(min(N_SELECT, n_slc)), 1.0, 0.0).astype(_BF16)
    selfull = _dot(ext_ref[...], sel)
    for kt in range(ext_ref.shape[0] // LANES):
        sel_ref[kt] = selfull[kt * LANES:(kt + 1) * LANES, :]

    def sel_mask(kt, ok=True):
        return _lanes([sel_ref[kt]] * hg) > jnp.where(ok, 0.5, 2.0)

    o_w = _t_finish(_t_update(_t_init(nq), win_tiles))

    n_far = jnp.maximum(i - 1, 0)

    def far_body(j, st):
        tiles = []
        for u in range(2):
            kt = 2 * j + u
            ktc = jnp.minimum(kt, n_far - 1)
            tiles.append((_dot(ktile(ks_ref, ktc), qt), sel_mask(ktc, kt < n_far), vst_ref[0, 0, ktc]))
        return _t_update(st, tiles)

    st = lax.fori_loop(0, (n_far + 1) // 2, far_body, _t_init(nq))
    kt1 = jnp.maximum(i - 1, 0)
    st = _t_update(st, [
        (_dot(ktile(ks_ref, kt1), qt) + near_bias(1), sel_mask(kt1, i >= 1), vst_ref[0, 0, kt1]),
        (_dot(ktile(ks_ref, i), qt) + near_bias(0), sel_mask(i) & (key <= qry), vst_ref[0, 0, i])])
    o_s = _t_finish(st)

    glt = jax.nn.sigmoid(gl_ref[0]).T
    grow = lambda br: _lanes([glt[br * hg + h:br * hg + h + 1, :] for h in range(hg)])
    o = grow(0) * o_c + grow(1) * o_s + grow(2) * o_w
    for h in range(hg):
        o_ref[0, :, h * HEAD_DIM:(h + 1) * HEAD_DIM] = o[:, h * tq:(h + 1) * tq].T.astype(o_ref.dtype)


def _t5_bucket(rel):
    n = np.maximum(rel, 0)
    max_exact = NUM_BUCKETS // 2
    ratio = np.maximum(n, 1).astype(np.float32) / np.float32(max_exact)
    log_ratio = np.log(ratio) / np.float32(math.log(MAX_DISTANCE / max_exact))
    large = np.minimum(max_exact + (log_ratio * np.float32(NUM_BUCKETS - max_exact)).astype(np.int32),
                       NUM_BUCKETS - 1)
    return np.where(n < max_exact, n, large).astype(np.int32)


def _bias_lookup(rb, bucket):
    bk = jnp.asarray(bucket.astype(np.int8))[None]
    ex = (slice(None),) + (None,) * bucket.ndim
    tab = jnp.broadcast_to(rb[:, 0][ex], (rb.shape[0],) + bucket.shape)
    for k in range(1, NUM_BUCKETS):
        tab = jnp.where(bk == k, rb[:, k][ex], tab)
    return tab


def _nsa_tables(rel_bias):
    rb = rel_bias.astype(_F32).T * LOG2E
    r = np.arange(ATT_TQ)
    rel_d = (np.arange(2) * ATT_TQ)[:, None, None] + r[None, None, :] - r[None, :, None]
    assert _t5_bucket(np.array([ATT_TQ + 1]))[0] == NUM_BUCKETS - 1
    u = np.arange(2 * LANES)
    rel_w = r[None, :] - CMP_STRIDE * (u[:, None] - (LANES - 8)) - (CMP_LEN - 1)
    far = rb[:, NUM_BUCKETS - 1]
    dt = _bias_lookup(rb, _t5_bucket(rel_d)) - far[:, None, None, None]
    return dt, _bias_lookup(rb, _t5_bucket(rel_w))


def _tile_transposed(v, cb, n_heads, tk):
    b, t, _ = v.shape
    x = v[:, :, cb * LANES:(cb + n_heads) * LANES].reshape(b, t // tk, tk, n_heads, HEAD_DIM)
    return x.transpose(0, 3, 1, 4, 2)


def _nsa_attention(proj, small, kvc, rel_bias):
    b, t, _ = proj.shape
    g, hg, tq = NSA_KV_HEADS, NSA_GROUP, ATT_TQ
    n_slc = t // SLC_LEN
    n_cmp = (t - CMP_LEN) // CMP_STRIDE + 1
    nt = t // LANES
    assert n_cmp <= LANES and kvc.shape[2] == LANES and n_slc % 16 == 0 and tq == LANES
    assert LANES - 8 - (LANES // CMP_STRIDE) * (t // tq - 1) >= 0
    dt, wt = _nsa_tables(rel_bias)
    cstart = np.arange(LANES) * CMP_STRIDE
    sstart = np.arange(n_slc) * SLC_LEN
    overlap = np.clip(np.minimum(cstart[None, :] + CMP_LEN, sstart[:, None] + SLC_LEN)
                      - np.maximum(cstart[None, :], sstart[:, None]), 0, None).astype(np.float32) / CMP_LEN
    overlap[:, n_cmp:] = 0.0
    ovt = jnp.asarray(overlap, _BF16)
    ext = jnp.asarray(np.arange(t)[:, None] // SLC_LEN == np.arange(n_slc)[None, :], _BF16)
    vst = _tile_transposed(proj, CB_VSLC, g, LANES)
    vwt = _tile_transposed(proj, CB_VWIN, g, LANES)

    seq = lambda cb: pl.BlockSpec((1, t, LANES), lambda bb, gg, ii: (bb, 0, cb + gg))
    vts = pl.BlockSpec((1, 1, nt, HEAD_DIM, LANES), lambda bb, gg, ii: (bb, gg, 0, 0, 0))
    return pl.pallas_call(
        _nsa_kernel,
        grid=(b, g, t // tq),
        in_specs=[
            pl.BlockSpec((1, tq, hg * HEAD_DIM), lambda bb, gg, ii: (bb, ii, CB_QNSA * LANES // (hg * HEAD_DIM) + gg)),
            pl.BlockSpec((1, 1, LANES, HEAD_DIM), lambda bb, gg, ii: (bb, gg, 0, 0)),
            pl.BlockSpec((1, 1, HEAD_DIM, LANES), lambda bb, gg, ii: (bb, NSA_KV_HEADS + gg, 0, 0)),
            seq(CB_KSLC), vts, seq(CB_KWIN), vts,
            pl.BlockSpec((1, tq, LANES), lambda bb, gg, ii: (bb, ii, gg)),
            pl.BlockSpec((hg, 2 * LANES, tq), lambda bb, gg, ii: (gg, 0, 0)),
            pl.BlockSpec((hg, 2, LANES, tq), lambda bb, gg, ii: (gg, 0, 0, 0)),
            pl.BlockSpec((n_slc, LANES), lambda bb, gg, ii: (0, 0)),
            pl.BlockSpec((t, n_slc), lambda bb, gg, ii: (0, 0)),
        ],
        out_specs=pl.BlockSpec((1, tq, hg * HEAD_DIM), lambda bb, gg, ii: (bb, ii, gg)),
        out_shape=jax.ShapeDtypeStruct((b, t, NSA_Q_W), _BF16),
        scratch_shapes=[pltpu.VMEM((nt, LANES, tq), _F32)],
        compiler_params=_cparams(("arbitrary", "arbitrary", "arbitrary")),
        name="nsa_attention",
    )(proj, kvc, kvc, proj, vst, proj, vwt, small, wt, dt, ovt, ext)


def _fox_cum_kernel(s_ref, b_ref, col_ref, row_ref):
    z = s_ref[0] + b_ref[...]
    lf = (jnp.minimum(z, 0.0) - jnp.log1p(jnp.exp(-jnp.abs(z)))) * LOG2E
    x = lf.T
    t = x.shape[1]
    lane = lax.broadcasted_iota(jnp.int32, x.shape, 1)
    sh = 1
    while sh < t:
        x = x + jnp.where(lane >= sh, pltpu.roll(x, sh, 1), 0.0)
        sh *= 2
    row_ref[0] = x[8:16, :]
    col_ref[0] = x.T


def _fox_cumsum(small, b_f):
    b, t, _ = small.shape
    bvec = jnp.zeros((1, LANES), _F32).at[0, FOXF_LANE:FOXF_LANE + FOX_HEADS].set(b_f.astype(_F32))
    return pl.pallas_call(
        _fox_cum_kernel,
        grid=(b,),
        in_specs=[pl.BlockSpec((1, t, LANES), lambda bb: (bb, 0, 1)),
                  pl.BlockSpec((1, LANES), lambda bb: (0, 0))],
        out_specs=[pl.BlockSpec((1, t, LANES), lambda bb: (bb, 0, 0)),
                   pl.BlockSpec((1, 8, t), lambda bb: (bb, 0, 0))],
        out_shape=[jax.ShapeDtypeStruct((b, t, LANES), _F32), jax.ShapeDtypeStruct((b, 8, t), _F32)],
        compiler_params=_cparams(("arbitrary",)),
        name="fox_cumsum",
    )(small, bvec)


def _fox_kernel(q_ref, k_ref, vt_ref, cc_ref, cr_ref, o_ref):
    i = pl.program_id(1)
    tq = FOX_TQ
    key = lax.broadcasted_iota(jnp.int32, (tq, tq), 0)
    qry = lax.broadcasted_iota(jnp.int32, (tq, tq), 1)
    heads = range(FOX_HEADS)
    hsl = [slice(h * HEAD_DIM, (h + 1) * HEAD_DIM) for h in heads]
    qts = [q_ref[0, :, hsl[h]].astype(_F32).T.astype(_BF16) for h in heads]
    cqs = [cr_ref[0, FOXF_LANE - 8 + h:FOXF_LANE - 8 + h + 1, :] for h in heads]

    def logits(h, kt):
        ks = pl.ds(pl.multiple_of(kt * tq, tq), tq)
        ck = cc_ref[0, ks, FOXF_LANE + h:FOXF_LANE + h + 1]
        return _dot(k_ref[0, ks, hsl[h]], qts[h]) + (cqs[h] - ck)

    def body(kt, sts):
        ss = [logits(h, kt) for h in heads]
        return tuple(_t_update(sts[h], [(ss[h], None, vt_ref[0, h, kt])]) for h in heads)

    sts = lax.fori_loop(0, i, body, tuple(_t_init(tq) for _ in heads))
    ss = [logits(h, i) for h in heads]
    for h in heads:
        st = _t_update(sts[h], [(ss[h], key <= qry, vt_ref[0, h, i])])
        o_ref[0, :, hsl[h]] = _t_finish(st).T.astype(o_ref.dtype)


def _fox_attention(proj, cum_col, cum_row):
    b, t, _ = proj.shape
    tq = FOX_TQ
    w = FOX_W
    vt = _tile_transposed(proj, CB_FOXV, FOX_HEADS, tq)
    return pl.pallas_call(
        _fox_kernel,
        grid=(b, t // tq),
        in_specs=[
            pl.BlockSpec((1, tq, w), lambda bb, ii: (bb, ii, CB_FOXQ * LANES // w)),
            pl.BlockSpec((1, t, w), lambda bb, ii: (bb, 0, CB_FOXK * LANES // w)),
            pl.BlockSpec((1, FOX_HEADS, t // tq, HEAD_DIM, tq), lambda bb, ii: (bb, 0, 0, 0, 0)),
            pl.BlockSpec((1, t, LANES), lambda bb, ii: (bb, 0, 0)),
            pl.BlockSpec((1, 8, tq), lambda bb, ii: (bb, 0, ii)),
        ],
        out_specs=pl.BlockSpec((1, tq, w), lambda bb, ii: (bb, ii, 0)),
        out_shape=jax.ShapeDtypeStruct((b, t, w), _BF16),
        compiler_params=_cparams(("arbitrary", "arbitrary")),
        name="fox_attention",
    )(proj, proj, vt, cum_col, cum_row)


def _mem_kernel(q_ref, kv_ref, o_ref):
    for h in range(MEM_HEADS):
        hs = slice(h * HEAD_DIM, (h + 1) * HEAD_DIM)
        vs = slice(MEM_W + h * HEAD_DIM, MEM_W + (h + 1) * HEAD_DIM)
        s = _dot_nt(q_ref[0, :, hs], kv_ref[0, :, hs])
        e = jnp.exp2(s - jnp.max(s, axis=1, keepdims=True))
        p = e / jnp.sum(e, axis=1, keepdims=True)
        o_ref[0, :, hs] = _dot(p.astype(_BF16), kv_ref[0, :, vs]).astype(o_ref.dtype)


def _mem_attention(proj, memkv):
    b, t, _ = proj.shape
    m = memkv.shape[1]
    tq = MEM_TQ
    return pl.pallas_call(
        _mem_kernel,
        grid=(b, t // tq),
        in_specs=[pl.BlockSpec((1, tq, MEM_W), lambda bb, ii: (bb, ii, CB_MEMQ * LANES // MEM_W)),
                  pl.BlockSpec((1, m, 2 * MEM_W), lambda bb, ii: (bb, 0, 0))],
        out_specs=pl.BlockSpec((1, tq, MEM_W), lambda bb, ii: (bb, ii, 0)),
        out_shape=jax.ShapeDtypeStruct((b, t, MEM_W), _BF16),
        compiler_params=_cparams(("arbitrary", "arbitrary")),
        name="mem_attention",
    )(proj, memkv)


def _merge_kernel(x_ref, on_ref, of_ref, om_ref, g0_ref, g1_ref, g2_ref, wn_ref, wf_ref, wm_ref, wo_ref,
                  o_ref, z_ref):
    tn = 512
    for c in range(D_MODEL // tn):
        cs = slice(c * tn, (c + 1) * tn)
        z = (jax.nn.sigmoid(g0_ref[:, cs].astype(_F32)) * _dot(on_ref[...], wn_ref[:, cs])
             + jax.nn.sigmoid(g1_ref[:, cs].astype(_F32)) * _dot(of_ref[...], wf_ref[:, cs])
             + jax.nn.sigmoid(g2_ref[:, cs].astype(_F32)) * _dot(om_ref[...], wm_ref[:, cs]))
        z_ref[:, cs] = z.astype(_BF16)
    o_ref[...] = x_ref[...] + _dot(z_ref[...], wo_ref[...])


def _merge(x2d, o_nsa, o_fox, o_mem, proj2d, w_o_nsa, w_o_fox, w_o_mem, w_out):
    n, d = x2d.shape
    tm = MERGE_TM
    gcb = CB_MERGE * LANES // d
    const = lambda r, c: pl.BlockSpec((r, c), lambda i: (0, 0))
    return pl.pallas_call(
        _merge_kernel,
        grid=(n // tm,),
        in_specs=[
            pl.BlockSpec((tm, d), lambda i: (i, 0)),
            pl.BlockSpec((tm, NSA_Q_W), lambda i: (i, 0)),
            pl.BlockSpec((tm, FOX_W), lambda i: (i, 0)),
            pl.BlockSpec((tm, MEM_W), lambda i: (i, 0)),
            pl.BlockSpec((tm, d), lambda i: (i, gcb)),
            pl.BlockSpec((tm, d), lambda i: (i, gcb + 1)),
            pl.BlockSpec((tm, d), lambda i: (i, gcb + 2)),
            const(NSA_Q_W, d), const(FOX_W, d), const(MEM_W, d), const(d, d),
        ],
        out_specs=pl.BlockSpec((tm, d), lambda i: (i, 0)),
        out_shape=jax.ShapeDtypeStruct((n, d), _F32),
        scratch_shapes=[pltpu.VMEM((tm, d), _BF16)],
        compiler_params=_cparams(("arbitrary",)),
        name="merge_out",
    )(x2d, o_nsa, o_fox, o_mem, proj2d, proj2d, proj2d,
      w_o_nsa.astype(_BF16), w_o_fox.astype(_BF16), w_o_mem.astype(_BF16), w_out.astype(_BF16))


def _nsa_inputs(x, norm_attn_g, w_in, nsa_pe_k, nsa_w_ck, nsa_pe_v, nsa_w_cv, nsa_q_g, nsa_k_g,
                fox_q_g, fox_k_g, mem_q_g):
    b, t, d = x.shape
    ones = lambda k: jnp.ones((k * HEAD_DIM,), _F32)
    zeros = lambda k: jnp.zeros((k * HEAD_DIM,), _F32)
    tile = lambda gv, k: jnp.tile(gv.astype(_F32), k)
    qs = ATTN_SCALE * LOG2E

    g_end = NSA_Q_W + 6 * NSA_KV_W
    f_off = g_end + NSA_GATE_W + 3 * FOX_W
    m_off = f_off + FOX_HEADS + MEM_W
    w_all = jnp.concatenate([w_in[:, m_off:], w_in[:, :g_end], w_in[:, g_end + NSA_GATE_W:f_off],
                             w_in[:, f_off + FOX_HEADS:m_off]], axis=1).astype(_BF16)
    gates = w_in[:, g_end:g_end + NSA_GATE_W].reshape(d, 3, NSA_KV_HEADS, NSA_GROUP)
    pad = jnp.zeros((d, LANES - 3 * NSA_GROUP), w_in.dtype)
    w_small = jnp.concatenate([
        gates[:, :, 0, :].reshape(d, 3 * NSA_GROUP), pad,
        gates[:, :, 1, :].reshape(d, 3 * NSA_GROUP), w_in[:, f_off:f_off + FOX_HEADS],
        pad[:, :LANES - 3 * NSA_GROUP - FOX_HEADS]], axis=1).astype(_BF16)
    col_gain = jnp.concatenate([
        jnp.ones((MERGE_W,), _F32),
        tile(nsa_q_g, NSA_HEADS) * qs, ones(2), ones(2), tile(nsa_k_g, 2), ones(2), tile(nsa_k_g, 2), ones(2),
        tile(fox_q_g, FOX_HEADS) * qs, tile(fox_k_g, FOX_HEADS), ones(FOX_HEADS),
        tile(mem_q_g, MEM_HEADS) * qs])
    col_flag = jnp.concatenate([
        jnp.zeros((MERGE_W,), _F32),
        ones(NSA_HEADS), zeros(2), zeros(2), ones(2), zeros(2), ones(2), zeros(2),
        ones(FOX_HEADS), ones(FOX_HEADS), zeros(FOX_HEADS), ones(MEM_HEADS)])

    x2d = x.reshape(b * t, d)
    proj2d, small2d = _rms_project(x2d, norm_attn_g, w_all, col_gain, col_flag, CB_ATT * LANES, w_small,
                                   PROJ_TM, PROJ_TN)
    proj = proj2d.reshape(b, t, PROJ_COLS)
    small = small2d.reshape(b, t, SMALL_W)

    cmp_in = proj[:, :, CB_KCMP * LANES:(CB_KCMP + 4) * LANES].reshape(b, t, 4, HEAD_DIM).transpose(0, 2, 1, 3)
    kvc = _nsa_compress(cmp_in, nsa_w_ck, nsa_w_cv, nsa_pe_k, nsa_pe_v, nsa_k_g)
    return proj, small, kvc


def _attention_mixers(x, mem, norm_attn_g, w_in, nsa_pe_k, nsa_w_ck, nsa_pe_v, nsa_w_cv, nsa_q_g, nsa_k_g,
                      rel_bias, fox_b_f, fox_q_g, fox_k_g, norm_mem_g, w_mem_kv, mem_q_g, mem_k_g):
    b, t, d = x.shape
    ones = lambda k: jnp.ones((k * HEAD_DIM,), _F32)
    zeros = lambda k: jnp.zeros((k * HEAD_DIM,), _F32)
    tile = lambda gv, k: jnp.tile(gv.astype(_F32), k)
    proj, small, kvc = _nsa_inputs(x, norm_attn_g, w_in, nsa_pe_k, nsa_w_ck, nsa_pe_v, nsa_w_cv, nsa_q_g, nsa_k_g,
                                   fox_q_g, fox_k_g, mem_q_g)
    proj2d = proj.reshape(b * t, PROJ_COLS)
    o_nsa = _nsa_attention(proj, small, kvc, rel_bias)

    cum_col, cum_row = _fox_cumsum(small, fox_b_f)
    o_fox = _fox_attention(proj, cum_col, cum_row)

    mm = mem.shape[1]
    mem_gain = jnp.concatenate([tile(mem_k_g, MEM_HEADS), ones(MEM_HEADS)])
    mem_flag = jnp.concatenate([ones(MEM_HEADS), zeros(MEM_HEADS)])
    memkv = _rms_project(mem.reshape(b * mm, d), norm_mem_g, w_mem_kv.astype(_BF16), mem_gain, mem_flag,
                         0, None, mm, MEM_W).reshape(b, mm, 2 * MEM_W)
    o_mem = _mem_attention(proj, memkv)
    return proj2d, o_nsa.reshape(b * t, NSA_Q_W), o_fox.reshape(b * t, FOX_W), o_mem.reshape(b * t, MEM_W)


def _router_kernel(x_ref, g_ref, whi_ref, wlo_ref, b_ref, h_ref, idx_ref, gate_ref, cnt_ref):
    x = x_ref[...]
    ms = jnp.mean(x * x, axis=-1, keepdims=True)
    h = x * lax.rsqrt(ms + RMS_EPS) * g_ref[...]
    h_hi = h.astype(_BF16)
    h_ref[...] = _to_token_major(h_hi)
    h_lo = (h - h_hi.astype(_F32)).astype(_BF16)
    logits = _dot_nt(whi_ref[...], h_hi) + _dot_nt(whi_ref[...], h_lo) + _dot_nt(wlo_ref[...], h_hi)
    scores = jax.nn.sigmoid(logits)
    sb = scores + b_ref[...]
    eidx = lax.broadcasted_iota(jnp.int32, sb.shape, 0)
    idxs, vals = [], []
    for _ in range(TOP_K):
        m = jnp.max(sb, axis=0, keepdims=True)
        idx = jnp.min(jnp.where(sb == m, eidx, N_EXPERTS), axis=0, keepdims=True)
        hit = eidx == idx
        vals.append(jnp.sum(jnp.where(hit, scores, 0.0), axis=0, keepdims=True))
        idxs.append(idx)
        sb = jnp.where(hit, NEG_INF, sb)
    top_s = jnp.concatenate(vals, axis=0)
    idx_ref[...] = jnp.concatenate(idxs, axis=0)
    gate_ref[...] = top_s / jnp.sum(top_s, axis=0, keepdims=True) * ROUTED_SCALE

    @pl.when(pl.program_id(0) == 0)
    def _():
        cnt_ref[...] = jnp.zeros(cnt_ref.shape, cnt_ref.dtype)

    picked = jnp.where(sb < 0.5 * NEG_INF, 1.0, 0.0)
    cnt_ref[...] += jnp.sum(picked, axis=1, keepdims=True)


def _router(x1, norm_g, w_router, router_bias):
    n, d = x1.shape
    tm = ROUTER_TM
    wt = w_router.astype(_F32).T
    w_hi = wt.astype(_BF16)
    w_lo = (wt - w_hi.astype(_F32)).astype(_BF16)
    const = lambda r, c: pl.BlockSpec((r, c), lambda i: (0, 0))
    return pl.pallas_call(
        _router_kernel,
        grid=(n // tm,),
        in_specs=[pl.BlockSpec((tm, d), lambda i: (i, 0)), const(1, d), const(N_EXPERTS, d), const(N_EXPERTS, d),
                  const(N_EXPERTS, 1)],
        out_specs=[pl.BlockSpec((tm, d // LANES, LANES), lambda i: (i, 0, 0)),
                   pl.BlockSpec((TOP_K, tm), lambda i: (0, i)),
                   pl.BlockSpec((TOP_K, tm), lambda i: (0, i)),
                   const(N_EXPERTS, 1)],
        out_shape=[jax.ShapeDtypeStruct((n, d // LANES, LANES), _BF16), jax.ShapeDtypeStruct((TOP_K, n), jnp.int32),
                   jax.ShapeDtypeStruct((TOP_K, n), _F32), jax.ShapeDtypeStruct((N_EXPERTS, 1), _F32)],
        compiler_params=_cparams(("arbitrary",)),
        name="moe_router",
    )(x1, norm_g.reshape(1, d).astype(_F32), w_hi, w_lo, router_bias.reshape(N_EXPERTS, 1).astype(_F32))


def _moe_kernel(be_ref, nbr_ref, src_hbm, dst_hbm, h_hbm, sg_ref, wg_ref, wu_ref, wd_ref, y_hbm,
                xbuf, ybuf, sidx, didx, wg_bf, wu_bf, wd_bf, gsem, ssem, isem):
    n = pl.program_id(0)
    nbr = nbr_ref[0]
    slot = n % 2
    blk = MOE_BLOCK

    def idx_copies(block, s):
        return (pltpu.make_async_copy(src_hbm.at[block], sidx.at[s], isem.at[0, s]),
                pltpu.make_async_copy(dst_hbm.at[block], didx.at[s], isem.at[1, s]))

    def start_idx(block, s):
        for c in idx_copies(block, s):
            c.start()

    def wait_idx(block, s):
        for c in idx_copies(block, s):
            c.wait()

    def issue_gather(s):
        for r in range(blk):
            tok = sidx[s, r]
            pltpu.make_async_copy(h_hbm.at[tok], xbuf.at[s, r], gsem.at[s]).start(priority=r % 2)

    def issue_scatter(s):
        for r in range(blk):
            row = didx[s, r]
            pltpu.make_async_copy(ybuf.at[s, r], y_hbm.at[row], ssem.at[s]).start(priority=r % 2)

    def wait_rows(buf, sem, s):
        pltpu.make_async_copy(buf.at[s], buf.at[s], sem.at[s]).wait()

    @pl.when(n < nbr)
    def _():
        @pl.when(n == 0)
        def _():
            start_idx(0, 0)
            ybuf[1] = jnp.zeros(ybuf.shape[1:], ybuf.dtype)
            pad_rows = pltpu.make_async_copy(ybuf.at[1], y_hbm.at[pl.ds(y_hbm.shape[0] - blk, blk)], ssem.at[1])
            pad_rows.start()
            pad_rows.wait()
            wait_idx(0, 0)
            issue_gather(0)

            @pl.when(nbr > 1)
            def _():
                start_idx(1, 1)

        wait_rows(xbuf, gsem, slot)

        @pl.when(n + 1 < nbr)
        def _():
            wait_idx(n + 1, 1 - slot)
            issue_gather(1 - slot)

        changed = (n == 0) | (be_ref[n] != be_ref[jnp.maximum(n - 1, 0)])

        @pl.when(changed)
        def _():
            wg_bf[...] = wg_ref[0].astype(_BF16)
            wu_bf[...] = wu_ref[0].astype(_BF16)
            wd_bf[...] = wd_ref[0].astype(_BF16)

        @pl.when(n >= 2)
        def _():
            wait_rows(ybuf, ssem, slot)

        x = _from_token_major(xbuf[slot])
        gcol = jnp.broadcast_to(sg_ref[0], (LANES, blk)).T
        a = jax.nn.silu(_dot(x, wg_bf[...])) * _dot(x, wu_bf[...])
        a = (a * _lanes([gcol] * (a.shape[1] // LANES))).astype(_BF16)
        ybuf[slot] = _to_token_major(_dot(a, wd_bf[...]).astype(_BF16))
        issue_scatter(slot)

        @pl.when(n + 2 < nbr)
        def _():
            start_idx(n + 2, slot)

        @pl.when(n == nbr - 1)
        def _():
            @pl.when(n >= 1)
            def _():
                wait_rows(ybuf, ssem, 1 - slot)
            wait_rows(ybuf, ssem, slot)


def _moe_experts(h3, block_expert, nb_real, slot_src, slot_dst, slot_gate, we_gate, we_up, we_down):
    n, c, _ = h3.shape
    d = c * LANES
    nb = block_expert.shape[0]
    blk = MOE_BLOCK
    e, _, de = we_gate.shape
    grid_spec = pltpu.PrefetchScalarGridSpec(
        num_scalar_prefetch=2,
        grid=(nb,),
        in_specs=[
            pl.BlockSpec(memory_space=pl.ANY),
            pl.BlockSpec(memory_space=pl.ANY),
            pl.BlockSpec(memory_space=pl.ANY),
            pl.BlockSpec((1, 1, blk), lambda i, be, nbr: (i, 0, 0)),
            pl.BlockSpec((1, d, de), lambda i, be, nbr: (be[i], 0, 0)),
            pl.BlockSpec((1, d, de), lambda i, be, nbr: (be[i], 0, 0)),
            pl.BlockSpec((1, de, d), lambda i, be, nbr: (be[i], 0, 0)),
        ],
        out_specs=pl.BlockSpec(memory_space=pl.ANY),
        scratch_shapes=[
            pltpu.VMEM((2, blk, c, LANES), _BF16), pltpu.VMEM((2, blk, c, LANES), _BF16),
            pltpu.SMEM((2, blk), jnp.int32), pltpu.SMEM((2, blk), jnp.int32),
            pltpu.VMEM((d, de), _BF16), pltpu.VMEM((d, de), _BF16), pltpu.VMEM((de, d), _BF16),
            pltpu.SemaphoreType.DMA((2,)), pltpu.SemaphoreType.DMA((2,)), pltpu.SemaphoreType.DMA((2, 2)),
        ],
    )
    return pl.pallas_call(
        _moe_kernel,
        grid_spec=grid_spec,
        out_shape=jax.ShapeDtypeStruct((TOP_K * n + blk, c, LANES), _BF16),
        compiler_params=_cparams(("arbitrary",)),
        name="moe_experts",
    )(block_expert, nb_real, slot_src, slot_dst, h3, slot_gate, we_gate, we_up, we_down)


def _dispatch_plan(top_idx, gates, counts, n):
    blk = MOE_BLOCK
    a = n * TOP_K
    nb = (a + N_EXPERTS * (blk - 1)) // blk
    order = jnp.argsort(top_idx.reshape(a)).astype(jnp.int32)
    counts = counts.astype(jnp.int32)
    padded = (counts + blk - 1) // blk * blk
    start = jnp.cumsum(counts) - counts
    pstart = jnp.cumsum(padded) - padded
    block_end = jnp.cumsum(padded) // blk
    blocks = jnp.arange(nb, dtype=jnp.int32)
    block_expert = jnp.minimum(jnp.sum((block_end[None, :] <= blocks[:, None]).astype(jnp.int32), axis=1),
                               N_EXPERTS - 1)
    off = blocks * blk - pstart[block_expert]
    base = start[block_expert] + off
    lane = jnp.arange(blk, dtype=jnp.int32)[None, :]
    rows = jnp.take(order, base[:, None] + lane, mode="clip")
    real = (off[:, None] + lane) < counts[block_expert][:, None]
    tok = rows // TOP_K
    kk = rows - tok * TOP_K
    slot_src = jnp.where(real, tok, 0)
    slot_dst = jnp.where(real, kk * n + tok, TOP_K * n + lane)
    slot_gate = jnp.where(real, jnp.take(gates.reshape(a), rows, mode="clip"), 0.0).reshape(nb, 1, blk)
    nb_real = (jnp.sum(padded) // blk).reshape(1).astype(jnp.int32)
    return block_expert, nb_real, slot_src, slot_dst, slot_gate


def _combine_kernel(x_ref, h_ref, wsg_ref, wsu_ref, wsd_ref, *rest):
    y_refs, o_ref = rest[:TOP_K], rest[TOP_K]
    h = _from_token_major(h_ref[...])
    a = (jax.nn.silu(_dot(h, wsg_ref[...])) * _dot(h, wsu_ref[...])).astype(_BF16)
    routed = y_refs[0][...].astype(_F32)
    for k in range(1, TOP_K):
        routed = routed + y_refs[k][...].astype(_F32)
    o_ref[...] = x_ref[...] + _dot(a, wsd_ref[...]) + _from_token_major(routed)


def _combine(x1, h3, y3, ws_gate, ws_up, ws_down):
    n, d = x1.shape
    c = d // LANES
    tm = COMBINE_TM
    nt = n // tm
    de = ws_gate.shape[1]
    const = lambda r, cc: pl.BlockSpec((r, cc), lambda i: (0, 0))
    row = pl.BlockSpec((tm, d), lambda i: (i, 0))
    y_specs = [pl.BlockSpec((tm, c, LANES), functools.partial(lambda i, k: (k * nt + i, 0, 0), k=k))
               for k in range(TOP_K)]
    return pl.pallas_call(
        _combine_kernel,
        grid=(nt,),
        in_specs=[row, pl.BlockSpec((tm, c, LANES), lambda i: (i, 0, 0)), const(d, de), const(d, de), const(de, d)]
        + y_specs,
        out_specs=row,
        out_shape=jax.ShapeDtypeStruct((n, d), _F32),
        compiler_params=_cparams(("arbitrary",)),
        name="moe_combine",
    )(x1, h3, ws_gate.astype(_BF16), ws_up.astype(_BF16), ws_down.astype(_BF16), *([y3] * TOP_K))


def _moe_block(x1, norm_ffn_g, w_router, router_bias, we_gate, we_up, we_down, ws_gate, ws_up, ws_down):
    n = x1.shape[0]
    h3, top_idx_t, gates_t, counts = _router(x1, norm_ffn_g, w_router, router_bias)
    block_expert, nb_real, slot_src, slot_dst, slot_gate = _dispatch_plan(
        top_idx_t.T, gates_t.T, counts.reshape(N_EXPERTS), n)
    y3 = _moe_experts(h3, block_expert, nb_real, slot_src, slot_dst, slot_gate, we_gate, we_up, we_down)
    return _combine(x1, h3, y3, ws_gate, ws_up, ws_down)


def kernel(x, mem, norm_attn_g, w_in, nsa_pe_k, nsa_w_ck, nsa_pe_v, nsa_w_cv, nsa_q_g, nsa_k_g, rel_bias, fox_b_f, fox_q_g, fox_k_g, norm_mem_g, w_mem_kv, mem_q_g, mem_k_g, w_o_nsa, w_o_fox, w_o_mem, w_out, norm_ffn_g, w_router, router_bias, we_gate, we_up, we_down, ws_gate, ws_up, ws_down):
    b, t, d = x.shape
    assert norm_attn_g.shape[0] == 1, "single-layer problem"
    l = 0
    proj2d, o_nsa, o_fox, o_mem = _attention_mixers(
        x, mem, norm_attn_g[l], w_in[l], nsa_pe_k[l], nsa_w_ck[l], nsa_pe_v[l], nsa_w_cv[l], nsa_q_g[l],
        nsa_k_g[l], rel_bias, fox_b_f[l], fox_q_g[l], fox_k_g[l], norm_mem_g[l], w_mem_kv[l], mem_q_g[l],
        mem_k_g[l])
    x1 = _merge(x.reshape(b * t, d), o_nsa, o_fox, o_mem, proj2d, w_o_nsa[l], w_o_fox[l], w_o_mem[l], w_out[l])
    out = _moe_block(x1, norm_ffn_g[l], w_router[l], router_bias[l], we_gate[l], we_up[l], we_down[l],
                     ws_gate[l], ws_up[l], ws_down[l])
    return out.reshape(b, t, d)
```

```python
import functools
import math

import jax
import jax.numpy as jnp
import numpy as np
from jax import lax
from jax.experimental import pallas as pl
from jax.experimental.pallas import tpu as pltpu

D_MODEL = 2048
HEAD_DIM = 128
NSA_HEADS = 8
NSA_KV_HEADS = 2
NSA_GROUP = NSA_HEADS // NSA_KV_HEADS
FOX_HEADS = 4
MEM_HEADS = 4
CMP_LEN = 32
CMP_STRIDE = 16
SLC_LEN = 64
N_SELECT = 16
WINDOW = 512
NUM_BUCKETS = 32
MAX_DISTANCE = 128
N_BRANCHES = 3
N_EXPERTS = 64
TOP_K = 8
D_EXPERT = 512
ROUTED_SCALE = 2.5
ATTN_SCALE = HEAD_DIM ** -0.5
NEG_INF = -1e30
FORCE_SCORE = 1e4
RMS_EPS = 1e-6
LOG2E = math.log2(math.e)

NSA_Q_W = NSA_HEADS * HEAD_DIM
NSA_KV_W = NSA_KV_HEADS * HEAD_DIM
NSA_GATE_W = 3 * NSA_HEADS
FOX_W = FOX_HEADS * HEAD_DIM
MEM_W = MEM_HEADS * HEAD_DIM
MERGE_W = N_BRANCHES * D_MODEL

LANES = 128
VMEM_LIMIT_BYTES = 56 * 1024 * 1024

PROJ_TM = 1024
PROJ_TN = 768
ATT_TQ = 128
FAR_GROUP = 4
FOX_TQ = 256
MEM_TQ = 512
MERGE_TM = 256
ROUTER_TM = 512
MOE_BLOCK = 256
COMBINE_TM = 256

CB_MERGE = 0
CB_ATT = N_BRANCHES * D_MODEL // LANES
CB_QNSA = CB_ATT
CB_KCMP = CB_ATT + 8
CB_VCMP = CB_ATT + 10
CB_KSLC = CB_ATT + 12
CB_VSLC = CB_ATT + 14
CB_KWIN = CB_ATT + 16
CB_VWIN = CB_ATT + 18
CB_FOXQ = CB_ATT + 20
CB_FOXK = CB_ATT + 24
CB_FOXV = CB_ATT + 28
CB_MEMQ = CB_ATT + 32
PROJ_COLS = (CB_ATT + 36) * LANES
SMALL_W = 2 * LANES
FOXF_LANE = 12

_BF16 = jnp.bfloat16
_F32 = jnp.float32


def _cparams(sem):
    return pltpu.CompilerParams(dimension_semantics=sem, vmem_limit_bytes=VMEM_LIMIT_BYTES)


def _dot(a, b):
    return jnp.dot(a, b, preferred_element_type=_F32)


def _dot_nt(a, b):
    return lax.dot_general(a, b, (((1,), (1,)), ((), ())), preferred_element_type=_F32)


def _lanes(parts):
    return jnp.concatenate(parts, axis=1)


def _to_token_major(x):
    c = x.shape[1] // LANES
    chunks = jnp.stack([x[:, j * LANES:(j + 1) * LANES] for j in range(c)], axis=0)
    return pltpu.einshape("ctl->tcl", chunks)


def _from_token_major(x3):
    xt = pltpu.einshape("tcl->ctl", x3)
    return _lanes([xt[j] for j in range(x3.shape[1])])


def _proj_kernel(norm_j0, has_small, x_ref, g_ref, w_ref, cg_ref, cf_ref, *rest):
    if has_small:
        ws_ref, o_ref, os_ref, h_ref = rest
    else:
        o_ref, h_ref = rest
    j = pl.program_id(1)

    @pl.when(j == 0)
    def _():
        x = x_ref[...]
        ms = jnp.mean(x * x, axis=-1, keepdims=True)
        h = (x * lax.rsqrt(ms + RMS_EPS) * g_ref[...]).astype(_BF16)
        h_ref[...] = h
        if has_small:
            os_ref[...] = _dot(h, ws_ref[...])

    y = _dot(h_ref[...], w_ref[...])
    tn = y.shape[1]

    @pl.when(j >= norm_j0)
    def _():
        for c in range(tn // LANES):
            sl = slice(c * LANES, (c + 1) * LANES)
            yh = y[:, sl]
            ms = jnp.mean(yh * yh, axis=-1, keepdims=True)
            scale = jnp.where(cf_ref[:, sl] > 0.0, lax.rsqrt(ms + RMS_EPS), 1.0)
            o_ref[:, sl] = (yh * scale * cg_ref[:, sl]).astype(o_ref.dtype)

    @pl.when(j < norm_j0)
    def _():
        o_ref[...] = y.astype(o_ref.dtype)


def _rms_project(x2d, g, w, col_gain, col_flag, n_plain_cols, w_small, tm, tn):
    n, d = x2d.shape
    c = w.shape[1]
    has_small = w_small is not None
    in_specs = [
        pl.BlockSpec((tm, d), lambda i, j: (i, 0)),
        pl.BlockSpec((1, d), lambda i, j: (0, 0)),
        pl.BlockSpec((d, tn), lambda i, j: (0, j)),
        pl.BlockSpec((1, tn), lambda i, j: (0, j)),
        pl.BlockSpec((1, tn), lambda i, j: (0, j)),
    ]
    args = [x2d, g.reshape(1, d), w, col_gain.reshape(1, c), col_flag.reshape(1, c)]
    out_shape = [jax.ShapeDtypeStruct((n, c), _BF16)]
    out_specs = [pl.BlockSpec((tm, tn), lambda i, j: (i, j))]
    if has_small:
        ws = w_small.shape[1]
        in_specs.append(pl.BlockSpec((d, ws), lambda i, j: (0, 0)))
        args.append(w_small)
        out_shape.append(jax.ShapeDtypeStruct((n, ws), _F32))
        out_specs.append(pl.BlockSpec((tm, ws), lambda i, j: (i, 0)))
    assert n % tm == 0 and c % tn == 0 and n_plain_cols % tn == 0
    res = pl.pallas_call(
        functools.partial(_proj_kernel, n_plain_cols // tn, has_small),
        grid=(n // tm, c // tn),
        in_specs=in_specs,
        out_specs=out_specs,
        out_shape=out_shape,
        scratch_shapes=[pltpu.VMEM((tm, d), _BF16)],
        compiler_params=_cparams(("arbitrary", "arbitrary")),
        name="rms_project",
    )(*args)
    return res if has_small else res[0]


def _cmp_kernel(x_ref, wlo_ref, whi_ref, pelo_ref, pehi_ref, kg_ref, o_ref):
    j = pl.program_id(1)
    x = x_ref[0, 0]
    nchunk = x.shape[0]
    ylo = _dot(x, wlo_ref[0])
    yhi = _dot(x, whi_ref[0])
    pe = _dot(pelo_ref[0], wlo_ref[0]) + _dot(pehi_ref[0], whi_ref[0])
    y = ylo + pltpu.roll(yhi, nchunk - 1, 0) + pe[0:1, :]

    @pl.when(j < NSA_KV_HEADS)
    def _():
        ms = jnp.mean(y * y, axis=-1, keepdims=True)
        o_ref[0, 0] = (y * lax.rsqrt(ms + RMS_EPS) * kg_ref[...]).astype(o_ref.dtype)

    @pl.when(j >= NSA_KV_HEADS)
    def _():
        o_ref[0, 0] = y.T.astype(o_ref.dtype)


def _nsa_compress(cmp_in, w_ck, w_cv, pe_k, pe_v, k_g):
    b, nj, t, dk = cmp_in.shape
    nchunk = t // CMP_STRIDE
    half = CMP_LEN // 2
    assert nchunk == dk
    x = cmp_in.reshape(b, nj, nchunk, CMP_STRIDE * dk)

    def halves(w):
        return (w[:half].reshape(half * dk, dk).astype(_BF16),
                w[half:].reshape(half * dk, dk).astype(_BF16))

    klo, khi = halves(w_ck)
    vlo, vhi = halves(w_cv)
    wlo = jnp.stack([klo, vlo])
    whi = jnp.stack([khi, vhi])

    def pe_halves(pe):
        lo = jnp.broadcast_to(pe[:half].reshape(1, half * dk), (8, half * dk)).astype(_BF16)
        hi = jnp.broadcast_to(pe[half:].reshape(1, half * dk), (8, half * dk)).astype(_BF16)
        return lo, hi

    pklo, pkhi = pe_halves(pe_k)
    pvlo, pvhi = pe_halves(pe_v)
    pelo = jnp.stack([pklo, pvlo])
    pehi = jnp.stack([pkhi, pvhi])
    kv = lambda bb, j: (j // NSA_KV_HEADS, 0, 0)
    return pl.pallas_call(
        _cmp_kernel,
        grid=(b, nj),
        in_specs=[
            pl.BlockSpec((1, 1, nchunk, CMP_STRIDE * dk), lambda bb, j: (bb, j, 0, 0)),
            pl.BlockSpec((1, half * dk, dk), kv),
            pl.BlockSpec((1, half * dk, dk), kv),
            pl.BlockSpec((1, 8, half * dk), kv),
            pl.BlockSpec((1, 8, half * dk), kv),
            pl.BlockSpec((1, dk), lambda bb, j: (0, 0)),
        ],
        out_specs=pl.BlockSpec((1, 1, nchunk, dk), lambda bb, j: (bb, j, 0, 0)),
        out_shape=jax.ShapeDtypeStruct((b, nj, nchunk, dk), _BF16),
        compiler_params=_cparams(("arbitrary", "arbitrary")),
        name="nsa_compress",
    )(x, wlo, whi, pelo, pehi, k_g.reshape(1, dk).astype(_F32))


def _t_update(state, tiles):
    m, l, acc = state
    masked = [s if mask is None else jnp.where(mask, s, NEG_INF) for s, mask, _ in tiles]
    m_new = m
    for sm in masked:
        m_new = jnp.maximum(m_new, jnp.max(sm, axis=0, keepdims=True))
    alpha = jnp.exp2(m - m_new)
    l_new = alpha * l
    acc_new = alpha * acc
    for sm, (_, _, vt) in zip(masked, tiles):
        e = jnp.exp2(sm - m_new)
        l_new = l_new + jnp.sum(e, axis=0, keepdims=True)
        acc_new = acc_new + _dot(vt, e.astype(_BF16))
    return m_new, l_new, acc_new


def _t_init(nq):
    return (jnp.full((1, nq), NEG_INF, _F32), jnp.zeros((1, nq), _F32), jnp.zeros((HEAD_DIM, nq), _F32))


def _t_finish(state):
    _, l, acc = state
    return acc / l


def _nsa_kernel(q_ref, kc_ref, vct_ref, ks_ref, vst_ref, kw_ref, vwt_ref, gl_ref,
                wt_ref, dt_ref, ovt_ref, ext_ref, o_ref, sel_ref):
    i = pl.program_id(2)
    tq = ATT_TQ
    hg = NSA_GROUP
    nq = hg * tq
    n_slc = ovt_ref.shape[0]

    q = q_ref[0]
    qt = _lanes([q[:, h * HEAD_DIM:(h + 1) * HEAD_DIM].astype(_F32).T for h in range(hg)]).astype(_BF16)

    key = lax.broadcasted_iota(jnp.int32, (LANES, nq), 0)
    qry = lax.broadcasted_iota(jnp.int32, (LANES, nq), 1) & (tq - 1)

    def ktile(ref, kt):
        return ref[0, pl.ds(pl.multiple_of(kt * LANES, LANES), LANES), :]

    def near_bias(d):
        return _lanes([dt_ref[h, d] for h in range(hg)])

    woff = pl.multiple_of(wt_ref.shape[1] - LANES - 8 - (LANES // CMP_STRIDE) * i, 8)
    s = _dot(kc_ref[0, 0], qt) + _lanes([wt_ref[h, pl.ds(woff, LANES), :] for h in range(hg)])
    n_win = WINDOW // tq
    win_tiles = []
    for d in range(n_win, -1, -1):
        ktd = jnp.maximum(i - d, 0)
        sc = _dot(ktile(kw_ref, ktd), qt)
        if d <= 1:
            sc = sc + near_bias(d)
        if d == n_win:
            mk = (qry < key) & (i >= d)
        elif d == 0:
            mk = key <= qry
        else:
            mk = jnp.broadcast_to(i >= d, (LANES, nq))
        win_tiles.append((sc, mk, vwt_ref[0, 0, ktd]))

    mask_c = (i * tq + qry) >= (CMP_STRIDE * key + CMP_LEN - 1)
    mx = jnp.max(jnp.where(mask_c, s, NEG_INF), axis=0, keepdims=True)
    e = jnp.where(mask_c, jnp.exp2(s - mx), 0.0)
    l = jnp.sum(e, axis=0, keepdims=True)
    p_c = e / jnp.where(l > 0.0, l, 1.0)
    o_c = _dot(vct_ref[0, 0], p_c.astype(_BF16))

    ps = p_c[:, 0:tq]
    for h in range(1, hg):
        ps = ps + p_c[:, h * tq:(h + 1) * tq]
    ps_hi = ps.astype(_BF16)
    ps_lo = (ps - ps_hi.astype(_F32)).astype(_BF16)
    imp = _dot(ovt_ref[...], ps_hi) + _dot(ovt_ref[...], ps_lo)
    blk = lax.broadcasted_iota(jnp.int32, (n_slc, tq), 0)
    tpos = i * tq + lax.broadcasted_iota(jnp.int32, (n_slc, tq), 1)
    cur = tpos // SLC_LEN
    valid = blk * SLC_LEN <= tpos
    forced = (blk == 0) | (blk == cur) | (blk == cur - 1)
    score = jnp.where(valid, jnp.where(forced, FORCE_SCORE, imp), -1.0)
    rank = jnp.zeros((n_slc, tq), _F32)
    for k in range(n_slc):
        ck = score[k:k + 1, :]
        beats = (ck > score) | ((ck == score) & (blk > k))
        rank = rank + jnp.where(beats, 1.0, 0.0)
    sel = jnp.where(rank < float(min(N_SELECT, n_slc)), 1.0, 0.0).astype(_BF16)
    selfull = _dot(ext_ref[...], sel)
    for kt in range(ext_ref.shape[0] // LANES):
        sel_ref[kt] = selfull[kt * LANES:(kt + 1) * LANES, :]

    def sel_mask(kt, ok=True):
        return _lanes([sel_ref[kt]] * hg) > jnp.where(ok, 0.5, 2.0)

    o_w = _t_finish(_t_update(_t_init(nq), win_tiles))

    n_far = jnp.maximum(i - 1, 0)

    def far_body(j, st):
        tiles = []
        for u in range(FAR_GROUP):
            kt = FAR_GROUP * j + u
            ktc = jnp.minimum(kt, n_far - 1)
            tiles.append((_dot(ktile(ks_ref, ktc), qt), sel_mask(ktc, kt < n_far), vst_ref[0, 0, ktc]))
        return _t_update(st, tiles)

    st = lax.fori_loop(0, (n_far + FAR_GROUP - 1) // FAR_GROUP, far_body, _t_init(nq))
    kt1 = jnp.maximum(i - 1, 0)
    st = _t_update(st, [
        (_dot(ktile(ks_ref, kt1), qt) + near_bias(1), sel_mask(kt1, i >= 1), vst_ref[0, 0, kt1]),
        (_dot(ktile(ks_ref, i), qt) + near_bias(0), sel_mask(i) & (key <= qry), vst_ref[0, 0, i])])
    o_s = _t_finish(st)

    glt = jax.nn.sigmoid(gl_ref[0]).T
    grow = lambda br: _lanes([glt[br * hg + h:br * hg + h + 1, :] for h in range(hg)])
    o = grow(0) * o_c + grow(1) * o_s + grow(2) * o_w
    for h in range(hg):
        o_ref[0, :, h * HEAD_DIM:(h + 1) * HEAD_DIM] = o[:, h * tq:(h + 1) * tq].T.astype(o_ref.dtype)


def _t5_bucket(rel):
    n = np.maximum(rel, 0)
    max_exact = NUM_BUCKETS // 2
    ratio = np.maximum(n, 1).astype(np.float32) / np.float32(max_exact)
    log_ratio = np.log(ratio) / np.float32(math.log(MAX_DISTANCE / max_exact))
    large = np.minimum(max_exact + (log_ratio * np.float32(NUM_BUCKETS - max_exact)).astype(np.int32),
                       NUM_BUCKETS - 1)
    return np.where(n < max_exact, n, large).astype(np.int32)


def _bias_lookup(rb, bucket):
    bk = jnp.asarray(bucket.astype(np.int8))[None]
    ex = (slice(None),) + (None,) * bucket.ndim
    tab = jnp.broadcast_to(rb[:, 0][ex], (rb.shape[0],) + bucket.shape)
    for k in range(1, NUM_BUCKETS):
        tab = jnp.where(bk == k, rb[:, k][ex], tab)
    return tab


def _nsa_tables(rel_bias):
    rb = rel_bias.astype(_F32).T * LOG2E
    r = np.arange(ATT_TQ)
    rel_d = (np.arange(2) * ATT_TQ)[:, None, None] + r[None, None, :] - r[None, :, None]
    assert _t5_bucket(np.array([ATT_TQ + 1]))[0] == NUM_BUCKETS - 1
    u = np.arange(2 * LANES)
    rel_w = r[None, :] - CMP_STRIDE * (u[:, None] - (LANES - 8)) - (CMP_LEN - 1)
    far = rb[:, NUM_BUCKETS - 1]
    dt = _bias_lookup(rb, _t5_bucket(rel_d)) - far[:, None, None, None]
    return dt, _bias_lookup(rb, _t5_bucket(rel_w))


def _tile_transposed(v, cb, n_heads, tk):
    b, t, _ = v.shape
    x = v[:, :, cb * LANES:(cb + n_heads) * LANES].reshape(b, t // tk, tk, n_heads, HEAD_DIM)
    return x.transpose(0, 3, 1, 4, 2)


def _nsa_attention(proj, small, kvc, rel_bias):
    b, t, _ = proj.shape
    g, hg, tq = NSA_KV_HEADS, NSA_GROUP, ATT_TQ
    n_slc = t // SLC_LEN
    n_cmp = (t - CMP_LEN) // CMP_STRIDE + 1
    nt = t // LANES
    assert n_cmp <= LANES and kvc.shape[2] == LANES and n_slc % 16 == 0 and tq == LANES
    assert LANES - 8 - (LANES // CMP_STRIDE) * (t // tq - 1) >= 0
    dt, wt = _nsa_tables(rel_bias)
    cstart = np.arange(LANES) * CMP_STRIDE
    sstart = np.arange(n_slc) * SLC_LEN
    overlap = np.clip(np.minimum(cstart[None, :] + CMP_LEN, sstart[:, None] + SLC_LEN)
                      - np.maximum(cstart[None, :], sstart[:, None]), 0, None).astype(np.float32) / CMP_LEN
    overlap[:, n_cmp:] = 0.0
    ovt = jnp.asarray(overlap, _BF16)
    ext = jnp.asarray(np.arange(t)[:, None] // SLC_LEN == np.arange(n_slc)[None, :], _BF16)
    vst = _tile_transposed(proj, CB_VSLC, g, LANES)
    vwt = _tile_transposed(proj, CB_VWIN, g, LANES)

    seq = lambda cb: pl.BlockSpec((1, t, LANES), lambda bb, gg, ii: (bb, 0, cb + gg))
    vts = pl.BlockSpec((1, 1, nt, HEAD_DIM, LANES), lambda bb, gg, ii: (bb, gg, 0, 0, 0))
    return pl.pallas_call(
        _nsa_kernel,
        grid=(b, g, t // tq),
        in_specs=[
            pl.BlockSpec((1, tq, hg * HEAD_DIM), lambda bb, gg, ii: (bb, ii, CB_QNSA * LANES // (hg * HEAD_DIM) + gg)),
            pl.BlockSpec((1, 1, LANES, HEAD_DIM), lambda bb, gg, ii: (bb, gg, 0, 0)),
            pl.BlockSpec((1, 1, HEAD_DIM, LANES), lambda bb, gg, ii: (bb, NSA_KV_HEADS + gg, 0, 0)),
            seq(CB_KSLC), vts, seq(CB_KWIN), vts,
            pl.BlockSpec((1, tq, LANES), lambda bb, gg, ii: (bb, ii, gg)),
            pl.BlockSpec((hg, 2 * LANES, tq), lambda bb, gg, ii: (gg, 0, 0)),
            pl.BlockSpec((hg, 2, LANES, tq), lambda bb, gg, ii: (gg, 0, 0, 0)),
            pl.BlockSpec((n_slc, LANES), lambda bb, gg, ii: (0, 0)),
            pl.BlockSpec((t, n_slc), lambda bb, gg, ii: (0, 0)),
        ],
        out_specs=pl.BlockSpec((1, tq, hg * HEAD_DIM), lambda bb, gg, ii: (bb, ii, gg)),
        out_shape=jax.ShapeDtypeStruct((b, t, NSA_Q_W), _BF16),
        scratch_shapes=[pltpu.VMEM((nt, LANES, tq), _F32)],
        compiler_params=_cparams(("arbitrary", "arbitrary", "arbitrary")),
        name="nsa_attention",
    )(proj, kvc, kvc, proj, vst, proj, vwt, small, wt, dt, ovt, ext)


def _fox_cum_kernel(s_ref, b_ref, col_ref, row_ref):
    z = s_ref[0] + b_ref[...]
    lf = (jnp.minimum(z, 0.0) - jnp.log1p(jnp.exp(-jnp.abs(z)))) * LOG2E
    x = lf.T
    t = x.shape[1]
    lane = lax.broadcasted_iota(jnp.int32, x.shape, 1)
    sh = 1
    while sh < t:
        x = x + jnp.where(lane >= sh, pltpu.roll(x, sh, 1), 0.0)
        sh *= 2
    row_ref[0] = x[8:16, :]
    col_ref[0] = x.T


def _fox_cumsum(small, b_f):
    b, t, _ = small.shape
    bvec = jnp.zeros((1, LANES), _F32).at[0, FOXF_LANE:FOXF_LANE + FOX_HEADS].set(b_f.astype(_F32))
    return pl.pallas_call(
        _fox_cum_kernel,
        grid=(b,),
        in_specs=[pl.BlockSpec((1, t, LANES), lambda bb: (bb, 0, 1)),
                  pl.BlockSpec((1, LANES), lambda bb: (0, 0))],
        out_specs=[pl.BlockSpec((1, t, LANES), lambda bb: (bb, 0, 0)),
                   pl.BlockSpec((1, 8, t), lambda bb: (bb, 0, 0))],
        out_shape=[jax.ShapeDtypeStruct((b, t, LANES), _F32), jax.ShapeDtypeStruct((b, 8, t), _F32)],
        compiler_params=_cparams(("arbitrary",)),
        name="fox_cumsum",
    )(small, bvec)


def _fox_kernel(q_ref, k_ref, vt_ref, cc_ref, cr_ref, o_ref):
    i = pl.program_id(1)
    tq = FOX_TQ
    key = lax.broadcasted_iota(jnp.int32, (tq, tq), 0)
    qry = lax.broadcasted_iota(jnp.int32, (tq, tq), 1)
    heads = range(FOX_HEADS)
    hsl = [slice(h * HEAD_DIM, (h + 1) * HEAD_DIM) for h in heads]
    qts = [q_ref[0, :, hsl[h]].astype(_F32).T.astype(_BF16) for h in heads]
    cqs = [cr_ref[0, FOXF_LANE - 8 + h:FOXF_LANE - 8 + h + 1, :] for h in heads]

    def logits(h, kt):
        ks = pl.ds(pl.multiple_of(kt * tq, tq), tq)
        ck = cc_ref[0, ks, FOXF_LANE + h:FOXF_LANE + h + 1]
        return _dot(k_ref[0, ks, hsl[h]], qts[h]) + (cqs[h] - ck)

    def body(kt, sts):
        ss = [logits(h, kt) for h in heads]
        return tuple(_t_update(sts[h], [(ss[h], None, vt_ref[0, h, kt])]) for h in heads)

    sts = lax.fori_loop(0, i, body, tuple(_t_init(tq) for _ in heads))
    ss = [logits(h, i) for h in heads]
    for h in heads:
        st = _t_update(sts[h], [(ss[h], key <= qry, vt_ref[0, h, i])])
        o_ref[0, :, hsl[h]] = _t_finish(st).T.astype(o_ref.dtype)


def _fox_attention(proj, cum_col, cum_row):
    b, t, _ = proj.shape
    tq = FOX_TQ
    w = FOX_W
    vt = _tile_transposed(proj, CB_FOXV, FOX_HEADS, tq)
    return pl.pallas_call(
        _fox_kernel,
        grid=(b, t // tq),
        in_specs=[
            pl.BlockSpec((1, tq, w), lambda bb, ii: (bb, ii, CB_FOXQ * LANES // w)),
            pl.BlockSpec((1, t, w), lambda bb, ii: (bb, 0, CB_FOXK * LANES // w)),
            pl.BlockSpec((1, FOX_HEADS, t // tq, HEAD_DIM, tq), lambda bb, ii: (bb, 0, 0, 0, 0)),
            pl.BlockSpec((1, t, LANES), lambda bb, ii: (bb, 0, 0)),
            pl.BlockSpec((1, 8, tq), lambda bb, ii: (bb, 0, ii)),
        ],
        out_specs=pl.BlockSpec((1, tq, w), lambda bb, ii: (bb, ii, 0)),
        out_shape=jax.ShapeDtypeStruct((b, t, w), _BF16),
        compiler_params=_cparams(("arbitrary", "arbitrary")),
        name="fox_attention",
    )(proj, proj, vt, cum_col, cum_row)


def _mem_kernel(q_ref, kv_ref, o_ref):
    for h in range(MEM_HEADS):
        hs = slice(h * HEAD_DIM, (h + 1) * HEAD_DIM)
        vs = slice(MEM_W + h * HEAD_DIM, MEM_W + (h + 1) * HEAD_DIM)
        s = _dot_nt(q_ref[0, :, hs], kv_ref[0, :, hs])
        e = jnp.exp2(s - jnp.max(s, axis=1, keepdims=True))
        p = e / jnp.sum(e, axis=1, keepdims=True)
        o_ref[0, :, hs] = _dot(p.astype(_BF16), kv_ref[0, :, vs]).astype(o_ref.dtype)


def _mem_attention(proj, memkv):
    b, t, _ = proj.shape
    m = memkv.shape[1]
    tq = MEM_TQ
    return pl.pallas_call(
        _mem_kernel,
        grid=(b, t // tq),
        in_specs=[pl.BlockSpec((1, tq, MEM_W), lambda bb, ii: (bb, ii, CB_MEMQ * LANES // MEM_W)),
                  pl.BlockSpec((1, m, 2 * MEM_W), lambda bb, ii: (bb, 0, 0))],
        out_specs=pl.BlockSpec((1, tq, MEM_W), lambda bb, ii: (bb, ii, 0)),
        out_shape=jax.ShapeDtypeStruct((b, t, MEM_W), _BF16),
        compiler_params=_cparams(("arbitrary", "arbitrary")),
        name="mem_attention",
    )(proj, memkv)


def _merge_kernel(x_ref, on_ref, of_ref, om_ref, g0_ref, g1_ref, g2_ref, wn_ref, wf_ref, wm_ref, wo_ref,
                  o_ref, z_ref):
    tn = 512
    for c in range(D_MODEL // tn):
        cs = slice(c * tn, (c + 1) * tn)
        z = (jax.nn.sigmoid(g0_ref[:, cs].astype(_F32)) * _dot(on_ref[...], wn_ref[:, cs])
             + jax.nn.sigmoid(g1_ref[:, cs].astype(_F32)) * _dot(of_ref[...], wf_ref[:, cs])
             + jax.nn.sigmoid(g2_ref[:, cs].astype(_F32)) * _dot(om_ref[...], wm_ref[:, cs]))
        z_ref[:, cs] = z.astype(_BF16)
    o_ref[...] = x_ref[...] + _dot(z_ref[...], wo_ref[...])


def _merge(x2d, o_nsa, o_fox, o_mem, proj2d, w_o_nsa, w_o_fox, w_o_mem, w_out):
    n, d = x2d.shape
    tm = MERGE_TM
    gcb = CB_MERGE * LANES // d
    const = lambda r, c: pl.BlockSpec((r, c), lambda i: (0, 0))
    return pl.pallas_call(
        _merge_kernel,
        grid=(n // tm,),
        in_specs=[
            pl.BlockSpec((tm, d), lambda i: (i, 0)),
            pl.BlockSpec((tm, NSA_Q_W), lambda i: (i, 0)),
            pl.BlockSpec((tm, FOX_W), lambda i: (i, 0)),
            pl.BlockSpec((tm, MEM_W), lambda i: (i, 0)),
            pl.BlockSpec((tm, d), lambda i: (i, gcb)),
            pl.BlockSpec((tm, d), lambda i: (i, gcb + 1)),
            pl.BlockSpec((tm, d), lambda i: (i, gcb + 2)),
            const(NSA_Q_W, d), const(FOX_W, d), const(MEM_W, d), const(d, d),
        ],
        out_specs=pl.BlockSpec((tm, d), lambda i: (i, 0)),
        out_shape=jax.ShapeDtypeStruct((n, d), _F32),
        scratch_shapes=[pltpu.VMEM((tm, d), _BF16)],
        compiler_params=_cparams(("arbitrary",)),
        name="merge_out",
    )(x2d, o_nsa, o_fox, o_mem, proj2d, proj2d, proj2d,
      w_o_nsa.astype(_BF16), w_o_fox.astype(_BF16), w_o_mem.astype(_BF16), w_out.astype(_BF16))


def _nsa_inputs(x, norm_attn_g, w_in, nsa_pe_k, nsa_w_ck, nsa_pe_v, nsa_w_cv, nsa_q_g, nsa_k_g,
                fox_q_g, fox_k_g, mem_q_g):
    b, t, d = x.shape
    ones = lambda k: jnp.ones((k * HEAD_DIM,), _F32)
    zeros = lambda k: jnp.zeros((k * HEAD_DIM,), _F32)
    tile = lambda gv, k: jnp.tile(gv.astype(_F32), k)
    qs = ATTN_SCALE * LOG2E

    g_end = NSA_Q_W + 6 * NSA_KV_W
    f_off = g_end + NSA_GATE_W + 3 * FOX_W
    m_off = f_off + FOX_HEADS + MEM_W
    w_all = jnp.concatenate([w_in[:, m_off:], w_in[:, :g_end], w_in[:, g_end + NSA_GATE_W:f_off],
                             w_in[:, f_off + FOX_HEADS:m_off]], axis=1).astype(_BF16)
    gates = w_in[:, g_end:g_end + NSA_GATE_W].reshape(d, 3, NSA_KV_HEADS, NSA_GROUP)
    pad = jnp.zeros((d, LANES - 3 * NSA_GROUP), w_in.dtype)
    w_small = jnp.concatenate([
        gates[:, :, 0, :].reshape(d, 3 * NSA_GROUP), pad,
        gates[:, :, 1, :].reshape(d, 3 * NSA_GROUP), w_in[:, f_off:f_off + FOX_HEADS],
        pad[:, :LANES - 3 * NSA_GROUP - FOX_HEADS]], axis=1).astype(_BF16)
    col_gain = jnp.concatenate([
        jnp.ones((MERGE_W,), _F32),
        tile(nsa_q_g, NSA_HEADS) * qs, ones(2), ones(2), tile(nsa_k_g, 2), ones(2), tile(nsa_k_g, 2), ones(2),
        tile(fox_q_g, FOX_HEADS) * qs, tile(fox_k_g, FOX_HEADS), ones(FOX_HEADS),
        tile(mem_q_g, MEM_HEADS) * qs])
    col_flag = jnp.concatenate([
        jnp.zeros((MERGE_W,), _F32),
        ones(NSA_HEADS), zeros(2), zeros(2), ones(2), zeros(2), ones(2), zeros(2),
        ones(FOX_HEADS), ones(FOX_HEADS), zeros(FOX_HEADS), ones(MEM_HEADS)])

    x2d = x.reshape(b * t, d)
    proj2d, small2d = _rms_project(x2d, norm_attn_g, w_all, col_gain, col_flag, CB_ATT * LANES, w_small,
                                   PROJ_TM, PROJ_TN)
    proj = proj2d.reshape(b, t, PROJ_COLS)
    small = small2d.reshape(b, t, SMALL_W)

    cmp_in = proj[:, :, CB_KCMP * LANES:(CB_KCMP + 4) * LANES].reshape(b, t, 4, HEAD_DIM).transpose(0, 2, 1, 3)
    kvc = _nsa_compress(cmp_in, nsa_w_ck, nsa_w_cv, nsa_pe_k, nsa_pe_v, nsa_k_g)
    return proj, small, kvc


def _attention_mixers(x, mem, norm_attn_g, w_in, nsa_pe_k, nsa_w_ck, nsa_pe_v, nsa_w_cv, nsa_q_g, nsa_k_g,
                      rel_bias, fox_b_f, fox_q_g, fox_k_g, norm_mem_g, w_mem_kv, mem_q_g, mem_k_g):
    b, t, d = x.shape
    ones = lambda k: jnp.ones((k * HEAD_DIM,), _F32)
    zeros = lambda k: jnp.zeros((k * HEAD_DIM,), _F32)
    tile = lambda gv, k: jnp.tile(gv.astype(_F32), k)
    proj, small, kvc = _nsa_inputs(x, norm_attn_g, w_in, nsa_pe_k, nsa_w_ck, nsa_pe_v, nsa_w_cv, nsa_q_g, nsa_k_g,
                                   fox_q_g, fox_k_g, mem_q_g)
    proj2d = proj.reshape(b * t, PROJ_COLS)
    o_nsa = _nsa_attention(proj, small, kvc, rel_bias)

    cum_col, cum_row = _fox_cumsum(small, fox_b_f)
    o_fox = _fox_attention(proj, cum_col, cum_row)

    mm = mem.shape[1]
    mem_gain = jnp.concatenate([tile(mem_k_g, MEM_HEADS), ones(MEM_HEADS)])
    mem_flag = jnp.concatenate([ones(MEM_HEADS), zeros(MEM_HEADS)])
    memkv = _rms_project(mem.reshape(b * mm, d), norm_mem_g, w_mem_kv.astype(_BF16), mem_gain, mem_flag,
                         0, None, mm, MEM_W).reshape(b, mm, 2 * MEM_W)
    o_mem = _mem_attention(proj, memkv)
    return proj2d, o_nsa.reshape(b * t, NSA_Q_W), o_fox.reshape(b * t, FOX_W), o_mem.reshape(b * t, MEM_W)


def _router_kernel(x_ref, g_ref, whi_ref, wlo_ref, b_ref, h_ref, idx_ref, gate_ref, cnt_ref):
    x = x_ref[...]
    ms = jnp.mean(x * x, axis=-1, keepdims=True)
    h = x * lax.rsqrt(ms + RMS_EPS) * g_ref[...]
    h_hi = h.astype(_BF16)
    h_ref[...] = _to_token_major(h_hi)
    h_lo = (h - h_hi.astype(_F32)).astype(_BF16)
    logits = _dot_nt(whi_ref[...], h_hi) + _dot_nt(whi_ref[...], h_lo) + _dot_nt(wlo_ref[...], h_hi)
    scores = jax.nn.sigmoid(logits)
    sb = scores + b_ref[...]
    eidx = lax.broadcasted_iota(jnp.int32, sb.shape, 0)
    idxs, vals = [], []
    for _ in range(TOP_K):
        m = jnp.max(sb, axis=0, keepdims=True)
        idx = jnp.min(jnp.where(sb == m, eidx, N_EXPERTS), axis=0, keepdims=True)
        hit = eidx == idx
        vals.append(jnp.sum(jnp.where(hit, scores, 0.0), axis=0, keepdims=True))
        idxs.append(idx)
        sb = jnp.where(hit, NEG_INF, sb)
    top_s = jnp.concatenate(vals, axis=0)
    idx_ref[...] = jnp.concatenate(idxs, axis=0)
    gate_ref[...] = top_s / jnp.sum(top_s, axis=0, keepdims=True) * ROUTED_SCALE

    @pl.when(pl.program_id(0) == 0)
    def _():
        cnt_ref[...] = jnp.zeros(cnt_ref.shape, cnt_ref.dtype)

    picked = jnp.where(sb < 0.5 * NEG_INF, 1.0, 0.0)
    cnt_ref[...] += jnp.sum(picked, axis=1, keepdims=True)


def _router(x1, norm_g, w_router, router_bias):
    n, d = x1.shape
    tm = ROUTER_TM
    wt = w_router.astype(_F32).T
    w_hi = wt.astype(_BF16)
    w_lo = (wt - w_hi.astype(_F32)).astype(_BF16)
    const = lambda r, c: pl.BlockSpec((r, c), lambda i: (0, 0))
    return pl.pallas_call(
        _router_kernel,
        grid=(n // tm,),
        in_specs=[pl.BlockSpec((tm, d), lambda i: (i, 0)), const(1, d), const(N_EXPERTS, d), const(N_EXPERTS, d),
                  const(N_EXPERTS, 1)],
        out_specs=[pl.BlockSpec((tm, d // LANES, LANES), lambda i: (i, 0, 0)),
                   pl.BlockSpec((TOP_K, tm), lambda i: (0, i)),
                   pl.BlockSpec((TOP_K, tm), lambda i: (0, i)),
                   const(N_EXPERTS, 1)],
        out_shape=[jax.ShapeDtypeStruct((n, d // LANES, LANES), _BF16), jax.ShapeDtypeStruct((TOP_K, n), jnp.int32),
                   jax.ShapeDtypeStruct((TOP_K, n), _F32), jax.ShapeDtypeStruct((N_EXPERTS, 1), _F32)],
        compiler_params=_cparams(("arbitrary",)),
        name="moe_router",
    )(x1, norm_g.reshape(1, d).astype(_F32), w_hi, w_lo, router_bias.reshape(N_EXPERTS, 1).astype(_F32))


def _moe_kernel(be_ref, nbr_ref, src_hbm, dst_hbm, h_hbm, sg_ref, wg_ref, wu_ref, wd_ref, y_hbm,
                xbuf, ybuf, sidx, didx, wg_bf, wu_bf, wd_bf, gsem, ssem, isem):
    n = pl.program_id(0)
    nbr = nbr_ref[0]
    last = nbr - 1
    slot = n % 2
    other = 1 - slot
    blk = MOE_BLOCK
    dump_block = dst_hbm.shape[0] - 1

    def src_copy(block, s):
        return pltpu.make_async_copy(src_hbm.at[block], sidx.at[s], isem.at[0, s])

    def dst_copy(block, s3):
        return pltpu.make_async_copy(dst_hbm.at[block], didx.at[s3], isem.at[1, s3])

    def issue_gather(s):
        for r in range(blk):
            tok = sidx[s, 0, r]
            pltpu.make_async_copy(h_hbm.at[tok], xbuf.at[s, r], gsem.at[s]).start(priority=r % 2)

    def issue_scatter(s, s3):
        for r in range(blk):
            row = didx[s3, 0, r]
            pltpu.make_async_copy(ybuf.at[s, r], y_hbm.at[row], ssem.at[s]).start(priority=r % 2)

    def wait_rows(buf, sem, s):
        pltpu.make_async_copy(buf.at[s], buf.at[s], sem.at[s]).wait()

    @pl.when(n < nbr)
    def _():
        @pl.when(n == 0)
        def _():
            src_copy(0, 0).start()
            dst_copy(0, 0).start()
            dst_copy(dump_block, 2).start()
            ybuf[1] = jnp.zeros(ybuf.shape[1:], ybuf.dtype)
            src_copy(0, 0).wait()
            dst_copy(0, 0).wait()
            dst_copy(dump_block, 2).wait()
            issue_gather(0)
            nxt0 = jnp.minimum(1, last)
            src_copy(nxt0, 1).start()
            dst_copy(nxt0, 1).start()

        wait_rows(xbuf, gsem, slot)

        @pl.when(n >= 1)
        def _():
            wait_rows(ybuf, ssem, slot)

        changed = (n == 0) | (be_ref[n] != be_ref[jnp.maximum(n - 1, 0)])

        @pl.when(changed)
        def _():
            wg_bf[...] = wg_ref[0].astype(_BF16)
            wu_bf[...] = wu_ref[0].astype(_BF16)
            wd_bf[...] = wd_ref[0].astype(_BF16)

        nxt = jnp.minimum(n + 1, last)
        src_copy(nxt, other).wait()
        dst_copy(nxt, lax.rem(n + 1, 3)).wait()

        x = _from_token_major(xbuf[slot])
        issue_scatter(other, lax.rem(n + 2, 3))
        issue_gather(other)
        gcol = jnp.broadcast_to(sg_ref[0], (LANES, blk)).T
        a = jax.nn.silu(_dot(x, wg_bf[...])) * _dot(x, wu_bf[...])
        a = (a * _lanes([gcol] * (a.shape[1] // LANES))).astype(_BF16)
        ybuf[slot] = _to_token_major(_dot(a, wd_bf[...]).astype(_BF16))

        @pl.when(n < last)
        def _():
            nn = jnp.minimum(n + 2, last)
            src_copy(nn, slot).start()
            dst_copy(nn, lax.rem(n + 2, 3)).start()

        @pl.when(n == last)
        def _():
            issue_scatter(slot, lax.rem(n, 3))
            wait_rows(xbuf, gsem, other)
            wait_rows(ybuf, ssem, other)
            wait_rows(ybuf, ssem, slot)


def _moe_experts(h3, block_expert, nb_real, slot_src, slot_dst, slot_gate, we_gate, we_up, we_down):
    n, c, _ = h3.shape
    d = c * LANES
    nb = block_expert.shape[0]
    blk = MOE_BLOCK
    e, _, de = we_gate.shape
    grid_spec = pltpu.PrefetchScalarGridSpec(
        num_scalar_prefetch=2,
        grid=(nb,),
        in_specs=[
            pl.BlockSpec(memory_space=pl.ANY),
            pl.BlockSpec(memory_space=pl.ANY),
            pl.BlockSpec(memory_space=pl.ANY),
            pl.BlockSpec((1, 1, blk), lambda i, be, nbr: (i, 0, 0)),
            pl.BlockSpec((1, d, de), lambda i, be, nbr: (be[i], 0, 0)),
            pl.BlockSpec((1, d, de), lambda i, be, nbr: (be[i], 0, 0)),
            pl.BlockSpec((1, de, d), lambda i, be, nbr: (be[i], 0, 0)),
        ],
        out_specs=pl.BlockSpec(memory_space=pl.ANY),
        scratch_shapes=[
            pltpu.VMEM((2, blk, c, LANES), _BF16), pltpu.VMEM((2, blk, c, LANES), _BF16),
            pltpu.SMEM((2, 1, blk), jnp.int32), pltpu.SMEM((3, 1, blk), jnp.int32),
            pltpu.VMEM((d, de), _BF16), pltpu.VMEM((d, de), _BF16), pltpu.VMEM((de, d), _BF16),
            pltpu.SemaphoreType.DMA((2,)), pltpu.SemaphoreType.DMA((2,)), pltpu.SemaphoreType.DMA((2, 3)),
        ],
    )
    return pl.pallas_call(
        _moe_kernel,
        grid_spec=grid_spec,
        out_shape=jax.ShapeDtypeStruct((TOP_K * n + blk, c, LANES), _BF16),
        compiler_params=_cparams(("arbitrary",)),
        name="moe_experts",
    )(block_expert, nb_real, slot_src, slot_dst, h3, slot_gate, we_gate, we_up, we_down)


def _dispatch_plan(top_idx, gates, counts, n):
    blk = MOE_BLOCK
    a = n * TOP_K
    nb = (a + N_EXPERTS * (blk - 1)) // blk
    order = jnp.argsort(top_idx.reshape(a)).astype(jnp.int32)
    counts = counts.astype(jnp.int32)
    padded = (counts + blk - 1) // blk * blk
    start = jnp.cumsum(counts) - counts
    pstart = jnp.cumsum(padded) - padded
    block_end = jnp.cumsum(padded) // blk
    blocks = jnp.arange(nb, dtype=jnp.int32)
    block_expert = jnp.minimum(jnp.sum((block_end[None, :] <= blocks[:, None]).astype(jnp.int32), axis=1),
                               N_EXPERTS - 1)
    off = blocks * blk - pstart[block_expert]
    base = start[block_expert] + off
    lane = jnp.arange(blk, dtype=jnp.int32)[None, :]
    rows = jnp.take(order, base[:, None] + lane, mode="clip")
    real = (off[:, None] + lane) < counts[block_expert][:, None]
    tok = rows // TOP_K
    kk = rows - tok * TOP_K
    slot_src = jnp.where(real, tok, 0)
    slot_dst = jnp.where(real, kk * n + tok, TOP_K * n + lane)
    slot_dst = jnp.concatenate([slot_dst, TOP_K * n + lane], axis=0)
    slot_gate = jnp.where(real, jnp.take(gates.reshape(a), rows, mode="clip"), 0.0).reshape(nb, 1, blk)
    nb_real = (jnp.sum(padded) // blk).reshape(1).astype(jnp.int32)
    return block_expert, nb_real, slot_src.reshape(nb, 1, blk), slot_dst.reshape(nb + 1, 1, blk), slot_gate


def _combine_kernel(x_ref, h_ref, wsg_ref, wsu_ref, wsd_ref, *rest):
    y_refs, o_ref = rest[:TOP_K], rest[TOP_K]
    h = _from_token_major(h_ref[...])
    a = (jax.nn.silu(_dot(h, wsg_ref[...])) * _dot(h, wsu_ref[...])).astype(_BF16)
    routed = y_refs[0][...].astype(_F32)
    for k in range(1, TOP_K):
        routed = routed + y_refs[k][...].astype(_F32)
    o_ref[...] = x_ref[...] + _dot(a, wsd_ref[...]) + _from_token_major(routed)


def _combine(x1, h3, y3, ws_gate, ws_up, ws_down):
    n, d = x1.shape
    c = d // LANES
    tm = COMBINE_TM
    nt = n // tm
    de = ws_gate.shape[1]
    const = lambda r, cc: pl.BlockSpec((r, cc), lambda i: (0, 0))
    row = pl.BlockSpec((tm, d), lambda i: (i, 0))
    y_specs = [pl.BlockSpec((tm, c, LANES), functools.partial(lambda i, k: (k * nt + i, 0, 0), k=k))
               for k in range(TOP_K)]
    return pl.pallas_call(
        _combine_kernel,
        grid=(nt,),
        in_specs=[row, pl.BlockSpec((tm, c, LANES), lambda i: (i, 0, 0)), const(d, de), const(d, de), const(de, d)]
        + y_specs,
        out_specs=row,
        out_shape=jax.ShapeDtypeStruct((n, d), _F32),
        compiler_params=_cparams(("arbitrary",)),
        name="moe_combine",
    )(x1, h3, ws_gate.astype(_BF16), ws_up.astype(_BF16), ws_down.astype(_BF16), *([y3] * TOP_K))


def _moe_block(x1, norm_ffn_g, w_router, router_bias, we_gate, we_up, we_down, ws_gate, ws_up, ws_down):
    n = x1.shape[0]
    h3, top_idx_t, gates_t, counts = _router(x1, norm_ffn_g, w_router, router_bias)
    block_expert, nb_real, slot_src, slot_dst, slot_gate = _dispatch_plan(
        top_idx_t.T, gates_t.T, counts.reshape(N_EXPERTS), n)
    y3 = _moe_experts(h3, block_expert, nb_real, slot_src, slot_dst, slot_gate, we_gate, we_up, we_down)
    return _combine(x1, h3, y3, ws_gate, ws_up, ws_down)


def kernel(x, mem, norm_attn_g, w_in, nsa_pe_k, nsa_w_ck, nsa_pe_v, nsa_w_cv, nsa_q_g, nsa_k_g, rel_bias, fox_b_f, fox_q_g, fox_k_g, norm_mem_g, w_mem_kv, mem_q_g, mem_k_g, w_o_nsa, w_o_fox, w_o_mem, w_out, norm_ffn_g, w_router, router_bias, we_gate, we_up, we_down, ws_gate, ws_up, ws_down):
    b, t, d = x.shape
    assert norm_attn_g.shape[0] == 1, "single-layer problem"
    l = 0
    proj2d, o_nsa, o_fox, o_mem = _attention_mixers(
        x, mem, norm_attn_g[l], w_in[l], nsa_pe_k[l], nsa_w_ck[l], nsa_pe_v[l], nsa_w_cv[l], nsa_q_g[l],
        nsa_k_g[l], rel_bias, fox_b_f[l], fox_q_g[l], fox_k_g[l], norm_mem_g[l], w_mem_kv[l], mem_q_g[l],
        mem_k_g[l])
    x1 = _merge(x.reshape(b * t, d), o_nsa, o_fox, o_mem, proj2d, w_o_nsa[l], w_o_fox[l], w_o_mem[l], w_out[l])
    out = _moe_block(x1, norm_ffn_g[l], w_router[l], router_bias[l], we_gate[l], we_up[l], we_down[l],
                     ws_gate[l], ws_up[l], ws_down[l])
    return out.reshape(b, t, d)
```

```python
import functools
import math

import jax
import jax.numpy as jnp
import numpy as np
from jax import lax
from jax.experimental import pallas as pl
from jax.experimental.pallas import tpu as pltpu

D_MODEL = 2048
HEAD_DIM = 128
NSA_HEADS = 8
NSA_KV_HEADS = 2
NSA_GROUP = NSA_HEADS // NSA_KV_HEADS
FOX_HEADS = 4
MEM_HEADS = 4
CMP_LEN = 32
CMP_STRIDE = 16
SLC_LEN = 64
N_SELECT = 16
WINDOW = 512
NUM_BUCKETS = 32
MAX_DISTANCE = 128
N_BRANCHES = 3
N_EXPERTS = 64
TOP_K = 8
D_EXPERT = 512
ROUTED_SCALE = 2.5
ATTN_SCALE = HEAD_DIM ** -0.5
NEG_INF = -1e30
FORCE_SCORE = 1e4
RMS_EPS = 1e-6
LOG2E = math.log2(math.e)

NSA_Q_W = NSA_HEADS * HEAD_DIM
NSA_KV_W = NSA_KV_HEADS * HEAD_DIM
NSA_GATE_W = 3 * NSA_HEADS
FOX_W = FOX_HEADS * HEAD_DIM
MEM_W = MEM_HEADS * HEAD_DIM
MERGE_W = N_BRANCHES * D_MODEL

LANES = 128
VMEM_LIMIT_BYTES = 56 * 1024 * 1024

PROJ_TM = 1024
PROJ_TN = 768
ATT_TQ = 128
FAR_GROUP = 4
FOX_TQ = 256
MEM_TQ = 512
MERGE_TM = 256
ROUTER_TM = 512
MOE_BLOCK = 256
COMBINE_TM = 256

CB_MERGE = 0
CB_ATT = N_BRANCHES * D_MODEL // LANES
CB_QNSA = CB_ATT
CB_KCMP = CB_ATT + 8
CB_VCMP = CB_ATT + 10
CB_KSLC = CB_ATT + 12
CB_VSLC = CB_ATT + 14
CB_KWIN = CB_ATT + 16
CB_VWIN = CB_ATT + 18
CB_FOXQ = CB_ATT + 20
CB_FOXK = CB_ATT + 24
CB_FOXV = CB_ATT + 28
CB_MEMQ = CB_ATT + 32
PROJ_COLS = (CB_ATT + 36) * LANES
SMALL_W = 2 * LANES
FOXF_LANE = 12

_BF16 = jnp.bfloat16
_F32 = jnp.float32


def _cparams(sem):
    return pltpu.CompilerParams(dimension_semantics=sem, vmem_limit_bytes=VMEM_LIMIT_BYTES)


def _dot(a, b):
    return jnp.dot(a, b, preferred_element_type=_F32)


def _dot_nt(a, b):
    return lax.dot_general(a, b, (((1,), (1,)), ((), ())), preferred_element_type=_F32)


def _lanes(parts):
    return jnp.concatenate(parts, axis=1)


def _to_token_major(x):
    c = x.shape[1] // LANES
    chunks = jnp.stack([x[:, j * LANES:(j + 1) * LANES] for j in range(c)], axis=0)
    return pltpu.einshape("ctl->tcl", chunks)


def _from_token_major(x3):
    xt = pltpu.einshape("tcl->ctl", x3)
    return _lanes([xt[j] for j in range(x3.shape[1])])


REORDER_TR = 128
REORDER_TC = 512


def _reorder_kernel(pieces, w_ref, o_ref):
    off = 0
    for src, width in pieces:
        for c in range(0, width, REORDER_TC):
            o_ref[:, off + c:off + c + REORDER_TC] = w_ref[:, src + c:src + c + REORDER_TC].astype(o_ref.dtype)
        off += width


def _reorder_cast(w, pieces):
    rows, cols = w.shape
    total = sum(width for _, width in pieces)
    assert rows % REORDER_TR == 0 and all(width % REORDER_TC == 0 for _, width in pieces)
    return pl.pallas_call(
        functools.partial(_reorder_kernel, pieces),
        grid=(rows // REORDER_TR,),
        in_specs=[pl.BlockSpec((REORDER_TR, cols), lambda i: (i, 0))],
        out_specs=pl.BlockSpec((REORDER_TR, total), lambda i: (i, 0)),
        out_shape=jax.ShapeDtypeStruct((rows, total), _BF16),
        compiler_params=_cparams(("arbitrary",)),
        name="reorder_cast",
    )(w)


def _proj_kernel(norm_j0, has_small, x_ref, g_ref, w_ref, cg_ref, cf_ref, *rest):
    if has_small:
        ws_ref, o_ref, os_ref, h_ref = rest
    else:
        o_ref, h_ref = rest
    j = pl.program_id(1)

    @pl.when(j == 0)
    def _():
        x = x_ref[...]
        ms = jnp.mean(x * x, axis=-1, keepdims=True)
        h = (x * lax.rsqrt(ms + RMS_EPS) * g_ref[...]).astype(_BF16)
        h_ref[...] = h
        if has_small:
            os_ref[...] = _dot(h, ws_ref[...])

    y = _dot(h_ref[...], w_ref[...])
    tn = y.shape[1]

    @pl.when(j >= norm_j0)
    def _():
        for c in range(tn // LANES):
            sl = slice(c * LANES, (c + 1) * LANES)
            yh = y[:, sl]
            ms = jnp.mean(yh * yh, axis=-1, keepdims=True)
            scale = jnp.where(cf_ref[:, sl] > 0.0, lax.rsqrt(ms + RMS_EPS), 1.0)
            o_ref[:, sl] = (yh * scale * cg_ref[:, sl]).astype(o_ref.dtype)

    @pl.when(j < norm_j0)
    def _():
        o_ref[...] = y.astype(o_ref.dtype)


def _rms_project(x2d, g, w, col_gain, col_flag, n_plain_cols, w_small, tm, tn):
    n, d = x2d.shape
    c = w.shape[1]
    has_small = w_small is not None
    in_specs = [
        pl.BlockSpec((tm, d), lambda i, j: (i, 0)),
        pl.BlockSpec((1, d), lambda i, j: (0, 0)),
        pl.BlockSpec((d, tn), lambda i, j: (0, j)),
        pl.BlockSpec((1, tn), lambda i, j: (0, j)),
        pl.BlockSpec((1, tn), lambda i, j: (0, j)),
    ]
    args = [x2d, g.reshape(1, d), w, col_gain.reshape(1, c), col_flag.reshape(1, c)]
    out_shape = [jax.ShapeDtypeStruct((n, c), _BF16)]
    out_specs = [pl.BlockSpec((tm, tn), lambda i, j: (i, j))]
    if has_small:
        ws = w_small.shape[1]
        in_specs.append(pl.BlockSpec((d, ws), lambda i, j: (0, 0)))
        args.append(w_small)
        out_shape.append(jax.ShapeDtypeStruct((n, ws), _F32))
        out_specs.append(pl.BlockSpec((tm, ws), lambda i, j: (i, 0)))
    assert n % tm == 0 and c % tn == 0 and n_plain_cols % tn == 0
    res = pl.pallas_call(
        functools.partial(_proj_kernel, n_plain_cols // tn, has_small),
        grid=(n // tm, c // tn),
        in_specs=in_specs,
        out_specs=out_specs,
        out_shape=out_shape,
        scratch_shapes=[pltpu.VMEM((tm, d), _BF16)],
        compiler_params=_cparams(("arbitrary", "arbitrary")),
        name="rms_project",
    )(*args)
    return res if has_small else res[0]


def _cmp_kernel(x_ref, wlo_ref, whi_ref, pelo_ref, pehi_ref, kg_ref, o_ref):
    j = pl.program_id(1)
    x = x_ref[0, 0]
    nchunk = x.shape[0]
    ylo = _dot(x, wlo_ref[0])
    yhi = _dot(x, whi_ref[0])
    pe = _dot(pelo_ref[0], wlo_ref[0]) + _dot(pehi_ref[0], whi_ref[0])
    y = ylo + pltpu.roll(yhi, nchunk - 1, 0) + pe[0:1, :]

    @pl.when(j < NSA_KV_HEADS)
    def _():
        ms = jnp.mean(y * y, axis=-1, keepdims=True)
        o_ref[0, 0] = (y * lax.rsqrt(ms + RMS_EPS) * kg_ref[...]).astype(o_ref.dtype)

    @pl.when(j >= NSA_KV_HEADS)
    def _():
        o_ref[0, 0] = y.T.astype(o_ref.dtype)


def _nsa_compress(cmp_in, w_ck, w_cv, pe_k, pe_v, k_g):
    b, nj, t, dk = cmp_in.shape
    nchunk = t // CMP_STRIDE
    half = CMP_LEN // 2
    assert nchunk == dk
    x = cmp_in.reshape(b, nj, nchunk, CMP_STRIDE * dk)

    def halves(w):
        return (w[:half].reshape(half * dk, dk).astype(_BF16),
                w[half:].reshape(half * dk, dk).astype(_BF16))

    klo, khi = halves(w_ck)
    vlo, vhi = halves(w_cv)
    wlo = jnp.stack([klo, vlo])
    whi = jnp.stack([khi, vhi])

    def pe_halves(pe):
        lo = jnp.broadcast_to(pe[:half].reshape(1, half * dk), (8, half * dk)).astype(_BF16)
        hi = jnp.broadcast_to(pe[half:].reshape(1, half * dk), (8, half * dk)).astype(_BF16)
        return lo, hi

    pklo, pkhi = pe_halves(pe_k)
    pvlo, pvhi = pe_halves(pe_v)
    pelo = jnp.stack([pklo, pvlo])
    pehi = jnp.stack([pkhi, pvhi])
    kv = lambda bb, j: (j // NSA_KV_HEADS, 0, 0)
    return pl.pallas_call(
        _cmp_kernel,
        grid=(b, nj),
        in_specs=[
            pl.BlockSpec((1, 1, nchunk, CMP_STRIDE * dk), lambda bb, j: (bb, j, 0, 0)),
            pl.BlockSpec((1, half * dk, dk), kv),
            pl.BlockSpec((1, half * dk, dk), kv),
            pl.BlockSpec((1, 8, half * dk), kv),
            pl.BlockSpec((1, 8, half * dk), kv),
            pl.BlockSpec((1, dk), lambda bb, j: (0, 0)),
        ],
        out_specs=pl.BlockSpec((1, 1, nchunk, dk), lambda bb, j: (bb, j, 0, 0)),
        out_shape=jax.ShapeDtypeStruct((b, nj, nchunk, dk), _BF16),
        compiler_params=_cparams(("arbitrary", "arbitrary")),
        name="nsa_compress",
    )(x, wlo, whi, pelo, pehi, k_g.reshape(1, dk).astype(_F32))


def _t_update(state, tiles):
    m, l, acc = state
    masked = [s if mask is None else jnp.where(mask, s, NEG_INF) for s, mask, _ in tiles]
    m_new = m
    for sm in masked:
        m_new = jnp.maximum(m_new, jnp.max(sm, axis=0, keepdims=True))
    alpha = jnp.exp2(m - m_new)
    l_new = alpha * l
    acc_new = alpha * acc
    for sm, (_, _, vt) in zip(masked, tiles):
        e = jnp.exp2(sm - m_new)
        l_new = l_new + jnp.sum(e, axis=0, keepdims=True)
        acc_new = acc_new + _dot(vt, e.astype(_BF16))
    return m_new, l_new, acc_new


def _t_init(nq):
    return (jnp.full((1, nq), NEG_INF, _F32), jnp.zeros((1, nq), _F32), jnp.zeros((HEAD_DIM, nq), _F32))


def _t_finish(state):
    _, l, acc = state
    return acc / l


def _nsa_kernel(q_ref, kc_ref, vct_ref, ks_ref, vst_ref, kw_ref, vwt_ref, gl_ref,
                wt_ref, dt_ref, ovt_ref, ext_ref, o_ref, sel_ref):
    i = pl.program_id(2)
    tq = ATT_TQ
    hg = NSA_GROUP
    nq = hg * tq
    n_slc = ovt_ref.shape[0]

    q = q_ref[0]
    qt = _lanes([q[:, h * HEAD_DIM:(h + 1) * HEAD_DIM].astype(_F32).T for h in range(hg)]).astype(_BF16)

    key = lax.broadcasted_iota(jnp.int32, (LANES, nq), 0)
    qry = lax.broadcasted_iota(jnp.int32, (LANES, nq), 1) & (tq - 1)

    def ktile(ref, kt):
        return ref[0, pl.ds(pl.multiple_of(kt * LANES, LANES), LANES), :]

    def near_bias(d):
        return _lanes([dt_ref[h, d] for h in range(hg)])

    woff = pl.multiple_of(wt_ref.shape[1] - LANES - 8 - (LANES // CMP_STRIDE) * i, 8)
    s = _dot(kc_ref[0, 0], qt) + _lanes([wt_ref[h, pl.ds(woff, LANES), :] for h in range(hg)])
    n_win = WINDOW // tq
    win_tiles = []
    for d in range(n_win, -1, -1):
        ktd = jnp.maximum(i - d, 0)
        sc = _dot(ktile(kw_ref, ktd), qt)
        if d <= 1:
            sc = sc + near_bias(d)
        if d == n_win:
            mk = (qry < key) & (i >= d)
        elif d == 0:
            mk = key <= qry
        else:
            mk = jnp.broadcast_to(i >= d, (LANES, nq))
        win_tiles.append((sc, mk, vwt_ref[0, 0, ktd]))

    mask_c = (i * tq + qry) >= (CMP_STRIDE * key + CMP_LEN - 1)
    mx = jnp.max(jnp.where(mask_c, s, NEG_INF), axis=0, keepdims=True)
    e = jnp.where(mask_c, jnp.exp2(s - mx), 0.0)
    l = jnp.sum(e, axis=0, keepdims=True)
    p_c = e / jnp.where(l > 0.0, l, 1.0)
    o_c = _dot(vct_ref[0, 0], p_c.astype(_BF16))

    ps = p_c[:, 0:tq]
    for h in range(1, hg):
        ps = ps + p_c[:, h * tq:(h + 1) * tq]
    ps_hi = ps.astype(_BF16)
    ps_lo = (ps - ps_hi.astype(_F32)).astype(_BF16)
    imp = _dot(ovt_ref[...], ps_hi) + _dot(ovt_ref[...], ps_lo)
    blk = lax.broadcasted_iota(jnp.int32, (n_slc, tq), 0)
    tpos = i * tq + lax.broadcasted_iota(jnp.int32, (n_slc, tq), 1)
    cur = tpos // SLC_LEN
    valid = blk * SLC_LEN <= tpos
    forced = (blk == 0) | (blk == cur) | (blk == cur - 1)
    score = jnp.where(valid, jnp.where(forced, FORCE_SCORE, imp), -1.0)
    rank = jnp.zeros((n_slc, tq), _F32)
    for k in range(n_slc):
        ck = score[k:k + 1, :]
        beats = (ck > score) | ((ck == score) & (blk > k))
        rank = rank + jnp.where(beats, 1.0, 0.0)
    sel = jnp.where(rank < float(min(N_SELECT, n_slc)), 1.0, 0.0).astype(_BF16)
    selfull = _dot(ext_ref[...], sel)
    for kt in range(ext_ref.shape[0] // LANES):
        sel_ref[kt] = selfull[kt * LANES:(kt + 1) * LANES, :]

    def sel_mask(kt, ok=True):
        return _lanes([sel_ref[kt]] * hg) > jnp.where(ok, 0.5, 2.0)

    o_w = _t_finish(_t_update(_t_init(nq), win_tiles))

    n_far = jnp.maximum(i - 1, 0)

    def far_body(j, st):
        tiles = []
        for u in range(FAR_GROUP):
            kt = FAR_GROUP * j + u
            ktc = jnp.minimum(kt, n_far - 1)
            tiles.append((_dot(ktile(ks_ref, ktc), qt), sel_mask(ktc, kt < n_far), vst_ref[0, 0, ktc]))
        return _t_update(st, tiles)

    st = lax.fori_loop(0, (n_far + FAR_GROUP - 1) // FAR_GROUP, far_body, _t_init(nq))
    kt1 = jnp.maximum(i - 1, 0)
    st = _t_update(st, [
        (_dot(ktile(ks_ref, kt1), qt) + near_bias(1), sel_mask(kt1, i >= 1), vst_ref[0, 0, kt1]),
        (_dot(ktile(ks_ref, i), qt) + near_bias(0), sel_mask(i) & (key <= qry), vst_ref[0, 0, i])])
    o_s = _t_finish(st)

    glt = jax.nn.sigmoid(gl_ref[0]).T
    grow = lambda br: _lanes([glt[br * hg + h:br * hg + h + 1, :] for h in range(hg)])
    o = grow(0) * o_c + grow(1) * o_s + grow(2) * o_w
    for h in range(hg):
        o_ref[0, :, h * HEAD_DIM:(h + 1) * HEAD_DIM] = o[:, h * tq:(h + 1) * tq].T.astype(o_ref.dtype)


def _t5_bucket(rel):
    n = np.maximum(rel, 0)
    max_exact = NUM_BUCKETS // 2
    ratio = np.maximum(n, 1).astype(np.float32) / np.float32(max_exact)
    log_ratio = np.log(ratio) / np.float32(math.log(MAX_DISTANCE / max_exact))
    large = np.minimum(max_exact + (log_ratio * np.float32(NUM_BUCKETS - max_exact)).astype(np.int32),
                       NUM_BUCKETS - 1)
    return np.where(n < max_exact, n, large).astype(np.int32)


def _bias_lookup(rb, bucket):
    bk = jnp.asarray(bucket.astype(np.int8))[None]
    ex = (slice(None),) + (None,) * bucket.ndim
    tab = jnp.broadcast_to(rb[:, 0][ex], (rb.shape[0],) + bucket.shape)
    for k in range(1, NUM_BUCKETS):
        tab = jnp.where(bk == k, rb[:, k][ex], tab)
    return tab


def _nsa_tables(rel_bias):
    rb = rel_bias.astype(_F32).T * LOG2E
    r = np.arange(ATT_TQ)
    rel_d = (np.arange(2) * ATT_TQ)[:, None, None] + r[None, None, :] - r[None, :, None]
    assert _t5_bucket(np.array([ATT_TQ + 1]))[0] == NUM_BUCKETS - 1
    u = np.arange(2 * LANES)
    rel_w = r[None, :] - CMP_STRIDE * (u[:, None] - (LANES - 8)) - (CMP_LEN - 1)
    far = rb[:, NUM_BUCKETS - 1]
    dt = _bias_lookup(rb, _t5_bucket(rel_d)) - far[:, None, None, None]
    return dt, _bias_lookup(rb, _t5_bucket(rel_w))


def _tile_transposed(v, cb, n_heads, tk):
    b, t, _ = v.shape
    x = v[:, :, cb * LANES:(cb + n_heads) * LANES].reshape(b, t // tk, tk, n_heads, HEAD_DIM)
    return x.transpose(0, 3, 1, 4, 2)


def _nsa_attention(proj, small, kvc, rel_bias):
    b, t, _ = proj.shape
    g, hg, tq = NSA_KV_HEADS, NSA_GROUP, ATT_TQ
    n_slc = t // SLC_LEN
    n_cmp = (t - CMP_LEN) // CMP_STRIDE + 1
    nt = t // LANES
    assert n_cmp <= LANES and kvc.shape[2] == LANES and n_slc % 16 == 0 and tq == LANES
    assert LANES - 8 - (LANES // CMP_STRIDE) * (t // tq - 1) >= 0
    dt, wt = _nsa_tables(rel_bias)
    cstart = np.arange(LANES) * CMP_STRIDE
    sstart = np.arange(n_slc) * SLC_LEN
    overlap = np.clip(np.minimum(cstart[None, :] + CMP_LEN, sstart[:, None] + SLC_LEN)
                      - np.maximum(cstart[None, :], sstart[:, None]), 0, None).astype(np.float32) / CMP_LEN
    overlap[:, n_cmp:] = 0.0
    ovt = jnp.asarray(overlap, _BF16)
    ext = jnp.asarray(np.arange(t)[:, None] // SLC_LEN == np.arange(n_slc)[None, :], _BF16)
    vst = _tile_transposed(proj, CB_VSLC, g, LANES)
    vwt = _tile_transposed(proj, CB_VWIN, g, LANES)

    seq = lambda cb: pl.BlockSpec((1, t, LANES), lambda bb, gg, ii: (bb, 0, cb + gg))
    vts = pl.BlockSpec((1, 1, nt, HEAD_DIM, LANES), lambda bb, gg, ii: (bb, gg, 0, 0, 0))
    return pl.pallas_call(
        _nsa_kernel,
        grid=(b, g, t // tq),
        in_specs=[
            pl.BlockSpec((1, tq, hg * HEAD_DIM), lambda bb, gg, ii: (bb, ii, CB_QNSA * LANES // (hg * HEAD_DIM) + gg)),
            pl.BlockSpec((1, 1, LANES, HEAD_DIM), lambda bb, gg, ii: (bb, gg, 0, 0)),
            pl.BlockSpec((1, 1, HEAD_DIM, LANES), lambda bb, gg, ii: (bb, NSA_KV_HEADS + gg, 0, 0)),
            seq(CB_KSLC), vts, seq(CB_KWIN), vts,
            pl.BlockSpec((1, tq, LANES), lambda bb, gg, ii: (bb, ii, gg)),
            pl.BlockSpec((hg, 2 * LANES, tq), lambda bb, gg, ii: (gg, 0, 0)),
            pl.BlockSpec((hg, 2, LANES, tq), lambda bb, gg, ii: (gg, 0, 0, 0)),
            pl.BlockSpec((n_slc, LANES), lambda bb, gg, ii: (0, 0)),
            pl.BlockSpec((t, n_slc), lambda bb, gg, ii: (0, 0)),
        ],
        out_specs=pl.BlockSpec((1, tq, hg * HEAD_DIM), lambda bb, gg, ii: (bb, ii, gg)),
        out_shape=jax.ShapeDtypeStruct((b, t, NSA_Q_W), _BF16),
        scratch_shapes=[pltpu.VMEM((nt, LANES, tq), _F32)],
        compiler_params=_cparams(("arbitrary", "arbitrary", "arbitrary")),
        name="nsa_attention",
    )(proj, kvc, kvc, proj, vst, proj, vwt, small, wt, dt, ovt, ext)


def _fox_cum_kernel(s_ref, b_ref, col_ref, row_ref):
    z = s_ref[0] + b_ref[...]
    lf = (jnp.minimum(z, 0.0) - jnp.log1p(jnp.exp(-jnp.abs(z)))) * LOG2E
    x = lf.T
    t = x.shape[1]
    lane = lax.broadcasted_iota(jnp.int32, x.shape, 1)
    sh = 1
    while sh < t:
        x = x + jnp.where(lane >= sh, pltpu.roll(x, sh, 1), 0.0)
        sh *= 2
    row_ref[0] = x[8:16, :]
    col_ref[0] = x.T


def _fox_cumsum(small, b_f):
    b, t, _ = small.shape
    bvec = jnp.zeros((1, LANES), _F32).at[0, FOXF_LANE:FOXF_LANE + FOX_HEADS].set(b_f.astype(_F32))
    return pl.pallas_call(
        _fox_cum_kernel,
        grid=(b,),
        in_specs=[pl.BlockSpec((1, t, LANES), lambda bb: (bb, 0, 1)),
                  pl.BlockSpec((1, LANES), lambda bb: (0, 0))],
        out_specs=[pl.BlockSpec((1, t, LANES), lambda bb: (bb, 0, 0)),
                   pl.BlockSpec((1, 8, t), lambda bb: (bb, 0, 0))],
        out_shape=[jax.ShapeDtypeStruct((b, t, LANES), _F32), jax.ShapeDtypeStruct((b, 8, t), _F32)],
        compiler_params=_cparams(("arbitrary",)),
        name="fox_cumsum",
    )(small, bvec)


def _fox_kernel(q_ref, k_ref, vt_ref, cc_ref, cr_ref, o_ref):
    i = pl.program_id(1)
    tq = FOX_TQ
    key = lax.broadcasted_iota(jnp.int32, (tq, tq), 0)
    qry = lax.broadcasted_iota(jnp.int32, (tq, tq), 1)
    heads = range(FOX_HEADS)
    hsl = [slice(h * HEAD_DIM, (h + 1) * HEAD_DIM) for h in heads]
    qts = [q_ref[0, :, hsl[h]].astype(_F32).T.astype(_BF16) for h in heads]
    cqs = [cr_ref[0, FOXF_LANE - 8 + h:FOXF_LANE - 8 + h + 1, :] for h in heads]

    def logits(h, kt):
        ks = pl.ds(pl.multiple_of(kt * tq, tq), tq)
        ck = cc_ref[0, ks, FOXF_LANE + h:FOXF_LANE + h + 1]
        return _dot(k_ref[0, ks, hsl[h]], qts[h]) + (cqs[h] - ck)

    def body(kt, sts):
        ss = [logits(h, kt) for h in heads]
        return tuple(_t_update(sts[h], [(ss[h], None, vt_ref[0, h, kt])]) for h in heads)

    sts = lax.fori_loop(0, i, body, tuple(_t_init(tq) for _ in heads))
    ss = [logits(h, i) for h in heads]
    for h in heads:
        st = _t_update(sts[h], [(ss[h], key <= qry, vt_ref[0, h, i])])
        o_ref[0, :, hsl[h]] = _t_finish(st).T.astype(o_ref.dtype)


def _fox_attention(proj, cum_col, cum_row):
    b, t, _ = proj.shape
    tq = FOX_TQ
    w = FOX_W
    vt = _tile_transposed(proj, CB_FOXV, FOX_HEADS, tq)
    return pl.pallas_call(
        _fox_kernel,
        grid=(b, t // tq),
        in_specs=[
            pl.BlockSpec((1, tq, w), lambda bb, ii: (bb, ii, CB_FOXQ * LANES // w)),
            pl.BlockSpec((1, t, w), lambda bb, ii: (bb, 0, CB_FOXK * LANES // w)),
            pl.BlockSpec((1, FOX_HEADS, t // tq, HEAD_DIM, tq), lambda bb, ii: (bb, 0, 0, 0, 0)),
            pl.BlockSpec((1, t, LANES), lambda bb, ii: (bb, 0, 0)),
            pl.BlockSpec((1, 8, tq), lambda bb, ii: (bb, 0, ii)),
        ],
        out_specs=pl.BlockSpec((1, tq, w), lambda bb, ii: (bb, ii, 0)),
        out_shape=jax.ShapeDtypeStruct((b, t, w), _BF16),
        compiler_params=_cparams(("arbitrary", "arbitrary")),
        name="fox_attention",
    )(proj, proj, vt, cum_col, cum_row)


def _mem_kernel(q_ref, kv_ref, o_ref):
    for h in range(MEM_HEADS):
        hs = slice(h * HEAD_DIM, (h + 1) * HEAD_DIM)
        vs = slice(MEM_W + h * HEAD_DIM, MEM_W + (h + 1) * HEAD_DIM)
        s = _dot_nt(q_ref[0, :, hs], kv_ref[0, :, hs])
        e = jnp.exp2(s - jnp.max(s, axis=1, keepdims=True))
        p = e / jnp.sum(e, axis=1, keepdims=True)
        o_ref[0, :, hs] = _dot(p.astype(_BF16), kv_ref[0, :, vs]).astype(o_ref.dtype)


def _mem_attention(proj, memkv):
    b, t, _ = proj.shape
    m = memkv.shape[1]
    tq = MEM_TQ
    return pl.pallas_call(
        _mem_kernel,
        grid=(b, t // tq),
        in_specs=[pl.BlockSpec((1, tq, MEM_W), lambda bb, ii: (bb, ii, CB_MEMQ * LANES // MEM_W)),
                  pl.BlockSpec((1, m, 2 * MEM_W), lambda bb, ii: (bb, 0, 0))],
        out_specs=pl.BlockSpec((1, tq, MEM_W), lambda bb, ii: (bb, ii, 0)),
        out_shape=jax.ShapeDtypeStruct((b, t, MEM_W), _BF16),
        compiler_params=_cparams(("arbitrary", "arbitrary")),
        name="mem_attention",
    )(proj, memkv)


def _merge_kernel(x_ref, on_ref, of_ref, om_ref, g0_ref, g1_ref, g2_ref, wn_ref, wf_ref, wm_ref, wo_ref,
                  o_ref, z_ref):
    tn = 512
    for c in range(D_MODEL // tn):
        cs = slice(c * tn, (c + 1) * tn)
        z = (jax.nn.sigmoid(g0_ref[:, cs].astype(_F32)) * _dot(on_ref[...], wn_ref[:, cs])
             + jax.nn.sigmoid(g1_ref[:, cs].astype(_F32)) * _dot(of_ref[...], wf_ref[:, cs])
             + jax.nn.sigmoid(g2_ref[:, cs].astype(_F32)) * _dot(om_ref[...], wm_ref[:, cs]))
        z_ref[:, cs] = z.astype(_BF16)
    o_ref[...] = x_ref[...] + _dot(z_ref[...], wo_ref[...])


def _merge(x2d, o_nsa, o_fox, o_mem, proj2d, w_o_nsa, w_o_fox, w_o_mem, w_out):
    n, d = x2d.shape
    tm = MERGE_TM
    gcb = CB_MERGE * LANES // d
    const = lambda r, c: pl.BlockSpec((r, c), lambda i: (0, 0))
    return pl.pallas_call(
        _merge_kernel,
        grid=(n // tm,),
        in_specs=[
            pl.BlockSpec((tm, d), lambda i: (i, 0)),
            pl.BlockSpec((tm, NSA_Q_W), lambda i: (i, 0)),
            pl.BlockSpec((tm, FOX_W), lambda i: (i, 0)),
            pl.BlockSpec((tm, MEM_W), lambda i: (i, 0)),
            pl.BlockSpec((tm, d), lambda i: (i, gcb)),
            pl.BlockSpec((tm, d), lambda i: (i, gcb + 1)),
            pl.BlockSpec((tm, d), lambda i: (i, gcb + 2)),
            const(NSA_Q_W, d), const(FOX_W, d), const(MEM_W, d), const(d, d),
        ],
        out_specs=pl.BlockSpec((tm, d), lambda i: (i, 0)),
        out_shape=jax.ShapeDtypeStruct((n, d), _F32),
        scratch_shapes=[pltpu.VMEM((tm, d), _BF16)],
        compiler_params=_cparams(("arbitrary",)),
        name="merge_out",
    )(x2d, o_nsa, o_fox, o_mem, proj2d, proj2d, proj2d,
      w_o_nsa.astype(_BF16), w_o_fox.astype(_BF16), w_o_mem.astype(_BF16), w_out.astype(_BF16))


def _nsa_inputs(x, norm_attn_g, w_in, nsa_pe_k, nsa_w_ck, nsa_pe_v, nsa_w_cv, nsa_q_g, nsa_k_g,
                fox_q_g, fox_k_g, mem_q_g):
    b, t, d = x.shape
    ones = lambda k: jnp.ones((k * HEAD_DIM,), _F32)
    zeros = lambda k: jnp.zeros((k * HEAD_DIM,), _F32)
    tile = lambda gv, k: jnp.tile(gv.astype(_F32), k)
    qs = ATTN_SCALE * LOG2E

    g_end = NSA_Q_W + 6 * NSA_KV_W
    f_off = g_end + NSA_GATE_W + 3 * FOX_W
    m_off = f_off + FOX_HEADS + MEM_W
    w_all = _reorder_cast(w_in, ((m_off, MERGE_W), (0, g_end), (g_end + NSA_GATE_W, 3 * FOX_W),
                                 (f_off + FOX_HEADS, MEM_W)))
    gates = w_in[:, g_end:g_end + NSA_GATE_W].reshape(d, 3, NSA_KV_HEADS, NSA_GROUP)
    pad = jnp.zeros((d, LANES - 3 * NSA_GROUP), w_in.dtype)
    w_small = jnp.concatenate([
        gates[:, :, 0, :].reshape(d, 3 * NSA_GROUP), pad,
        gates[:, :, 1, :].reshape(d, 3 * NSA_GROUP), w_in[:, f_off:f_off + FOX_HEADS],
        pad[:, :LANES - 3 * NSA_GROUP - FOX_HEADS]], axis=1).astype(_BF16)
    col_gain = jnp.concatenate([
        jnp.ones((MERGE_W,), _F32),
        tile(nsa_q_g, NSA_HEADS) * qs, ones(2), ones(2), tile(nsa_k_g, 2), ones(2), tile(nsa_k_g, 2), ones(2),
        tile(fox_q_g, FOX_HEADS) * qs, tile(fox_k_g, FOX_HEADS), ones(FOX_HEADS),
        tile(mem_q_g, MEM_HEADS) * qs])
    col_flag = jnp.concatenate([
        jnp.zeros((MERGE_W,), _F32),
        ones(NSA_HEADS), zeros(2), zeros(2), ones(2), zeros(2), ones(2), zeros(2),
        ones(FOX_HEADS), ones(FOX_HEADS), zeros(FOX_HEADS), ones(MEM_HEADS)])

    x2d = x.reshape(b * t, d)
    proj2d, small2d = _rms_project(x2d, norm_attn_g, w_all, col_gain, col_flag, CB_ATT * LANES, w_small,
                                   PROJ_TM, PROJ_TN)
    proj = proj2d.reshape(b, t, PROJ_COLS)
    small = small2d.reshape(b, t, SMALL_W)

    cmp_in = proj[:, :, CB_KCMP * LANES:(CB_KCMP + 4) * LANES].reshape(b, t, 4, HEAD_DIM).transpose(0, 2, 1, 3)
    kvc = _nsa_compress(cmp_in, nsa_w_ck, nsa_w_cv, nsa_pe_k, nsa_pe_v, nsa_k_g)
    return proj, small, kvc


def _attention_mixers(x, mem, norm_attn_g, w_in, nsa_pe_k, nsa_w_ck, nsa_pe_v, nsa_w_cv, nsa_q_g, nsa_k_g,
                      rel_bias, fox_b_f, fox_q_g, fox_k_g, norm_mem_g, w_mem_kv, mem_q_g, mem_k_g):
    b, t, d = x.shape
    ones = lambda k: jnp.ones((k * HEAD_DIM,), _F32)
    zeros = lambda k: jnp.zeros((k * HEAD_DIM,), _F32)
    tile = lambda gv, k: jnp.tile(gv.astype(_F32), k)
    proj, small, kvc = _nsa_inputs(x, norm_attn_g, w_in, nsa_pe_k, nsa_w_ck, nsa_pe_v, nsa_w_cv, nsa_q_g, nsa_k_g,
                                   fox_q_g, fox_k_g, mem_q_g)
    proj2d = proj.reshape(b * t, PROJ_COLS)
    o_nsa = _nsa_attention(proj, small, kvc, rel_bias)

    cum_col, cum_row = _fox_cumsum(small, fox_b_f)
    o_fox = _fox_attention(proj, cum_col, cum_row)

    mm = mem.shape[1]
    mem_gain = jnp.concatenate([tile(mem_k_g, MEM_HEADS), ones(MEM_HEADS)])
    mem_flag = jnp.concatenate([ones(MEM_HEADS), zeros(MEM_HEADS)])
    memkv = _rms_project(mem.reshape(b * mm, d), norm_mem_g, w_mem_kv.astype(_BF16), mem_gain, mem_flag,
                         0, None, mm, MEM_W).reshape(b, mm, 2 * MEM_W)
    o_mem = _mem_attention(proj, memkv)
    return proj2d, o_nsa.reshape(b * t, NSA_Q_W), o_fox.reshape(b * t, FOX_W), o_mem.reshape(b * t, MEM_W)


def _router_kernel(x_ref, g_ref, whi_ref, wlo_ref, b_ref, h_ref, idx_ref, gate_ref, cnt_ref):
    x = x_ref[...]
    ms = jnp.mean(x * x, axis=-1, keepdims=True)
    h = x * lax.rsqrt(ms + RMS_EPS) * g_ref[...]
    h_hi = h.astype(_BF16)
    h_ref[...] = _to_token_major(h_hi)
    h_lo = (h - h_hi.astype(_F32)).astype(_BF16)
    logits = _dot_nt(whi_ref[...], h_hi) + _dot_nt(whi_ref[...], h_lo) + _dot_nt(wlo_ref[...], h_hi)
    scores = jax.nn.sigmoid(logits)
    sb = scores + b_ref[...]
    eidx = lax.broadcasted_iota(jnp.int32, sb.shape, 0)
    idxs, vals = [], []
    for _ in range(TOP_K):
        m = jnp.max(sb, axis=0, keepdims=True)
        idx = jnp.min(jnp.where(sb == m, eidx, N_EXPERTS), axis=0, keepdims=True)
        hit = eidx == idx
        vals.append(jnp.sum(jnp.where(hit, scores, 0.0), axis=0, keepdims=True))
        idxs.append(idx)
        sb = jnp.where(hit, NEG_INF, sb)
    top_s = jnp.concatenate(vals, axis=0)
    idx_ref[...] = jnp.concatenate(idxs, axis=0)
    gate_ref[...] = top_s / jnp.sum(top_s, axis=0, keepdims=True) * ROUTED_SCALE

    @pl.when(pl.program_id(0) == 0)
    def _():
        cnt_ref[...] = jnp.zeros(cnt_ref.shape, cnt_ref.dtype)

    picked = jnp.where(sb < 0.5 * NEG_INF, 1.0, 0.0)
    cnt_ref[...] += jnp.sum(picked, axis=1, keepdims=True)


def _router(x1, norm_g, w_router, router_bias):
    n, d = x1.shape
    tm = ROUTER_TM
    wt = w_router.astype(_F32).T
    w_hi = wt.astype(_BF16)
    w_lo = (wt - w_hi.astype(_F32)).astype(_BF16)
    const = lambda r, c: pl.BlockSpec((r, c), lambda i: (0, 0))
    return pl.pallas_call(
        _router_kernel,
        grid=(n // tm,),
        in_specs=[pl.BlockSpec((tm, d), lambda i: (i, 0)), const(1, d), const(N_EXPERTS, d), const(N_EXPERTS, d),
                  const(N_EXPERTS, 1)],
        out_specs=[pl.BlockSpec((tm, d // LANES, LANES), lambda i: (i, 0, 0)),
                   pl.BlockSpec((TOP_K, tm), lambda i: (0, i)),
                   pl.BlockSpec((TOP_K, tm), lambda i: (0, i)),
                   const(N_EXPERTS, 1)],
        out_shape=[jax.ShapeDtypeStruct((n, d // LANES, LANES), _BF16), jax.ShapeDtypeStruct((TOP_K, n), jnp.int32),
                   jax.ShapeDtypeStruct((TOP_K, n), _F32), jax.ShapeDtypeStruct((N_EXPERTS, 1), _F32)],
        compiler_params=_cparams(("arbitrary",)),
        name="moe_router",
    )(x1, norm_g.reshape(1, d).astype(_F32), w_hi, w_lo, router_bias.reshape(N_EXPERTS, 1).astype(_F32))


def _moe_kernel(be_ref, nbr_ref, src_hbm, dst_hbm, h_hbm, sg_ref, wg_ref, wu_ref, wd_ref, y_hbm,
                xbuf, ybuf, xmat, sidx, didx, wg_bf, wu_bf, wd_bf, gsem, ssem, isem):
    n = pl.program_id(0)
    nbr = nbr_ref[0]
    last = nbr - 1
    slot = n % 2
    other = 1 - slot
    blk = MOE_BLOCK
    dump_block = dst_hbm.shape[0] - 1
    ring = didx.shape[0]

    def src_copy(block, s):
        return pltpu.make_async_copy(src_hbm.at[block], sidx.at[s], isem.at[0, s])

    def dst_copy(block, s3):
        return pltpu.make_async_copy(dst_hbm.at[block], didx.at[s3], isem.at[1, s3])

    def issue_gather(s):
        for r in range(blk):
            tok = sidx[s, 0, r]
            pltpu.make_async_copy(h_hbm.at[tok], xbuf.at[s, r], gsem.at[s]).start(priority=r % 2)

    def issue_scatter(s, s3):
        for r in range(blk):
            row = didx[s3, 0, r]
            pltpu.make_async_copy(ybuf.at[s, r], y_hbm.at[row], ssem.at[s]).start(priority=r % 2)

    def wait_rows(buf, sem, s):
        pltpu.make_async_copy(buf.at[s], buf.at[s], sem.at[s]).wait()

    @pl.when(n < nbr)
    def _():
        @pl.when(n == 0)
        def _():
            src_copy(0, 0).start()
            dst_copy(0, 0).start()
            dst_copy(dump_block, ring - 1).start()
            ybuf[1] = jnp.zeros(ybuf.shape[1:], ybuf.dtype)
            src_copy(0, 0).wait()
            dst_copy(0, 0).wait()
            dst_copy(dump_block, ring - 1).wait()
            issue_gather(0)
            nxt0 = jnp.minimum(1, last)
            src_copy(nxt0, 1).start()
            dst_copy(nxt0, 1).start()

        wait_rows(xbuf, gsem, slot)

        @pl.when(n < last)
        def _():
            nn = jnp.minimum(n + 2, last)
            src_copy(nn, slot).start()
            dst_copy(nn, (n + 2) % ring).start()

        changed = (n == 0) | (be_ref[n] != be_ref[jnp.maximum(n - 1, 0)])

        @pl.when(changed)
        def _():
            wg_bf[...] = wg_ref[0].astype(_BF16)
            wu_bf[...] = wu_ref[0].astype(_BF16)
            wd_bf[...] = wd_ref[0].astype(_BF16)

        @pl.when(n >= 1)
        def _():
            wait_rows(ybuf, ssem, slot)

        nxt = jnp.minimum(n + 1, last)
        src_copy(nxt, other).wait()
        dst_copy(nxt, (n + 1) % ring).wait()

        issue_scatter(other, (n + ring - 1) % ring)
        xmat[...] = _from_token_major(xbuf[slot])
        issue_gather(other)
        gcol = jnp.broadcast_to(sg_ref[0], (LANES, blk)).T
        a = jax.nn.silu(_dot(xmat[...], wg_bf[...])) * _dot(xmat[...], wu_bf[...])
        a = (a * _lanes([gcol] * (a.shape[1] // LANES))).astype(_BF16)
        ybuf[slot] = _to_token_major(_dot(a, wd_bf[...]).astype(_BF16))

        @pl.when(n == last)
        def _():
            issue_scatter(slot, n % ring)
            wait_rows(xbuf, gsem, other)
            wait_rows(ybuf, ssem, other)
            wait_rows(ybuf, ssem, slot)


def _moe_experts(h3, block_expert, nb_real, slot_src, slot_dst, slot_gate, we_gate, we_up, we_down):
    n, c, _ = h3.shape
    d = c * LANES
    nb = block_expert.shape[0]
    blk = MOE_BLOCK
    e, _, de = we_gate.shape
    grid_spec = pltpu.PrefetchScalarGridSpec(
        num_scalar_prefetch=2,
        grid=(nb,),
        in_specs=[
            pl.BlockSpec(memory_space=pl.ANY),
            pl.BlockSpec(memory_space=pl.ANY),
            pl.BlockSpec(memory_space=pl.ANY),
            pl.BlockSpec((1, 1, blk), lambda i, be, nbr: (i, 0, 0)),
            pl.BlockSpec((1, d, de), lambda i, be, nbr: (be[i], 0, 0)),
            pl.BlockSpec((1, d, de), lambda i, be, nbr: (be[i], 0, 0)),
            pl.BlockSpec((1, de, d), lambda i, be, nbr: (be[i], 0, 0)),
        ],
        out_specs=pl.BlockSpec(memory_space=pl.ANY),
        scratch_shapes=[
            pltpu.VMEM((2, blk, c, LANES), _BF16), pltpu.VMEM((2, blk, c, LANES), _BF16),
            pltpu.VMEM((blk, d), _BF16),
            pltpu.SMEM((2, 1, blk), jnp.int32), pltpu.SMEM((4, 1, blk), jnp.int32),
            pltpu.VMEM((d, de), _BF16), pltpu.VMEM((d, de), _BF16), pltpu.VMEM((de, d), _BF16),
            pltpu.SemaphoreType.DMA((2,)), pltpu.SemaphoreType.DMA((2,)), pltpu.SemaphoreType.DMA((2, 4)),
        ],
    )
    return pl.pallas_call(
        _moe_kernel,
        grid_spec=grid_spec,
        out_shape=jax.ShapeDtypeStruct((TOP_K * n + blk, c, LANES), _BF16),
        compiler_params=_cparams(("arbitrary",)),
        name="moe_experts",
    )(block_expert, nb_real, slot_src, slot_dst, h3, slot_gate, we_gate, we_up, we_down)


def _dispatch_plan(top_idx, gates, counts, n):
    blk = MOE_BLOCK
    a = n * TOP_K
    nb = (a + N_EXPERTS * (blk - 1)) // blk
    order = jnp.argsort(top_idx.reshape(a)).astype(jnp.int32)
    counts = counts.astype(jnp.int32)
    padded = (counts + blk - 1) // blk * blk
    start = jnp.cumsum(counts) - counts
    pstart = jnp.cumsum(padded) - padded
    block_end = jnp.cumsum(padded) // blk
    blocks = jnp.arange(nb, dtype=jnp.int32)
    block_expert = jnp.minimum(jnp.sum((block_end[None, :] <= blocks[:, None]).astype(jnp.int32), axis=1),
                               N_EXPERTS - 1)
    off = blocks * blk - pstart[block_expert]
    base = start[block_expert] + off
    lane = jnp.arange(blk, dtype=jnp.int32)[None, :]
    rows = jnp.take(order, base[:, None] + lane, mode="clip")
    real = (off[:, None] + lane) < counts[block_expert][:, None]
    tok = rows // TOP_K
    kk = rows - tok * TOP_K
    slot_src = jnp.where(real, tok, 0)
    slot_dst = jnp.where(real, kk * n + tok, TOP_K * n + lane)
    slot_dst = jnp.concatenate([slot_dst, TOP_K * n + lane], axis=0)
    slot_gate = jnp.where(real, jnp.take(gates.reshape(a), rows, mode="clip"), 0.0).reshape(nb, 1, blk)
    nb_real = (jnp.sum(padded) // blk).reshape(1).astype(jnp.int32)
    return block_expert, nb_real, slot_src.reshape(nb, 1, blk), slot_dst.reshape(nb + 1, 1, blk), slot_gate


def _combine_kernel(x_ref, h_ref, wsg_ref, wsu_ref, wsd_ref, *rest):
    y_refs, o_ref = rest[:TOP_K], rest[TOP_K]
    h = _from_token_major(h_ref[...])
    a = (jax.nn.silu(_dot(h, wsg_ref[...])) * _dot(h, wsu_ref[...])).astype(_BF16)
    routed = y_refs[0][...].astype(_F32)
    for k in range(1, TOP_K):
        routed = routed + y_refs[k][...].astype(_F32)
    o_ref[...] = x_ref[...] + _dot(a, wsd_ref[...]) + _from_token_major(routed)


def _combine(x1, h3, y3, ws_gate, ws_up, ws_down):
    n, d = x1.shape
    c = d // LANES
    tm = COMBINE_TM
    nt = n // tm
    de = ws_gate.shape[1]
    const = lambda r, cc: pl.BlockSpec((r, cc), lambda i: (0, 0))
    row = pl.BlockSpec((tm, d), lambda i: (i, 0))
    y_specs = [pl.BlockSpec((tm, c, LANES), functools.partial(lambda i, k: (k * nt + i, 0, 0), k=k))
               for k in range(TOP_K)]
    return pl.pallas_call(
        _combine_kernel,
        grid=(nt,),
        in_specs=[row, pl.BlockSpec((tm, c, LANES), lambda i: (i, 0, 0)), const(d, de), const(d, de), const(de, d)]
        + y_specs,
        out_specs=row,
        out_shape=jax.ShapeDtypeStruct((n, d), _F32),
        compiler_params=_cparams(("arbitrary",)),
        name="moe_combine",
    )(x1, h3, ws_gate.astype(_BF16), ws_up.astype(_BF16), ws_down.astype(_BF16), *([y3] * TOP_K))


def _moe_block(x1, norm_ffn_g, w_router, router_bias, we_gate, we_up, we_down, ws_gate, ws_up, ws_down):
    n = x1.shape[0]
    h3, top_idx_t, gates_t, counts = _router(x1, norm_ffn_g, w_router, router_bias)
    block_expert, nb_real, slot_src, slot_dst, slot_gate = _dispatch_plan(
        top_idx_t.T, gates_t.T, counts.reshape(N_EXPERTS), n)
    y3 = _moe_experts(h3, block_expert, nb_real, slot_src, slot_dst, slot_gate, we_gate, we_up, we_down)
    return _combine(x1, h3, y3, ws_gate, ws_up, ws_down)


def kernel(x, mem, norm_attn_g, w_in, nsa_pe_k, nsa_w_ck, nsa_pe_v, nsa_w_cv, nsa_q_g, nsa_k_g, rel_bias, fox_b_f, fox_q_g, fox_k_g, norm_mem_g, w_mem_kv, mem_q_g, mem_k_g, w_o_nsa, w_o_fox, w_o_mem, w_out, norm_ffn_g, w_router, router_bias, we_gate, we_up, we_down, ws_gate, ws_up, ws_down):
    b, t, d = x.shape
    assert norm_attn_g.shape[0] == 1, "single-layer problem"
    l = 0
    proj2d, o_nsa, o_fox, o_mem = _attention_mixers(
        x, mem, norm_attn_g[l], w_in[l], nsa_pe_k[l], nsa_w_ck[l], nsa_pe_v[l], nsa_w_cv[l], nsa_q_g[l],
        nsa_k_g[l], rel_bias, fox_b_f[l], fox_q_g[l], fox_k_g[l], norm_mem_g[l], w_mem_kv[l], mem_q_g[l],
        mem_k_g[l])
    x1 = _merge(x.reshape(b * t, d), o_nsa, o_fox, o_mem, proj2d, w_o_nsa[l], w_o_fox[l], w_o_mem[l], w_out[l])
    out = _moe_block(x1, norm_ffn_g[l], w_router[l], router_bias[l], we_gate[l], we_up[l], we_down[l],
                     ws_gate[l], ws_up[l], ws_down[l])
    return out.reshape(b, t, d)
```

```python
import functools
import math

import jax
import jax.numpy as jnp
import numpy as np
from jax import lax
from jax.experimental import pallas as pl
from jax.experimental.pallas import tpu as pltpu
from jax.experimental.pallas import tpu_sc as plsc

D_MODEL = 2048
HEAD_DIM = 128
NSA_HEADS = 8
NSA_KV_HEADS = 2
NSA_GROUP = NSA_HEADS // NSA_KV_HEADS
FOX_HEADS = 4
MEM_HEADS = 4
CMP_LEN = 32
CMP_STRIDE = 16
SLC_LEN = 64
N_SELECT = 16
WINDOW = 512
NUM_BUCKETS = 32
MAX_DISTANCE = 128
N_BRANCHES = 3
N_EXPERTS = 64
TOP_K = 8
D_EXPERT = 512
ROUTED_SCALE = 2.5
ATTN_SCALE = HEAD_DIM ** -0.5
NEG_INF = -1e30
FORCE_SCORE = 1e4
RMS_EPS = 1e-6
LOG2E = math.log2(math.e)

NSA_Q_W = NSA_HEADS * HEAD_DIM
NSA_KV_W = NSA_KV_HEADS * HEAD_DIM
NSA_GATE_W = 3 * NSA_HEADS
FOX_W = FOX_HEADS * HEAD_DIM
MEM_W = MEM_HEADS * HEAD_DIM
MERGE_W = N_BRANCHES * D_MODEL

LANES = 128
VMEM_LIMIT_BYTES = 56 * 1024 * 1024

PROJ_TM = 1024
PROJ_TN = 768
ATT_TQ = 128
FAR_GROUP = 4
FOX_TQ = 256
MEM_TQ = 512
MERGE_TM = 256
ROUTER_TM = 512
MOE_BLOCK = 256
COMBINE_TM = 256

CB_MERGE = 0
CB_ATT = N_BRANCHES * D_MODEL // LANES
CB_QNSA = CB_ATT
CB_KCMP = CB_ATT + 8
CB_VCMP = CB_ATT + 10
CB_KSLC = CB_ATT + 12
CB_VSLC = CB_ATT + 14
CB_KWIN = CB_ATT + 16
CB_VWIN = CB_ATT + 18
CB_FOXQ = CB_ATT + 20
CB_FOXK = CB_ATT + 24
CB_FOXV = CB_ATT + 28
CB_MEMQ = CB_ATT + 32
PROJ_COLS = (CB_ATT + 36) * LANES
SMALL_W = 2 * LANES
FOXF_LANE = 12

_BF16 = jnp.bfloat16
_F32 = jnp.float32


def _cparams(sem):
    return pltpu.CompilerParams(dimension_semantics=sem, vmem_limit_bytes=VMEM_LIMIT_BYTES)


def _dot(a, b):
    return jnp.dot(a, b, preferred_element_type=_F32)


def _dot_nt(a, b):
    return lax.dot_general(a, b, (((1,), (1,)), ((), ())), preferred_element_type=_F32)


def _lanes(parts):
    return jnp.concatenate(parts, axis=1)


def _to_token_major(x):
    c = x.shape[1] // LANES
    chunks = jnp.stack([x[:, j * LANES:(j + 1) * LANES] for j in range(c)], axis=0)
    return pltpu.einshape("ctl->tcl", chunks)


def _from_token_major(x3):
    xt = pltpu.einshape("tcl->ctl", x3)
    return _lanes([xt[j] for j in range(x3.shape[1])])


REORDER_TR = 128
REORDER_TC = 512


def _reorder_kernel(pieces, w_ref, o_ref):
    off = 0
    for src, width in pieces:
        for c in range(0, width, REORDER_TC):
            o_ref[:, off + c:off + c + REORDER_TC] = w_ref[:, src + c:src + c + REORDER_TC].astype(o_ref.dtype)
        off += width


def _reorder_cast(w, pieces):
    rows, cols = w.shape
    total = sum(width for _, width in pieces)
    assert rows % REORDER_TR == 0 and all(width % REORDER_TC == 0 for _, width in pieces)
    return pl.pallas_call(
        functools.partial(_reorder_kernel, pieces),
        grid=(rows // REORDER_TR,),
        in_specs=[pl.BlockSpec((REORDER_TR, cols), lambda i: (i, 0))],
        out_specs=pl.BlockSpec((REORDER_TR, total), lambda i: (i, 0)),
        out_shape=jax.ShapeDtypeStruct((rows, total), _BF16),
        compiler_params=_cparams(("arbitrary",)),
        name="reorder_cast",
    )(w)


def _proj_kernel(norm_j0, has_small, x_ref, g_ref, w_ref, cg_ref, cf_ref, *rest):
    if has_small:
        ws_ref, o_ref, os_ref, h_ref = rest
    else:
        o_ref, h_ref = rest
    j = pl.program_id(1)

    @pl.when(j == 0)
    def _():
        x = x_ref[...]
        ms = jnp.mean(x * x, axis=-1, keepdims=True)
        h = (x * lax.rsqrt(ms + RMS_EPS) * g_ref[...]).astype(_BF16)
        h_ref[...] = h
        if has_small:
            os_ref[...] = _dot(h, ws_ref[...])

    y = _dot(h_ref[...], w_ref[...])
    tn = y.shape[1]

    @pl.when(j >= norm_j0)
    def _():
        for c in range(tn // LANES):
            sl = slice(c * LANES, (c + 1) * LANES)
            yh = y[:, sl]
            ms = jnp.mean(yh * yh, axis=-1, keepdims=True)
            scale = jnp.where(cf_ref[:, sl] > 0.0, lax.rsqrt(ms + RMS_EPS), 1.0)
            o_ref[:, sl] = (yh * scale * cg_ref[:, sl]).astype(o_ref.dtype)

    @pl.when(j < norm_j0)
    def _():
        o_ref[...] = y.astype(o_ref.dtype)


def _rms_project(x2d, g, w, col_gain, col_flag, n_plain_cols, w_small, tm, tn):
    n, d = x2d.shape
    c = w.shape[1]
    has_small = w_small is not None
    in_specs = [
        pl.BlockSpec((tm, d), lambda i, j: (i, 0)),
        pl.BlockSpec((1, d), lambda i, j: (0, 0)),
        pl.BlockSpec((d, tn), lambda i, j: (0, j)),
        pl.BlockSpec((1, tn), lambda i, j: (0, j)),
        pl.BlockSpec((1, tn), lambda i, j: (0, j)),
    ]
    args = [x2d, g.reshape(1, d), w, col_gain.reshape(1, c), col_flag.reshape(1, c)]
    out_shape = [jax.ShapeDtypeStruct((n, c), _BF16)]
    out_specs = [pl.BlockSpec((tm, tn), lambda i, j: (i, j))]
    if has_small:
        ws = w_small.shape[1]
        in_specs.append(pl.BlockSpec((d, ws), lambda i, j: (0, 0)))
        args.append(w_small)
        out_shape.append(jax.ShapeDtypeStruct((n, ws), _F32))
        out_specs.append(pl.BlockSpec((tm, ws), lambda i, j: (i, 0)))
    assert n % tm == 0 and c % tn == 0 and n_plain_cols % tn == 0
    res = pl.pallas_call(
        functools.partial(_proj_kernel, n_plain_cols // tn, has_small),
        grid=(n // tm, c // tn),
        in_specs=in_specs,
        out_specs=out_specs,
        out_shape=out_shape,
        scratch_shapes=[pltpu.VMEM((tm, d), _BF16)],
        compiler_params=_cparams(("arbitrary", "arbitrary")),
        name="rms_project",
    )(*args)
    return res if has_small else res[0]


def _cmp_kernel(x_ref, wlo_ref, whi_ref, pelo_ref, pehi_ref, kg_ref, o_ref):
    j = pl.program_id(1)
    x = x_ref[0, 0]
    nchunk = x.shape[0]
    ylo = _dot(x, wlo_ref[0])
    yhi = _dot(x, whi_ref[0])
    pe = _dot(pelo_ref[0], wlo_ref[0]) + _dot(pehi_ref[0], whi_ref[0])
    y = ylo + pltpu.roll(yhi, nchunk - 1, 0) + pe[0:1, :]

    @pl.when(j < NSA_KV_HEADS)
    def _():
        ms = jnp.mean(y * y, axis=-1, keepdims=True)
        o_ref[0, 0] = (y * lax.rsqrt(ms + RMS_EPS) * kg_ref[...]).astype(o_ref.dtype)

    @pl.when(j >= NSA_KV_HEADS)
    def _():
        o_ref[0, 0] = y.T.astype(o_ref.dtype)


def _nsa_compress(cmp_in, w_ck, w_cv, pe_k, pe_v, k_g):
    b, nj, t, dk = cmp_in.shape
    nchunk = t // CMP_STRIDE
    half = CMP_LEN // 2
    assert nchunk == dk
    x = cmp_in.reshape(b, nj, nchunk, CMP_STRIDE * dk)

    def halves(w):
        return (w[:half].reshape(half * dk, dk).astype(_BF16),
                w[half:].reshape(half * dk, dk).astype(_BF16))

    klo, khi = halves(w_ck)
    vlo, vhi = halves(w_cv)
    wlo = jnp.stack([klo, vlo])
    whi = jnp.stack([khi, vhi])

    def pe_halves(pe):
        lo = jnp.broadcast_to(pe[:half].reshape(1, half * dk), (8, half * dk)).astype(_BF16)
        hi = jnp.broadcast_to(pe[half:].reshape(1, half * dk), (8, half * dk)).astype(_BF16)
        return lo, hi

    pklo, pkhi = pe_halves(pe_k)
    pvlo, pvhi = pe_halves(pe_v)
    pelo = jnp.stack([pklo, pvlo])
    pehi = jnp.stack([pkhi, pvhi])
    kv = lambda bb, j: (j // NSA_KV_HEADS, 0, 0)
    return pl.pallas_call(
        _cmp_kernel,
        grid=(b, nj),
        in_specs=[
            pl.BlockSpec((1, 1, nchunk, CMP_STRIDE * dk), lambda bb, j: (bb, j, 0, 0)),
            pl.BlockSpec((1, half * dk, dk), kv),
            pl.BlockSpec((1, half * dk, dk), kv),
            pl.BlockSpec((1, 8, half * dk), kv),
            pl.BlockSpec((1, 8, half * dk), kv),
            pl.BlockSpec((1, dk), lambda bb, j: (0, 0)),
        ],
        out_specs=pl.BlockSpec((1, 1, nchunk, dk), lambda bb, j: (bb, j, 0, 0)),
        out_shape=jax.ShapeDtypeStruct((b, nj, nchunk, dk), _BF16),
        compiler_params=_cparams(("arbitrary", "arbitrary")),
        name="nsa_compress",
    )(x, wlo, whi, pelo, pehi, k_g.reshape(1, dk).astype(_F32))


def _t_update(state, tiles):
    m, l, acc = state
    masked = [s if mask is None else jnp.where(mask, s, NEG_INF) for s, mask, _ in tiles]
    m_new = m
    for sm in masked:
        m_new = jnp.maximum(m_new, jnp.max(sm, axis=0, keepdims=True))
    alpha = jnp.exp2(m - m_new)
    l_new = alpha * l
    acc_new = alpha * acc
    for sm, (_, _, vt) in zip(masked, tiles):
        e = jnp.exp2(sm - m_new)
        l_new = l_new + jnp.sum(e, axis=0, keepdims=True)
        acc_new = acc_new + _dot(vt, e.astype(_BF16))
    return m_new, l_new, acc_new


def _t_init(nq):
    return (jnp.full((1, nq), NEG_INF, _F32), jnp.zeros((1, nq), _F32), jnp.zeros((HEAD_DIM, nq), _F32))


def _t_finish(state):
    _, l, acc = state
    return acc / l


def _nsa_kernel(q_ref, kc_ref, vct_ref, ks_ref, vst_ref, kw_ref, vwt_ref, gl_ref,
                wt_ref, dt_ref, ovt_ref, ext_ref, o_ref, sel_ref):
    i = pl.program_id(2)
    tq = ATT_TQ
    hg = NSA_GROUP
    nq = hg * tq
    n_slc = ovt_ref.shape[0]

    q = q_ref[0]
    qt = _lanes([q[:, h * HEAD_DIM:(h + 1) * HEAD_DIM].astype(_F32).T for h in range(hg)]).astype(_BF16)

    key = lax.broadcasted_iota(jnp.int32, (LANES, nq), 0)
    qry = lax.broadcasted_iota(jnp.int32, (LANES, nq), 1) & (tq - 1)

    def ktile(ref, kt):
        return ref[0, pl.ds(pl.multiple_of(kt * LANES, LANES), LANES), :]

    def near_bias(d):
        return _lanes([dt_ref[h, d] for h in range(hg)])

    woff = pl.multiple_of(wt_ref.shape[1] - LANES - 8 - (LANES // CMP_STRIDE) * i, 8)
    s = _dot(kc_ref[0, 0], qt) + _lanes([wt_ref[h, pl.ds(woff, LANES), :] for h in range(hg)])
    n_win = WINDOW // tq
    win_tiles = []
    for d in range(n_win, -1, -1):
        ktd = jnp.maximum(i - d, 0)
        sc = _dot(ktile(kw_ref, ktd), qt)
        if d <= 1:
            sc = sc + near_bias(d)
        if d == n_win:
            mk = (qry < key) & (i >= d)
        elif d == 0:
            mk = key <= qry
        else:
            mk = jnp.broadcast_to(i >= d, (LANES, nq))
        win_tiles.append((sc, mk, vwt_ref[0, 0, ktd]))

    mask_c = (i * tq + qry) >= (CMP_STRIDE * key + CMP_LEN - 1)
    mx = jnp.max(jnp.where(mask_c, s, NEG_INF), axis=0, keepdims=True)
    e = jnp.where(mask_c, jnp.exp2(s - mx), 0.0)
    l = jnp.sum(e, axis=0, keepdims=True)
    p_c = e / jnp.where(l > 0.0, l, 1.0)
    o_c = _dot(vct_ref[0, 0], p_c.astype(_BF16))

    ps = p_c[:, 0:tq]
    for h in range(1, hg):
        ps = ps + p_c[:, h * tq:(h + 1) * tq]
    ps_hi = ps.astype(_BF16)
    ps_lo = (ps - ps_hi.astype(_F32)).astype(_BF16)
    imp = _dot(ovt_ref[...], ps_hi) + _dot(ovt_ref[...], ps_lo)
    blk = lax.broadcasted_iota(jnp.int32, (n_slc, tq), 0)
    tpos = i * tq + lax.broadcasted_iota(jnp.int32, (n_slc, tq), 1)
    cur = tpos // SLC_LEN
    valid = blk * SLC_LEN <= tpos
    forced = (blk == 0) | (blk == cur) | (blk == cur - 1)
    score = jnp.where(valid, jnp.where(forced, FORCE_SCORE, imp), -1.0)
    rank = jnp.zeros((n_slc, tq), _F32)
    for k in range(n_slc):
        ck = score[k:k + 1, :]
        beats = (ck > score) | ((ck == score) & (blk > k))
        rank = rank + jnp.where(beats, 1.0, 0.0)
    sel = jnp.where(rank < float(min(N_SELECT, n_slc)), 1.0, 0.0).astype(_BF16)
    selfull = _dot(ext_ref[...], sel)
    for kt in range(ext_ref.shape[0] // LANES):
        sel_ref[kt] = selfull[kt * LANES:(kt + 1) * LANES, :]

    def sel_mask(kt, ok=True):
        return _lanes([sel_ref[kt]] * hg) > jnp.where(ok, 0.5, 2.0)

    o_w = _t_finish(_t_update(_t_init(nq), win_tiles))

    n_far = jnp.maximum(i - 1, 0)

    def far_body(j, st):
        tiles = []
        for u in range(FAR_GROUP):
            kt = FAR_GROUP * j + u
            ktc = jnp.minimum(kt, n_far - 1)
            tiles.append((_dot(ktile(ks_ref, ktc), qt), sel_mask(ktc, kt < n_far), vst_ref[0, 0, ktc]))
        return _t_update(st, tiles)

    st = lax.fori_loop(0, (n_far + FAR_GROUP - 1) // FAR_GROUP, far_body, _t_init(nq))
    kt1 = jnp.maximum(i - 1, 0)
    st = _t_update(st, [
        (_dot(ktile(ks_ref, kt1), qt) + near_bias(1), sel_mask(kt1, i >= 1), vst_ref[0, 0, kt1]),
        (_dot(ktile(ks_ref, i), qt) + near_bias(0), sel_mask(i) & (key <= qry), vst_ref[0, 0, i])])
    o_s = _t_finish(st)

    glt = jax.nn.sigmoid(gl_ref[0]).T
    grow = lambda br: _lanes([glt[br * hg + h:br * hg + h + 1, :] for h in range(hg)])
    o = grow(0) * o_c + grow(1) * o_s + grow(2) * o_w
    for h in range(hg):
        o_ref[0, :, h * HEAD_DIM:(h + 1) * HEAD_DIM] = o[:, h * tq:(h + 1) * tq].T.astype(o_ref.dtype)


def _t5_bucket(rel):
    n = np.maximum(rel, 0)
    max_exact = NUM_BUCKETS // 2
    ratio = np.maximum(n, 1).astype(np.float32) / np.float32(max_exact)
    log_ratio = np.log(ratio) / np.float32(math.log(MAX_DISTANCE / max_exact))
    large = np.minimum(max_exact + (log_ratio * np.float32(NUM_BUCKETS - max_exact)).astype(np.int32),
                       NUM_BUCKETS - 1)
    return np.where(n < max_exact, n, large).astype(np.int32)


def _bias_lookup(rb, bucket):
    bk = jnp.asarray(bucket.astype(np.int8))[None]
    ex = (slice(None),) + (None,) * bucket.ndim
    tab = jnp.broadcast_to(rb[:, 0][ex], (rb.shape[0],) + bucket.shape)
    for k in range(1, NUM_BUCKETS):
        tab = jnp.where(bk == k, rb[:, k][ex], tab)
    return tab


def _nsa_tables(rel_bias):
    rb = rel_bias.astype(_F32).T * LOG2E
    r = np.arange(ATT_TQ)
    rel_d = (np.arange(2) * ATT_TQ)[:, None, None] + r[None, None, :] - r[None, :, None]
    assert _t5_bucket(np.array([ATT_TQ + 1]))[0] == NUM_BUCKETS - 1
    u = np.arange(2 * LANES)
    rel_w = r[None, :] - CMP_STRIDE * (u[:, None] - (LANES - 8)) - (CMP_LEN - 1)
    far = rb[:, NUM_BUCKETS - 1]
    dt = _bias_lookup(rb, _t5_bucket(rel_d)) - far[:, None, None, None]
    return dt, _bias_lookup(rb, _t5_bucket(rel_w))


def _tile_transposed(v, cb, n_heads, tk):
    b, t, _ = v.shape
    x = v[:, :, cb * LANES:(cb + n_heads) * LANES].reshape(b, t // tk, tk, n_heads, HEAD_DIM)
    return x.transpose(0, 3, 1, 4, 2)


def _nsa_attention(proj, small, kvc, rel_bias):
    b, t, _ = proj.shape
    g, hg, tq = NSA_KV_HEADS, NSA_GROUP, ATT_TQ
    n_slc = t // SLC_LEN
    n_cmp = (t - CMP_LEN) // CMP_STRIDE + 1
    nt = t // LANES
    assert n_cmp <= LANES and kvc.shape[2] == LANES and n_slc % 16 == 0 and tq == LANES
    assert LANES - 8 - (LANES // CMP_STRIDE) * (t // tq - 1) >= 0
    dt, wt = _nsa_tables(rel_bias)
    cstart = np.arange(LANES) * CMP_STRIDE
    sstart = np.arange(n_slc) * SLC_LEN
    overlap = np.clip(np.minimum(cstart[None, :] + CMP_LEN, sstart[:, None] + SLC_LEN)
                      - np.maximum(cstart[None, :], sstart[:, None]), 0, None).astype(np.float32) / CMP_LEN
    overlap[:, n_cmp:] = 0.0
    ovt = jnp.asarray(overlap, _BF16)
    ext = jnp.asarray(np.arange(t)[:, None] // SLC_LEN == np.arange(n_slc)[None, :], _BF16)
    vst = _tile_transposed(proj, CB_VSLC, g, LANES)
    vwt = _tile_transposed(proj, CB_VWIN, g, LANES)

    seq = lambda cb: pl.BlockSpec((1, t, LANES), lambda bb, gg, ii: (bb, 0, cb + gg))
    vts = pl.BlockSpec((1, 1, nt, HEAD_DIM, LANES), lambda bb, gg, ii: (bb, gg, 0, 0, 0))
    return pl.pallas_call(
        _nsa_kernel,
        grid=(b, g, t // tq),
        in_specs=[
            pl.BlockSpec((1, tq, hg * HEAD_DIM), lambda bb, gg, ii: (bb, ii, CB_QNSA * LANES // (hg * HEAD_DIM) + gg)),
            pl.BlockSpec((1, 1, LANES, HEAD_DIM), lambda bb, gg, ii: (bb, gg, 0, 0)),
            pl.BlockSpec((1, 1, HEAD_DIM, LANES), lambda bb, gg, ii: (bb, NSA_KV_HEADS + gg, 0, 0)),
            seq(CB_KSLC), vts, seq(CB_KWIN), vts,
            pl.BlockSpec((1, tq, LANES), lambda bb, gg, ii: (bb, ii, gg)),
            pl.BlockSpec((hg, 2 * LANES, tq), lambda bb, gg, ii: (gg, 0, 0)),
            pl.BlockSpec((hg, 2, LANES, tq), lambda bb, gg, ii: (gg, 0, 0, 0)),
            pl.BlockSpec((n_slc, LANES), lambda bb, gg, ii: (0, 0)),
            pl.BlockSpec((t, n_slc), lambda bb, gg, ii: (0, 0)),
        ],
        out_specs=pl.BlockSpec((1, tq, hg * HEAD_DIM), lambda bb, gg, ii: (bb, ii, gg)),
        out_shape=jax.ShapeDtypeStruct((b, t, NSA_Q_W), _BF16),
        scratch_shapes=[pltpu.VMEM((nt, LANES, tq), _F32)],
        compiler_params=_cparams(("arbitrary", "arbitrary", "arbitrary")),
        name="nsa_attention",
    )(proj, kvc, kvc, proj, vst, proj, vwt, small, wt, dt, ovt, ext)


def _fox_cum_kernel(s_ref, b_ref, col_ref, row_ref):
    z = s_ref[0] + b_ref[...]
    lf = (jnp.minimum(z, 0.0) - jnp.log1p(jnp.exp(-jnp.abs(z)))) * LOG2E
    x = lf.T
    t = x.shape[1]
    lane = lax.broadcasted_iota(jnp.int32, x.shape, 1)
    sh = 1
    while sh < t:
        x = x + jnp.where(lane >= sh, pltpu.roll(x, sh, 1), 0.0)
        sh *= 2
    row_ref[0] = x[8:16, :]
    col_ref[0] = x.T


def _fox_cumsum(small, b_f):
    b, t, _ = small.shape
    bvec = jnp.zeros((1, LANES), _F32).at[0, FOXF_LANE:FOXF_LANE + FOX_HEADS].set(b_f.astype(_F32))
    return pl.pallas_call(
        _fox_cum_kernel,
        grid=(b,),
        in_specs=[pl.BlockSpec((1, t, LANES), lambda bb: (bb, 0, 1)),
                  pl.BlockSpec((1, LANES), lambda bb: (0, 0))],
        out_specs=[pl.BlockSpec((1, t, LANES), lambda bb: (bb, 0, 0)),
                   pl.BlockSpec((1, 8, t), lambda bb: (bb, 0, 0))],
        out_shape=[jax.ShapeDtypeStruct((b, t, LANES), _F32), jax.ShapeDtypeStruct((b, 8, t), _F32)],
        compiler_params=_cparams(("arbitrary",)),
        name="fox_cumsum",
    )(small, bvec)


def _fox_kernel(q_ref, k_ref, vt_ref, cc_ref, cr_ref, o_ref):
    i = pl.program_id(1)
    tq = FOX_TQ
    key = lax.broadcasted_iota(jnp.int32, (tq, tq), 0)
    qry = lax.broadcasted_iota(jnp.int32, (tq, tq), 1)
    heads = range(FOX_HEADS)
    hsl = [slice(h * HEAD_DIM, (h + 1) * HEAD_DIM) for h in heads]
    qts = [q_ref[0, :, hsl[h]].astype(_F32).T.astype(_BF16) for h in heads]
    cqs = [cr_ref[0, FOXF_LANE - 8 + h:FOXF_LANE - 8 + h + 1, :] for h in heads]

    def logits(h, kt):
        ks = pl.ds(pl.multiple_of(kt * tq, tq), tq)
        ck = cc_ref[0, ks, FOXF_LANE + h:FOXF_LANE + h + 1]
        return _dot(k_ref[0, ks, hsl[h]], qts[h]) + (cqs[h] - ck)

    def body(kt, sts):
        ss = [logits(h, kt) for h in heads]
        return tuple(_t_update(sts[h], [(ss[h], None, vt_ref[0, h, kt])]) for h in heads)

    sts = lax.fori_loop(0, i, body, tuple(_t_init(tq) for _ in heads))
    ss = [logits(h, i) for h in heads]
    for h in heads:
        st = _t_update(sts[h], [(ss[h], key <= qry, vt_ref[0, h, i])])
        o_ref[0, :, hsl[h]] = _t_finish(st).T.astype(o_ref.dtype)


def _fox_attention(proj, cum_col, cum_row):
    b, t, _ = proj.shape
    tq = FOX_TQ
    w = FOX_W
    vt = _tile_transposed(proj, CB_FOXV, FOX_HEADS, tq)
    return pl.pallas_call(
        _fox_kernel,
        grid=(b, t // tq),
        in_specs=[
            pl.BlockSpec((1, tq, w), lambda bb, ii: (bb, ii, CB_FOXQ * LANES // w)),
            pl.BlockSpec((1, t, w), lambda bb, ii: (bb, 0, CB_FOXK * LANES // w)),
            pl.BlockSpec((1, FOX_HEADS, t // tq, HEAD_DIM, tq), lambda bb, ii: (bb, 0, 0, 0, 0)),
            pl.BlockSpec((1, t, LANES), lambda bb, ii: (bb, 0, 0)),
            pl.BlockSpec((1, 8, tq), lambda bb, ii: (bb, 0, ii)),
        ],
        out_specs=pl.BlockSpec((1, tq, w), lambda bb, ii: (bb, ii, 0)),
        out_shape=jax.ShapeDtypeStruct((b, t, w), _BF16),
        compiler_params=_cparams(("arbitrary", "arbitrary")),
        name="fox_attention",
    )(proj, proj, vt, cum_col, cum_row)


def _mem_kernel(q_ref, kv_ref, o_ref):
    for h in range(MEM_HEADS):
        hs = slice(h * HEAD_DIM, (h + 1) * HEAD_DIM)
        vs = slice(MEM_W + h * HEAD_DIM, MEM_W + (h + 1) * HEAD_DIM)
        s = _dot_nt(q_ref[0, :, hs], kv_ref[0, :, hs])
        e = jnp.exp2(s - jnp.max(s, axis=1, keepdims=True))
        p = e / jnp.sum(e, axis=1, keepdims=True)
        o_ref[0, :, hs] = _dot(p.astype(_BF16), kv_ref[0, :, vs]).astype(o_ref.dtype)


def _mem_attention(proj, memkv):
    b, t, _ = proj.shape
    m = memkv.shape[1]
    tq = MEM_TQ
    return pl.pallas_call(
        _mem_kernel,
        grid=(b, t // tq),
        in_specs=[pl.BlockSpec((1, tq, MEM_W), lambda bb, ii: (bb, ii, CB_MEMQ * LANES // MEM_W)),
                  pl.BlockSpec((1, m, 2 * MEM_W), lambda bb, ii: (bb, 0, 0))],
        out_specs=pl.BlockSpec((1, tq, MEM_W), lambda bb, ii: (bb, ii, 0)),
        out_shape=jax.ShapeDtypeStruct((b, t, MEM_W), _BF16),
        compiler_params=_cparams(("arbitrary", "arbitrary")),
        name="mem_attention",
    )(proj, memkv)


def _merge_kernel(x_ref, on_ref, of_ref, om_ref, g0_ref, g1_ref, g2_ref, wn_ref, wf_ref, wm_ref, wo_ref,
                  o_ref, z_ref):
    tn = 512
    for c in range(D_MODEL // tn):
        cs = slice(c * tn, (c + 1) * tn)
        z = (jax.nn.sigmoid(g0_ref[:, cs].astype(_F32)) * _dot(on_ref[...], wn_ref[:, cs])
             + jax.nn.sigmoid(g1_ref[:, cs].astype(_F32)) * _dot(of_ref[...], wf_ref[:, cs])
             + jax.nn.sigmoid(g2_ref[:, cs].astype(_F32)) * _dot(om_ref[...], wm_ref[:, cs]))
        z_ref[:, cs] = z.astype(_BF16)
    o_ref[...] = x_ref[...] + _dot(z_ref[...], wo_ref[...])


def _merge(x2d, o_nsa, o_fox, o_mem, proj2d, w_o_nsa, w_o_fox, w_o_mem, w_out):
    n, d = x2d.shape
    tm = MERGE_TM
    gcb = CB_MERGE * LANES // d
    const = lambda r, c: pl.BlockSpec((r, c), lambda i: (0, 0))
    return pl.pallas_call(
        _merge_kernel,
        grid=(n // tm,),
        in_specs=[
            pl.BlockSpec((tm, d), lambda i: (i, 0)),
            pl.BlockSpec((tm, NSA_Q_W), lambda i: (i, 0)),
            pl.BlockSpec((tm, FOX_W), lambda i: (i, 0)),
            pl.BlockSpec((tm, MEM_W), lambda i: (i, 0)),
            pl.BlockSpec((tm, d), lambda i: (i, gcb)),
            pl.BlockSpec((tm, d), lambda i: (i, gcb + 1)),
            pl.BlockSpec((tm, d), lambda i: (i, gcb + 2)),
            const(NSA_Q_W, d), const(FOX_W, d), const(MEM_W, d), const(d, d),
        ],
        out_specs=pl.BlockSpec((tm, d), lambda i: (i, 0)),
        out_shape=jax.ShapeDtypeStruct((n, d), _F32),
        scratch_shapes=[pltpu.VMEM((tm, d), _BF16)],
        compiler_params=_cparams(("arbitrary",)),
        name="merge_out",
    )(x2d, o_nsa, o_fox, o_mem, proj2d, proj2d, proj2d,
      w_o_nsa.astype(_BF16), w_o_fox.astype(_BF16), w_o_mem.astype(_BF16), w_out.astype(_BF16))


def _nsa_inputs(x, norm_attn_g, w_in, nsa_pe_k, nsa_w_ck, nsa_pe_v, nsa_w_cv, nsa_q_g, nsa_k_g,
                fox_q_g, fox_k_g, mem_q_g):
    b, t, d = x.shape
    ones = lambda k: jnp.ones((k * HEAD_DIM,), _F32)
    zeros = lambda k: jnp.zeros((k * HEAD_DIM,), _F32)
    tile = lambda gv, k: jnp.tile(gv.astype(_F32), k)
    qs = ATTN_SCALE * LOG2E

    g_end = NSA_Q_W + 6 * NSA_KV_W
    f_off = g_end + NSA_GATE_W + 3 * FOX_W
    m_off = f_off + FOX_HEADS + MEM_W
    w_all = _reorder_cast(w_in, ((m_off, MERGE_W), (0, g_end), (g_end + NSA_GATE_W, 3 * FOX_W),
                                 (f_off + FOX_HEADS, MEM_W)))
    gates = w_in[:, g_end:g_end + NSA_GATE_W].reshape(d, 3, NSA_KV_HEADS, NSA_GROUP)
    pad = jnp.zeros((d, LANES - 3 * NSA_GROUP), w_in.dtype)
    w_small = jnp.concatenate([
        gates[:, :, 0, :].reshape(d, 3 * NSA_GROUP), pad,
        gates[:, :, 1, :].reshape(d, 3 * NSA_GROUP), w_in[:, f_off:f_off + FOX_HEADS],
        pad[:, :LANES - 3 * NSA_GROUP - FOX_HEADS]], axis=1).astype(_BF16)
    col_gain = jnp.concatenate([
        jnp.ones((MERGE_W,), _F32),
        tile(nsa_q_g, NSA_HEADS) * qs, ones(2), ones(2), tile(nsa_k_g, 2), ones(2), tile(nsa_k_g, 2), ones(2),
        tile(fox_q_g, FOX_HEADS) * qs, tile(fox_k_g, FOX_HEADS), ones(FOX_HEADS),
        tile(mem_q_g, MEM_HEADS) * qs])
    col_flag = jnp.concatenate([
        jnp.zeros((MERGE_W,), _F32),
        ones(NSA_HEADS), zeros(2), zeros(2), ones(2), zeros(2), ones(2), zeros(2),
        ones(FOX_HEADS), ones(FOX_HEADS), zeros(FOX_HEADS), ones(MEM_HEADS)])

    x2d = x.reshape(b * t, d)
    proj2d, small2d = _rms_project(x2d, norm_attn_g, w_all, col_gain, col_flag, CB_ATT * LANES, w_small,
                                   PROJ_TM, PROJ_TN)
    proj = proj2d.reshape(b, t, PROJ_COLS)
    small = small2d.reshape(b, t, SMALL_W)

    cmp_in = proj[:, :, CB_KCMP * LANES:(CB_KCMP + 4) * LANES].reshape(b, t, 4, HEAD_DIM).transpose(0, 2, 1, 3)
    kvc = _nsa_compress(cmp_in, nsa_w_ck, nsa_w_cv, nsa_pe_k, nsa_pe_v, nsa_k_g)
    return proj, small, kvc


def _attention_mixers(x, mem, norm_attn_g, w_in, nsa_pe_k, nsa_w_ck, nsa_pe_v, nsa_w_cv, nsa_q_g, nsa_k_g,
                      rel_bias, fox_b_f, fox_q_g, fox_k_g, norm_mem_g, w_mem_kv, mem_q_g, mem_k_g):
    b, t, d = x.shape
    ones = lambda k: jnp.ones((k * HEAD_DIM,), _F32)
    zeros = lambda k: jnp.zeros((k * HEAD_DIM,), _F32)
    tile = lambda gv, k: jnp.tile(gv.astype(_F32), k)
    proj, small, kvc = _nsa_inputs(x, norm_attn_g, w_in, nsa_pe_k, nsa_w_ck, nsa_pe_v, nsa_w_cv, nsa_q_g, nsa_k_g,
                                   fox_q_g, fox_k_g, mem_q_g)
    proj2d = proj.reshape(b * t, PROJ_COLS)
    o_nsa = _nsa_attention(proj, small, kvc, rel_bias)

    cum_col, cum_row = _fox_cumsum(small, fox_b_f)
    o_fox = _fox_attention(proj, cum_col, cum_row)

    mm = mem.shape[1]
    mem_gain = jnp.concatenate([tile(mem_k_g, MEM_HEADS), ones(MEM_HEADS)])
    mem_flag = jnp.concatenate([ones(MEM_HEADS), zeros(MEM_HEADS)])
    memkv = _rms_project(mem.reshape(b * mm, d), norm_mem_g, w_mem_kv.astype(_BF16), mem_gain, mem_flag,
                         0, None, mm, MEM_W).reshape(b, mm, 2 * MEM_W)
    o_mem = _mem_attention(proj, memkv)
    return proj2d, o_nsa.reshape(b * t, NSA_Q_W), o_fox.reshape(b * t, FOX_W), o_mem.reshape(b * t, MEM_W)


def _router_kernel(x_ref, g_ref, whi_ref, wlo_ref, b_ref, h_ref, idx_ref, gate_ref, cnt_ref):
    x = x_ref[...]
    ms = jnp.mean(x * x, axis=-1, keepdims=True)
    h = x * lax.rsqrt(ms + RMS_EPS) * g_ref[...]
    h_hi = h.astype(_BF16)
    h_ref[...] = pltpu.bitcast(_to_token_major(h_hi), jnp.int32)
    h_lo = (h - h_hi.astype(_F32)).astype(_BF16)
    logits = _dot_nt(whi_ref[...], h_hi) + _dot_nt(whi_ref[...], h_lo) + _dot_nt(wlo_ref[...], h_hi)
    scores = jax.nn.sigmoid(logits)
    sb = scores + b_ref[...]
    eidx = lax.broadcasted_iota(jnp.int32, sb.shape, 0)
    idxs, vals = [], []
    for _ in range(TOP_K):
        m = jnp.max(sb, axis=0, keepdims=True)
        idx = jnp.min(jnp.where(sb == m, eidx, N_EXPERTS), axis=0, keepdims=True)
        hit = eidx == idx
        vals.append(jnp.sum(jnp.where(hit, scores, 0.0), axis=0, keepdims=True))
        idxs.append(idx)
        sb = jnp.where(hit, NEG_INF, sb)
    top_s = jnp.concatenate(vals, axis=0)
    idx_ref[...] = jnp.concatenate(idxs, axis=0)
    gate_ref[...] = top_s / jnp.sum(top_s, axis=0, keepdims=True) * ROUTED_SCALE

    @pl.when(pl.program_id(0) == 0)
    def _():
        cnt_ref[...] = jnp.zeros(cnt_ref.shape, cnt_ref.dtype)

    picked = jnp.where(sb < 0.5 * NEG_INF, 1.0, 0.0)
    cnt_ref[...] += jnp.sum(picked, axis=1, keepdims=True)


def _router(x1, norm_g, w_router, router_bias):
    n, d = x1.shape
    tm = ROUTER_TM
    wt = w_router.astype(_F32).T
    w_hi = wt.astype(_BF16)
    w_lo = (wt - w_hi.astype(_F32)).astype(_BF16)
    const = lambda r, c: pl.BlockSpec((r, c), lambda i: (0, 0))
    return pl.pallas_call(
        _router_kernel,
        grid=(n // tm,),
        in_specs=[pl.BlockSpec((tm, d), lambda i: (i, 0)), const(1, d), const(N_EXPERTS, d), const(N_EXPERTS, d),
                  const(N_EXPERTS, 1)],
        out_specs=[pl.BlockSpec((tm, d // LANES // 2, LANES), lambda i: (i, 0, 0)),
                   pl.BlockSpec((TOP_K, tm), lambda i: (0, i)),
                   pl.BlockSpec((TOP_K, tm), lambda i: (0, i)),
                   const(N_EXPERTS, 1)],
        out_shape=[jax.ShapeDtypeStruct((n, d // LANES // 2, LANES), jnp.int32),
                   jax.ShapeDtypeStruct((TOP_K, n), jnp.int32),
                   jax.ShapeDtypeStruct((TOP_K, n), _F32), jax.ShapeDtypeStruct((N_EXPERTS, 1), _F32)],
        compiler_params=_cparams(("arbitrary",)),
        name="moe_router",
    )(x1, norm_g.reshape(1, d).astype(_F32), w_hi, w_lo, router_bias.reshape(N_EXPERTS, 1).astype(_F32))


SC_CORES = 2
SC_SUBCORES = 16
SC_CHUNK = 40
SC_DEPTH = 3


def _sc_row_gather(table, idx):
    v, c, _ = table.shape
    b = idx.shape[0]
    nw = SC_CORES * SC_SUBCORES
    per_w = b // nw
    assert b % nw == 0 and per_w % SC_CHUNK == 0 and SC_CHUNK % 8 == 0
    n_chunks = per_w // SC_CHUNK
    n_groups, tail = divmod(n_chunks, SC_DEPTH)
    mesh = plsc.VectorSubcoreMesh(core_axis_name="c", subcore_axis_name="s")

    def body(table_hbm, idx_hbm, out_hbm, idx_v, rows_v, gsem, wsem):
        wid = lax.axis_index("s") * SC_CORES + lax.axis_index("c")
        base = wid * per_w
        pltpu.sync_copy(idx_hbm.at[pl.ds(base, per_w)], idx_v)

        def group(first, count):
            offs = [pl.multiple_of((first + u) * SC_CHUNK, 8) for u in range(count)]
            gathers = [pltpu.async_copy(table_hbm.at[idx_v.at[pl.ds(offs[u], SC_CHUNK)]], rows_v.at[u], gsem)
                       for u in range(count)]
            for cp in gathers:
                cp.wait()
            writes = [pltpu.async_copy(rows_v.at[u], out_hbm.at[pl.ds(base + offs[u], SC_CHUNK)], wsem)
                      for u in range(count)]
            for cp in writes:
                cp.wait()

        @pl.loop(0, n_groups)
        def _(g):
            group(g * SC_DEPTH, SC_DEPTH)

        if tail:
            group(n_groups * SC_DEPTH, tail)

    return pl.kernel(
        body,
        out_type=jax.ShapeDtypeStruct((b, c, LANES), table.dtype),
        mesh=mesh,
        scratch_types=[pltpu.VMEM((per_w,), jnp.int32), pltpu.VMEM((SC_DEPTH, SC_CHUNK, c, LANES), table.dtype),
                       pltpu.SemaphoreType.DMA, pltpu.SemaphoreType.DMA],
        compiler_params=pltpu.CompilerParams(use_tc_tiling_on_sc=True),
        name="sc_row_gather",
    )(table, idx)


def _moe_kernel(be_ref, nbr_ref, dst_hbm, x_ref, sg_ref, wg_ref, wu_ref, wd_ref, y_hbm,
                ybuf, didx, wg_bf, wu_bf, wd_bf, ssem, isem):
    n = pl.program_id(0)
    nbr = nbr_ref[0]
    last = nbr - 1
    slot = n % 2
    other = 1 - slot
    blk = MOE_BLOCK
    dump_block = dst_hbm.shape[0] - 1
    ring = didx.shape[0]

    def dst_copy(block, s3):
        return pltpu.make_async_copy(dst_hbm.at[block], didx.at[s3], isem.at[s3])

    def issue_scatter(s, s3):
        for r in range(blk):
            row = didx[s3, 0, r]
            pltpu.make_async_copy(ybuf.at[s, r], y_hbm.at[row], ssem.at[s]).start(priority=r % 2)

    def wait_rows(s):
        pltpu.make_async_copy(ybuf.at[s], ybuf.at[s], ssem.at[s]).wait()

    @pl.when(n < nbr)
    def _():
        @pl.when(n == 0)
        def _():
            dst_copy(0, 0).start()
            dst_copy(dump_block, ring - 1).start()
            dst_copy(jnp.minimum(1, last), 1).start()
            ybuf[1] = jnp.zeros(ybuf.shape[1:], ybuf.dtype)
            dst_copy(0, 0).wait()
            dst_copy(dump_block, ring - 1).wait()

        @pl.when(n < last)
        def _():
            dst_copy(jnp.minimum(n + 2, last), (n + 2) % ring).start()

        changed = (n == 0) | (be_ref[n] != be_ref[jnp.maximum(n - 1, 0)])

        @pl.when(changed)
        def _():
            wg_bf[...] = wg_ref[0].astype(_BF16)
            wu_bf[...] = wu_ref[0].astype(_BF16)
            wd_bf[...] = wd_ref[0].astype(_BF16)

        @pl.when(n >= 1)
        def _():
            wait_rows(slot)

        dst_copy(jnp.minimum(n + 1, last), (n + 1) % ring).wait()

        issue_scatter(other, (n + ring - 1) % ring)
        x = _from_token_major(pltpu.bitcast(x_ref[...], _BF16))
        gcol = jnp.broadcast_to(sg_ref[0], (LANES, blk)).T
        a = jax.nn.silu(_dot(x, wg_bf[...])) * _dot(x, wu_bf[...])
        a = (a * _lanes([gcol] * (a.shape[1] // LANES))).astype(_BF16)
        ybuf[slot] = _to_token_major(_dot(a, wd_bf[...]).astype(_BF16))

        @pl.when(n == last)
        def _():
            issue_scatter(slot, n % ring)
            wait_rows(other)
            wait_rows(slot)


def _moe_experts(x_sorted, block_expert, nb_real, slot_dst, slot_gate, we_gate, we_up, we_down, n):
    rows, c2, _ = x_sorted.shape
    c = 2 * c2
    d = c * LANES
    nb = block_expert.shape[0]
    blk = MOE_BLOCK
    assert rows == nb * blk
    e, _, de = we_gate.shape
    grid_spec = pltpu.PrefetchScalarGridSpec(
        num_scalar_prefetch=2,
        grid=(nb,),
        in_specs=[
            pl.BlockSpec(memory_space=pl.ANY),
            pl.BlockSpec((blk, c2, LANES), lambda i, be, nbr: (jnp.minimum(i, nbr[0] - 1), 0, 0)),
            pl.BlockSpec((1, 1, blk), lambda i, be, nbr: (i, 0, 0)),
            pl.BlockSpec((1, d, de), lambda i, be, nbr: (be[i], 0, 0)),
            pl.BlockSpec((1, d, de), lambda i, be, nbr: (be[i], 0, 0)),
            pl.BlockSpec((1, de, d), lambda i, be, nbr: (be[i], 0, 0)),
        ],
        out_specs=pl.BlockSpec(memory_space=pl.ANY),
        scratch_shapes=[
            pltpu.VMEM((2, blk, c, LANES), _BF16),
            pltpu.SMEM((4, 1, blk), jnp.int32),
            pltpu.VMEM((d, de), _BF16), pltpu.VMEM((d, de), _BF16), pltpu.VMEM((de, d), _BF16),
            pltpu.SemaphoreType.DMA((2,)), pltpu.SemaphoreType.DMA((4,)),
        ],
    )
    return pl.pallas_call(
        _moe_kernel,
        grid_spec=grid_spec,
        out_shape=jax.ShapeDtypeStruct((TOP_K * n + blk, c, LANES), _BF16),
        compiler_params=_cparams(("arbitrary",)),
        name="moe_experts",
    )(block_expert, nb_real, slot_dst, x_sorted, slot_gate, we_gate, we_up, we_down)


def _dispatch_plan(top_idx, gates, counts, n):
    blk = MOE_BLOCK
    a = n * TOP_K
    nb = (a + N_EXPERTS * (blk - 1)) // blk
    order = jnp.argsort(top_idx.reshape(a)).astype(jnp.int32)
    counts = counts.astype(jnp.int32)
    padded = (counts + blk - 1) // blk * blk
    start = jnp.cumsum(counts) - counts
    pstart = jnp.cumsum(padded) - padded
    block_end = jnp.cumsum(padded) // blk
    blocks = jnp.arange(nb, dtype=jnp.int32)
    block_expert = jnp.minimum(jnp.sum((block_end[None, :] <= blocks[:, None]).astype(jnp.int32), axis=1),
                               N_EXPERTS - 1)
    off = blocks * blk - pstart[block_expert]
    base = start[block_expert] + off
    lane = jnp.arange(blk, dtype=jnp.int32)[None, :]
    rows = jnp.take(order, base[:, None] + lane, mode="clip")
    real = (off[:, None] + lane) < counts[block_expert][:, None]
    tok = rows // TOP_K
    kk = rows - tok * TOP_K
    slot_src = jnp.where(real, tok, 0)
    slot_dst = jnp.where(real, kk * n + tok, TOP_K * n + lane)
    slot_dst = jnp.concatenate([slot_dst, TOP_K * n + lane], axis=0)
    slot_gate = jnp.where(real, jnp.take(gates.reshape(a), rows, mode="clip"), 0.0).reshape(nb, 1, blk)
    nb_real = (jnp.sum(padded) // blk).reshape(1).astype(jnp.int32)
    return block_expert, nb_real, slot_src.reshape(nb, 1, blk), slot_dst.reshape(nb + 1, 1, blk), slot_gate


def _combine_kernel(x_ref, h_ref, wsg_ref, wsu_ref, wsd_ref, *rest):
    y_refs, o_ref = rest[:TOP_K], rest[TOP_K]
    h = _from_token_major(pltpu.bitcast(h_ref[...], _BF16))
    a = (jax.nn.silu(_dot(h, wsg_ref[...])) * _dot(h, wsu_ref[...])).astype(_BF16)
    routed = y_refs[0][...].astype(_F32)
    for k in range(1, TOP_K):
        routed = routed + y_refs[k][...].astype(_F32)
    o_ref[...] = x_ref[...] + _dot(a, wsd_ref[...]) + _from_token_major(routed)


def _combine(x1, h3, y3, ws_gate, ws_up, ws_down):
    n, d = x1.shape
    c = d // LANES
    tm = COMBINE_TM
    nt = n // tm
    de = ws_gate.shape[1]
    const = lambda r, cc: pl.BlockSpec((r, cc), lambda i: (0, 0))
    row = pl.BlockSpec((tm, d), lambda i: (i, 0))
    y_specs = [pl.BlockSpec((tm, c, LANES), functools.partial(lambda i, k: (k * nt + i, 0, 0), k=k))
               for k in range(TOP_K)]
    return pl.pallas_call(
        _combine_kernel,
        grid=(nt,),
        in_specs=[row, pl.BlockSpec((tm, c // 2, LANES), lambda i: (i, 0, 0)), const(d, de), const(d, de),
                  const(de, d)] + y_specs,
        out_specs=row,
        out_shape=jax.ShapeDtypeStruct((n, d), _F32),
        compiler_params=_cparams(("arbitrary",)),
        name="moe_combine",
    )(x1, h3, ws_gate.astype(_BF16), ws_up.astype(_BF16), ws_down.astype(_BF16), *([y3] * TOP_K))


def _moe_block(x1, norm_ffn_g, w_router, router_bias, we_gate, we_up, we_down, ws_gate, ws_up, ws_down):
    n = x1.shape[0]
    h3, top_idx_t, gates_t, counts = _router(x1, norm_ffn_g, w_router, router_bias)
    block_expert, nb_real, slot_src, slot_dst, slot_gate = _dispatch_plan(
        top_idx_t.T, gates_t.T, counts.reshape(N_EXPERTS), n)
    x_sorted = _sc_row_gather(h3, slot_src.reshape(-1))
    y3 = _moe_experts(x_sorted, block_expert, nb_real, slot_dst, slot_gate, we_gate, we_up, we_down, n)
    return _combine(x1, h3, y3, ws_gate, ws_up, ws_down)


def kernel(x, mem, norm_attn_g, w_in, nsa_pe_k, nsa_w_ck, nsa_pe_v, nsa_w_cv, nsa_q_g, nsa_k_g, rel_bias, fox_b_f, fox_q_g, fox_k_g, norm_mem_g, w_mem_kv, mem_q_g, mem_k_g, w_o_nsa, w_o_fox, w_o_mem, w_out, norm_ffn_g, w_router, router_bias, we_gate, we_up, we_down, ws_gate, ws_up, ws_down):
    b, t, d = x.shape
    assert norm_attn_g.shape[0] == 1, "single-layer problem"
    l = 0
    proj2d, o_nsa, o_fox, o_mem = _attention_mixers(
        x, mem, norm_attn_g[l], w_in[l], nsa_pe_k[l], nsa_w_ck[l], nsa_pe_v[l], nsa_w_cv[l], nsa_q_g[l],
        nsa_k_g[l], rel_bias, fox_b_f[l], fox_q_g[l], fox_k_g[l], norm_mem_g[l], w_mem_kv[l], mem_q_g[l],
        mem_k_g[l])
    x1 = _merge(x.reshape(b * t, d), o_nsa, o_fox, o_mem, proj2d, w_o_nsa[l], w_o_fox[l], w_o_mem[l], w_out[l])
    out = _moe_block(x1, norm_ffn_g[l], w_router[l], router_bias[l], we_gate[l], we_up[l], we_down[l],
                     ws_gate[l], ws_up[l], ws_down[l])
    return out.reshape(b, t, d)
```

```python
import functools
import math

import jax
import jax.numpy as jnp
import numpy as np
from jax import lax
from jax.experimental import pallas as pl
from jax.experimental.pallas import tpu as pltpu
from jax.experimental.pallas import tpu_sc as plsc

D_MODEL = 2048
HEAD_DIM = 128
NSA_HEADS = 8
NSA_KV_HEADS = 2
NSA_GROUP = NSA_HEADS // NSA_KV_HEADS
FOX_HEADS = 4
MEM_HEADS = 4
CMP_LEN = 32
CMP_STRIDE = 16
SLC_LEN = 64
N_SELECT = 16
WINDOW = 512
NUM_BUCKETS = 32
MAX_DISTANCE = 128
N_BRANCHES = 3
N_EXPERTS = 64
TOP_K = 8
D_EXPERT = 512
ROUTED_SCALE = 2.5
ATTN_SCALE = HEAD_DIM ** -0.5
NEG_INF = -1e30
FORCE_SCORE = 1e4
RMS_EPS = 1e-6
LOG2E = math.log2(math.e)

NSA_Q_W = NSA_HEADS * HEAD_DIM
NSA_KV_W = NSA_KV_HEADS * HEAD_DIM
NSA_GATE_W = 3 * NSA_HEADS
FOX_W = FOX_HEADS * HEAD_DIM
MEM_W = MEM_HEADS * HEAD_DIM
MERGE_W = N_BRANCHES * D_MODEL

LANES = 128
VMEM_LIMIT_BYTES = 56 * 1024 * 1024

PROJ_TM = 1024
PROJ_TN = 768
ATT_TQ = 128
FAR_GROUP = 4
FOX_TQ = 256
MEM_TQ = 512
MERGE_TM = 256
ROUTER_TM = 512
MOE_BLOCK = 256
MOE_CHUNKS = 5
SCATTER_UNROLL = 16
COMBINE_TM = 256

CB_MERGE = 0
CB_ATT = N_BRANCHES * D_MODEL // LANES
CB_QNSA = CB_ATT
CB_KCMP = CB_ATT + 8
CB_VCMP = CB_ATT + 10
CB_KSLC = CB_ATT + 12
CB_VSLC = CB_ATT + 14
CB_KWIN = CB_ATT + 16
CB_VWIN = CB_ATT + 18
CB_FOXQ = CB_ATT + 20
CB_FOXK = CB_ATT + 24
CB_FOXV = CB_ATT + 28
CB_MEMQ = CB_ATT + 32
PROJ_COLS = (CB_ATT + 36) * LANES
SMALL_W = 2 * LANES
FOXF_LANE = 12

_BF16 = jnp.bfloat16
_F32 = jnp.float32


def _cparams(sem):
    return pltpu.CompilerParams(dimension_semantics=sem, vmem_limit_bytes=VMEM_LIMIT_BYTES)


def _dot(a, b):
    return jnp.dot(a, b, preferred_element_type=_F32)


def _dot_nt(a, b):
    return lax.dot_general(a, b, (((1,), (1,)), ((), ())), preferred_element_type=_F32)


def _lanes(parts):
    return jnp.concatenate(parts, axis=1)


def _to_token_major(x):
    c = x.shape[1] // LANES
    chunks = jnp.stack([x[:, j * LANES:(j + 1) * LANES] for j in range(c)], axis=0)
    return pltpu.einshape("ctl->tcl", chunks)


def _from_token_major(x3):
    xt = pltpu.einshape("tcl->ctl", x3)
    return _lanes([xt[j] for j in range(x3.shape[1])])


REORDER_TR = 128
REORDER_TC = 512


def _reorder_kernel(pieces, w_ref, o_ref):
    off = 0
    for src, width in pieces:
        for c in range(0, width, REORDER_TC):
            o_ref[:, off + c:off + c + REORDER_TC] = w_ref[:, src + c:src + c + REORDER_TC].astype(o_ref.dtype)
        off += width


def _reorder_cast(w, pieces):
    rows, cols = w.shape
    total = sum(width for _, width in pieces)
    assert rows % REORDER_TR == 0 and all(width % REORDER_TC == 0 for _, width in pieces)
    return pl.pallas_call(
        functools.partial(_reorder_kernel, pieces),
        grid=(rows // REORDER_TR,),
        in_specs=[pl.BlockSpec((REORDER_TR, cols), lambda i: (i, 0))],
        out_specs=pl.BlockSpec((REORDER_TR, total), lambda i: (i, 0)),
        out_shape=jax.ShapeDtypeStruct((rows, total), _BF16),
        compiler_params=_cparams(("arbitrary",)),
        name="reorder_cast",
    )(w)


def _proj_kernel(norm_j0, has_small, x_ref, g_ref, w_ref, cg_ref, cf_ref, *rest):
    if has_small:
        ws_ref, o_ref, os_ref, h_ref = rest
    else:
        o_ref, h_ref = rest
    j = pl.program_id(1)

    @pl.when(j == 0)
    def _():
        x = x_ref[...]
        ms = jnp.mean(x * x, axis=-1, keepdims=True)
        h = (x * lax.rsqrt(ms + RMS_EPS) * g_ref[...]).astype(_BF16)
        h_ref[...] = h
        if has_small:
            os_ref[...] = _dot(h, ws_ref[...])

    y = _dot(h_ref[...], w_ref[...])
    tn = y.shape[1]

    @pl.when(j >= norm_j0)
    def _():
        for c in range(tn // LANES):
            sl = slice(c * LANES, (c + 1) * LANES)
            yh = y[:, sl]
            ms = jnp.mean(yh * yh, axis=-1, keepdims=True)
            scale = jnp.where(cf_ref[:, sl] > 0.0, lax.rsqrt(ms + RMS_EPS), 1.0)
            o_ref[:, sl] = (yh * scale * cg_ref[:, sl]).astype(o_ref.dtype)

    @pl.when(j < norm_j0)
    def _():
        o_ref[...] = y.astype(o_ref.dtype)


def _rms_project(x2d, g, w, col_gain, col_flag, n_plain_cols, w_small, tm, tn):
    n, d = x2d.shape
    c = w.shape[1]
    has_small = w_small is not None
    in_specs = [
        pl.BlockSpec((tm, d), lambda i, j: (i, 0)),
        pl.BlockSpec((1, d), lambda i, j: (0, 0)),
        pl.BlockSpec((d, tn), lambda i, j: (0, j)),
        pl.BlockSpec((1, tn), lambda i, j: (0, j)),
        pl.BlockSpec((1, tn), lambda i, j: (0, j)),
    ]
    args = [x2d, g.reshape(1, d), w, col_gain.reshape(1, c), col_flag.reshape(1, c)]
    out_shape = [jax.ShapeDtypeStruct((n, c), _BF16)]
    out_specs = [pl.BlockSpec((tm, tn), lambda i, j: (i, j))]
    if has_small:
        ws = w_small.shape[1]
        in_specs.append(pl.BlockSpec((d, ws), lambda i, j: (0, 0)))
        args.append(w_small)
        out_shape.append(jax.ShapeDtypeStruct((n, ws), _F32))
        out_specs.append(pl.BlockSpec((tm, ws), lambda i, j: (i, 0)))
    assert n % tm == 0 and c % tn == 0 and n_plain_cols % tn == 0
    res = pl.pallas_call(
        functools.partial(_proj_kernel, n_plain_cols // tn, has_small),
        grid=(n // tm, c // tn),
        in_specs=in_specs,
        out_specs=out_specs,
        out_shape=out_shape,
        scratch_shapes=[pltpu.VMEM((tm, d), _BF16)],
        compiler_params=_cparams(("arbitrary", "arbitrary")),
        name="rms_project",
    )(*args)
    return res if has_small else res[0]


def _cmp_kernel(x_ref, wlo_ref, whi_ref, pelo_ref, pehi_ref, kg_ref, o_ref):
    j = pl.program_id(1)
    x = x_ref[0, 0]
    nchunk = x.shape[0]
    ylo = _dot(x, wlo_ref[0])
    yhi = _dot(x, whi_ref[0])
    pe = _dot(pelo_ref[0], wlo_ref[0]) + _dot(pehi_ref[0], whi_ref[0])
    y = ylo + pltpu.roll(yhi, nchunk - 1, 0) + pe[0:1, :]

    @pl.when(j < NSA_KV_HEADS)
    def _():
        ms = jnp.mean(y * y, axis=-1, keepdims=True)
        o_ref[0, 0] = (y * lax.rsqrt(ms + RMS_EPS) * kg_ref[...]).astype(o_ref.dtype)

    @pl.when(j >= NSA_KV_HEADS)
    def _():
        o_ref[0, 0] = y.T.astype(o_ref.dtype)


def _nsa_compress(cmp_in, w_ck, w_cv, pe_k, pe_v, k_g):
    b, nj, t, dk = cmp_in.shape
    nchunk = t // CMP_STRIDE
    half = CMP_LEN // 2
    assert nchunk == dk
    x = cmp_in.reshape(b, nj, nchunk, CMP_STRIDE * dk)

    def halves(w):
        return (w[:half].reshape(half * dk, dk).astype(_BF16),
                w[half:].reshape(half * dk, dk).astype(_BF16))

    klo, khi = halves(w_ck)
    vlo, vhi = halves(w_cv)
    wlo = jnp.stack([klo, vlo])
    whi = jnp.stack([khi, vhi])

    def pe_halves(pe):
        lo = jnp.broadcast_to(pe[:half].reshape(1, half * dk), (8, half * dk)).astype(_BF16)
        hi = jnp.broadcast_to(pe[half:].reshape(1, half * dk), (8, half * dk)).astype(_BF16)
        return lo, hi

    pklo, pkhi = pe_halves(pe_k)
    pvlo, pvhi = pe_halves(pe_v)
    pelo = jnp.stack([pklo, pvlo])
    pehi = jnp.stack([pkhi, pvhi])
    kv = lambda bb, j: (j // NSA_KV_HEADS, 0, 0)
    return pl.pallas_call(
        _cmp_kernel,
        grid=(b, nj),
        in_specs=[
            pl.BlockSpec((1, 1, nchunk, CMP_STRIDE * dk), lambda bb, j: (bb, j, 0, 0)),
            pl.BlockSpec((1, half * dk, dk), kv),
            pl.BlockSpec((1, half * dk, dk), kv),
            pl.BlockSpec((1, 8, half * dk), kv),
            pl.BlockSpec((1, 8, half * dk), kv),
            pl.BlockSpec((1, dk), lambda bb, j: (0, 0)),
        ],
        out_specs=pl.BlockSpec((1, 1, nchunk, dk), lambda bb, j: (bb, j, 0, 0)),
        out_shape=jax.ShapeDtypeStruct((b, nj, nchunk, dk), _BF16),
        compiler_params=_cparams(("arbitrary", "arbitrary")),
        name="nsa_compress",
    )(x, wlo, whi, pelo, pehi, k_g.reshape(1, dk).astype(_F32))


def _t_update(state, tiles):
    m, l, acc = state
    masked = [s if mask is None else jnp.where(mask, s, NEG_INF) for s, mask, _ in tiles]
    m_new = m
    for sm in masked:
        m_new = jnp.maximum(m_new, jnp.max(sm, axis=0, keepdims=True))
    alpha = jnp.exp2(m - m_new)
    l_new = alpha * l
    acc_new = alpha * acc
    for sm, (_, _, vt) in zip(masked, tiles):
        e = jnp.exp2(sm - m_new)
        l_new = l_new + jnp.sum(e, axis=0, keepdims=True)
        acc_new = acc_new + _dot(vt, e.astype(_BF16))
    return m_new, l_new, acc_new


def _t_init(nq):
    return (jnp.full((1, nq), NEG_INF, _F32), jnp.zeros((1, nq), _F32), jnp.zeros((HEAD_DIM, nq), _F32))


def _t_finish(state):
    _, l, acc = state
    return acc / l


def _nsa_kernel(q_ref, kc_ref, vct_ref, ks_ref, vst_ref, kw_ref, vwt_ref, gl_ref,
                wt_ref, dt_ref, ovt_ref, ext_ref, o_ref, sel_ref):
    i = pl.program_id(2)
    tq = ATT_TQ
    hg = NSA_GROUP
    nq = hg * tq
    n_slc = ovt_ref.shape[0]

    q = q_ref[0]
    qt = _lanes([q[:, h * HEAD_DIM:(h + 1) * HEAD_DIM].astype(_F32).T for h in range(hg)]).astype(_BF16)

    key = lax.broadcasted_iota(jnp.int32, (LANES, nq), 0)
    qry = lax.broadcasted_iota(jnp.int32, (LANES, nq), 1) & (tq - 1)

    def ktile(ref, kt):
        return ref[0, pl.ds(pl.multiple_of(kt * LANES, LANES), LANES), :]

    def near_bias(d):
        return _lanes([dt_ref[h, d] for h in range(hg)])

    woff = pl.multiple_of(wt_ref.shape[1] - LANES - 8 - (LANES // CMP_STRIDE) * i, 8)
    s = _dot(kc_ref[0, 0], qt) + _lanes([wt_ref[h, pl.ds(woff, LANES), :] for h in range(hg)])
    n_win = WINDOW // tq
    win_tiles = []
    for d in range(n_win, -1, -1):
        ktd = jnp.maximum(i - d, 0)
        sc = _dot(ktile(kw_ref, ktd), qt)
        if d <= 1:
            sc = sc + near_bias(d)
        if d == n_win:
            mk = (qry < key) & (i >= d)
        elif d == 0:
            mk = key <= qry
        else:
            mk = jnp.broadcast_to(i >= d, (LANES, nq))
        win_tiles.append((sc, mk, vwt_ref[0, 0, ktd]))

    mask_c = (i * tq + qry) >= (CMP_STRIDE * key + CMP_LEN - 1)
    mx = jnp.max(jnp.where(mask_c, s, NEG_INF), axis=0, keepdims=True)
    e = jnp.where(mask_c, jnp.exp2(s - mx), 0.0)
    l = jnp.sum(e, axis=0, keepdims=True)
    p_c = e / jnp.where(l > 0.0, l, 1.0)
    o_c = _dot(vct_ref[0, 0], p_c.astype(_BF16))

    ps = p_c[:, 0:tq]
    for h in range(1, hg):
        ps = ps + p_c[:, h * tq:(h + 1) * tq]
    ps_hi = ps.astype(_BF16)
    ps_lo = (ps - ps_hi.astype(_F32)).astype(_BF16)
    imp = _dot(ovt_ref[...], ps_hi) + _dot(ovt_ref[...], ps_lo)
    blk = lax.broadcasted_iota(jnp.int32, (n_slc, tq), 0)
    tpos = i * tq + lax.broadcasted_iota(jnp.int32, (n_slc, tq), 1)
    cur = tpos // SLC_LEN
    valid = blk * SLC_LEN <= tpos
    forced = (blk == 0) | (blk == cur) | (blk == cur - 1)
    score = jnp.where(valid, jnp.where(forced, FORCE_SCORE, imp), -1.0)
    rank = jnp.zeros((n_slc, tq), _F32)
    for k in range(n_slc):
        ck = score[k:k + 1, :]
        beats = (ck > score) | ((ck == score) & (blk > k))
        rank = rank + jnp.where(beats, 1.0, 0.0)
    sel = jnp.where(rank < float(min(N_SELECT, n_slc)), 1.0, 0.0).astype(_BF16)
    selfull = _dot(ext_ref[...], sel)
    for kt in range(ext_ref.shape[0] // LANES):
        sel_ref[kt] = selfull[kt * LANES:(kt + 1) * LANES, :]

    def sel_mask(kt, ok=True):
        return _lanes([sel_ref[kt]] * hg) > jnp.where(ok, 0.5, 2.0)

    o_w = _t_finish(_t_update(_t_init(nq), win_tiles))

    n_far = jnp.maximum(i - 1, 0)

    def far_body(j, st):
        tiles = []
        for u in range(FAR_GROUP):
            kt = FAR_GROUP * j + u
            ktc = jnp.minimum(kt, n_far - 1)
            tiles.append((_dot(ktile(ks_ref, ktc), qt), sel_mask(ktc, kt < n_far), vst_ref[0, 0, ktc]))
        return _t_update(st, tiles)

    st = lax.fori_loop(0, (n_far + FAR_GROUP - 1) // FAR_GROUP, far_body, _t_init(nq))
    kt1 = jnp.maximum(i - 1, 0)
    st = _t_update(st, [
        (_dot(ktile(ks_ref, kt1), qt) + near_bias(1), sel_mask(kt1, i >= 1), vst_ref[0, 0, kt1]),
        (_dot(ktile(ks_ref, i), qt) + near_bias(0), sel_mask(i) & (key <= qry), vst_ref[0, 0, i])])
    o_s = _t_finish(st)

    glt = jax.nn.sigmoid(gl_ref[0]).T
    grow = lambda br: _lanes([glt[br * hg + h:br * hg + h + 1, :] for h in range(hg)])
    o = grow(0) * o_c + grow(1) * o_s + grow(2) * o_w
    for h in range(hg):
        o_ref[0, :, h * HEAD_DIM:(h + 1) * HEAD_DIM] = o[:, h * tq:(h + 1) * tq].T.astype(o_ref.dtype)


def _t5_bucket(rel):
    n = np.maximum(rel, 0)
    max_exact = NUM_BUCKETS // 2
    ratio = np.maximum(n, 1).astype(np.float32) / np.float32(max_exact)
    log_ratio = np.log(ratio) / np.float32(math.log(MAX_DISTANCE / max_exact))
    large = np.minimum(max_exact + (log_ratio * np.float32(NUM_BUCKETS - max_exact)).astype(np.int32),
                       NUM_BUCKETS - 1)
    return np.where(n < max_exact, n, large).astype(np.int32)


def _bias_lookup(rb, bucket):
    bk = jnp.asarray(bucket.astype(np.int8))[None]
    ex = (slice(None),) + (None,) * bucket.ndim
    tab = jnp.broadcast_to(rb[:, 0][ex], (rb.shape[0],) + bucket.shape)
    for k in range(1, NUM_BUCKETS):
        tab = jnp.where(bk == k, rb[:, k][ex], tab)
    return tab


def _nsa_tables(rel_bias):
    rb = rel_bias.astype(_F32).T * LOG2E
    r = np.arange(ATT_TQ)
    rel_d = (np.arange(2) * ATT_TQ)[:, None, None] + r[None, None, :] - r[None, :, None]
    assert _t5_bucket(np.array([ATT_TQ + 1]))[0] == NUM_BUCKETS - 1
    u = np.arange(2 * LANES)
    rel_w = r[None, :] - CMP_STRIDE * (u[:, None] - (LANES - 8)) - (CMP_LEN - 1)
    far = rb[:, NUM_BUCKETS - 1]
    dt = _bias_lookup(rb, _t5_bucket(rel_d)) - far[:, None, None, None]
    return dt, _bias_lookup(rb, _t5_bucket(rel_w))


def _tile_transposed(v, cb, n_heads, tk):
    b, t, _ = v.shape
    x = v[:, :, cb * LANES:(cb + n_heads) * LANES].reshape(b, t // tk, tk, n_heads, HEAD_DIM)
    return x.transpose(0, 3, 1, 4, 2)


def _nsa_attention(proj, small, kvc, rel_bias):
    b, t, _ = proj.shape
    g, hg, tq = NSA_KV_HEADS, NSA_GROUP, ATT_TQ
    n_slc = t // SLC_LEN
    n_cmp = (t - CMP_LEN) // CMP_STRIDE + 1
    nt = t // LANES
    assert n_cmp <= LANES and kvc.shape[2] == LANES and n_slc % 16 == 0 and tq == LANES
    assert LANES - 8 - (LANES // CMP_STRIDE) * (t // tq - 1) >= 0
    dt, wt = _nsa_tables(rel_bias)
    cstart = np.arange(LANES) * CMP_STRIDE
    sstart = np.arange(n_slc) * SLC_LEN
    overlap = np.clip(np.minimum(cstart[None, :] + CMP_LEN, sstart[:, None] + SLC_LEN)
                      - np.maximum(cstart[None, :], sstart[:, None]), 0, None).astype(np.float32) / CMP_LEN
    overlap[:, n_cmp:] = 0.0
    ovt = jnp.asarray(overlap, _BF16)
    ext = jnp.asarray(np.arange(t)[:, None] // SLC_LEN == np.arange(n_slc)[None, :], _BF16)
    vst = _tile_transposed(proj, CB_VSLC, g, LANES)
    vwt = _tile_transposed(proj, CB_VWIN, g, LANES)

    seq = lambda cb: pl.BlockSpec((1, t, LANES), lambda bb, gg, ii: (bb, 0, cb + gg))
    vts = pl.BlockSpec((1, 1, nt, HEAD_DIM, LANES), lambda bb, gg, ii: (bb, gg, 0, 0, 0))
    return pl.pallas_call(
        _nsa_kernel,
        grid=(b, g, t // tq),
        in_specs=[
            pl.BlockSpec((1, tq, hg * HEAD_DIM), lambda bb, gg, ii: (bb, ii, CB_QNSA * LANES // (hg * HEAD_DIM) + gg)),
            pl.BlockSpec((1, 1, LANES, HEAD_DIM), lambda bb, gg, ii: (bb, gg, 0, 0)),
            pl.BlockSpec((1, 1, HEAD_DIM, LANES), lambda bb, gg, ii: (bb, NSA_KV_HEADS + gg, 0, 0)),
            seq(CB_KSLC), vts, seq(CB_KWIN), vts,
            pl.BlockSpec((1, tq, LANES), lambda bb, gg, ii: (bb, ii, gg)),
            pl.BlockSpec((hg, 2 * LANES, tq), lambda bb, gg, ii: (gg, 0, 0)),
            pl.BlockSpec((hg, 2, LANES, tq), lambda bb, gg, ii: (gg, 0, 0, 0)),
            pl.BlockSpec((n_slc, LANES), lambda bb, gg, ii: (0, 0)),
            pl.BlockSpec((t, n_slc), lambda bb, gg, ii: (0, 0)),
        ],
        out_specs=pl.BlockSpec((1, tq, hg * HEAD_DIM), lambda bb, gg, ii: (bb, ii, gg)),
        out_shape=jax.ShapeDtypeStruct((b, t, NSA_Q_W), _BF16),
        scratch_shapes=[pltpu.VMEM((nt, LANES, tq), _F32)],
        compiler_params=_cparams(("arbitrary", "arbitrary", "arbitrary")),
        name="nsa_attention",
    )(proj, kvc, kvc, proj, vst, proj, vwt, small, wt, dt, ovt, ext)


def _fox_cum_kernel(s_ref, b_ref, col_ref, row_ref):
    z = s_ref[0] + b_ref[...]
    lf = (jnp.minimum(z, 0.0) - jnp.log1p(jnp.exp(-jnp.abs(z)))) * LOG2E
    x = lf.T
    t = x.shape[1]
    lane = lax.broadcasted_iota(jnp.int32, x.shape, 1)
    sh = 1
    while sh < t:
        x = x + jnp.where(lane >= sh, pltpu.roll(x, sh, 1), 0.0)
        sh *= 2
    row_ref[0] = x[8:16, :]
    col_ref[0] = x.T


def _fox_cumsum(small, b_f):
    b, t, _ = small.shape
    bvec = jnp.zeros((1, LANES), _F32).at[0, FOXF_LANE:FOXF_LANE + FOX_HEADS].set(b_f.astype(_F32))
    return pl.pallas_call(
        _fox_cum_kernel,
        grid=(b,),
        in_specs=[pl.BlockSpec((1, t, LANES), lambda bb: (bb, 0, 1)),
                  pl.BlockSpec((1, LANES), lambda bb: (0, 0))],
        out_specs=[pl.BlockSpec((1, t, LANES), lambda bb: (bb, 0, 0)),
                   pl.BlockSpec((1, 8, t), lambda bb: (bb, 0, 0))],
        out_shape=[jax.ShapeDtypeStruct((b, t, LANES), _F32), jax.ShapeDtypeStruct((b, 8, t), _F32)],
        compiler_params=_cparams(("arbitrary",)),
        name="fox_cumsum",
    )(small, bvec)


def _fox_kernel(q_ref, k_ref, vt_ref, cc_ref, cr_ref, o_ref):
    i = pl.program_id(1)
    tq = FOX_TQ
    key = lax.broadcasted_iota(jnp.int32, (tq, tq), 0)
    qry = lax.broadcasted_iota(jnp.int32, (tq, tq), 1)
    heads = range(FOX_HEADS)
    hsl = [slice(h * HEAD_DIM, (h + 1) * HEAD_DIM) for h in heads]
    qts = [q_ref[0, :, hsl[h]].astype(_F32).T.astype(_BF16) for h in heads]
    cqs = [cr_ref[0, FOXF_LANE - 8 + h:FOXF_LANE - 8 + h + 1, :] for h in heads]

    def logits(h, kt):
        ks = pl.ds(pl.multiple_of(kt * tq, tq), tq)
        ck = cc_ref[0, ks, FOXF_LANE + h:FOXF_LANE + h + 1]
        return _dot(k_ref[0, ks, hsl[h]], qts[h]) + (cqs[h] - ck)

    def body(kt, sts):
        ss = [logits(h, kt) for h in heads]
        return tuple(_t_update(sts[h], [(ss[h], None, vt_ref[0, h, kt])]) for h in heads)

    sts = lax.fori_loop(0, i, body, tuple(_t_init(tq) for _ in heads))
    ss = [logits(h, i) for h in heads]
    for h in heads:
        st = _t_update(sts[h], [(ss[h], key <= qry, vt_ref[0, h, i])])
        o_ref[0, :, hsl[h]] = _t_finish(st).T.astype(o_ref.dtype)


def _fox_attention(proj, cum_col, cum_row):
    b, t, _ = proj.shape
    tq = FOX_TQ
    w = FOX_W
    vt = _tile_transposed(proj, CB_FOXV, FOX_HEADS, tq)
    return pl.pallas_call(
        _fox_kernel,
        grid=(b, t // tq),
        in_specs=[
            pl.BlockSpec((1, tq, w), lambda bb, ii: (bb, ii, CB_FOXQ * LANES // w)),
            pl.BlockSpec((1, t, w), lambda bb, ii: (bb, 0, CB_FOXK * LANES // w)),
            pl.BlockSpec((1, FOX_HEADS, t // tq, HEAD_DIM, tq), lambda bb, ii: (bb, 0, 0, 0, 0)),
            pl.BlockSpec((1, t, LANES), lambda bb, ii: (bb, 0, 0)),
            pl.BlockSpec((1, 8, tq), lambda bb, ii: (bb, 0, ii)),
        ],
        out_specs=pl.BlockSpec((1, tq, w), lambda bb, ii: (bb, ii, 0)),
        out_shape=jax.ShapeDtypeStruct((b, t, w), _BF16),
        compiler_params=_cparams(("arbitrary", "arbitrary")),
        name="fox_attention",
    )(proj, proj, vt, cum_col, cum_row)


def _mem_kernel(q_ref, kv_ref, o_ref):
    for h in range(MEM_HEADS):
        hs = slice(h * HEAD_DIM, (h + 1) * HEAD_DIM)
        vs = slice(MEM_W + h * HEAD_DIM, MEM_W + (h + 1) * HEAD_DIM)
        s = _dot_nt(q_ref[0, :, hs], kv_ref[0, :, hs])
        e = jnp.exp2(s - jnp.max(s, axis=1, keepdims=True))
        p = e / jnp.sum(e, axis=1, keepdims=True)
        o_ref[0, :, hs] = _dot(p.astype(_BF16), kv_ref[0, :, vs]).astype(o_ref.dtype)


def _mem_attention(proj, memkv):
    b, t, _ = proj.shape
    m = memkv.shape[1]
    tq = MEM_TQ
    return pl.pallas_call(
        _mem_kernel,
        grid=(b, t // tq),
        in_specs=[pl.BlockSpec((1, tq, MEM_W), lambda bb, ii: (bb, ii, CB_MEMQ * LANES // MEM_W)),
                  pl.BlockSpec((1, m, 2 * MEM_W), lambda bb, ii: (bb, 0, 0))],
        out_specs=pl.BlockSpec((1, tq, MEM_W), lambda bb, ii: (bb, ii, 0)),
        out_shape=jax.ShapeDtypeStruct((b, t, MEM_W), _BF16),
        compiler_params=_cparams(("arbitrary", "arbitrary")),
        name="mem_attention",
    )(proj, memkv)


def _merge_kernel(x_ref, on_ref, of_ref, om_ref, g0_ref, g1_ref, g2_ref, wn_ref, wf_ref, wm_ref, wo_ref,
                  o_ref, z_ref):
    tn = 512
    for c in range(D_MODEL // tn):
        cs = slice(c * tn, (c + 1) * tn)
        z = (jax.nn.sigmoid(g0_ref[:, cs].astype(_F32)) * _dot(on_ref[...], wn_ref[:, cs])
             + jax.nn.sigmoid(g1_ref[:, cs].astype(_F32)) * _dot(of_ref[...], wf_ref[:, cs])
             + jax.nn.sigmoid(g2_ref[:, cs].astype(_F32)) * _dot(om_ref[...], wm_ref[:, cs]))
        z_ref[:, cs] = z.astype(_BF16)
    o_ref[...] = x_ref[...] + _dot(z_ref[...], wo_ref[...])


def _merge(x2d, o_nsa, o_fox, o_mem, proj2d, w_o_nsa, w_o_fox, w_o_mem, w_out):
    n, d = x2d.shape
    tm = MERGE_TM
    gcb = CB_MERGE * LANES // d
    const = lambda r, c: pl.BlockSpec((r, c), lambda i: (0, 0))
    return pl.pallas_call(
        _merge_kernel,
        grid=(n // tm,),
        in_specs=[
            pl.BlockSpec((tm, d), lambda i: (i, 0)),
            pl.BlockSpec((tm, NSA_Q_W), lambda i: (i, 0)),
            pl.BlockSpec((tm, FOX_W), lambda i: (i, 0)),
            pl.BlockSpec((tm, MEM_W), lambda i: (i, 0)),
            pl.BlockSpec((tm, d), lambda i: (i, gcb)),
            pl.BlockSpec((tm, d), lambda i: (i, gcb + 1)),
            pl.BlockSpec((tm, d), lambda i: (i, gcb + 2)),
            const(NSA_Q_W, d), const(FOX_W, d), const(MEM_W, d), const(d, d),
        ],
        out_specs=pl.BlockSpec((tm, d), lambda i: (i, 0)),
        out_shape=jax.ShapeDtypeStruct((n, d), _F32),
        scratch_shapes=[pltpu.VMEM((tm, d), _BF16)],
        compiler_params=_cparams(("arbitrary",)),
        name="merge_out",
    )(x2d, o_nsa, o_fox, o_mem, proj2d, proj2d, proj2d,
      w_o_nsa.astype(_BF16), w_o_fox.astype(_BF16), w_o_mem.astype(_BF16), w_out.astype(_BF16))


def _nsa_inputs(x, norm_attn_g, w_in, nsa_pe_k, nsa_w_ck, nsa_pe_v, nsa_w_cv, nsa_q_g, nsa_k_g,
                fox_q_g, fox_k_g, mem_q_g):
    b, t, d = x.shape
    ones = lambda k: jnp.ones((k * HEAD_DIM,), _F32)
    zeros = lambda k: jnp.zeros((k * HEAD_DIM,), _F32)
    tile = lambda gv, k: jnp.tile(gv.astype(_F32), k)
    qs = ATTN_SCALE * LOG2E

    g_end = NSA_Q_W + 6 * NSA_KV_W
    f_off = g_end + NSA_GATE_W + 3 * FOX_W
    m_off = f_off + FOX_HEADS + MEM_W
    w_all = _reorder_cast(w_in, ((m_off, MERGE_W), (0, g_end), (g_end + NSA_GATE_W, 3 * FOX_W),
                                 (f_off + FOX_HEADS, MEM_W)))
    gates = w_in[:, g_end:g_end + NSA_GATE_W].reshape(d, 3, NSA_KV_HEADS, NSA_GROUP)
    pad = jnp.zeros((d, LANES - 3 * NSA_GROUP), w_in.dtype)
    w_small = jnp.concatenate([
        gates[:, :, 0, :].reshape(d, 3 * NSA_GROUP), pad,
        gates[:, :, 1, :].reshape(d, 3 * NSA_GROUP), w_in[:, f_off:f_off + FOX_HEADS],
        pad[:, :LANES - 3 * NSA_GROUP - FOX_HEADS]], axis=1).astype(_BF16)
    col_gain = jnp.concatenate([
        jnp.ones((MERGE_W,), _F32),
        tile(nsa_q_g, NSA_HEADS) * qs, ones(2), ones(2), tile(nsa_k_g, 2), ones(2), tile(nsa_k_g, 2), ones(2),
        tile(fox_q_g, FOX_HEADS) * qs, tile(fox_k_g, FOX_HEADS), ones(FOX_HEADS),
        tile(mem_q_g, MEM_HEADS) * qs])
    col_flag = jnp.concatenate([
        jnp.zeros((MERGE_W,), _F32),
        ones(NSA_HEADS), zeros(2), zeros(2), ones(2), zeros(2), ones(2), zeros(2),
        ones(FOX_HEADS), ones(FOX_HEADS), zeros(FOX_HEADS), ones(MEM_HEADS)])

    x2d = x.reshape(b * t, d)
    proj2d, small2d = _rms_project(x2d, norm_attn_g, w_all, col_gain, col_flag, CB_ATT * LANES, w_small,
                                   PROJ_TM, PROJ_TN)
    proj = proj2d.reshape(b, t, PROJ_COLS)
    small = small2d.reshape(b, t, SMALL_W)

    cmp_in = proj[:, :, CB_KCMP * LANES:(CB_KCMP + 4) * LANES].reshape(b, t, 4, HEAD_DIM).transpose(0, 2, 1, 3)
    kvc = _nsa_compress(cmp_in, nsa_w_ck, nsa_w_cv, nsa_pe_k, nsa_pe_v, nsa_k_g)
    return proj, small, kvc


def _attention_mixers(x, mem, norm_attn_g, w_in, nsa_pe_k, nsa_w_ck, nsa_pe_v, nsa_w_cv, nsa_q_g, nsa_k_g,
                      rel_bias, fox_b_f, fox_q_g, fox_k_g, norm_mem_g, w_mem_kv, mem_q_g, mem_k_g):
    b, t, d = x.shape
    ones = lambda k: jnp.ones((k * HEAD_DIM,), _F32)
    zeros = lambda k: jnp.zeros((k * HEAD_DIM,), _F32)
    tile = lambda gv, k: jnp.tile(gv.astype(_F32), k)
    proj, small, kvc = _nsa_inputs(x, norm_attn_g, w_in, nsa_pe_k, nsa_w_ck, nsa_pe_v, nsa_w_cv, nsa_q_g, nsa_k_g,
                                   fox_q_g, fox_k_g, mem_q_g)
    proj2d = proj.reshape(b * t, PROJ_COLS)
    o_nsa = _nsa_attention(proj, small, kvc, rel_bias)

    cum_col, cum_row = _fox_cumsum(small, fox_b_f)
    o_fox = _fox_attention(proj, cum_col, cum_row)

    mm = mem.shape[1]
    mem_gain = jnp.concatenate([tile(mem_k_g, MEM_HEADS), ones(MEM_HEADS)])
    mem_flag = jnp.concatenate([ones(MEM_HEADS), zeros(MEM_HEADS)])
    memkv = _rms_project(mem.reshape(b * mm, d), norm_mem_g, w_mem_kv.astype(_BF16), mem_gain, mem_flag,
                         0, None, mm, MEM_W).reshape(b, mm, 2 * MEM_W)
    o_mem = _mem_attention(proj, memkv)
    return proj2d, o_nsa.reshape(b * t, NSA_Q_W), o_fox.reshape(b * t, FOX_W), o_mem.reshape(b * t, MEM_W)


def _router_kernel(x_ref, g_ref, whi_ref, wlo_ref, b_ref, h_ref, idx_ref, gate_ref, cnt_ref):
    x = x_ref[...]
    ms = jnp.mean(x * x, axis=-1, keepdims=True)
    h = x * lax.rsqrt(ms + RMS_EPS) * g_ref[...]
    h_hi = h.astype(_BF16)
    h_ref[...] = pltpu.bitcast(_to_token_major(h_hi), jnp.int32)
    h_lo = (h - h_hi.astype(_F32)).astype(_BF16)
    logits = _dot_nt(whi_ref[...], h_hi) + _dot_nt(whi_ref[...], h_lo) + _dot_nt(wlo_ref[...], h_hi)
    scores = jax.nn.sigmoid(logits)
    sb = scores + b_ref[...]
    eidx = lax.broadcasted_iota(jnp.int32, sb.shape, 0)
    idxs, vals = [], []
    for _ in range(TOP_K):
        m = jnp.max(sb, axis=0, keepdims=True)
        idx = jnp.min(jnp.where(sb == m, eidx, N_EXPERTS), axis=0, keepdims=True)
        hit = eidx == idx
        vals.append(jnp.sum(jnp.where(hit, scores, 0.0), axis=0, keepdims=True))
        idxs.append(idx)
        sb = jnp.where(hit, NEG_INF, sb)
    top_s = jnp.concatenate(vals, axis=0)
    idx_ref[...] = jnp.concatenate(idxs, axis=0)
    gate_ref[...] = top_s / jnp.sum(top_s, axis=0, keepdims=True) * ROUTED_SCALE

    @pl.when(pl.program_id(0) == 0)
    def _():
        cnt_ref[...] = jnp.zeros(cnt_ref.shape, cnt_ref.dtype)

    picked = jnp.where(sb < 0.5 * NEG_INF, 1.0, 0.0)
    cnt_ref[...] += jnp.sum(picked, axis=1, keepdims=True)


def _router(x1, norm_g, w_router, router_bias):
    n, d = x1.shape
    tm = ROUTER_TM
    wt = w_router.astype(_F32).T
    w_hi = wt.astype(_BF16)
    w_lo = (wt - w_hi.astype(_F32)).astype(_BF16)
    const = lambda r, c: pl.BlockSpec((r, c), lambda i: (0, 0))
    return pl.pallas_call(
        _router_kernel,
        grid=(n // tm,),
        in_specs=[pl.BlockSpec((tm, d), lambda i: (i, 0)), const(1, d), const(N_EXPERTS, d), const(N_EXPERTS, d),
                  const(N_EXPERTS, 1)],
        out_specs=[pl.BlockSpec((tm, d // LANES // 2, LANES), lambda i: (i, 0, 0)),
                   pl.BlockSpec((TOP_K, tm), lambda i: (0, i)),
                   pl.BlockSpec((TOP_K, tm), lambda i: (0, i)),
                   const(N_EXPERTS, 1)],
        out_shape=[jax.ShapeDtypeStruct((n, d // LANES // 2, LANES), jnp.int32),
                   jax.ShapeDtypeStruct((TOP_K, n), jnp.int32),
                   jax.ShapeDtypeStruct((TOP_K, n), _F32), jax.ShapeDtypeStruct((N_EXPERTS, 1), _F32)],
        compiler_params=_cparams(("arbitrary",)),
        name="moe_router",
    )(x1, norm_g.reshape(1, d).astype(_F32), w_hi, w_lo, router_bias.reshape(N_EXPERTS, 1).astype(_F32))


SC_CORES = 2
SC_SUBCORES = 16
SC_CHUNK = 40
SC_DEPTH = 3


def _sc_row_gather(table, idx):
    v, c, _ = table.shape
    b = idx.shape[0]
    nw = SC_CORES * SC_SUBCORES
    per_w = b // nw
    assert b % nw == 0 and per_w % SC_CHUNK == 0 and SC_CHUNK % 8 == 0
    n_chunks = per_w // SC_CHUNK
    n_groups, tail = divmod(n_chunks, SC_DEPTH)
    mesh = plsc.VectorSubcoreMesh(core_axis_name="c", subcore_axis_name="s")

    def body(table_hbm, idx_hbm, out_hbm, idx_v, rows_v, gsem, wsem):
        wid = lax.axis_index("s") * SC_CORES + lax.axis_index("c")
        base = wid * per_w
        pltpu.sync_copy(idx_hbm.at[pl.ds(base, per_w)], idx_v)

        def group(first, count):
            offs = [pl.multiple_of((first + u) * SC_CHUNK, 8) for u in range(count)]
            gathers = [pltpu.async_copy(table_hbm.at[idx_v.at[pl.ds(offs[u], SC_CHUNK)]], rows_v.at[u], gsem)
                       for u in range(count)]
            for cp in gathers:
                cp.wait()
            writes = [pltpu.async_copy(rows_v.at[u], out_hbm.at[pl.ds(base + offs[u], SC_CHUNK)], wsem)
                      for u in range(count)]
            for cp in writes:
                cp.wait()

        @pl.loop(0, n_groups)
        def _(g):
            group(g * SC_DEPTH, SC_DEPTH)

        if tail:
            group(n_groups * SC_DEPTH, tail)

    return pl.kernel(
        body,
        out_type=jax.ShapeDtypeStruct((b, c, LANES), table.dtype),
        mesh=mesh,
        scratch_types=[pltpu.VMEM((per_w,), jnp.int32), pltpu.VMEM((SC_DEPTH, SC_CHUNK, c, LANES), table.dtype),
                       pltpu.SemaphoreType.DMA, pltpu.SemaphoreType.DMA],
        compiler_params=pltpu.CompilerParams(use_tc_tiling_on_sc=True),
        name="sc_row_gather",
    )(table, idx)


def _moe_kernel(chained, be_ref, nbr_ref, dst_hbm, x_ref, sg_ref, wg_ref, wu_ref, wd_ref, *rest):
    y_hbm, ybuf, didx, wg_bf, wu_bf, wd_bf, ssem, isem = rest[1:] if chained else rest
    n = pl.program_id(0)
    nbr = nbr_ref[0]
    last = nbr - 1
    slot = n % 2
    other = 1 - slot
    blk = MOE_BLOCK
    dump_block = dst_hbm.shape[0] - 1
    ring = didx.shape[0]

    def dst_copy(block, s3):
        return pltpu.make_async_copy(dst_hbm.at[block], didx.at[s3], isem.at[s3])

    def issue_scatter(s, s3):
        def body(j, carry):
            for u in range(SCATTER_UNROLL):
                r = j * SCATTER_UNROLL + u
                row = didx[s3, 0, r]
                pltpu.make_async_copy(ybuf.at[s, r], y_hbm.at[row], ssem.at[s]).start(priority=u % 2)
            return carry
        lax.fori_loop(0, blk // SCATTER_UNROLL, body, 0)

    def wait_rows(s):
        pltpu.make_async_copy(ybuf.at[s], ybuf.at[s], ssem.at[s]).wait()

    @pl.when(n < nbr)
    def _():
        @pl.when(n == 0)
        def _():
            dst_copy(0, 0).start()
            dst_copy(dump_block, ring - 1).start()
            dst_copy(jnp.minimum(1, last), 1).start()
            ybuf[1] = jnp.zeros(ybuf.shape[1:], ybuf.dtype)
            dst_copy(0, 0).wait()
            dst_copy(dump_block, ring - 1).wait()

        @pl.when(n < last)
        def _():
            dst_copy(jnp.minimum(n + 2, last), (n + 2) % ring).start()

        changed = (n == 0) | (be_ref[n] != be_ref[jnp.maximum(n - 1, 0)])

        @pl.when(changed)
        def _():
            wg_bf[...] = wg_ref[0].astype(_BF16)
            wu_bf[...] = wu_ref[0].astype(_BF16)
            wd_bf[...] = wd_ref[0].astype(_BF16)

        @pl.when(n >= 1)
        def _():
            wait_rows(slot)

        dst_copy(jnp.minimum(n + 1, last), (n + 1) % ring).wait()

        issue_scatter(other, (n + ring - 1) % ring)
        x = _from_token_major(pltpu.bitcast(x_ref[...], _BF16))
        gcol = jnp.broadcast_to(sg_ref[0], (LANES, blk)).T
        a = jax.nn.silu(_dot(x, wg_bf[...])) * _dot(x, wu_bf[...])
        a = (a * _lanes([gcol] * (a.shape[1] // LANES))).astype(_BF16)
        ybuf[slot] = _to_token_major(_dot(a, wd_bf[...]).astype(_BF16))

        @pl.when(n == last)
        def _():
            issue_scatter(slot, n % ring)
            wait_rows(other)
            wait_rows(slot)


def _moe_experts(x_sorted, block_expert, nb_real, slot_dst, slot_gate, we_gate, we_up, we_down, n, y_prev):
    rows, c2, _ = x_sorted.shape
    c = 2 * c2
    d = c * LANES
    nb = block_expert.shape[0]
    blk = MOE_BLOCK
    assert rows == nb * blk
    e, _, de = we_gate.shape
    chained = y_prev is not None
    in_specs = [
        pl.BlockSpec(memory_space=pl.ANY),
        pl.BlockSpec((blk, c2, LANES), lambda i, be, nbr: (jnp.maximum(jnp.minimum(i, nbr[0] - 1), 0), 0, 0)),
        pl.BlockSpec((1, 1, blk), lambda i, be, nbr: (i, 0, 0)),
        pl.BlockSpec((1, d, de), lambda i, be, nbr: (be[i], 0, 0)),
        pl.BlockSpec((1, d, de), lambda i, be, nbr: (be[i], 0, 0)),
        pl.BlockSpec((1, de, d), lambda i, be, nbr: (be[i], 0, 0)),
    ]
    args = [block_expert, nb_real, slot_dst, x_sorted, slot_gate, we_gate, we_up, we_down]
    if chained:
        in_specs.append(pl.BlockSpec(memory_space=pl.ANY))
        args.append(y_prev)
    grid_spec = pltpu.PrefetchScalarGridSpec(
        num_scalar_prefetch=2,
        grid=(nb,),
        in_specs=in_specs,
        out_specs=pl.BlockSpec(memory_space=pl.ANY),
        scratch_shapes=[
            pltpu.VMEM((2, blk, c, LANES), _BF16),
            pltpu.SMEM((4, 1, blk), jnp.int32),
            pltpu.VMEM((d, de), _BF16), pltpu.VMEM((d, de), _BF16), pltpu.VMEM((de, d), _BF16),
            pltpu.SemaphoreType.DMA((2,)), pltpu.SemaphoreType.DMA((4,)),
        ],
    )
    return pl.pallas_call(
        functools.partial(_moe_kernel, chained),
        grid_spec=grid_spec,
        out_shape=jax.ShapeDtypeStruct((TOP_K * n + blk, c, LANES), _BF16),
        input_output_aliases={len(args) - 1: 0} if chained else {},
        compiler_params=_cparams(("arbitrary",)),
        name="moe_experts",
    )(*args)


def _dispatch_plan(top_idx, gates, counts, n):
    blk = MOE_BLOCK
    a = n * TOP_K
    nb = (a + N_EXPERTS * (blk - 1)) // blk
    order = jnp.argsort(top_idx.reshape(a)).astype(jnp.int32)
    counts = counts.astype(jnp.int32)
    padded = (counts + blk - 1) // blk * blk
    start = jnp.cumsum(counts) - counts
    pstart = jnp.cumsum(padded) - padded
    block_end = jnp.cumsum(padded) // blk
    blocks = jnp.arange(nb, dtype=jnp.int32)
    block_expert = jnp.minimum(jnp.sum((block_end[None, :] <= blocks[:, None]).astype(jnp.int32), axis=1),
                               N_EXPERTS - 1)
    off = blocks * blk - pstart[block_expert]
    base = start[block_expert] + off
    lane = jnp.arange(blk, dtype=jnp.int32)[None, :]
    rows = jnp.take(order, base[:, None] + lane, mode="clip")
    real = (off[:, None] + lane) < counts[block_expert][:, None]
    tok = rows // TOP_K
    kk = rows - tok * TOP_K
    slot_src = jnp.where(real, tok, 0)
    slot_dst = jnp.where(real, kk * n + tok, TOP_K * n + lane)
    slot_gate = jnp.where(real, jnp.take(gates.reshape(a), rows, mode="clip"), 0.0).reshape(nb, 1, blk)
    nb_real = jnp.sum(padded) // blk
    return block_expert, nb_real, slot_src, slot_dst.reshape(nb, 1, blk), slot_gate


def _combine_kernel(x_ref, h_ref, wsg_ref, wsu_ref, wsd_ref, *rest):
    y_refs, o_ref = rest[:TOP_K], rest[TOP_K]
    h = _from_token_major(pltpu.bitcast(h_ref[...], _BF16))
    a = (jax.nn.silu(_dot(h, wsg_ref[...])) * _dot(h, wsu_ref[...])).astype(_BF16)
    routed = y_refs[0][...].astype(_F32)
    for k in range(1, TOP_K):
        routed = routed + y_refs[k][...].astype(_F32)
    o_ref[...] = x_ref[...] + _dot(a, wsd_ref[...]) + _from_token_major(routed)


def _combine(x1, h3, y3, ws_gate, ws_up, ws_down):
    n, d = x1.shape
    c = d // LANES
    tm = COMBINE_TM
    nt = n // tm
    de = ws_gate.shape[1]
    const = lambda r, cc: pl.BlockSpec((r, cc), lambda i: (0, 0))
    row = pl.BlockSpec((tm, d), lambda i: (i, 0))
    y_specs = [pl.BlockSpec((tm, c, LANES), functools.partial(lambda i, k: (k * nt + i, 0, 0), k=k))
               for k in range(TOP_K)]
    return pl.pallas_call(
        _combine_kernel,
        grid=(nt,),
        in_specs=[row, pl.BlockSpec((tm, c // 2, LANES), lambda i: (i, 0, 0)), const(d, de), const(d, de),
                  const(de, d)] + y_specs,
        out_specs=row,
        out_shape=jax.ShapeDtypeStruct((n, d), _F32),
        compiler_params=_cparams(("arbitrary",)),
        name="moe_combine",
    )(x1, h3, ws_gate.astype(_BF16), ws_up.astype(_BF16), ws_down.astype(_BF16), *([y3] * TOP_K))


def _moe_block(x1, norm_ffn_g, w_router, router_bias, we_gate, we_up, we_down, ws_gate, ws_up, ws_down):
    n = x1.shape[0]
    h3, top_idx_t, gates_t, counts = _router(x1, norm_ffn_g, w_router, router_bias)
    block_expert, nb_real, slot_src, slot_dst, slot_gate = _dispatch_plan(
        top_idx_t.T, gates_t.T, counts.reshape(N_EXPERTS), n)
    nb, blk = slot_src.shape
    assert nb % MOE_CHUNKS == 0
    nbc = nb // MOE_CHUNKS
    pad_rows = (TOP_K * n + jnp.arange(blk, dtype=jnp.int32)).reshape(1, 1, blk)
    xs = [_sc_row_gather(h3, slot_src[c * nbc:(c + 1) * nbc].reshape(-1)) for c in range(MOE_CHUNKS)]
    y3 = None
    for c in range(MOE_CHUNKS):
        sl = slice(c * nbc, (c + 1) * nbc)
        nbr_c = jnp.clip(nb_real - c * nbc, 0, nbc).reshape(1).astype(jnp.int32)
        y3 = _moe_experts(xs[c], block_expert[sl], nbr_c, jnp.concatenate([slot_dst[sl], pad_rows], axis=0),
                          slot_gate[sl], we_gate, we_up, we_down, n, y3)
    return _combine(x1, h3, y3, ws_gate, ws_up, ws_down)


def kernel(x, mem, norm_attn_g, w_in, nsa_pe_k, nsa_w_ck, nsa_pe_v, nsa_w_cv, nsa_q_g, nsa_k_g, rel_bias, fox_b_f, fox_q_g, fox_k_g, norm_mem_g, w_mem_kv, mem_q_g, mem_k_g, w_o_nsa, w_o_fox, w_o_mem, w_out, norm_ffn_g, w_router, router_bias, we_gate, we_up, we_down, ws_gate, ws_up, ws_down):
    b, t, d = x.shape
    assert norm_attn_g.shape[0] == 1, "single-layer problem"
    l = 0
    proj2d, o_nsa, o_fox, o_mem = _attention_mixers(
        x, mem, norm_attn_g[l], w_in[l], nsa_pe_k[l], nsa_w_ck[l], nsa_pe_v[l], nsa_w_cv[l], nsa_q_g[l],
        nsa_k_g[l], rel_bias, fox_b_f[l], fox_q_g[l], fox_k_g[l], norm_mem_g[l], w_mem_kv[l], mem_q_g[l],
        mem_k_g[l])
    x1 = _merge(x.reshape(b * t, d), o_nsa, o_fox, o_mem, proj2d, w_o_nsa[l], w_o_fox[l], w_o_mem[l], w_out[l])
    out = _moe_block(x1, norm_ffn_g[l], w_router[l], router_bias[l], we_gate[l], we_up[l], we_down[l],
                     ws_gate[l], ws_up[l], ws_down[l])
    return out.reshape(b, t, d)
```

```python
import functools
import math

import jax
import jax.numpy as jnp
import numpy as np
from jax import lax
from jax.experimental import pallas as pl
from jax.experimental.pallas import tpu as pltpu

D_MODEL = 2048
HEAD_DIM = 128
NSA_HEADS = 8
NSA_KV_HEADS = 2
NSA_GROUP = NSA_HEADS // NSA_KV_HEADS
FOX_HEADS = 4
MEM_HEADS = 4
CMP_LEN = 32
CMP_STRIDE = 16
SLC_LEN = 64
N_SELECT = 16
WINDOW = 512
NUM_BUCKETS = 32
MAX_DISTANCE = 128
N_BRANCHES = 3
N_EXPERTS = 64
TOP_K = 8
D_EXPERT = 512
ROUTED_SCALE = 2.5
ATTN_SCALE = HEAD_DIM ** -0.5
NEG_INF = -1e30
FORCE_SCORE = 1e4
RMS_EPS = 1e-6
LOG2E = math.log2(math.e)

NSA_Q_W = NSA_HEADS * HEAD_DIM
NSA_KV_W = NSA_KV_HEADS * HEAD_DIM
NSA_GATE_W = 3 * NSA_HEADS
FOX_W = FOX_HEADS * HEAD_DIM
MEM_W = MEM_HEADS * HEAD_DIM
MERGE_W = N_BRANCHES * D_MODEL

LANES = 128
VMEM_LIMIT_BYTES = 56 * 1024 * 1024

PROJ_TM = 1024
PROJ_TN = 768
ATT_TQ = 128
FAR_GROUP = 4
FOX_TQ = 256
MEM_TQ = 512
MERGE_TM = 256
ROUTER_TM = 512
MOE_BLOCK = 256
COPY_UNROLL = 16
COMBINE_TM = 256

CB_MERGE = 0
CB_ATT = N_BRANCHES * D_MODEL // LANES
CB_QNSA = CB_ATT
CB_KCMP = CB_ATT + 8
CB_VCMP = CB_ATT + 10
CB_KSLC = CB_ATT + 12
CB_VSLC = CB_ATT + 14
CB_KWIN = CB_ATT + 16
CB_VWIN = CB_ATT + 18
CB_FOXQ = CB_ATT + 20
CB_FOXK = CB_ATT + 24
CB_FOXV = CB_ATT + 28
CB_MEMQ = CB_ATT + 32
PROJ_COLS = (CB_ATT + 36) * LANES
SMALL_W = 2 * LANES
FOXF_LANE = 12

_BF16 = jnp.bfloat16
_F32 = jnp.float32


def _cparams(sem):
    return pltpu.CompilerParams(dimension_semantics=sem, vmem_limit_bytes=VMEM_LIMIT_BYTES)


def _dot(a, b):
    return jnp.dot(a, b, preferred_element_type=_F32)


def _dot_nt(a, b):
    return lax.dot_general(a, b, (((1,), (1,)), ((), ())), preferred_element_type=_F32)


def _lanes(parts):
    return jnp.concatenate(parts, axis=1)


def _to_token_major(x):
    c = x.shape[1] // LANES
    chunks = jnp.stack([x[:, j * LANES:(j + 1) * LANES] for j in range(c)], axis=0)
    return pltpu.einshape("ctl->tcl", chunks)


def _from_token_major(x3):
    xt = pltpu.einshape("tcl->ctl", x3)
    return _lanes([xt[j] for j in range(x3.shape[1])])


REORDER_TR = 128
REORDER_TC = 512


def _reorder_kernel(pieces, w_ref, o_ref):
    off = 0
    for src, width in pieces:
        for c in range(0, width, REORDER_TC):
            o_ref[:, off + c:off + c + REORDER_TC] = w_ref[:, src + c:src + c + REORDER_TC].astype(o_ref.dtype)
        off += width


def _reorder_cast(w, pieces):
    rows, cols = w.shape
    total = sum(width for _, width in pieces)
    assert rows % REORDER_TR == 0 and all(width % REORDER_TC == 0 for _, width in pieces)
    return pl.pallas_call(
        functools.partial(_reorder_kernel, pieces),
        grid=(rows // REORDER_TR,),
        in_specs=[pl.BlockSpec((REORDER_TR, cols), lambda i: (i, 0))],
        out_specs=pl.BlockSpec((REORDER_TR, total), lambda i: (i, 0)),
        out_shape=jax.ShapeDtypeStruct((rows, total), _BF16),
        compiler_params=_cparams(("arbitrary",)),
        name="reorder_cast",
    )(w)


def _proj_kernel(norm_j0, has_small, x_ref, g_ref, w_ref, cg_ref, cf_ref, *rest):
    if has_small:
        ws_ref, o_ref, os_ref, h_ref = rest
    else:
        o_ref, h_ref = rest
    j = pl.program_id(1)

    @pl.when(j == 0)
    def _():
        x = x_ref[...]
        ms = jnp.mean(x * x, axis=-1, keepdims=True)
        h = (x * lax.rsqrt(ms + RMS_EPS) * g_ref[...]).astype(_BF16)
        h_ref[...] = h
        if has_small:
            os_ref[...] = _dot(h, ws_ref[...])

    y = _dot(h_ref[...], w_ref[...])
    tn = y.shape[1]

    @pl.when(j >= norm_j0)
    def _():
        for c in range(tn // LANES):
            sl = slice(c * LANES, (c + 1) * LANES)
            yh = y[:, sl]
            ms = jnp.mean(yh * yh, axis=-1, keepdims=True)
            scale = jnp.where(cf_ref[:, sl] > 0.0, lax.rsqrt(ms + RMS_EPS), 1.0)
            o_ref[:, sl] = (yh * scale * cg_ref[:, sl]).astype(o_ref.dtype)

    @pl.when(j < norm_j0)
    def _():
        o_ref[...] = y.astype(o_ref.dtype)


def _rms_project(x2d, g, w, col_gain, col_flag, n_plain_cols, w_small, tm, tn):
    n, d = x2d.shape
    c = w.shape[1]
    has_small = w_small is not None
    in_specs = [
        pl.BlockSpec((tm, d), lambda i, j: (i, 0)),
        pl.BlockSpec((1, d), lambda i, j: (0, 0)),
        pl.BlockSpec((d, tn), lambda i, j: (0, j)),
        pl.BlockSpec((1, tn), lambda i, j: (0, j)),
        pl.BlockSpec((1, tn), lambda i, j: (0, j)),
    ]
    args = [x2d, g.reshape(1, d), w, col_gain.reshape(1, c), col_flag.reshape(1, c)]
    out_shape = [jax.ShapeDtypeStruct((n, c), _BF16)]
    out_specs = [pl.BlockSpec((tm, tn), lambda i, j: (i, j))]
    if has_small:
        ws = w_small.shape[1]
        in_specs.append(pl.BlockSpec((d, ws), lambda i, j: (0, 0)))
        args.append(w_small)
        out_shape.append(jax.ShapeDtypeStruct((n, ws), _F32))
        out_specs.append(pl.BlockSpec((tm, ws), lambda i, j: (i, 0)))
    assert n % tm == 0 and c % tn == 0 and n_plain_cols % tn == 0
    res = pl.pallas_call(
        functools.partial(_proj_kernel, n_plain_cols // tn, has_small),
        grid=(n // tm, c // tn),
        in_specs=in_specs,
        out_specs=out_specs,
        out_shape=out_shape,
        scratch_shapes=[pltpu.VMEM((tm, d), _BF16)],
        compiler_params=_cparams(("arbitrary", "arbitrary")),
        name="rms_project",
    )(*args)
    return res if has_small else res[0]


def _cmp_kernel(x_ref, wlo_ref, whi_ref, pelo_ref, pehi_ref, kg_ref, o_ref):
    j = pl.program_id(1)
    x = x_ref[0, 0]
    nchunk = x.shape[0]
    ylo = _dot(x, wlo_ref[0])
    yhi = _dot(x, whi_ref[0])
    pe = _dot(pelo_ref[0], wlo_ref[0]) + _dot(pehi_ref[0], whi_ref[0])
    y = ylo + pltpu.roll(yhi, nchunk - 1, 0) + pe[0:1, :]

    @pl.when(j < NSA_KV_HEADS)
    def _():
        ms = jnp.mean(y * y, axis=-1, keepdims=True)
        o_ref[0, 0] = (y * lax.rsqrt(ms + RMS_EPS) * kg_ref[...]).astype(o_ref.dtype)

    @pl.when(j >= NSA_KV_HEADS)
    def _():
        o_ref[0, 0] = y.T.astype(o_ref.dtype)


def _nsa_compress(cmp_in, w_ck, w_cv, pe_k, pe_v, k_g):
    b, nj, t, dk = cmp_in.shape
    nchunk = t // CMP_STRIDE
    half = CMP_LEN // 2
    assert nchunk == dk
    x = cmp_in.reshape(b, nj, nchunk, CMP_STRIDE * dk)

    def halves(w):
        return (w[:half].reshape(half * dk, dk).astype(_BF16),
                w[half:].reshape(half * dk, dk).astype(_BF16))

    klo, khi = halves(w_ck)
    vlo, vhi = halves(w_cv)
    wlo = jnp.stack([klo, vlo])
    whi = jnp.stack([khi, vhi])

    def pe_halves(pe):
        lo = jnp.broadcast_to(pe[:half].reshape(1, half * dk), (8, half * dk)).astype(_BF16)
        hi = jnp.broadcast_to(pe[half:].reshape(1, half * dk), (8, half * dk)).astype(_BF16)
        return lo, hi

    pklo, pkhi = pe_halves(pe_k)
    pvlo, pvhi = pe_halves(pe_v)
    pelo = jnp.stack([pklo, pvlo])
    pehi = jnp.stack([pkhi, pvhi])
    kv = lambda bb, j: (j // NSA_KV_HEADS, 0, 0)
    return pl.pallas_call(
        _cmp_kernel,
        grid=(b, nj),
        in_specs=[
            pl.BlockSpec((1, 1, nchunk, CMP_STRIDE * dk), lambda bb, j: (bb, j, 0, 0)),
            pl.BlockSpec((1, half * dk, dk), kv),
            pl.BlockSpec((1, half * dk, dk), kv),
            pl.BlockSpec((1, 8, half * dk), kv),
            pl.BlockSpec((1, 8, half * dk), kv),
            pl.BlockSpec((1, dk), lambda bb, j: (0, 0)),
        ],
        out_specs=pl.BlockSpec((1, 1, nchunk, dk), lambda bb, j: (bb, j, 0, 0)),
        out_shape=jax.ShapeDtypeStruct((b, nj, nchunk, dk), _BF16),
        compiler_params=_cparams(("arbitrary", "arbitrary")),
        name="nsa_compress",
    )(x, wlo, whi, pelo, pehi, k_g.reshape(1, dk).astype(_F32))


def _t_update(state, tiles):
    m, l, acc = state
    masked = [s if mask is None else jnp.where(mask, s, NEG_INF) for s, mask, _ in tiles]
    m_new = m
    for sm in masked:
        m_new = jnp.maximum(m_new, jnp.max(sm, axis=0, keepdims=True))
    alpha = jnp.exp2(m - m_new)
    l_new = alpha * l
    acc_new = alpha * acc
    for sm, (_, _, vt) in zip(masked, tiles):
        e = jnp.exp2(sm - m_new)
        l_new = l_new + jnp.sum(e, axis=0, keepdims=True)
        acc_new = acc_new + _dot(vt, e.astype(_BF16))
    return m_new, l_new, acc_new


def _t_init(nq):
    return (jnp.full((1, nq), NEG_INF, _F32), jnp.zeros((1, nq), _F32), jnp.zeros((HEAD_DIM, nq), _F32))


def _t_finish(state):
    _, l, acc = state
    return acc / l


def _nsa_kernel(q_ref, kc_ref, vct_ref, ks_ref, vst_ref, kw_ref, vwt_ref, gl_ref,
                wt_ref, dt_ref, ovt_ref, ext_ref, o_ref, sel_ref):
    i = pl.program_id(2)
    tq = ATT_TQ
    hg = NSA_GROUP
    nq = hg * tq
    n_slc = ovt_ref.shape[0]

    q = q_ref[0]
    qt = _lanes([q[:, h * HEAD_DIM:(h + 1) * HEAD_DIM].astype(_F32).T for h in range(hg)]).astype(_BF16)

    key = lax.broadcasted_iota(jnp.int32, (LANES, nq), 0)
    qry = lax.broadcasted_iota(jnp.int32, (LANES, nq), 1) & (tq - 1)

    def ktile(ref, kt):
        return ref[0, pl.ds(pl.multiple_of(kt * LANES, LANES), LANES), :]

    def near_bias(d):
        return _lanes([dt_ref[h, d] for h in range(hg)])

    woff = pl.multiple_of(wt_ref.shape[1] - LANES - 8 - (LANES // CMP_STRIDE) * i, 8)
    s = _dot(kc_ref[0, 0], qt) + _lanes([wt_ref[h, pl.ds(woff, LANES), :] for h in range(hg)])
    n_win = WINDOW // tq
    win_tiles = []
    for d in range(n_win, -1, -1):
        ktd = jnp.maximum(i - d, 0)
        sc = _dot(ktile(kw_ref, ktd), qt)
        if d <= 1:
            sc = sc + near_bias(d)
        if d == n_win:
            mk = (qry < key) & (i >= d)
        elif d == 0:
            mk = key <= qry
        else:
            mk = jnp.broadcast_to(i >= d, (LANES, nq))
        win_tiles.append((sc, mk, vwt_ref[0, 0, ktd]))

    mask_c = (i * tq + qry) >= (CMP_STRIDE * key + CMP_LEN - 1)
    mx = jnp.max(jnp.where(mask_c, s, NEG_INF), axis=0, keepdims=True)
    e = jnp.where(mask_c, jnp.exp2(s - mx), 0.0)
    l = jnp.sum(e, axis=0, keepdims=True)
    p_c = e / jnp.where(l > 0.0, l, 1.0)
    o_c = _dot(vct_ref[0, 0], p_c.astype(_BF16))

    ps = p_c[:, 0:tq]
    for h in range(1, hg):
        ps = ps + p_c[:, h * tq:(h + 1) * tq]
    ps_hi = ps.astype(_BF16)
    ps_lo = (ps - ps_hi.astype(_F32)).astype(_BF16)
    imp = _dot(ovt_ref[...], ps_hi) + _dot(ovt_ref[...], ps_lo)
    blk = lax.broadcasted_iota(jnp.int32, (n_slc, tq), 0)
    tpos = i * tq + lax.broadcasted_iota(jnp.int32, (n_slc, tq), 1)
    cur = tpos // SLC_LEN
    valid = blk * SLC_LEN <= tpos
    forced = (blk == 0) | (blk == cur) | (blk == cur - 1)
    score = jnp.where(valid, jnp.where(forced, FORCE_SCORE, imp), -1.0)
    rank = jnp.zeros((n_slc, tq), _F32)
    for k in range(n_slc):
        ck = score[k:k + 1, :]
        beats = (ck > score) | ((ck == score) & (blk > k))
        rank = rank + jnp.where(beats, 1.0, 0.0)
    sel = jnp.where(rank < float(min(N_SELECT, n_slc)), 1.0, 0.0).astype(_BF16)
    selfull = _dot(ext_ref[...], sel)
    for kt in range(ext_ref.shape[0] // LANES):
        sel_ref[kt] = selfull[kt * LANES:(kt + 1) * LANES, :]

    def sel_mask(kt, ok=True):
        return _lanes([sel_ref[kt]] * hg) > jnp.where(ok, 0.5, 2.0)

    o_w = _t_finish(_t_update(_t_init(nq), win_tiles))

    n_far = jnp.maximum(i - 1, 0)

    def far_body(j, st):
        k0 = pl.multiple_of(j * (FAR_GROUP * LANES), FAR_GROUP * LANES)
        sg = _dot(ks_ref[0, pl.ds(k0, FAR_GROUP * LANES), :], qt)
        tiles = []
        for u in range(FAR_GROUP):
            kt = FAR_GROUP * j + u
            tiles.append((sg[u * LANES:(u + 1) * LANES], sel_mask(kt, kt < n_far), vst_ref[0, 0, kt]))
        return _t_update(st, tiles)

    st = lax.fori_loop(0, (n_far + FAR_GROUP - 1) // FAR_GROUP, far_body, _t_init(nq))
    kt1 = jnp.maximum(i - 1, 0)
    st = _t_update(st, [
        (_dot(ktile(ks_ref, kt1), qt) + near_bias(1), sel_mask(kt1, i >= 1), vst_ref[0, 0, kt1]),
        (_dot(ktile(ks_ref, i), qt) + near_bias(0), sel_mask(i) & (key <= qry), vst_ref[0, 0, i])])
    o_s = _t_finish(st)

    glt = jax.nn.sigmoid(gl_ref[0]).T
    grow = lambda br: _lanes([glt[br * hg + h:br * hg + h + 1, :] for h in range(hg)])
    o = grow(0) * o_c + grow(1) * o_s + grow(2) * o_w
    for h in range(hg):
        o_ref[0, :, h * HEAD_DIM:(h + 1) * HEAD_DIM] = o[:, h * tq:(h + 1) * tq].T.astype(o_ref.dtype)


def _t5_bucket(rel):
    n = np.maximum(rel, 0)
    max_exact = NUM_BUCKETS // 2
    ratio = np.maximum(n, 1).astype(np.float32) / np.float32(max_exact)
    log_ratio = np.log(ratio) / np.float32(math.log(MAX_DISTANCE / max_exact))
    large = np.minimum(max_exact + (log_ratio * np.float32(NUM_BUCKETS - max_exact)).astype(np.int32),
                       NUM_BUCKETS - 1)
    return np.where(n < max_exact, n, large).astype(np.int32)


def _bias_lookup(rb, bucket):
    bk = jnp.asarray(bucket.astype(np.int8))[None]
    ex = (slice(None),) + (None,) * bucket.ndim
    tab = jnp.broadcast_to(rb[:, 0][ex], (rb.shape[0],) + bucket.shape)
    for k in range(1, NUM_BUCKETS):
        tab = jnp.where(bk == k, rb[:, k][ex], tab)
    return tab


def _nsa_tables(rel_bias):
    rb = rel_bias.astype(_F32).T * LOG2E
    r = np.arange(ATT_TQ)
    rel_d = (np.arange(2) * ATT_TQ)[:, None, None] + r[None, None, :] - r[None, :, None]
    assert _t5_bucket(np.array([ATT_TQ + 1]))[0] == NUM_BUCKETS - 1
    u = np.arange(2 * LANES)
    rel_w = r[None, :] - CMP_STRIDE * (u[:, None] - (LANES - 8)) - (CMP_LEN - 1)
    far = rb[:, NUM_BUCKETS - 1]
    dt = _bias_lookup(rb, _t5_bucket(rel_d)) - far[:, None, None, None]
    return dt, _bias_lookup(rb, _t5_bucket(rel_w))


def _tile_transposed(v, cb, n_heads, tk):
    b, t, _ = v.shape
    x = v[:, :, cb * LANES:(cb + n_heads) * LANES].reshape(b, t // tk, tk, n_heads, HEAD_DIM)
    return x.transpose(0, 3, 1, 4, 2)


def _nsa_attention(proj, small, kvc, rel_bias):
    b, t, _ = proj.shape
    g, hg, tq = NSA_KV_HEADS, NSA_GROUP, ATT_TQ
    n_slc = t // SLC_LEN
    n_cmp = (t - CMP_LEN) // CMP_STRIDE + 1
    nt = t // LANES
    assert n_cmp <= LANES and kvc.shape[2] == LANES and n_slc % 16 == 0 and tq == LANES
    assert nt % FAR_GROUP == 0
    assert LANES - 8 - (LANES // CMP_STRIDE) * (t // tq - 1) >= 0
    dt, wt = _nsa_tables(rel_bias)
    cstart = np.arange(LANES) * CMP_STRIDE
    sstart = np.arange(n_slc) * SLC_LEN
    overlap = np.clip(np.minimum(cstart[None, :] + CMP_LEN, sstart[:, None] + SLC_LEN)
                      - np.maximum(cstart[None, :], sstart[:, None]), 0, None).astype(np.float32) / CMP_LEN
    overlap[:, n_cmp:] = 0.0
    ovt = jnp.asarray(overlap, _BF16)
    ext = jnp.asarray(np.arange(t)[:, None] // SLC_LEN == np.arange(n_slc)[None, :], _BF16)
    vst = _tile_transposed(proj, CB_VSLC, g, LANES)
    vwt = _tile_transposed(proj, CB_VWIN, g, LANES)

    seq = lambda cb: pl.BlockSpec((1, t, LANES), lambda bb, gg, ii: (bb, 0, cb + gg))
    vts = pl.BlockSpec((1, 1, nt, HEAD_DIM, LANES), lambda bb, gg, ii: (bb, gg, 0, 0, 0))
    return pl.pallas_call(
        _nsa_kernel,
        grid=(b, g, t // tq),
        in_specs=[
            pl.BlockSpec((1, tq, hg * HEAD_DIM), lambda bb, gg, ii: (bb, ii, CB_QNSA * LANES // (hg * HEAD_DIM) + gg)),
            pl.BlockSpec((1, 1, LANES, HEAD_DIM), lambda bb, gg, ii: (bb, gg, 0, 0)),
            pl.BlockSpec((1, 1, HEAD_DIM, LANES), lambda bb, gg, ii: (bb, NSA_KV_HEADS + gg, 0, 0)),
            seq(CB_KSLC), vts, seq(CB_KWIN), vts,
            pl.BlockSpec((1, tq, LANES), lambda bb, gg, ii: (bb, ii, gg)),
            pl.BlockSpec((hg, 2 * LANES, tq), lambda bb, gg, ii: (gg, 0, 0)),
            pl.BlockSpec((hg, 2, LANES, tq), lambda bb, gg, ii: (gg, 0, 0, 0)),
            pl.BlockSpec((n_slc, LANES), lambda bb, gg, ii: (0, 0)),
            pl.BlockSpec((t, n_slc), lambda bb, gg, ii: (0, 0)),
        ],
        out_specs=pl.BlockSpec((1, tq, hg * HEAD_DIM), lambda bb, gg, ii: (bb, ii, gg)),
        out_shape=jax.ShapeDtypeStruct((b, t, NSA_Q_W), _BF16),
        scratch_shapes=[pltpu.VMEM((nt, LANES, tq), _F32)],
        compiler_params=_cparams(("arbitrary", "arbitrary", "arbitrary")),
        name="nsa_attention",
    )(proj, kvc, kvc, proj, vst, proj, vwt, small, wt, dt, ovt, ext)


def _fox_cum_kernel(s_ref, b_ref, col_ref, row_ref):
    z = s_ref[0] + b_ref[...]
    lf = (jnp.minimum(z, 0.0) - jnp.log1p(jnp.exp(-jnp.abs(z)))) * LOG2E
    x = lf.T
    t = x.shape[1]
    lane = lax.broadcasted_iota(jnp.int32, x.shape, 1)
    sh = 1
    while sh < t:
        x = x + jnp.where(lane >= sh, pltpu.roll(x, sh, 1), 0.0)
        sh *= 2
    row_ref[0] = x[8:16, :]
    col_ref[0] = x.T


def _fox_cumsum(small, b_f):
    b, t, _ = small.shape
    bvec = jnp.zeros((1, LANES), _F32).at[0, FOXF_LANE:FOXF_LANE + FOX_HEADS].set(b_f.astype(_F32))
    return pl.pallas_call(
        _fox_cum_kernel,
        grid=(b,),
        in_specs=[pl.BlockSpec((1, t, LANES), lambda bb: (bb, 0, 1)),
                  pl.BlockSpec((1, LANES), lambda bb: (0, 0))],
        out_specs=[pl.BlockSpec((1, t, LANES), lambda bb: (bb, 0, 0)),
                   pl.BlockSpec((1, 8, t), lambda bb: (bb, 0, 0))],
        out_shape=[jax.ShapeDtypeStruct((b, t, LANES), _F32), jax.ShapeDtypeStruct((b, 8, t), _F32)],
        compiler_params=_cparams(("arbitrary",)),
        name="fox_cumsum",
    )(small, bvec)


def _fox_kernel(q_ref, k_ref, vt_ref, cc_ref, cr_ref, o_ref):
    i = pl.program_id(1)
    tq = FOX_TQ
    key = lax.broadcasted_iota(jnp.int32, (tq, tq), 0)
    qry = lax.broadcasted_iota(jnp.int32, (tq, tq), 1)
    heads = range(FOX_HEADS)
    hsl = [slice(h * HEAD_DIM, (h + 1) * HEAD_DIM) for h in heads]
    qts = [q_ref[0, :, hsl[h]].astype(_F32).T.astype(_BF16) for h in heads]
    cqs = [cr_ref[0, FOXF_LANE - 8 + h:FOXF_LANE - 8 + h + 1, :] for h in heads]

    def logits(h, kt):
        ks = pl.ds(pl.multiple_of(kt * tq, tq), tq)
        ck = cc_ref[0, ks, FOXF_LANE + h:FOXF_LANE + h + 1]
        return _dot(k_ref[0, ks, hsl[h]], qts[h]) + (cqs[h] - ck)

    def body(kt, sts):
        ss = [logits(h, kt) for h in heads]
        return tuple(_t_update(sts[h], [(ss[h], None, vt_ref[0, h, kt])]) for h in heads)

    sts = lax.fori_loop(0, i, body, tuple(_t_init(tq) for _ in heads))
    ss = [logits(h, i) for h in heads]
    for h in heads:
        st = _t_update(sts[h], [(ss[h], key <= qry, vt_ref[0, h, i])])
        o_ref[0, :, hsl[h]] = _t_finish(st).T.astype(o_ref.dtype)


def _fox_attention(proj, cum_col, cum_row):
    b, t, _ = proj.shape
    tq = FOX_TQ
    w = FOX_W
    vt = _tile_transposed(proj, CB_FOXV, FOX_HEADS, tq)
    return pl.pallas_call(
        _fox_kernel,
        grid=(b, t // tq),
        in_specs=[
            pl.BlockSpec((1, tq, w), lambda bb, ii: (bb, ii, CB_FOXQ * LANES // w)),
            pl.BlockSpec((1, t, w), lambda bb, ii: (bb, 0, CB_FOXK * LANES // w)),
            pl.BlockSpec((1, FOX_HEADS, t // tq, HEAD_DIM, tq), lambda bb, ii: (bb, 0, 0, 0, 0)),
            pl.BlockSpec((1, t, LANES), lambda bb, ii: (bb, 0, 0)),
            pl.BlockSpec((1, 8, tq), lambda bb, ii: (bb, 0, ii)),
        ],
        out_specs=pl.BlockSpec((1, tq, w), lambda bb, ii: (bb, ii, 0)),
        out_shape=jax.ShapeDtypeStruct((b, t, w), _BF16),
        compiler_params=_cparams(("arbitrary", "arbitrary")),
        name="fox_attention",
    )(proj, proj, vt, cum_col, cum_row)


def _mem_kernel(q_ref, kv_ref, o_ref):
    for h in range(MEM_HEADS):
        hs = slice(h * HEAD_DIM, (h + 1) * HEAD_DIM)
        vs = slice(MEM_W + h * HEAD_DIM, MEM_W + (h + 1) * HEAD_DIM)
        s = _dot_nt(q_ref[0, :, hs], kv_ref[0, :, hs])
        e = jnp.exp2(s - jnp.max(s, axis=1, keepdims=True))
        p = e / jnp.sum(e, axis=1, keepdims=True)
        o_ref[0, :, hs] = _dot(p.astype(_BF16), kv_ref[0, :, vs]).astype(o_ref.dtype)


def _mem_attention(proj, memkv):
    b, t, _ = proj.shape
    m = memkv.shape[1]
    tq = MEM_TQ
    return pl.pallas_call(
        _mem_kernel,
        grid=(b, t // tq),
        in_specs=[pl.BlockSpec((1, tq, MEM_W), lambda bb, ii: (bb, ii, CB_MEMQ * LANES // MEM_W)),
                  pl.BlockSpec((1, m, 2 * MEM_W), lambda bb, ii: (bb, 0, 0))],
        out_specs=pl.BlockSpec((1, tq, MEM_W), lambda bb, ii: (bb, ii, 0)),
        out_shape=jax.ShapeDtypeStruct((b, t, MEM_W), _BF16),
        compiler_params=_cparams(("arbitrary", "arbitrary")),
        name="mem_attention",
    )(proj, memkv)


def _merge_kernel(x_ref, on_ref, of_ref, om_ref, g0_ref, g1_ref, g2_ref, wn_ref, wf_ref, wm_ref, wo_ref,
                  o_ref, z_ref):
    tn = 512
    for c in range(D_MODEL // tn):
        cs = slice(c * tn, (c + 1) * tn)
        z = (jax.nn.sigmoid(g0_ref[:, cs].astype(_F32)) * _dot(on_ref[...], wn_ref[:, cs])
             + jax.nn.sigmoid(g1_ref[:, cs].astype(_F32)) * _dot(of_ref[...], wf_ref[:, cs])
             + jax.nn.sigmoid(g2_ref[:, cs].astype(_F32)) * _dot(om_ref[...], wm_ref[:, cs]))
        z_ref[:, cs] = z.astype(_BF16)
    o_ref[...] = x_ref[...] + _dot(z_ref[...], wo_ref[...])


def _merge(x2d, o_nsa, o_fox, o_mem, proj2d, w_o_nsa, w_o_fox, w_o_mem, w_out):
    n, d = x2d.shape
    tm = MERGE_TM
    gcb = CB_MERGE * LANES // d
    const = lambda r, c: pl.BlockSpec((r, c), lambda i: (0, 0))
    return pl.pallas_call(
        _merge_kernel,
        grid=(n // tm,),
        in_specs=[
            pl.BlockSpec((tm, d), lambda i: (i, 0)),
            pl.BlockSpec((tm, NSA_Q_W), lambda i: (i, 0)),
            pl.BlockSpec((tm, FOX_W), lambda i: (i, 0)),
            pl.BlockSpec((tm, MEM_W), lambda i: (i, 0)),
            pl.BlockSpec((tm, d), lambda i: (i, gcb)),
            pl.BlockSpec((tm, d), lambda i: (i, gcb + 1)),
            pl.BlockSpec((tm, d), lambda i: (i, gcb + 2)),
            const(NSA_Q_W, d), const(FOX_W, d), const(MEM_W, d), const(d, d),
        ],
        out_specs=pl.BlockSpec((tm, d), lambda i: (i, 0)),
        out_shape=jax.ShapeDtypeStruct((n, d), _F32),
        scratch_shapes=[pltpu.VMEM((tm, d), _BF16)],
        compiler_params=_cparams(("arbitrary",)),
        name="merge_out",
    )(x2d, o_nsa, o_fox, o_mem, proj2d, proj2d, proj2d,
      w_o_nsa.astype(_BF16), w_o_fox.astype(_BF16), w_o_mem.astype(_BF16), w_out.astype(_BF16))


def _nsa_inputs(x, norm_attn_g, w_in, nsa_pe_k, nsa_w_ck, nsa_pe_v, nsa_w_cv, nsa_q_g, nsa_k_g,
                fox_q_g, fox_k_g, mem_q_g):
    b, t, d = x.shape
    ones = lambda k: jnp.ones((k * HEAD_DIM,), _F32)
    zeros = lambda k: jnp.zeros((k * HEAD_DIM,), _F32)
    tile = lambda gv, k: jnp.tile(gv.astype(_F32), k)
    qs = ATTN_SCALE * LOG2E

    g_end = NSA_Q_W + 6 * NSA_KV_W
    f_off = g_end + NSA_GATE_W + 3 * FOX_W
    m_off = f_off + FOX_HEADS + MEM_W
    w_all = _reorder_cast(w_in, ((m_off, MERGE_W), (0, g_end), (g_end + NSA_GATE_W, 3 * FOX_W),
                                 (f_off + FOX_HEADS, MEM_W)))
    gates = w_in[:, g_end:g_end + NSA_GATE_W].reshape(d, 3, NSA_KV_HEADS, NSA_GROUP)
    pad = jnp.zeros((d, LANES - 3 * NSA_GROUP), w_in.dtype)
    w_small = jnp.concatenate([
        gates[:, :, 0, :].reshape(d, 3 * NSA_GROUP), pad,
        gates[:, :, 1, :].reshape(d, 3 * NSA_GROUP), w_in[:, f_off:f_off + FOX_HEADS],
        pad[:, :LANES - 3 * NSA_GROUP - FOX_HEADS]], axis=1).astype(_BF16)
    col_gain = jnp.concatenate([
        jnp.ones((MERGE_W,), _F32),
        tile(nsa_q_g, NSA_HEADS) * qs, ones(2), ones(2), tile(nsa_k_g, 2), ones(2), tile(nsa_k_g, 2), ones(2),
        tile(fox_q_g, FOX_HEADS) * qs, tile(fox_k_g, FOX_HEADS), ones(FOX_HEADS),
        tile(mem_q_g, MEM_HEADS) * qs])
    col_flag = jnp.concatenate([
        jnp.zeros((MERGE_W,), _F32),
        ones(NSA_HEADS), zeros(2), zeros(2), ones(2), zeros(2), ones(2), zeros(2),
        ones(FOX_HEADS), ones(FOX_HEADS), zeros(FOX_HEADS), ones(MEM_HEADS)])

    x2d = x.reshape(b * t, d)
    proj2d, small2d = _rms_project(x2d, norm_attn_g, w_all, col_gain, col_flag, CB_ATT * LANES, w_small,
                                   PROJ_TM, PROJ_TN)
    proj = proj2d.reshape(b, t, PROJ_COLS)
    small = small2d.reshape(b, t, SMALL_W)

    cmp_in = proj[:, :, CB_KCMP * LANES:(CB_KCMP + 4) * LANES].reshape(b, t, 4, HEAD_DIM).transpose(0, 2, 1, 3)
    kvc = _nsa_compress(cmp_in, nsa_w_ck, nsa_w_cv, nsa_pe_k, nsa_pe_v, nsa_k_g)
    return proj, small, kvc


def _attention_mixers(x, mem, norm_attn_g, w_in, nsa_pe_k, nsa_w_ck, nsa_pe_v, nsa_w_cv, nsa_q_g, nsa_k_g,
                      rel_bias, fox_b_f, fox_q_g, fox_k_g, norm_mem_g, w_mem_kv, mem_q_g, mem_k_g):
    b, t, d = x.shape
    ones = lambda k: jnp.ones((k * HEAD_DIM,), _F32)
    zeros = lambda k: jnp.zeros((k * HEAD_DIM,), _F32)
    tile = lambda gv, k: jnp.tile(gv.astype(_F32), k)
    proj, small, kvc = _nsa_inputs(x, norm_attn_g, w_in, nsa_pe_k, nsa_w_ck, nsa_pe_v, nsa_w_cv, nsa_q_g, nsa_k_g,
                                   fox_q_g, fox_k_g, mem_q_g)
    proj2d = proj.reshape(b * t, PROJ_COLS)
    o_nsa = _nsa_attention(proj, small, kvc, rel_bias)

    cum_col, cum_row = _fox_cumsum(small, fox_b_f)
    o_fox = _fox_attention(proj, cum_col, cum_row)

    mm = mem.shape[1]
    mem_gain = jnp.concatenate([tile(mem_k_g, MEM_HEADS), ones(MEM_HEADS)])
    mem_flag = jnp.concatenate([ones(MEM_HEADS), zeros(MEM_HEADS)])
    memkv = _rms_project(mem.reshape(b * mm, d), norm_mem_g, w_mem_kv.astype(_BF16), mem_gain, mem_flag,
                         0, None, mm, MEM_W).reshape(b, mm, 2 * MEM_W)
    o_mem = _mem_attention(proj, memkv)
    return proj2d, o_nsa.reshape(b * t, NSA_Q_W), o_fox.reshape(b * t, FOX_W), o_mem.reshape(b * t, MEM_W)


def _router_kernel(x_ref, g_ref, whi_ref, wlo_ref, b_ref, h_ref, idx_ref, gate_ref, cnt_ref):
    x = x_ref[...]
    ms = jnp.mean(x * x, axis=-1, keepdims=True)
    h = x * lax.rsqrt(ms + RMS_EPS) * g_ref[...]
    h_hi = h.astype(_BF16)
    h_ref[...] = _to_token_major(h_hi)
    h_lo = (h - h_hi.astype(_F32)).astype(_BF16)
    logits = _dot_nt(whi_ref[...], h_hi) + _dot_nt(whi_ref[...], h_lo) + _dot_nt(wlo_ref[...], h_hi)
    scores = jax.nn.sigmoid(logits)
    sb = scores + b_ref[...]
    eidx = lax.broadcasted_iota(jnp.int32, sb.shape, 0)
    idxs, vals = [], []
    for _ in range(TOP_K):
        m = jnp.max(sb, axis=0, keepdims=True)
        idx = jnp.min(jnp.where(sb == m, eidx, N_EXPERTS), axis=0, keepdims=True)
        hit = eidx == idx
        vals.append(jnp.sum(jnp.where(hit, scores, 0.0), axis=0, keepdims=True))
        idxs.append(idx)
        sb = jnp.where(hit, NEG_INF, sb)
    top_s = jnp.concatenate(vals, axis=0)
    idx_ref[...] = jnp.concatenate(idxs, axis=0)
    gate_ref[...] = top_s / jnp.sum(top_s, axis=0, keepdims=True) * ROUTED_SCALE

    @pl.when(pl.program_id(0) == 0)
    def _():
        cnt_ref[...] = jnp.zeros(cnt_ref.shape, cnt_ref.dtype)

    picked = jnp.where(sb < 0.5 * NEG_INF, 1.0, 0.0)
    cnt_ref[...] += jnp.sum(picked, axis=1, keepdims=True)


def _router(x1, norm_g, w_router, router_bias):
    n, d = x1.shape
    tm = ROUTER_TM
    wt = w_router.astype(_F32).T
    w_hi = wt.astype(_BF16)
    w_lo = (wt - w_hi.astype(_F32)).astype(_BF16)
    const = lambda r, c: pl.BlockSpec((r, c), lambda i: (0, 0))
    return pl.pallas_call(
        _router_kernel,
        grid=(n // tm,),
        in_specs=[pl.BlockSpec((tm, d), lambda i: (i, 0)), const(1, d), const(N_EXPERTS, d), const(N_EXPERTS, d),
                  const(N_EXPERTS, 1)],
        out_specs=[pl.BlockSpec((tm, d // LANES, LANES), lambda i: (i, 0, 0)),
                   pl.BlockSpec((TOP_K, tm), lambda i: (0, i)),
                   pl.BlockSpec((TOP_K, tm), lambda i: (0, i)),
                   const(N_EXPERTS, 1)],
        out_shape=[jax.ShapeDtypeStruct((n, d // LANES, LANES), _BF16), jax.ShapeDtypeStruct((TOP_K, n), jnp.int32),
                   jax.ShapeDtypeStruct((TOP_K, n), _F32), jax.ShapeDtypeStruct((N_EXPERTS, 1), _F32)],
        compiler_params=_cparams(("arbitrary",)),
        name="moe_router",
    )(x1, norm_g.reshape(1, d).astype(_F32), w_hi, w_lo, router_bias.reshape(N_EXPERTS, 1).astype(_F32))


def _moe_kernel(be_ref, nbr_ref, cnt_ref, src_hbm, dst_hbm, h_hbm, sg_ref, wg_ref, wu_ref, wd_ref, y_hbm,
                xbuf, ybuf, xmat, sidx, didx, wg_bf, wu_bf, wd_bf, gsem, ssem, isem):
    n = pl.program_id(0)
    nbr = nbr_ref[0]
    last = nbr - 1
    slot = n % 2
    other = 1 - slot
    blk = MOE_BLOCK
    dump_block = dst_hbm.shape[0] - 1
    ring = didx.shape[0]

    def rows_of(block):
        return (cnt_ref[block] + COPY_UNROLL - 1) // COPY_UNROLL * COPY_UNROLL

    def src_copy(block, s):
        return pltpu.make_async_copy(src_hbm.at[block], sidx.at[s], isem.at[0, s])

    def dst_copy(block, s3):
        return pltpu.make_async_copy(dst_hbm.at[block], didx.at[s3], isem.at[1, s3])

    def issue_gather(s, nrows):
        def body(j, carry):
            for u in range(COPY_UNROLL):
                r = j * COPY_UNROLL + u
                pltpu.make_async_copy(h_hbm.at[sidx[s, 0, r]], xbuf.at[s, r], gsem.at[s]).start(priority=u % 2)
            return carry
        lax.fori_loop(0, nrows // COPY_UNROLL, body, 0)

    def issue_scatter(s, s3, nrows):
        def body(j, carry):
            for u in range(COPY_UNROLL):
                r = j * COPY_UNROLL + u
                pltpu.make_async_copy(ybuf.at[s, r], y_hbm.at[didx[s3, 0, r]], ssem.at[s]).start(priority=u % 2)
            return carry
        lax.fori_loop(0, nrows // COPY_UNROLL, body, 0)

    def wait_rows(buf, sem, s, nrows):
        @pl.when(nrows > 0)
        def _():
            rows = pl.ds(0, nrows)
            pltpu.make_async_copy(buf.at[s, rows], buf.at[s, rows], sem.at[s]).wait()

    @pl.when(n < nbr)
    def _():
        @pl.when(n == 0)
        def _():
            src_copy(0, 0).start()
            dst_copy(0, 0).start()
            dst_copy(dump_block, ring - 1).start()
            ybuf[1] = jnp.zeros(ybuf.shape[1:], ybuf.dtype)
            xbuf[...] = jnp.zeros(xbuf.shape, xbuf.dtype)
            src_copy(0, 0).wait()
            dst_copy(0, 0).wait()
            dst_copy(dump_block, ring - 1).wait()
            issue_gather(0, rows_of(0))
            nxt0 = jnp.minimum(1, last)
            src_copy(nxt0, 1).start()
            dst_copy(nxt0, 1).start()

        wait_rows(xbuf, gsem, slot, rows_of(n))

        @pl.when(n < last)
        def _():
            nn = jnp.minimum(n + 2, last)
            src_copy(nn, slot).start()
            dst_copy(nn, (n + 2) % ring).start()

        changed = (n == 0) | (be_ref[n] != be_ref[jnp.maximum(n - 1, 0)])

        @pl.when(changed)
        def _():
            wg_bf[...] = wg_ref[0].astype(_BF16)
            wu_bf[...] = wu_ref[0].astype(_BF16)
            wd_bf[...] = wd_ref[0].astype(_BF16)

        wait_rows(ybuf, ssem, slot, jnp.where(n >= 2, rows_of(jnp.maximum(n - 2, 0)), jnp.where(n == 1, blk, 0)))

        nxt = jnp.minimum(n + 1, last)
        src_copy(nxt, other).wait()
        dst_copy(nxt, (n + 1) % ring).wait()

        issue_scatter(other, (n + ring - 1) % ring,
                      jnp.where(n >= 1, rows_of(jnp.maximum(n - 1, 0)), blk))
        xmat[...] = _from_token_major(xbuf[slot])
        issue_gather(other, jnp.where(n < last, rows_of(nxt), 0))
        gcol = jnp.broadcast_to(sg_ref[0], (LANES, blk)).T
        a = jax.nn.silu(_dot(xmat[...], wg_bf[...])) * _dot(xmat[...], wu_bf[...])
        a = (a * _lanes([gcol] * (a.shape[1] // LANES))).astype(_BF16)
        ybuf[slot] = _to_token_major(_dot(a, wd_bf[...]).astype(_BF16))

        @pl.when(n == last)
        def _():
            issue_scatter(slot, n % ring, rows_of(n))
            wait_rows(ybuf, ssem, other, jnp.where(n >= 1, rows_of(jnp.maximum(n - 1, 0)), blk))
            wait_rows(ybuf, ssem, slot, rows_of(n))


def _moe_experts(h3, block_expert, nb_real, block_rows, slot_src, slot_dst, slot_gate, we_gate, we_up, we_down):
    n, c, _ = h3.shape
    d = c * LANES
    nb = block_expert.shape[0]
    blk = MOE_BLOCK
    e, _, de = we_gate.shape
    grid_spec = pltpu.PrefetchScalarGridSpec(
        num_scalar_prefetch=3,
        grid=(nb,),
        in_specs=[
            pl.BlockSpec(memory_space=pl.ANY),
            pl.BlockSpec(memory_space=pl.ANY),
            pl.BlockSpec(memory_space=pl.ANY),
            pl.BlockSpec((1, 1, blk), lambda i, be, nbr, cnt: (i, 0, 0)),
            pl.BlockSpec((1, d, de), lambda i, be, nbr, cnt: (be[i], 0, 0)),
            pl.BlockSpec((1, d, de), lambda i, be, nbr, cnt: (be[i], 0, 0)),
            pl.BlockSpec((1, de, d), lambda i, be, nbr, cnt: (be[i], 0, 0)),
        ],
        out_specs=pl.BlockSpec(memory_space=pl.ANY),
        scratch_shapes=[
            pltpu.VMEM((2, blk, c, LANES), _BF16), pltpu.VMEM((2, blk, c, LANES), _BF16),
            pltpu.VMEM((blk, d), _BF16),
            pltpu.SMEM((2, 1, blk), jnp.int32), pltpu.SMEM((4, 1, blk), jnp.int32),
            pltpu.VMEM((d, de), _BF16), pltpu.VMEM((d, de), _BF16), pltpu.VMEM((de, d), _BF16),
            pltpu.SemaphoreType.DMA((2,)), pltpu.SemaphoreType.DMA((2,)), pltpu.SemaphoreType.DMA((2, 4)),
        ],
    )
    return pl.pallas_call(
        _moe_kernel,
        grid_spec=grid_spec,
        out_shape=jax.ShapeDtypeStruct((TOP_K * n + blk, c, LANES), _BF16),
        compiler_params=_cparams(("arbitrary",)),
        name="moe_experts",
    )(block_expert, nb_real, block_rows, slot_src, slot_dst, h3, slot_gate, we_gate, we_up, we_down)


def _dispatch_plan(top_idx, gates, counts, n):
    blk = MOE_BLOCK
    a = n * TOP_K
    nb = (a + N_EXPERTS * (blk - 1)) // blk
    order = jnp.argsort(top_idx.reshape(a)).astype(jnp.int32)
    counts = counts.astype(jnp.int32)
    padded = (counts + blk - 1) // blk * blk
    start = jnp.cumsum(counts) - counts
    pstart = jnp.cumsum(padded) - padded
    block_end = jnp.cumsum(padded) // blk
    blocks = jnp.arange(nb, dtype=jnp.int32)
    block_expert = jnp.minimum(jnp.sum((block_end[None, :] <= blocks[:, None]).astype(jnp.int32), axis=1),
                               N_EXPERTS - 1)
    off = blocks * blk - pstart[block_expert]
    base = start[block_expert] + off
    lane = jnp.arange(blk, dtype=jnp.int32)[None, :]
    rows = jnp.take(order, base[:, None] + lane, mode="clip")
    real = (off[:, None] + lane) < counts[block_expert][:, None]
    tok = rows // TOP_K
    kk = rows - tok * TOP_K
    slot_src = jnp.where(real, tok, 0)
    slot_dst = jnp.where(real, kk * n + tok, TOP_K * n + lane)
    slot_dst = jnp.concatenate([slot_dst, TOP_K * n + lane], axis=0)
    slot_gate = jnp.where(real, jnp.take(gates.reshape(a), rows, mode="clip"), 0.0).reshape(nb, 1, blk)
    nb_real = (jnp.sum(padded) // blk).reshape(1).astype(jnp.int32)
    block_rows = jnp.clip(counts[block_expert] - off, 0, blk).astype(jnp.int32)
    return (block_expert, nb_real, block_rows, slot_src.reshape(nb, 1, blk), slot_dst.reshape(nb + 1, 1, blk),
            slot_gate)


def _combine_kernel(x_ref, h_ref, wsg_ref, wsu_ref, wsd_ref, *rest):
    y_refs, o_ref = rest[:TOP_K], rest[TOP_K]
    h = _from_token_major(h_ref[...])
    a = (jax.nn.silu(_dot(h, wsg_ref[...])) * _dot(h, wsu_ref[...])).astype(_BF16)
    routed = y_refs[0][...].astype(_F32)
    for k in range(1, TOP_K):
        routed = routed + y_refs[k][...].astype(_F32)
    o_ref[...] = x_ref[...] + _dot(a, wsd_ref[...]) + _from_token_major(routed)


def _combine(x1, h3, y3, ws_gate, ws_up, ws_down):
    n, d = x1.shape
    c = d // LANES
    tm = COMBINE_TM
    nt = n // tm
    de = ws_gate.shape[1]
    const = lambda r, cc: pl.BlockSpec((r, cc), lambda i: (0, 0))
    row = pl.BlockSpec((tm, d), lambda i: (i, 0))
    y_specs = [pl.BlockSpec((tm, c, LANES), functools.partial(lambda i, k: (k * nt + i, 0, 0), k=k))
               for k in range(TOP_K)]
    return pl.pallas_call(
        _combine_kernel,
        grid=(nt,),
        in_specs=[row, pl.BlockSpec((tm, c, LANES), lambda i: (i, 0, 0)), const(d, de), const(d, de), const(de, d)]
        + y_specs,
        out_specs=row,
        out_shape=jax.ShapeDtypeStruct((n, d), _F32),
        compiler_params=_cparams(("arbitrary",)),
        name="moe_combine",
    )(x1, h3, ws_gate.astype(_BF16), ws_up.astype(_BF16), ws_down.astype(_BF16), *([y3] * TOP_K))


def _moe_block(x1, norm_ffn_g, w_router, router_bias, we_gate, we_up, we_down, ws_gate, ws_up, ws_down):
    n = x1.shape[0]
    h3, top_idx_t, gates_t, counts = _router(x1, norm_ffn_g, w_router, router_bias)
    block_expert, nb_real, block_rows, slot_src, slot_dst, slot_gate = _dispatch_plan(
        top_idx_t.T, gates_t.T, counts.reshape(N_EXPERTS), n)
    y3 = _moe_experts(h3, block_expert, nb_real, block_rows, slot_src, slot_dst, slot_gate,
                      we_gate, we_up, we_down)
    return _combine(x1, h3, y3, ws_gate, ws_up, ws_down)


def kernel(x, mem, norm_attn_g, w_in, nsa_pe_k, nsa_w_ck, nsa_pe_v, nsa_w_cv, nsa_q_g, nsa_k_g, rel_bias, fox_b_f, fox_q_g, fox_k_g, norm_mem_g, w_mem_kv, mem_q_g, mem_k_g, w_o_nsa, w_o_fox, w_o_mem, w_out, norm_ffn_g, w_router, router_bias, we_gate, we_up, we_down, ws_gate, ws_up, ws_down):
    b, t, d = x.shape
    assert norm_attn_g.shape[0] == 1, "single-layer problem"
    l = 0
    proj2d, o_nsa, o_fox, o_mem = _attention_mixers(
        x, mem, norm_attn_g[l], w_in[l], nsa_pe_k[l], nsa_w_ck[l], nsa_pe_v[l], nsa_w_cv[l], nsa_q_g[l],
        nsa_k_g[l], rel_bias, fox_b_f[l], fox_q_g[l], fox_k_g[l], norm_mem_g[l], w_mem_kv[l], mem_q_g[l],
        mem_k_g[l])
    x1 = _merge(x.reshape(b * t, d), o_nsa, o_fox, o_mem, proj2d, w_o_nsa[l], w_o_fox[l], w_o_mem[l], w_out[l])
    out = _moe_block(x1, norm_ffn_g[l], w_router[l], router_bias[l], we_gate[l], we_up[l], we_down[l],
                     ws_gate[l], ws_up[l], ws_down[l])
    return out.reshape(b, t, d)
```

```python
import functools
import math

import jax
import jax.numpy as jnp
import numpy as np
from jax import lax
from jax.experimental import pallas as pl
from jax.experimental.pallas import tpu as pltpu

D_MODEL = 2048
HEAD_DIM = 128
NSA_HEADS = 8
NSA_KV_HEADS = 2
NSA_GROUP = NSA_HEADS // NSA_KV_HEADS
FOX_HEADS = 4
MEM_HEADS = 4
CMP_LEN = 32
CMP_STRIDE = 16
SLC_LEN = 64
N_SELECT = 16
WINDOW = 512
NUM_BUCKETS = 32
MAX_DISTANCE = 128
N_BRANCHES = 3
N_EXPERTS = 64
TOP_K = 8
D_EXPERT = 512
ROUTED_SCALE = 2.5
ATTN_SCALE = HEAD_DIM ** -0.5
NEG_INF = -1e30
FORCE_SCORE = 1e4
RMS_EPS = 1e-6
LOG2E = math.log2(math.e)

NSA_Q_W = NSA_HEADS * HEAD_DIM
NSA_KV_W = NSA_KV_HEADS * HEAD_DIM
NSA_GATE_W = 3 * NSA_HEADS
FOX_W = FOX_HEADS * HEAD_DIM
MEM_W = MEM_HEADS * HEAD_DIM
MERGE_W = N_BRANCHES * D_MODEL

LANES = 128
VMEM_LIMIT_BYTES = 56 * 1024 * 1024

PROJ_TM = 1024
PROJ_TN = 768
ATT_TQ = 128
FAR_GROUP = 4
FOX_TQ = 256
MEM_TQ = 512
MERGE_TM = 256
ROUTER_TM = 512
MOE_BLOCK = 256
COMBINE_TM = 256

CB_MERGE = 0
CB_ATT = N_BRANCHES * D_MODEL // LANES
CB_QNSA = CB_ATT
CB_KCMP = CB_ATT + 8
CB_VCMP = CB_ATT + 10
CB_KSLC = CB_ATT + 12
CB_VSLC = CB_ATT + 14
CB_KWIN = CB_ATT + 16
CB_VWIN = CB_ATT + 18
CB_FOXQ = CB_ATT + 20
CB_FOXK = CB_ATT + 24
CB_FOXV = CB_ATT + 28
CB_MEMQ = CB_ATT + 32
PROJ_COLS = (CB_ATT + 36) * LANES
SMALL_W = 2 * LANES
FOXF_LANE = 12

_BF16 = jnp.bfloat16
_F32 = jnp.float32


def _cparams(sem):
    return pltpu.CompilerParams(dimension_semantics=sem, vmem_limit_bytes=VMEM_LIMIT_BYTES)


def _dot(a, b):
    return jnp.dot(a, b, preferred_element_type=_F32)


def _dot_nt(a, b):
    return lax.dot_general(a, b, (((1,), (1,)), ((), ())), preferred_element_type=_F32)


def _dot_tn(a, b):
    return lax.dot_general(a, b, (((0,), (0,)), ((), ())), preferred_element_type=_F32)


def _lanes(parts):
    return jnp.concatenate(parts, axis=1)


def _to_token_major(x):
    c = x.shape[1] // LANES
    chunks = jnp.stack([x[:, j * LANES:(j + 1) * LANES] for j in range(c)], axis=0)
    return pltpu.einshape("ctl->tcl", chunks)


def _from_token_major(x3):
    xt = pltpu.einshape("tcl->ctl", x3)
    return _lanes([xt[j] for j in range(x3.shape[1])])


REORDER_TR = 128
REORDER_TC = 512


def _reorder_kernel(pieces, w_ref, o_ref):
    off = 0
    for src, width in pieces:
        for c in range(0, width, REORDER_TC):
            o_ref[:, off + c:off + c + REORDER_TC] = w_ref[:, src + c:src + c + REORDER_TC].astype(o_ref.dtype)
        off += width


def _reorder_cast(w, pieces):
    rows, cols = w.shape
    total = sum(width for _, width in pieces)
    assert rows % REORDER_TR == 0 and all(width % REORDER_TC == 0 for _, width in pieces)
    return pl.pallas_call(
        functools.partial(_reorder_kernel, pieces),
        grid=(rows // REORDER_TR,),
        in_specs=[pl.BlockSpec((REORDER_TR, cols), lambda i: (i, 0))],
        out_specs=pl.BlockSpec((REORDER_TR, total), lambda i: (i, 0)),
        out_shape=jax.ShapeDtypeStruct((rows, total), _BF16),
        compiler_params=_cparams(("arbitrary",)),
        name="reorder_cast",
    )(w)


def _proj_kernel(norm_j0, has_small, x_ref, g_ref, w_ref, cg_ref, cf_ref, *rest):
    if has_small:
        ws_ref, o_ref, os_ref, h_ref = rest
    else:
        o_ref, h_ref = rest
    j = pl.program_id(1)

    @pl.when(j == 0)
    def _():
        x = x_ref[...]
        ms = jnp.mean(x * x, axis=-1, keepdims=True)
        h = (x * lax.rsqrt(ms + RMS_EPS) * g_ref[...]).astype(_BF16)
        h_ref[...] = h
        if has_small:
            os_ref[...] = _dot(h, ws_ref[...])

    y = _dot(h_ref[...], w_ref[...])
    tn = y.shape[1]

    @pl.when(j >= norm_j0)
    def _():
        for c in range(tn // LANES):
            sl = slice(c * LANES, (c + 1) * LANES)
            yh = y[:, sl]
            ms = jnp.mean(yh * yh, axis=-1, keepdims=True)
            scale = jnp.where(cf_ref[:, sl] > 0.0, lax.rsqrt(ms + RMS_EPS), 1.0)
            o_ref[:, sl] = (yh * scale * cg_ref[:, sl]).astype(o_ref.dtype)

    @pl.when(j < norm_j0)
    def _():
        o_ref[...] = y.astype(o_ref.dtype)


def _rms_project(x2d, g, w, col_gain, col_flag, n_plain_cols, w_small, tm, tn):
    n, d = x2d.shape
    c = w.shape[1]
    has_small = w_small is not None
    in_specs = [
        pl.BlockSpec((tm, d), lambda i, j: (i, 0)),
        pl.BlockSpec((1, d), lambda i, j: (0, 0)),
        pl.BlockSpec((d, tn), lambda i, j: (0, j)),
        pl.BlockSpec((1, tn), lambda i, j: (0, j)),
        pl.BlockSpec((1, tn), lambda i, j: (0, j)),
    ]
    args = [x2d, g.reshape(1, d), w, col_gain.reshape(1, c), col_flag.reshape(1, c)]
    out_shape = [jax.ShapeDtypeStruct((n, c), _BF16)]
    out_specs = [pl.BlockSpec((tm, tn), lambda i, j: (i, j))]
    if has_small:
        ws = w_small.shape[1]
        in_specs.append(pl.BlockSpec((d, ws), lambda i, j: (0, 0)))
        args.append(w_small)
        out_shape.append(jax.ShapeDtypeStruct((n, ws), _F32))
        out_specs.append(pl.BlockSpec((tm, ws), lambda i, j: (i, 0)))
    assert n % tm == 0 and c % tn == 0 and n_plain_cols % tn == 0
    res = pl.pallas_call(
        functools.partial(_proj_kernel, n_plain_cols // tn, has_small),
        grid=(n // tm, c // tn),
        in_specs=in_specs,
        out_specs=out_specs,
        out_shape=out_shape,
        scratch_shapes=[pltpu.VMEM((tm, d), _BF16)],
        compiler_params=_cparams(("arbitrary", "arbitrary")),
        name="rms_project",
    )(*args)
    return res if has_small else res[0]


def _cmp_kernel(x_ref, wlo_ref, whi_ref, pelo_ref, pehi_ref, kg_ref, o_ref):
    j = pl.program_id(1)
    x = x_ref[0, 0]
    nchunk = x.shape[0]
    ylo = _dot(x, wlo_ref[0])
    yhi = _dot(x, whi_ref[0])
    pe = _dot(pelo_ref[0], wlo_ref[0]) + _dot(pehi_ref[0], whi_ref[0])
    y = ylo + pltpu.roll(yhi, nchunk - 1, 0) + pe[0:1, :]

    @pl.when(j < NSA_KV_HEADS)
    def _():
        ms = jnp.mean(y * y, axis=-1, keepdims=True)
        o_ref[0, 0] = (y * lax.rsqrt(ms + RMS_EPS) * kg_ref[...]).astype(o_ref.dtype)

    @pl.when(j >= NSA_KV_HEADS)
    def _():
        o_ref[0, 0] = y.T.astype(o_ref.dtype)


def _nsa_compress(cmp_in, w_ck, w_cv, pe_k, pe_v, k_g):
    b, nj, t, dk = cmp_in.shape
    nchunk = t // CMP_STRIDE
    half = CMP_LEN // 2
    assert nchunk == dk
    x = cmp_in.reshape(b, nj, nchunk, CMP_STRIDE * dk)

    def halves(w):
        return (w[:half].reshape(half * dk, dk).astype(_BF16),
                w[half:].reshape(half * dk, dk).astype(_BF16))

    klo, khi = halves(w_ck)
    vlo, vhi = halves(w_cv)
    wlo = jnp.stack([klo, vlo])
    whi = jnp.stack([khi, vhi])

    def pe_halves(pe):
        lo = jnp.broadcast_to(pe[:half].reshape(1, half * dk), (8, half * dk)).astype(_BF16)
        hi = jnp.broadcast_to(pe[half:].reshape(1, half * dk), (8, half * dk)).astype(_BF16)
        return lo, hi

    pklo, pkhi = pe_halves(pe_k)
    pvlo, pvhi = pe_halves(pe_v)
    pelo = jnp.stack([pklo, pvlo])
    pehi = jnp.stack([pkhi, pvhi])
    kv = lambda bb, j: (j // NSA_KV_HEADS, 0, 0)
    return pl.pallas_call(
        _cmp_kernel,
        grid=(b, nj),
        in_specs=[
            pl.BlockSpec((1, 1, nchunk, CMP_STRIDE * dk), lambda bb, j: (bb, j, 0, 0)),
            pl.BlockSpec((1, half * dk, dk), kv),
            pl.BlockSpec((1, half * dk, dk), kv),
            pl.BlockSpec((1, 8, half * dk), kv),
            pl.BlockSpec((1, 8, half * dk), kv),
            pl.BlockSpec((1, dk), lambda bb, j: (0, 0)),
        ],
        out_specs=pl.BlockSpec((1, 1, nchunk, dk), lambda bb, j: (bb, j, 0, 0)),
        out_shape=jax.ShapeDtypeStruct((b, nj, nchunk, dk), _BF16),
        compiler_params=_cparams(("arbitrary", "arbitrary")),
        name="nsa_compress",
    )(x, wlo, whi, pelo, pehi, k_g.reshape(1, dk).astype(_F32))


def _t_update(state, tiles):
    m, l, acc = state
    masked = [s if mask is None else jnp.where(mask, s, NEG_INF) for s, mask, _ in tiles]
    m_new = m
    for sm in masked:
        m_new = jnp.maximum(m_new, jnp.max(sm, axis=0, keepdims=True))
    alpha = jnp.exp2(m - m_new)
    l_new = alpha * l
    acc_new = alpha * acc
    for sm, (_, _, v) in zip(masked, tiles):
        e = jnp.exp2(sm - m_new)
        l_new = l_new + jnp.sum(e, axis=0, keepdims=True)
        acc_new = acc_new + _dot_tn(v, e.astype(_BF16))
    return m_new, l_new, acc_new


def _t_init(nq):
    return (jnp.full((1, nq), NEG_INF, _F32), jnp.zeros((1, nq), _F32), jnp.zeros((HEAD_DIM, nq), _F32))


def _t_finish(state):
    _, l, acc = state
    return acc / l


def _nsa_kernel(q_ref, kc_ref, vct_ref, ks_ref, vs_ref, kw_ref, vw_ref, gl_ref,
                wt_ref, dt_ref, ovt_ref, ext_ref, o_ref, sel_ref):
    i = pl.program_id(2)
    tq = ATT_TQ
    hg = NSA_GROUP
    nq = hg * tq
    n_slc = ovt_ref.shape[0]

    q = q_ref[0]
    qt = _lanes([q[:, h * HEAD_DIM:(h + 1) * HEAD_DIM].astype(_F32).T for h in range(hg)]).astype(_BF16)

    key = lax.broadcasted_iota(jnp.int32, (LANES, nq), 0)
    qry = lax.broadcasted_iota(jnp.int32, (LANES, nq), 1) & (tq - 1)

    def ktile(ref, kt):
        return ref[0, pl.ds(pl.multiple_of(kt * LANES, LANES), LANES), :]

    def near_bias(d):
        return _lanes([dt_ref[h, d] for h in range(hg)])

    woff = pl.multiple_of(wt_ref.shape[1] - LANES - 8 - (LANES // CMP_STRIDE) * i, 8)
    s = _dot(kc_ref[0, 0], qt) + _lanes([wt_ref[h, pl.ds(woff, LANES), :] for h in range(hg)])
    n_win = WINDOW // tq
    win_tiles = []
    for d in range(n_win, -1, -1):
        ktd = jnp.maximum(i - d, 0)
        sc = _dot(ktile(kw_ref, ktd), qt)
        if d <= 1:
            sc = sc + near_bias(d)
        if d == n_win:
            mk = (qry < key) & (i >= d)
        elif d == 0:
            mk = key <= qry
        else:
            mk = jnp.broadcast_to(i >= d, (LANES, nq))
        win_tiles.append((sc, mk, ktile(vw_ref, ktd)))

    mask_c = (i * tq + qry) >= (CMP_STRIDE * key + CMP_LEN - 1)
    mx = jnp.max(jnp.where(mask_c, s, NEG_INF), axis=0, keepdims=True)
    e = jnp.where(mask_c, jnp.exp2(s - mx), 0.0)
    l = jnp.sum(e, axis=0, keepdims=True)
    p_c = e / jnp.where(l > 0.0, l, 1.0)
    o_c = _dot(vct_ref[0, 0], p_c.astype(_BF16))

    ps = p_c[:, 0:tq]
    for h in range(1, hg):
        ps = ps + p_c[:, h * tq:(h + 1) * tq]
    ps_hi = ps.astype(_BF16)
    ps_lo = (ps - ps_hi.astype(_F32)).astype(_BF16)
    imp = _dot(ovt_ref[...], ps_hi) + _dot(ovt_ref[...], ps_lo)
    blk = lax.broadcasted_iota(jnp.int32, (n_slc, tq), 0)
    tpos = i * tq + lax.broadcasted_iota(jnp.int32, (n_slc, tq), 1)
    cur = tpos // SLC_LEN
    valid = blk * SLC_LEN <= tpos
    forced = (blk == 0) | (blk == cur) | (blk == cur - 1)
    score = jnp.where(valid, jnp.where(forced, FORCE_SCORE, imp), -1.0)
    rank = jnp.zeros((n_slc, tq), _F32)
    for k in range(n_slc):
        ck = score[k:k + 1, :]
        beats = (ck > score) | ((ck == score) & (blk > k))
        rank = rank + jnp.where(beats, 1.0, 0.0)
    sel = jnp.where(rank < float(min(N_SELECT, n_slc)), 1.0, 0.0).astype(_BF16)
    selfull = _dot(ext_ref[...], sel)
    for kt in range(ext_ref.shape[0] // LANES):
        sel_ref[kt] = selfull[kt * LANES:(kt + 1) * LANES, :]

    def sel_mask(kt, ok=True):
        return _lanes([sel_ref[kt]] * hg) > jnp.where(ok, 0.5, 2.0)

    o_w = _t_finish(_t_update(_t_init(nq), win_tiles))

    n_far = jnp.maximum(i - 1, 0)

    def far_body(j, st):
        k0 = pl.multiple_of(j * (FAR_GROUP * LANES), FAR_GROUP * LANES)
        sg = _dot(ks_ref[0, pl.ds(k0, FAR_GROUP * LANES), :], qt)
        tiles = []
        for u in range(FAR_GROUP):
            kt = FAR_GROUP * j + u
            tiles.append((sg[u * LANES:(u + 1) * LANES], sel_mask(kt, kt < n_far), ktile(vs_ref, kt)))
        return _t_update(st, tiles)

    st = lax.fori_loop(0, (n_far + FAR_GROUP - 1) // FAR_GROUP, far_body, _t_init(nq))
    kt1 = jnp.maximum(i - 1, 0)
    st = _t_update(st, [
        (_dot(ktile(ks_ref, kt1), qt) + near_bias(1), sel_mask(kt1, i >= 1), ktile(vs_ref, kt1)),
        (_dot(ktile(ks_ref, i), qt) + near_bias(0), sel_mask(i) & (key <= qry), ktile(vs_ref, i))])
    o_s = _t_finish(st)

    glt = jax.nn.sigmoid(gl_ref[0]).T
    grow = lambda br: _lanes([glt[br * hg + h:br * hg + h + 1, :] for h in range(hg)])
    o = grow(0) * o_c + grow(1) * o_s + grow(2) * o_w
    for h in range(hg):
        o_ref[0, :, h * HEAD_DIM:(h + 1) * HEAD_DIM] = o[:, h * tq:(h + 1) * tq].T.astype(o_ref.dtype)


def _t5_bucket(rel):
    n = np.maximum(rel, 0)
    max_exact = NUM_BUCKETS // 2
    ratio = np.maximum(n, 1).astype(np.float32) / np.float32(max_exact)
    log_ratio = np.log(ratio) / np.float32(math.log(MAX_DISTANCE / max_exact))
    large = np.minimum(max_exact + (log_ratio * np.float32(NUM_BUCKETS - max_exact)).astype(np.int32),
                       NUM_BUCKETS - 1)
    return np.where(n < max_exact, n, large).astype(np.int32)


def _bias_lookup(rb, bucket):
    bk = jnp.asarray(bucket.astype(np.int8))[None]
    ex = (slice(None),) + (None,) * bucket.ndim
    tab = jnp.broadcast_to(rb[:, 0][ex], (rb.shape[0],) + bucket.shape)
    for k in range(1, NUM_BUCKETS):
        tab = jnp.where(bk == k, rb[:, k][ex], tab)
    return tab


def _nsa_tables(rel_bias):
    rb = rel_bias.astype(_F32).T * LOG2E
    r = np.arange(ATT_TQ)
    rel_d = (np.arange(2) * ATT_TQ)[:, None, None] + r[None, None, :] - r[None, :, None]
    assert _t5_bucket(np.array([ATT_TQ + 1]))[0] == NUM_BUCKETS - 1
    u = np.arange(2 * LANES)
    rel_w = r[None, :] - CMP_STRIDE * (u[:, None] - (LANES - 8)) - (CMP_LEN - 1)
    far = rb[:, NUM_BUCKETS - 1]
    dt = _bias_lookup(rb, _t5_bucket(rel_d)) - far[:, None, None, None]
    return dt, _bias_lookup(rb, _t5_bucket(rel_w))


def _nsa_attention(proj, small, kvc, rel_bias):
    b, t, _ = proj.shape
    g, hg, tq = NSA_KV_HEADS, NSA_GROUP, ATT_TQ
    n_slc = t // SLC_LEN
    n_cmp = (t - CMP_LEN) // CMP_STRIDE + 1
    nt = t // LANES
    assert n_cmp <= LANES and kvc.shape[2] == LANES and n_slc % 16 == 0 and tq == LANES
    assert nt % FAR_GROUP == 0
    assert LANES - 8 - (LANES // CMP_STRIDE) * (t // tq - 1) >= 0
    dt, wt = _nsa_tables(rel_bias)
    cstart = np.arange(LANES) * CMP_STRIDE
    sstart = np.arange(n_slc) * SLC_LEN
    overlap = np.clip(np.minimum(cstart[None, :] + CMP_LEN, sstart[:, None] + SLC_LEN)
                      - np.maximum(cstart[None, :], sstart[:, None]), 0, None).astype(np.float32) / CMP_LEN
    overlap[:, n_cmp:] = 0.0
    ovt = jnp.asarray(overlap, _BF16)
    ext = jnp.asarray(np.arange(t)[:, None] // SLC_LEN == np.arange(n_slc)[None, :], _BF16)

    seq = lambda cb: pl.BlockSpec((1, t, LANES), lambda bb, gg, ii: (bb, 0, cb + gg))
    return pl.pallas_call(
        _nsa_kernel,
        grid=(b, g, t // tq),
        in_specs=[
            pl.BlockSpec((1, tq, hg * HEAD_DIM), lambda bb, gg, ii: (bb, ii, CB_QNSA * LANES // (hg * HEAD_DIM) + gg)),
            pl.BlockSpec((1, 1, LANES, HEAD_DIM), lambda bb, gg, ii: (bb, gg, 0, 0)),
            pl.BlockSpec((1, 1, HEAD_DIM, LANES), lambda bb, gg, ii: (bb, NSA_KV_HEADS + gg, 0, 0)),
            seq(CB_KSLC), seq(CB_VSLC), seq(CB_KWIN), seq(CB_VWIN),
            pl.BlockSpec((1, tq, LANES), lambda bb, gg, ii: (bb, ii, gg)),
            pl.BlockSpec((hg, 2 * LANES, tq), lambda bb, gg, ii: (gg, 0, 0)),
            pl.BlockSpec((hg, 2, LANES, tq), lambda bb, gg, ii: (gg, 0, 0, 0)),
            pl.BlockSpec((n_slc, LANES), lambda bb, gg, ii: (0, 0)),
            pl.BlockSpec((t, n_slc), lambda bb, gg, ii: (0, 0)),
        ],
        out_specs=pl.BlockSpec((1, tq, hg * HEAD_DIM), lambda bb, gg, ii: (bb, ii, gg)),
        out_shape=jax.ShapeDtypeStruct((b, t, NSA_Q_W), _BF16),
        scratch_shapes=[pltpu.VMEM((nt, LANES, tq), _F32)],
        compiler_params=_cparams(("arbitrary", "arbitrary", "arbitrary")),
        name="nsa_attention",
    )(proj, kvc, kvc, proj, proj, proj, proj, small, wt, dt, ovt, ext)


def _fox_cum_kernel(s_ref, b_ref, col_ref, row_ref):
    z = s_ref[0] + b_ref[...]
    lf = (jnp.minimum(z, 0.0) - jnp.log1p(jnp.exp(-jnp.abs(z)))) * LOG2E
    x = lf.T
    t = x.shape[1]
    lane = lax.broadcasted_iota(jnp.int32, x.shape, 1)
    sh = 1
    while sh < t:
        x = x + jnp.where(lane >= sh, pltpu.roll(x, sh, 1), 0.0)
        sh *= 2
    row_ref[0] = x[8:16, :]
    col_ref[0] = x.T


def _fox_cumsum(small, b_f):
    b, t, _ = small.shape
    bvec = jnp.zeros((1, LANES), _F32).at[0, FOXF_LANE:FOXF_LANE + FOX_HEADS].set(b_f.astype(_F32))
    return pl.pallas_call(
        _fox_cum_kernel,
        grid=(b,),
        in_specs=[pl.BlockSpec((1, t, LANES), lambda bb: (bb, 0, 1)),
                  pl.BlockSpec((1, LANES), lambda bb: (0, 0))],
        out_specs=[pl.BlockSpec((1, t, LANES), lambda bb: (bb, 0, 0)),
                   pl.BlockSpec((1, 8, t), lambda bb: (bb, 0, 0))],
        out_shape=[jax.ShapeDtypeStruct((b, t, LANES), _F32), jax.ShapeDtypeStruct((b, 8, t), _F32)],
        compiler_params=_cparams(("arbitrary",)),
        name="fox_cumsum",
    )(small, bvec)


def _fox_kernel(q_ref, k_ref, v_ref, cc_ref, cr_ref, o_ref):
    i = pl.program_id(1)
    tq = FOX_TQ
    key = lax.broadcasted_iota(jnp.int32, (tq, tq), 0)
    qry = lax.broadcasted_iota(jnp.int32, (tq, tq), 1)
    heads = range(FOX_HEADS)
    hsl = [slice(h * HEAD_DIM, (h + 1) * HEAD_DIM) for h in heads]
    qts = [q_ref[0, :, hsl[h]].astype(_F32).T.astype(_BF16) for h in heads]
    cqs = [cr_ref[0, FOXF_LANE - 8 + h:FOXF_LANE - 8 + h + 1, :] for h in heads]

    def keys(kt):
        return pl.ds(pl.multiple_of(kt * tq, tq), tq)

    def logits(h, kt):
        ck = cc_ref[0, keys(kt), FOXF_LANE + h:FOXF_LANE + h + 1]
        return _dot(k_ref[0, keys(kt), hsl[h]], qts[h]) + (cqs[h] - ck)

    def body(kt, sts):
        ss = [logits(h, kt) for h in heads]
        return tuple(_t_update(sts[h], [(ss[h], None, v_ref[0, keys(kt), hsl[h]])]) for h in heads)

    sts = lax.fori_loop(0, i, body, tuple(_t_init(tq) for _ in heads))
    ss = [logits(h, i) for h in heads]
    for h in heads:
        st = _t_update(sts[h], [(ss[h], key <= qry, v_ref[0, keys(i), hsl[h]])])
        o_ref[0, :, hsl[h]] = _t_finish(st).T.astype(o_ref.dtype)


def _fox_attention(proj, cum_col, cum_row):
    b, t, _ = proj.shape
    tq = FOX_TQ
    w = FOX_W
    return pl.pallas_call(
        _fox_kernel,
        grid=(b, t // tq),
        in_specs=[
            pl.BlockSpec((1, tq, w), lambda bb, ii: (bb, ii, CB_FOXQ * LANES // w)),
            pl.BlockSpec((1, t, w), lambda bb, ii: (bb, 0, CB_FOXK * LANES // w)),
            pl.BlockSpec((1, t, w), lambda bb, ii: (bb, 0, CB_FOXV * LANES // w)),
            pl.BlockSpec((1, t, LANES), lambda bb, ii: (bb, 0, 0)),
            pl.BlockSpec((1, 8, tq), lambda bb, ii: (bb, 0, ii)),
        ],
        out_specs=pl.BlockSpec((1, tq, w), lambda bb, ii: (bb, ii, 0)),
        out_shape=jax.ShapeDtypeStruct((b, t, w), _BF16),
        compiler_params=_cparams(("arbitrary", "arbitrary")),
        name="fox_attention",
    )(proj, proj, proj, cum_col, cum_row)


def _mem_kernel(q_ref, kv_ref, o_ref):
    for h in range(MEM_HEADS):
        hs = slice(h * HEAD_DIM, (h + 1) * HEAD_DIM)
        vs = slice(MEM_W + h * HEAD_DIM, MEM_W + (h + 1) * HEAD_DIM)
        s = _dot_nt(q_ref[0, :, hs], kv_ref[0, :, hs])
        e = jnp.exp2(s - jnp.max(s, axis=1, keepdims=True))
        p = e / jnp.sum(e, axis=1, keepdims=True)
        o_ref[0, :, hs] = _dot(p.astype(_BF16), kv_ref[0, :, vs]).astype(o_ref.dtype)


def _mem_attention(proj, memkv):
    b, t, _ = proj.shape
    m = memkv.shape[1]
    tq = MEM_TQ
    return pl.pallas_call(
        _mem_kernel,
        grid=(b, t // tq),
        in_specs=[pl.BlockSpec((1, tq, MEM_W), lambda bb, ii: (bb, ii, CB_MEMQ * LANES // MEM_W)),
                  pl.BlockSpec((1, m, 2 * MEM_W), lambda bb, ii: (bb, 0, 0))],
        out_specs=pl.BlockSpec((1, tq, MEM_W), lambda bb, ii: (bb, ii, 0)),
        out_shape=jax.ShapeDtypeStruct((b, t, MEM_W), _BF16),
        compiler_params=_cparams(("arbitrary", "arbitrary")),
        name="mem_attention",
    )(proj, memkv)


def _merge_kernel(x_ref, on_ref, of_ref, om_ref, g0_ref, g1_ref, g2_ref, wn_ref, wf_ref, wm_ref, wo_ref,
                  o_ref, z_ref):
    tn = 512
    for c in range(D_MODEL // tn):
        cs = slice(c * tn, (c + 1) * tn)
        z = (jax.nn.sigmoid(g0_ref[:, cs].astype(_F32)) * _dot(on_ref[...], wn_ref[:, cs])
             + jax.nn.sigmoid(g1_ref[:, cs].astype(_F32)) * _dot(of_ref[...], wf_ref[:, cs])
             + jax.nn.sigmoid(g2_ref[:, cs].astype(_F32)) * _dot(om_ref[...], wm_ref[:, cs]))
        z_ref[:, cs] = z.astype(_BF16)
    o_ref[...] = x_ref[...] + _dot(z_ref[...], wo_ref[...])


def _merge(x2d, o_nsa, o_fox, o_mem, proj2d, w_o_nsa, w_o_fox, w_o_mem, w_out):
    n, d = x2d.shape
    tm = MERGE_TM
    gcb = CB_MERGE * LANES // d
    const = lambda r, c: pl.BlockSpec((r, c), lambda i: (0, 0))
    return pl.pallas_call(
        _merge_kernel,
        grid=(n // tm,),
        in_specs=[
            pl.BlockSpec((tm, d), lambda i: (i, 0)),
            pl.BlockSpec((tm, NSA_Q_W), lambda i: (i, 0)),
            pl.BlockSpec((tm, FOX_W), lambda i: (i, 0)),
            pl.BlockSpec((tm, MEM_W), lambda i: (i, 0)),
            pl.BlockSpec((tm, d), lambda i: (i, gcb)),
            pl.BlockSpec((tm, d), lambda i: (i, gcb + 1)),
            pl.BlockSpec((tm, d), lambda i: (i, gcb + 2)),
            const(NSA_Q_W, d), const(FOX_W, d), const(MEM_W, d), const(d, d),
        ],
        out_specs=pl.BlockSpec((tm, d), lambda i: (i, 0)),
        out_shape=jax.ShapeDtypeStruct((n, d), _F32),
        scratch_shapes=[pltpu.VMEM((tm, d), _BF16)],
        compiler_params=_cparams(("arbitrary",)),
        name="merge_out",
    )(x2d, o_nsa, o_fox, o_mem, proj2d, proj2d, proj2d,
      w_o_nsa.astype(_BF16), w_o_fox.astype(_BF16), w_o_mem.astype(_BF16), w_out.astype(_BF16))


def _nsa_inputs(x, norm_attn_g, w_in, nsa_pe_k, nsa_w_ck, nsa_pe_v, nsa_w_cv, nsa_q_g, nsa_k_g,
                fox_q_g, fox_k_g, mem_q_g):
    b, t, d = x.shape
    ones = lambda k: jnp.ones((k * HEAD_DIM,), _F32)
    zeros = lambda k: jnp.zeros((k * HEAD_DIM,), _F32)
    tile = lambda gv, k: jnp.tile(gv.astype(_F32), k)
    qs = ATTN_SCALE * LOG2E

    g_end = NSA_Q_W + 6 * NSA_KV_W
    f_off = g_end + NSA_GATE_W + 3 * FOX_W
    m_off = f_off + FOX_HEADS + MEM_W
    w_all = _reorder_cast(w_in, ((m_off, MERGE_W), (0, g_end), (g_end + NSA_GATE_W, 3 * FOX_W),
                                 (f_off + FOX_HEADS, MEM_W)))
    gates = w_in[:, g_end:g_end + NSA_GATE_W].reshape(d, 3, NSA_KV_HEADS, NSA_GROUP)
    pad = jnp.zeros((d, LANES - 3 * NSA_GROUP), w_in.dtype)
    w_small = jnp.concatenate([
        gates[:, :, 0, :].reshape(d, 3 * NSA_GROUP), pad,
        gates[:, :, 1, :].reshape(d, 3 * NSA_GROUP), w_in[:, f_off:f_off + FOX_HEADS],
        pad[:, :LANES - 3 * NSA_GROUP - FOX_HEADS]], axis=1).astype(_BF16)
    col_gain = jnp.concatenate([
        jnp.ones((MERGE_W,), _F32),
        tile(nsa_q_g, NSA_HEADS) * qs, ones(2), ones(2), tile(nsa_k_g, 2), ones(2), tile(nsa_k_g, 2), ones(2),
        tile(fox_q_g, FOX_HEADS) * qs, tile(fox_k_g, FOX_HEADS), ones(FOX_HEADS),
        tile(mem_q_g, MEM_HEADS) * qs])
    col_flag = jnp.concatenate([
        jnp.zeros((MERGE_W,), _F32),
        ones(NSA_HEADS), zeros(2), zeros(2), ones(2), zeros(2), ones(2), zeros(2),
        ones(FOX_HEADS), ones(FOX_HEADS), zeros(FOX_HEADS), ones(MEM_HEADS)])

    x2d = x.reshape(b * t, d)
    proj2d, small2d = _rms_project(x2d, norm_attn_g, w_all, col_gain, col_flag, CB_ATT * LANES, w_small,
                                   PROJ_TM, PROJ_TN)
    proj = proj2d.reshape(b, t, PROJ_COLS)
    small = small2d.reshape(b, t, SMALL_W)

    cmp_in = proj[:, :, CB_KCMP * LANES:(CB_KCMP + 4) * LANES].reshape(b, t, 4, HEAD_DIM).transpose(0, 2, 1, 3)
    kvc = _nsa_compress(cmp_in, nsa_w_ck, nsa_w_cv, nsa_pe_k, nsa_pe_v, nsa_k_g)
    return proj, small, kvc


def _attention_mixers(x, mem, norm_attn_g, w_in, nsa_pe_k, nsa_w_ck, nsa_pe_v, nsa_w_cv, nsa_q_g, nsa_k_g,
                      rel_bias, fox_b_f, fox_q_g, fox_k_g, norm_mem_g, w_mem_kv, mem_q_g, mem_k_g):
    b, t, d = x.shape
    ones = lambda k: jnp.ones((k * HEAD_DIM,), _F32)
    zeros = lambda k: jnp.zeros((k * HEAD_DIM,), _F32)
    tile = lambda gv, k: jnp.tile(gv.astype(_F32), k)
    proj, small, kvc = _nsa_inputs(x, norm_attn_g, w_in, nsa_pe_k, nsa_w_ck, nsa_pe_v, nsa_w_cv, nsa_q_g, nsa_k_g,
                                   fox_q_g, fox_k_g, mem_q_g)
    proj2d = proj.reshape(b * t, PROJ_COLS)
    o_nsa = _nsa_attention(proj, small, kvc, rel_bias)

    cum_col, cum_row = _fox_cumsum(small, fox_b_f)
    o_fox = _fox_attention(proj, cum_col, cum_row)

    mm = mem.shape[1]
    mem_gain = jnp.concatenate([tile(mem_k_g, MEM_HEADS), ones(MEM_HEADS)])
    mem_flag = jnp.concatenate([ones(MEM_HEADS), zeros(MEM_HEADS)])
    memkv = _rms_project(mem.reshape(b * mm, d), norm_mem_g, w_mem_kv.astype(_BF16), mem_gain, mem_flag,
                         0, None, mm, MEM_W).reshape(b, mm, 2 * MEM_W)
    o_mem = _mem_attention(proj, memkv)
    return proj2d, o_nsa.reshape(b * t, NSA_Q_W), o_fox.reshape(b * t, FOX_W), o_mem.reshape(b * t, MEM_W)


def _router_kernel(x_ref, g_ref, whi_ref, wlo_ref, b_ref, h_ref, idx_ref, gate_ref, cnt_ref):
    x = x_ref[...]
    ms = jnp.mean(x * x, axis=-1, keepdims=True)
    h = x * lax.rsqrt(ms + RMS_EPS) * g_ref[...]
    h_hi = h.astype(_BF16)
    h_ref[...] = _to_token_major(h_hi)
    h_lo = (h - h_hi.astype(_F32)).astype(_BF16)
    logits = _dot_nt(whi_ref[...], h_hi) + _dot_nt(whi_ref[...], h_lo) + _dot_nt(wlo_ref[...], h_hi)
    scores = jax.nn.sigmoid(logits)
    sb = scores + b_ref[...]
    eidx = lax.broadcasted_iota(jnp.int32, sb.shape, 0)
    idxs, vals = [], []
    for _ in range(TOP_K):
        m = jnp.max(sb, axis=0, keepdims=True)
        idx = jnp.min(jnp.where(sb == m, eidx, N_EXPERTS), axis=0, keepdims=True)
        hit = eidx == idx
        vals.append(jnp.sum(jnp.where(hit, scores, 0.0), axis=0, keepdims=True))
        idxs.append(idx)
        sb = jnp.where(hit, NEG_INF, sb)
    top_s = jnp.concatenate(vals, axis=0)
    idx_ref[...] = jnp.concatenate(idxs, axis=0)
    gate_ref[...] = top_s / jnp.sum(top_s, axis=0, keepdims=True) * ROUTED_SCALE

    @pl.when(pl.program_id(0) == 0)
    def _():
        cnt_ref[...] = jnp.zeros(cnt_ref.shape, cnt_ref.dtype)

    picked = jnp.where(sb < 0.5 * NEG_INF, 1.0, 0.0)
    cnt_ref[...] += jnp.sum(picked, axis=1, keepdims=True)


def _router(x1, norm_g, w_router, router_bias):
    n, d = x1.shape
    tm = ROUTER_TM
    wt = w_router.astype(_F32).T
    w_hi = wt.astype(_BF16)
    w_lo = (wt - w_hi.astype(_F32)).astype(_BF16)
    const = lambda r, c: pl.BlockSpec((r, c), lambda i: (0, 0))
    return pl.pallas_call(
        _router_kernel,
        grid=(n // tm,),
        in_specs=[pl.BlockSpec((tm, d), lambda i: (i, 0)), const(1, d), const(N_EXPERTS, d), const(N_EXPERTS, d),
                  const(N_EXPERTS, 1)],
        out_specs=[pl.BlockSpec((tm, d // LANES, LANES), lambda i: (i, 0, 0)),
                   pl.BlockSpec((TOP_K, tm), lambda i: (0, i)),
                   pl.BlockSpec((TOP_K, tm), lambda i: (0, i)),
                   const(N_EXPERTS, 1)],
        out_shape=[jax.ShapeDtypeStruct((n, d // LANES, LANES), _BF16), jax.ShapeDtypeStruct((TOP_K, n), jnp.int32),
                   jax.ShapeDtypeStruct((TOP_K, n), _F32), jax.ShapeDtypeStruct((N_EXPERTS, 1), _F32)],
        compiler_params=_cparams(("arbitrary",)),
        name="moe_router",
    )(x1, norm_g.reshape(1, d).astype(_F32), w_hi, w_lo, router_bias.reshape(N_EXPERTS, 1).astype(_F32))


def _moe_kernel(be_ref, nbr_ref, src_hbm, dst_hbm, h_hbm, sg_ref, wg_ref, wu_ref, wd_ref, y_hbm,
                xbuf, ybuf, xmat, sidx, didx, wg_bf, wu_bf, wd_bf, gsem, ssem, isem):
    n = pl.program_id(0)
    nbr = nbr_ref[0]
    last = nbr - 1
    slot = n % 2
    other = 1 - slot
    blk = MOE_BLOCK
    dump_block = dst_hbm.shape[0] - 1
    ring = didx.shape[0]

    def src_copy(block, s):
        return pltpu.make_async_copy(src_hbm.at[block], sidx.at[s], isem.at[0, s])

    def dst_copy(block, s3):
        return pltpu.make_async_copy(dst_hbm.at[block], didx.at[s3], isem.at[1, s3])

    def issue_gather(s):
        for r in range(blk):
            tok = sidx[s, 0, r]
            pltpu.make_async_copy(h_hbm.at[tok], xbuf.at[s, r], gsem.at[s]).start(priority=r % 2)

    def issue_scatter(s, s3):
        for r in range(blk):
            row = didx[s3, 0, r]
            pltpu.make_async_copy(ybuf.at[s, r], y_hbm.at[row], ssem.at[s]).start(priority=r % 2)

    def wait_rows(buf, sem, s):
        pltpu.make_async_copy(buf.at[s], buf.at[s], sem.at[s]).wait()

    @pl.when(n < nbr)
    def _():
        @pl.when(n == 0)
        def _():
            src_copy(0, 0).start()
            dst_copy(0, 0).start()
            dst_copy(dump_block, ring - 1).start()
            ybuf[1] = jnp.zeros(ybuf.shape[1:], ybuf.dtype)
            src_copy(0, 0).wait()
            dst_copy(0, 0).wait()
            dst_copy(dump_block, ring - 1).wait()
            issue_gather(0)
            nxt0 = jnp.minimum(1, last)
            src_copy(nxt0, 1).start()
            dst_copy(nxt0, 1).start()

        wait_rows(xbuf, gsem, slot)

        @pl.when(n < last)
        def _():
            nn = jnp.minimum(n + 2, last)
            src_copy(nn, slot).start()
            dst_copy(nn, (n + 2) % ring).start()

        changed = (n == 0) | (be_ref[n] != be_ref[jnp.maximum(n - 1, 0)])

        @pl.when(changed)
        def _():
            wg_bf[...] = wg_ref[0].astype(_BF16)
            wu_bf[...] = wu_ref[0].astype(_BF16)
            wd_bf[...] = wd_ref[0].astype(_BF16)

        @pl.when(n >= 1)
        def _():
            wait_rows(ybuf, ssem, slot)

        nxt = jnp.minimum(n + 1, last)
        src_copy(nxt, other).wait()
        dst_copy(nxt, (n + 1) % ring).wait()

        issue_scatter(other, (n + ring - 1) % ring)
        xmat[...] = _from_token_major(xbuf[slot])
        issue_gather(other)
        gcol = jnp.broadcast_to(sg_ref[0], (LANES, blk)).T
        a = jax.nn.silu(_dot(xmat[...], wg_bf[...])) * _dot(xmat[...], wu_bf[...])
        a = (a * _lanes([gcol] * (a.shape[1] // LANES))).astype(_BF16)
        ybuf[slot] = _to_token_major(_dot(a, wd_bf[...]).astype(_BF16))

        @pl.when(n == last)
        def _():
            issue_scatter(slot, n % ring)
            wait_rows(xbuf, gsem, other)
            wait_rows(ybuf, ssem, other)
            wait_rows(ybuf, ssem, slot)


def _moe_experts(h3, block_expert, nb_real, slot_src, slot_dst, slot_gate, we_gate, we_up, we_down):
    n, c, _ = h3.shape
    d = c * LANES
    nb = block_expert.shape[0]
    blk = MOE_BLOCK
    e, _, de = we_gate.shape
    grid_spec = pltpu.PrefetchScalarGridSpec(
        num_scalar_prefetch=2,
        grid=(nb,),
        in_specs=[
            pl.BlockSpec(memory_space=pl.ANY),
            pl.BlockSpec(memory_space=pl.ANY),
            pl.BlockSpec(memory_space=pl.ANY),
            pl.BlockSpec((1, 1, blk), lambda i, be, nbr: (i, 0, 0)),
            pl.BlockSpec((1, d, de), lambda i, be, nbr: (be[i], 0, 0)),
            pl.BlockSpec((1, d, de), lambda i, be, nbr: (be[i], 0, 0)),
            pl.BlockSpec((1, de, d), lambda i, be, nbr: (be[i], 0, 0)),
        ],
        out_specs=pl.BlockSpec(memory_space=pl.ANY),
        scratch_shapes=[
            pltpu.VMEM((2, blk, c, LANES), _BF16), pltpu.VMEM((2, blk, c, LANES), _BF16),
            pltpu.VMEM((blk, d), _BF16),
            pltpu.SMEM((2, 1, blk), jnp.int32), pltpu.SMEM((4, 1, blk), jnp.int32),
            pltpu.VMEM((d, de), _BF16), pltpu.VMEM((d, de), _BF16), pltpu.VMEM((de, d), _BF16),
            pltpu.SemaphoreType.DMA((2,)), pltpu.SemaphoreType.DMA((2,)), pltpu.SemaphoreType.DMA((2, 4)),
        ],
    )
    return pl.pallas_call(
        _moe_kernel,
        grid_spec=grid_spec,
        out_shape=jax.ShapeDtypeStruct((TOP_K * n + blk, c, LANES), _BF16),
        compiler_params=_cparams(("arbitrary",)),
        name="moe_experts",
    )(block_expert, nb_real, slot_src, slot_dst, h3, slot_gate, we_gate, we_up, we_down)


def _dispatch_plan(top_idx, gates, counts, n):
    blk = MOE_BLOCK
    a = n * TOP_K
    nb = (a + N_EXPERTS * (blk - 1)) // blk
    order = jnp.argsort(top_idx.reshape(a)).astype(jnp.int32)
    counts = counts.astype(jnp.int32)
    padded = (counts + blk - 1) // blk * blk
    start = jnp.cumsum(counts) - counts
    pstart = jnp.cumsum(padded) - padded
    block_end = jnp.cumsum(padded) // blk
    blocks = jnp.arange(nb, dtype=jnp.int32)
    block_expert = jnp.minimum(jnp.sum((block_end[None, :] <= blocks[:, None]).astype(jnp.int32), axis=1),
                               N_EXPERTS - 1)
    off = blocks * blk - pstart[block_expert]
    base = start[block_expert] + off
    lane = jnp.arange(blk, dtype=jnp.int32)[None, :]
    rows = jnp.take(order, base[:, None] + lane, mode="clip")
    real = (off[:, None] + lane) < counts[block_expert][:, None]
    tok = rows // TOP_K
    kk = rows - tok * TOP_K
    slot_src = jnp.where(real, tok, 0)
    slot_dst = jnp.where(real, kk * n + tok, TOP_K * n + lane)
    slot_dst = jnp.concatenate([slot_dst, TOP_K * n + lane], axis=0)
    slot_gate = jnp.where(real, jnp.take(gates.reshape(a), rows, mode="clip"), 0.0).reshape(nb, 1, blk)
    nb_real = (jnp.sum(padded) // blk).reshape(1).astype(jnp.int32)
    return block_expert, nb_real, slot_src.reshape(nb, 1, blk), slot_dst.reshape(nb + 1, 1, blk), slot_gate


def _combine_kernel(x_ref, h_ref, wsg_ref, wsu_ref, wsd_ref, *rest):
    y_refs, o_ref = rest[:TOP_K], rest[TOP_K]
    h = _from_token_major(h_ref[...])
    a = (jax.nn.silu(_dot(h, wsg_ref[...])) * _dot(h, wsu_ref[...])).astype(_BF16)
    routed = y_refs[0][...].astype(_F32)
    for k in range(1, TOP_K):
        routed = routed + y_refs[k][...].astype(_F32)
    o_ref[...] = x_ref[...] + _dot(a, wsd_ref[...]) + _from_token_major(routed)


def _combine(x1, h3, y3, ws_gate, ws_up, ws_down):
    n, d = x1.shape
    c = d // LANES
    tm = COMBINE_TM
    nt = n // tm
    de = ws_gate.shape[1]
    const = lambda r, cc: pl.BlockSpec((r, cc), lambda i: (0, 0))
    row = pl.BlockSpec((tm, d), lambda i: (i, 0))
    y_specs = [pl.BlockSpec((tm, c, LANES), functools.partial(lambda i, k: (k * nt + i, 0, 0), k=k))
               for k in range(TOP_K)]
    return pl.pallas_call(
        _combine_kernel,
        grid=(nt,),
        in_specs=[row, pl.BlockSpec((tm, c, LANES), lambda i: (i, 0, 0)), const(d, de), const(d, de), const(de, d)]
        + y_specs,
        out_specs=row,
        out_shape=jax.ShapeDtypeStruct((n, d), _F32),
        compiler_params=_cparams(("arbitrary",)),
        name="moe_combine",
    )(x1, h3, ws_gate.astype(_BF16), ws_up.astype(_BF16), ws_down.astype(_BF16), *([y3] * TOP_K))


def _moe_block(x1, norm_ffn_g, w_router, router_bias, we_gate, we_up, we_down, ws_gate, ws_up, ws_down):
    n = x1.shape[0]
    h3, top_idx_t, gates_t, counts = _router(x1, norm_ffn_g, w_router, router_bias)
    block_expert, nb_real, slot_src, slot_dst, slot_gate = _dispatch_plan(
        top_idx_t.T, gates_t.T, counts.reshape(N_EXPERTS), n)
    y3 = _moe_experts(h3, block_expert, nb_real, slot_src, slot_dst, slot_gate, we_gate, we_up, we_down)
    return _combine(x1, h3, y3, ws_gate, ws_up, ws_down)


def kernel(x, mem, norm_attn_g, w_in, nsa_pe_k, nsa_w_ck, nsa_pe_v, nsa_w_cv, nsa_q_g, nsa_k_g, rel_bias, fox_b_f, fox_q_g, fox_k_g, norm_mem_g, w_mem_kv, mem_q_g, mem_k_g, w_o_nsa, w_o_fox, w_o_mem, w_out, norm_ffn_g, w_router, router_bias, we_gate, we_up, we_down, ws_gate, ws_up, ws_down):
    b, t, d = x.shape
    assert norm_attn_g.shape[0] == 1, "single-layer problem"
    l = 0
    proj2d, o_nsa, o_fox, o_mem = _attention_mixers(
        x, mem, norm_attn_g[l], w_in[l], nsa_pe_k[l], nsa_w_ck[l], nsa_pe_v[l], nsa_w_cv[l], nsa_q_g[l],
        nsa_k_g[l], rel_bias, fox_b_f[l], fox_q_g[l], fox_k_g[l], norm_mem_g[l], w_mem_kv[l], mem_q_g[l],
        mem_k_g[l])
    x1 = _merge(x.reshape(b * t, d), o_nsa, o_fox, o_mem, proj2d, w_o_nsa[l], w_o_fox[l], w_o_mem[l], w_out[l])
    out = _moe_block(x1, norm_ffn_g[l], w_router[l], router_bias[l], we_gate[l], we_up[l], we_down[l],
                     ws_gate[l], ws_up[l], ws_down[l])
    return out.reshape(b, t, d)
```

```python
import functools
import math

import jax
import jax.numpy as jnp
import numpy as np
from jax import lax
from jax.experimental import pallas as pl
from jax.experimental.pallas import tpu as pltpu

D_MODEL = 2048
HEAD_DIM = 128
NSA_HEADS = 8
NSA_KV_HEADS = 2
NSA_GROUP = NSA_HEADS // NSA_KV_HEADS
FOX_HEADS = 4
MEM_HEADS = 4
CMP_LEN = 32
CMP_STRIDE = 16
SLC_LEN = 64
N_SELECT = 16
WINDOW = 512
NUM_BUCKETS = 32
MAX_DISTANCE = 128
N_BRANCHES = 3
N_EXPERTS = 64
TOP_K = 8
D_EXPERT = 512
ROUTED_SCALE = 2.5
ATTN_SCALE = HEAD_DIM ** -0.5
NEG_INF = -1e30
FORCE_SCORE = 1e4
RMS_EPS = 1e-6
LOG2E = math.log2(math.e)

NSA_Q_W = NSA_HEADS * HEAD_DIM
NSA_KV_W = NSA_KV_HEADS * HEAD_DIM
NSA_GATE_W = 3 * NSA_HEADS
FOX_W = FOX_HEADS * HEAD_DIM
MEM_W = MEM_HEADS * HEAD_DIM
MERGE_W = N_BRANCHES * D_MODEL

LANES = 128
VMEM_LIMIT_BYTES = 56 * 1024 * 1024

PROJ_TM = 1024
PROJ_TN = 768
ATT_TQ = 128
FAR_GROUP = 4
FOX_TQ = 256
MEM_TQ = 512
MERGE_TM = 256
ROUTER_TM = 512
MOE_BLOCK = 256
COMBINE_TM = 256

CB_MERGE = 0
CB_ATT = N_BRANCHES * D_MODEL // LANES
CB_QNSA = CB_ATT
CB_KCMP = CB_ATT + 8
CB_VCMP = CB_ATT + 10
CB_KSLC = CB_ATT + 12
CB_VSLC = CB_ATT + 14
CB_KWIN = CB_ATT + 16
CB_VWIN = CB_ATT + 18
CB_FOXQ = CB_ATT + 20
CB_FOXK = CB_ATT + 24
CB_FOXV = CB_ATT + 28
CB_MEMQ = CB_ATT + 32
PROJ_COLS = (CB_ATT + 36) * LANES
SMALL_W = 2 * LANES
FOXF_LANE = 12

_BF16 = jnp.bfloat16
_F32 = jnp.float32


def _cparams(sem):
    return pltpu.CompilerParams(dimension_semantics=sem, vmem_limit_bytes=VMEM_LIMIT_BYTES)


def _dot(a, b):
    return jnp.dot(a, b, preferred_element_type=_F32)


def _dot_nt(a, b):
    return lax.dot_general(a, b, (((1,), (1,)), ((), ())), preferred_element_type=_F32)


def _dot_tn(a, b):
    return lax.dot_general(a, b, (((0,), (0,)), ((), ())), preferred_element_type=_F32)


def _lanes(parts):
    return jnp.concatenate(parts, axis=1)


def _to_token_major(x):
    c = x.shape[1] // LANES
    chunks = jnp.stack([x[:, j * LANES:(j + 1) * LANES] for j in range(c)], axis=0)
    return pltpu.einshape("ctl->tcl", chunks)


def _from_token_major(x3):
    xt = pltpu.einshape("tcl->ctl", x3)
    return _lanes([xt[j] for j in range(x3.shape[1])])


REORDER_TR = 128
REORDER_TC = 512


def _reorder_kernel(pieces, narrow, w_ref, o_ref, s_ref):
    off = 0
    for src, width in pieces:
        for c in range(0, width, REORDER_TC):
            o_ref[:, off + c:off + c + REORDER_TC] = w_ref[:, src + c:src + c + REORDER_TC].astype(o_ref.dtype)
        off += width
    s_ref[...] = jnp.zeros(s_ref.shape, s_ref.dtype)
    for src, width, dst in narrow:
        s_ref[:, dst:dst + width] = w_ref[:, src:src + width].astype(s_ref.dtype)


def _reorder_cast(w, pieces, narrow, narrow_cols):
    rows, cols = w.shape
    total = sum(width for _, width in pieces)
    assert rows % REORDER_TR == 0 and all(width % REORDER_TC == 0 for _, width in pieces)
    return pl.pallas_call(
        functools.partial(_reorder_kernel, pieces, narrow),
        grid=(rows // REORDER_TR,),
        in_specs=[pl.BlockSpec((REORDER_TR, cols), lambda i: (i, 0))],
        out_specs=[pl.BlockSpec((REORDER_TR, total), lambda i: (i, 0)),
                   pl.BlockSpec((REORDER_TR, narrow_cols), lambda i: (i, 0))],
        out_shape=[jax.ShapeDtypeStruct((rows, total), _BF16), jax.ShapeDtypeStruct((rows, narrow_cols), _BF16)],
        compiler_params=_cparams(("arbitrary",)),
        name="reorder_cast",
    )(w)


def _proj_kernel(norm_j0, has_small, x_ref, g_ref, w_ref, cg_ref, cf_ref, *rest):
    if has_small:
        ws_ref, o_ref, os_ref, h_ref = rest
    else:
        o_ref, h_ref = rest
    j = pl.program_id(1)

    @pl.when(j == 0)
    def _():
        x = x_ref[...]
        ms = jnp.mean(x * x, axis=-1, keepdims=True)
        h = (x * lax.rsqrt(ms + RMS_EPS) * g_ref[...]).astype(_BF16)
        h_ref[...] = h
        if has_small:
            os_ref[...] = _dot(h, ws_ref[...])

    y = _dot(h_ref[...], w_ref[...])
    tn = y.shape[1]

    @pl.when(j >= norm_j0)
    def _():
        for c in range(tn // LANES):
            sl = slice(c * LANES, (c + 1) * LANES)
            yh = y[:, sl]
            ms = jnp.mean(yh * yh, axis=-1, keepdims=True)
            scale = jnp.where(cf_ref[:, sl] > 0.0, lax.rsqrt(ms + RMS_EPS), 1.0)
            o_ref[:, sl] = (yh * scale * cg_ref[:, sl]).astype(o_ref.dtype)

    @pl.when(j < norm_j0)
    def _():
        o_ref[...] = y.astype(o_ref.dtype)


def _rms_project(x2d, g, w, col_gain, col_flag, n_plain_cols, w_small, tm, tn):
    n, d = x2d.shape
    c = w.shape[1]
    has_small = w_small is not None
    in_specs = [
        pl.BlockSpec((tm, d), lambda i, j: (i, 0)),
        pl.BlockSpec((1, d), lambda i, j: (0, 0)),
        pl.BlockSpec((d, tn), lambda i, j: (0, j)),
        pl.BlockSpec((1, tn), lambda i, j: (0, j)),
        pl.BlockSpec((1, tn), lambda i, j: (0, j)),
    ]
    args = [x2d, g.reshape(1, d), w, col_gain.reshape(1, c), col_flag.reshape(1, c)]
    out_shape = [jax.ShapeDtypeStruct((n, c), _BF16)]
    out_specs = [pl.BlockSpec((tm, tn), lambda i, j: (i, j))]
    if has_small:
        ws = w_small.shape[1]
        in_specs.append(pl.BlockSpec((d, ws), lambda i, j: (0, 0)))
        args.append(w_small)
        out_shape.append(jax.ShapeDtypeStruct((n, ws), _F32))
        out_specs.append(pl.BlockSpec((tm, ws), lambda i, j: (i, 0)))
    assert n % tm == 0 and c % tn == 0 and n_plain_cols % tn == 0
    res = pl.pallas_call(
        functools.partial(_proj_kernel, n_plain_cols // tn, has_small),
        grid=(n // tm, c // tn),
        in_specs=in_specs,
        out_specs=out_specs,
        out_shape=out_shape,
        scratch_shapes=[pltpu.VMEM((tm, d), _BF16)],
        compiler_params=_cparams(("arbitrary", "arbitrary")),
        name="rms_project",
    )(*args)
    return res if has_small else res[0]


def _cmp_kernel(x_ref, wlo_ref, whi_ref, pelo_ref, pehi_ref, kg_ref, o_ref):
    j = pl.program_id(1)
    x = x_ref[0, 0]
    nchunk = x.shape[0]
    ylo = _dot(x, wlo_ref[0])
    yhi = _dot(x, whi_ref[0])
    pe = _dot(pelo_ref[0], wlo_ref[0]) + _dot(pehi_ref[0], whi_ref[0])
    y = ylo + pltpu.roll(yhi, nchunk - 1, 0) + pe[0:1, :]

    @pl.when(j < NSA_KV_HEADS)
    def _():
        ms = jnp.mean(y * y, axis=-1, keepdims=True)
        o_ref[0, 0] = (y * lax.rsqrt(ms + RMS_EPS) * kg_ref[...]).astype(o_ref.dtype)

    @pl.when(j >= NSA_KV_HEADS)
    def _():
        o_ref[0, 0] = y.T.astype(o_ref.dtype)


def _nsa_compress(cmp_in, w_ck, w_cv, pe_k, pe_v, k_g):
    b, nj, t, dk = cmp_in.shape
    nchunk = t // CMP_STRIDE
    half = CMP_LEN // 2
    assert nchunk == dk
    x = cmp_in.reshape(b, nj, nchunk, CMP_STRIDE * dk)

    def halves(w):
        return (w[:half].reshape(half * dk, dk).astype(_BF16),
                w[half:].reshape(half * dk, dk).astype(_BF16))

    klo, khi = halves(w_ck)
    vlo, vhi = halves(w_cv)
    wlo = jnp.stack([klo, vlo])
    whi = jnp.stack([khi, vhi])

    def pe_halves(pe):
        lo = jnp.broadcast_to(pe[:half].reshape(1, half * dk), (8, half * dk)).astype(_BF16)
        hi = jnp.broadcast_to(pe[half:].reshape(1, half * dk), (8, half * dk)).astype(_BF16)
        return lo, hi

    pklo, pkhi = pe_halves(pe_k)
    pvlo, pvhi = pe_halves(pe_v)
    pelo = jnp.stack([pklo, pvlo])
    pehi = jnp.stack([pkhi, pvhi])
    kv = lambda bb, j: (j // NSA_KV_HEADS, 0, 0)
    return pl.pallas_call(
        _cmp_kernel,
        grid=(b, nj),
        in_specs=[
            pl.BlockSpec((1, 1, nchunk, CMP_STRIDE * dk), lambda bb, j: (bb, j, 0, 0)),
            pl.BlockSpec((1, half * dk, dk), kv),
            pl.BlockSpec((1, half * dk, dk), kv),
            pl.BlockSpec((1, 8, half * dk), kv),
            pl.BlockSpec((1, 8, half * dk), kv),
            pl.BlockSpec((1, dk), lambda bb, j: (0, 0)),
        ],
        out_specs=pl.BlockSpec((1, 1, nchunk, dk), lambda bb, j: (bb, j, 0, 0)),
        out_shape=jax.ShapeDtypeStruct((b, nj, nchunk, dk), _BF16),
        compiler_params=_cparams(("arbitrary", "arbitrary")),
        name="nsa_compress",
    )(x, wlo, whi, pelo, pehi, k_g.reshape(1, dk).astype(_F32))


def _t_update(state, tiles):
    m, l, acc = state
    masked = [s if mask is None else jnp.where(mask, s, NEG_INF) for s, mask, _ in tiles]
    m_new = m
    for sm in masked:
        m_new = jnp.maximum(m_new, jnp.max(sm, axis=0, keepdims=True))
    alpha = jnp.exp2(m - m_new)
    l_new = alpha * l
    acc_new = alpha * acc
    for sm, (_, _, v) in zip(masked, tiles):
        e = jnp.exp2(sm - m_new)
        l_new = l_new + jnp.sum(e, axis=0, keepdims=True)
        acc_new = acc_new + _dot_tn(v, e.astype(_BF16))
    return m_new, l_new, acc_new


def _t_init(nq):
    return (jnp.full((1, nq), NEG_INF, _F32), jnp.zeros((1, nq), _F32), jnp.zeros((HEAD_DIM, nq), _F32))


def _t_finish(state):
    _, l, acc = state
    return acc / l


def _nsa_kernel(q_ref, kc_ref, vct_ref, ks_ref, vs_ref, kw_ref, vw_ref, gl_ref,
                wt_ref, dt_ref, ovt_ref, ext_ref, o_ref, sel_ref):
    i = pl.program_id(2)
    tq = ATT_TQ
    hg = NSA_GROUP
    nq = hg * tq
    n_slc = ovt_ref.shape[0]

    q = q_ref[0]
    qt = _lanes([q[:, h * HEAD_DIM:(h + 1) * HEAD_DIM].astype(_F32).T for h in range(hg)]).astype(_BF16)

    key = lax.broadcasted_iota(jnp.int32, (LANES, nq), 0)
    qry = lax.broadcasted_iota(jnp.int32, (LANES, nq), 1) & (tq - 1)

    def ktile(ref, kt):
        return ref[0, pl.ds(pl.multiple_of(kt * LANES, LANES), LANES), :]

    def near_bias(d):
        return _lanes([dt_ref[h, d] for h in range(hg)])

    woff = pl.multiple_of(wt_ref.shape[1] - LANES - 8 - (LANES // CMP_STRIDE) * i, 8)
    s = _dot(kc_ref[0, 0], qt) + _lanes([wt_ref[h, pl.ds(woff, LANES), :] for h in range(hg)])
    n_win = WINDOW // tq
    win_tiles = []
    for d in range(n_win, -1, -1):
        ktd = jnp.maximum(i - d, 0)
        sc = _dot(ktile(kw_ref, ktd), qt)
        if d <= 1:
            sc = sc + near_bias(d)
        if d == n_win:
            mk = (qry < key) & (i >= d)
        elif d == 0:
            mk = key <= qry
        else:
            mk = jnp.broadcast_to(i >= d, (LANES, nq))
        win_tiles.append((sc, mk, ktile(vw_ref, ktd)))

    mask_c = (i * tq + qry) >= (CMP_STRIDE * key + CMP_LEN - 1)
    mx = jnp.max(jnp.where(mask_c, s, NEG_INF), axis=0, keepdims=True)
    e = jnp.where(mask_c, jnp.exp2(s - mx), 0.0)
    l = jnp.sum(e, axis=0, keepdims=True)
    p_c = e / jnp.where(l > 0.0, l, 1.0)
    o_c = _dot(vct_ref[0, 0], p_c.astype(_BF16))

    ps = p_c[:, 0:tq]
    for h in range(1, hg):
        ps = ps + p_c[:, h * tq:(h + 1) * tq]
    ps_hi = ps.astype(_BF16)
    ps_lo = (ps - ps_hi.astype(_F32)).astype(_BF16)
    imp = _dot(ovt_ref[...], ps_hi) + _dot(ovt_ref[...], ps_lo)
    blk = lax.broadcasted_iota(jnp.int32, (n_slc, tq), 0)
    tpos = i * tq + lax.broadcasted_iota(jnp.int32, (n_slc, tq), 1)
    cur = tpos // SLC_LEN
    valid = blk * SLC_LEN <= tpos
    forced = (blk == 0) | (blk == cur) | (blk == cur - 1)
    score = jnp.where(valid, jnp.where(forced, FORCE_SCORE, imp), -1.0)
    rank = jnp.zeros((n_slc, tq), _F32)
    for k in range(n_slc):
        ck = score[k:k + 1, :]
        beats = (ck > score) | ((ck == score) & (blk > k))
        rank = rank + jnp.where(beats, 1.0, 0.0)
    sel = jnp.where(rank < float(min(N_SELECT, n_slc)), 1.0, 0.0).astype(_BF16)
    selfull = _dot(ext_ref[...], sel)
    for kt in range(ext_ref.shape[0] // LANES):
        sel_ref[kt] = selfull[kt * LANES:(kt + 1) * LANES, :]

    def sel_mask(kt, ok=True):
        return _lanes([sel_ref[kt]] * hg) > jnp.where(ok, 0.5, 2.0)

    o_w = _t_finish(_t_update(_t_init(nq), win_tiles))

    n_far = jnp.maximum(i - 1, 0)

    def far_body(j, st):
        k0 = pl.multiple_of(j * (FAR_GROUP * LANES), FAR_GROUP * LANES)
        sg = _dot(ks_ref[0, pl.ds(k0, FAR_GROUP * LANES), :], qt)
        tiles = []
        for u in range(FAR_GROUP):
            kt = FAR_GROUP * j + u
            tiles.append((sg[u * LANES:(u + 1) * LANES], sel_mask(kt, kt < n_far), ktile(vs_ref, kt)))
        return _t_update(st, tiles)

    st = lax.fori_loop(0, (n_far + FAR_GROUP - 1) // FAR_GROUP, far_body, _t_init(nq))
    kt1 = jnp.maximum(i - 1, 0)
    st = _t_update(st, [
        (_dot(ktile(ks_ref, kt1), qt) + near_bias(1), sel_mask(kt1, i >= 1), ktile(vs_ref, kt1)),
        (_dot(ktile(ks_ref, i), qt) + near_bias(0), sel_mask(i) & (key <= qry), ktile(vs_ref, i))])
    o_s = _t_finish(st)

    glt = jax.nn.sigmoid(gl_ref[0]).T
    grow = lambda br: _lanes([glt[br * hg + h:br * hg + h + 1, :] for h in range(hg)])
    o = grow(0) * o_c + grow(1) * o_s + grow(2) * o_w
    for h in range(hg):
        o_ref[0, :, h * HEAD_DIM:(h + 1) * HEAD_DIM] = o[:, h * tq:(h + 1) * tq].T.astype(o_ref.dtype)


def _t5_bucket(rel):
    n = np.maximum(rel, 0)
    max_exact = NUM_BUCKETS // 2
    ratio = np.maximum(n, 1).astype(np.float32) / np.float32(max_exact)
    log_ratio = np.log(ratio) / np.float32(math.log(MAX_DISTANCE / max_exact))
    large = np.minimum(max_exact + (log_ratio * np.float32(NUM_BUCKETS - max_exact)).astype(np.int32),
                       NUM_BUCKETS - 1)
    return np.where(n < max_exact, n, large).astype(np.int32)


def _bias_lookup(rb, bucket):
    bk = jnp.asarray(bucket.astype(np.int8))[None]
    ex = (slice(None),) + (None,) * bucket.ndim
    tab = jnp.broadcast_to(rb[:, 0][ex], (rb.shape[0],) + bucket.shape)
    for k in range(1, NUM_BUCKETS):
        tab = jnp.where(bk == k, rb[:, k][ex], tab)
    return tab


def _nsa_tables(rel_bias):
    rb = rel_bias.astype(_F32).T * LOG2E
    r = np.arange(ATT_TQ)
    rel_d = (np.arange(2) * ATT_TQ)[:, None, None] + r[None, None, :] - r[None, :, None]
    assert _t5_bucket(np.array([ATT_TQ + 1]))[0] == NUM_BUCKETS - 1
    u = np.arange(2 * LANES)
    rel_w = r[None, :] - CMP_STRIDE * (u[:, None] - (LANES - 8)) - (CMP_LEN - 1)
    far = rb[:, NUM_BUCKETS - 1]
    dt = _bias_lookup(rb, _t5_bucket(rel_d)) - far[:, None, None, None]
    return dt, _bias_lookup(rb, _t5_bucket(rel_w))


def _nsa_attention(proj, small, kvc, rel_bias):
    b, t, _ = proj.shape
    g, hg, tq = NSA_KV_HEADS, NSA_GROUP, ATT_TQ
    n_slc = t // SLC_LEN
    n_cmp = (t - CMP_LEN) // CMP_STRIDE + 1
    nt = t // LANES
    assert n_cmp <= LANES and kvc.shape[2] == LANES and n_slc % 16 == 0 and tq == LANES
    assert nt % FAR_GROUP == 0
    assert LANES - 8 - (LANES // CMP_STRIDE) * (t // tq - 1) >= 0
    dt, wt = _nsa_tables(rel_bias)
    cstart = np.arange(LANES) * CMP_STRIDE
    sstart = np.arange(n_slc) * SLC_LEN
    overlap = np.clip(np.minimum(cstart[None, :] + CMP_LEN, sstart[:, None] + SLC_LEN)
                      - np.maximum(cstart[None, :], sstart[:, None]), 0, None).astype(np.float32) / CMP_LEN
    overlap[:, n_cmp:] = 0.0
    ovt = jnp.asarray(overlap, _BF16)
    ext = jnp.asarray(np.arange(t)[:, None] // SLC_LEN == np.arange(n_slc)[None, :], _BF16)

    seq = lambda cb: pl.BlockSpec((1, t, LANES), lambda bb, gg, ii: (bb, 0, cb + gg))
    return pl.pallas_call(
        _nsa_kernel,
        grid=(b, g, t // tq),
        in_specs=[
            pl.BlockSpec((1, tq, hg * HEAD_DIM), lambda bb, gg, ii: (bb, ii, CB_QNSA * LANES // (hg * HEAD_DIM) + gg)),
            pl.BlockSpec((1, 1, LANES, HEAD_DIM), lambda bb, gg, ii: (bb, gg, 0, 0)),
            pl.BlockSpec((1, 1, HEAD_DIM, LANES), lambda bb, gg, ii: (bb, NSA_KV_HEADS + gg, 0, 0)),
            seq(CB_KSLC), seq(CB_VSLC), seq(CB_KWIN), seq(CB_VWIN),
            pl.BlockSpec((1, tq, LANES), lambda bb, gg, ii: (bb, ii, gg)),
            pl.BlockSpec((hg, 2 * LANES, tq), lambda bb, gg, ii: (gg, 0, 0)),
            pl.BlockSpec((hg, 2, LANES, tq), lambda bb, gg, ii: (gg, 0, 0, 0)),
            pl.BlockSpec((n_slc, LANES), lambda bb, gg, ii: (0, 0)),
            pl.BlockSpec((t, n_slc), lambda bb, gg, ii: (0, 0)),
        ],
        out_specs=pl.BlockSpec((1, tq, hg * HEAD_DIM), lambda bb, gg, ii: (bb, ii, gg)),
        out_shape=jax.ShapeDtypeStruct((b, t, NSA_Q_W), _BF16),
        scratch_shapes=[pltpu.VMEM((nt, LANES, tq), _F32)],
        compiler_params=_cparams(("arbitrary", "arbitrary", "arbitrary")),
        name="nsa_attention",
    )(proj, kvc, kvc, proj, proj, proj, proj, small, wt, dt, ovt, ext)


def _fox_cum_kernel(s_ref, b_ref, col_ref, row_ref):
    z = s_ref[0] + b_ref[...]
    lf = (jnp.minimum(z, 0.0) - jnp.log1p(jnp.exp(-jnp.abs(z)))) * LOG2E
    x = lf.T
    t = x.shape[1]
    lane = lax.broadcasted_iota(jnp.int32, x.shape, 1)
    sh = 1
    while sh < t:
        x = x + jnp.where(lane >= sh, pltpu.roll(x, sh, 1), 0.0)
        sh *= 2
    row_ref[0] = x[8:16, :]
    col_ref[0] = x.T


def _fox_cumsum(small, b_f):
    b, t, _ = small.shape
    bvec = jnp.zeros((1, LANES), _F32).at[0, FOXF_LANE:FOXF_LANE + FOX_HEADS].set(b_f.astype(_F32))
    return pl.pallas_call(
        _fox_cum_kernel,
        grid=(b,),
        in_specs=[pl.BlockSpec((1, t, LANES), lambda bb: (bb, 0, 1)),
                  pl.BlockSpec((1, LANES), lambda bb: (0, 0))],
        out_specs=[pl.BlockSpec((1, t, LANES), lambda bb: (bb, 0, 0)),
                   pl.BlockSpec((1, 8, t), lambda bb: (bb, 0, 0))],
        out_shape=[jax.ShapeDtypeStruct((b, t, LANES), _F32), jax.ShapeDtypeStruct((b, 8, t), _F32)],
        compiler_params=_cparams(("arbitrary",)),
        name="fox_cumsum",
    )(small, bvec)


def _fox_kernel(q_ref, k_ref, v_ref, cc_ref, cr_ref, o_ref):
    i = pl.program_id(1)
    tq = FOX_TQ
    key = lax.broadcasted_iota(jnp.int32, (tq, tq), 0)
    qry = lax.broadcasted_iota(jnp.int32, (tq, tq), 1)
    heads = range(FOX_HEADS)
    hsl = [slice(h * HEAD_DIM, (h + 1) * HEAD_DIM) for h in heads]
    qts = [q_ref[0, :, hsl[h]].astype(_F32).T.astype(_BF16) for h in heads]
    cqs = [cr_ref[0, FOXF_LANE - 8 + h:FOXF_LANE - 8 + h + 1, :] for h in heads]

    def keys(kt):
        return pl.ds(pl.multiple_of(kt * tq, tq), tq)

    def logits(h, kt):
        ck = cc_ref[0, keys(kt), FOXF_LANE + h:FOXF_LANE + h + 1]
        return _dot(k_ref[0, keys(kt), hsl[h]], qts[h]) + (cqs[h] - ck)

    def body(kt, sts):
        ss = [logits(h, kt) for h in heads]
        return tuple(_t_update(sts[h], [(ss[h], None, v_ref[0, keys(kt), hsl[h]])]) for h in heads)

    sts = lax.fori_loop(0, i, body, tuple(_t_init(tq) for _ in heads))
    ss = [logits(h, i) for h in heads]
    for h in heads:
        st = _t_update(sts[h], [(ss[h], key <= qry, v_ref[0, keys(i), hsl[h]])])
        o_ref[0, :, hsl[h]] = _t_finish(st).T.astype(o_ref.dtype)


def _fox_attention(proj, cum_col, cum_row):
    b, t, _ = proj.shape
    tq = FOX_TQ
    w = FOX_W
    return pl.pallas_call(
        _fox_kernel,
        grid=(b, t // tq),
        in_specs=[
            pl.BlockSpec((1, tq, w), lambda bb, ii: (bb, ii, CB_FOXQ * LANES // w)),
            pl.BlockSpec((1, t, w), lambda bb, ii: (bb, 0, CB_FOXK * LANES // w)),
            pl.BlockSpec((1, t, w), lambda bb, ii: (bb, 0, CB_FOXV * LANES // w)),
            pl.BlockSpec((1, t, LANES), lambda bb, ii: (bb, 0, 0)),
            pl.BlockSpec((1, 8, tq), lambda bb, ii: (bb, 0, ii)),
        ],
        out_specs=pl.BlockSpec((1, tq, w), lambda bb, ii: (bb, ii, 0)),
        out_shape=jax.ShapeDtypeStruct((b, t, w), _BF16),
        compiler_params=_cparams(("arbitrary", "arbitrary")),
        name="fox_attention",
    )(proj, proj, proj, cum_col, cum_row)


def _mem_kernel(q_ref, kv_ref, o_ref):
    for h in range(MEM_HEADS):
        hs = slice(h * HEAD_DIM, (h + 1) * HEAD_DIM)
        vs = slice(MEM_W + h * HEAD_DIM, MEM_W + (h + 1) * HEAD_DIM)
        s = _dot_nt(q_ref[0, :, hs], kv_ref[0, :, hs])
        e = jnp.exp2(s - jnp.max(s, axis=1, keepdims=True))
        p = e / jnp.sum(e, axis=1, keepdims=True)
        o_ref[0, :, hs] = _dot(p.astype(_BF16), kv_ref[0, :, vs]).astype(o_ref.dtype)


def _mem_attention(proj, memkv):
    b, t, _ = proj.shape
    m = memkv.shape[1]
    tq = MEM_TQ
    return pl.pallas_call(
        _mem_kernel,
        grid=(b, t // tq),
        in_specs=[pl.BlockSpec((1, tq, MEM_W), lambda bb, ii: (bb, ii, CB_MEMQ * LANES // MEM_W)),
                  pl.BlockSpec((1, m, 2 * MEM_W), lambda bb, ii: (bb, 0, 0))],
        out_specs=pl.BlockSpec((1, tq, MEM_W), lambda bb, ii: (bb, ii, 0)),
        out_shape=jax.ShapeDtypeStruct((b, t, MEM_W), _BF16),
        compiler_params=_cparams(("arbitrary", "arbitrary")),
        name="mem_attention",
    )(proj, memkv)


def _merge_kernel(x_ref, on_ref, of_ref, om_ref, g0_ref, g1_ref, g2_ref, wn_ref, wf_ref, wm_ref, wo_ref,
                  o_ref, z_ref):
    tn = 512
    for c in range(D_MODEL // tn):
        cs = slice(c * tn, (c + 1) * tn)
        z = (jax.nn.sigmoid(g0_ref[:, cs].astype(_F32)) * _dot(on_ref[...], wn_ref[:, cs])
             + jax.nn.sigmoid(g1_ref[:, cs].astype(_F32)) * _dot(of_ref[...], wf_ref[:, cs])
             + jax.nn.sigmoid(g2_ref[:, cs].astype(_F32)) * _dot(om_ref[...], wm_ref[:, cs]))
        z_ref[:, cs] = z.astype(_BF16)
    o_ref[...] = x_ref[...] + _dot(z_ref[...], wo_ref[...])


def _merge(x2d, o_nsa, o_fox, o_mem, proj2d, w_o_nsa, w_o_fox, w_o_mem, w_out):
    n, d = x2d.shape
    tm = MERGE_TM
    gcb = CB_MERGE * LANES // d
    const = lambda r, c: pl.BlockSpec((r, c), lambda i: (0, 0))
    return pl.pallas_call(
        _merge_kernel,
        grid=(n // tm,),
        in_specs=[
            pl.BlockSpec((tm, d), lambda i: (i, 0)),
            pl.BlockSpec((tm, NSA_Q_W), lambda i: (i, 0)),
            pl.BlockSpec((tm, FOX_W), lambda i: (i, 0)),
            pl.BlockSpec((tm, MEM_W), lambda i: (i, 0)),
            pl.BlockSpec((tm, d), lambda i: (i, gcb)),
            pl.BlockSpec((tm, d), lambda i: (i, gcb + 1)),
            pl.BlockSpec((tm, d), lambda i: (i, gcb + 2)),
            const(NSA_Q_W, d), const(FOX_W, d), const(MEM_W, d), const(d, d),
        ],
        out_specs=pl.BlockSpec((tm, d), lambda i: (i, 0)),
        out_shape=jax.ShapeDtypeStruct((n, d), _F32),
        scratch_shapes=[pltpu.VMEM((tm, d), _BF16)],
        compiler_params=_cparams(("arbitrary",)),
        name="merge_out",
    )(x2d, o_nsa, o_fox, o_mem, proj2d, proj2d, proj2d,
      w_o_nsa.astype(_BF16), w_o_fox.astype(_BF16), w_o_mem.astype(_BF16), w_out.astype(_BF16))


def _nsa_inputs(x, norm_attn_g, w_in, nsa_pe_k, nsa_w_ck, nsa_pe_v, nsa_w_cv, nsa_q_g, nsa_k_g,
                fox_q_g, fox_k_g, mem_q_g):
    b, t, d = x.shape
    ones = lambda k: jnp.ones((k * HEAD_DIM,), _F32)
    zeros = lambda k: jnp.zeros((k * HEAD_DIM,), _F32)
    tile = lambda gv, k: jnp.tile(gv.astype(_F32), k)
    qs = ATTN_SCALE * LOG2E

    g_end = NSA_Q_W + 6 * NSA_KV_W
    f_off = g_end + NSA_GATE_W + 3 * FOX_W
    m_off = f_off + FOX_HEADS + MEM_W
    narrow = tuple((g_end + (br * NSA_KV_HEADS + gg) * NSA_GROUP, NSA_GROUP, gg * LANES + br * NSA_GROUP)
                   for br in range(3) for gg in range(NSA_KV_HEADS))
    narrow += ((f_off, FOX_HEADS, LANES + FOXF_LANE),)
    w_all, w_small = _reorder_cast(
        w_in, ((m_off, MERGE_W), (0, g_end), (g_end + NSA_GATE_W, 3 * FOX_W), (f_off + FOX_HEADS, MEM_W)),
        narrow, SMALL_W)
    col_gain = jnp.concatenate([
        jnp.ones((MERGE_W,), _F32),
        tile(nsa_q_g, NSA_HEADS) * qs, ones(2), ones(2), tile(nsa_k_g, 2), ones(2), tile(nsa_k_g, 2), ones(2),
        tile(fox_q_g, FOX_HEADS) * qs, tile(fox_k_g, FOX_HEADS), ones(FOX_HEADS),
        tile(mem_q_g, MEM_HEADS) * qs])
    col_flag = jnp.concatenate([
        jnp.zeros((MERGE_W,), _F32),
        ones(NSA_HEADS), zeros(2), zeros(2), ones(2), zeros(2), ones(2), zeros(2),
        ones(FOX_HEADS), ones(FOX_HEADS), zeros(FOX_HEADS), ones(MEM_HEADS)])

    x2d = x.reshape(b * t, d)
    proj2d, small2d = _rms_project(x2d, norm_attn_g, w_all, col_gain, col_flag, CB_ATT * LANES, w_small,
                                   PROJ_TM, PROJ_TN)
    proj = proj2d.reshape(b, t, PROJ_COLS)
    small = small2d.reshape(b, t, SMALL_W)

    cmp_in = proj[:, :, CB_KCMP * LANES:(CB_KCMP + 4) * LANES].reshape(b, t, 4, HEAD_DIM).transpose(0, 2, 1, 3)
    kvc = _nsa_compress(cmp_in, nsa_w_ck, nsa_w_cv, nsa_pe_k, nsa_pe_v, nsa_k_g)
    return proj, small, kvc


def _attention_mixers(x, mem, norm_attn_g, w_in, nsa_pe_k, nsa_w_ck, nsa_pe_v, nsa_w_cv, nsa_q_g, nsa_k_g,
                      rel_bias, fox_b_f, fox_q_g, fox_k_g, norm_mem_g, w_mem_kv, mem_q_g, mem_k_g):
    b, t, d = x.shape
    ones = lambda k: jnp.ones((k * HEAD_DIM,), _F32)
    zeros = lambda k: jnp.zeros((k * HEAD_DIM,), _F32)
    tile = lambda gv, k: jnp.tile(gv.astype(_F32), k)
    proj, small, kvc = _nsa_inputs(x, norm_attn_g, w_in, nsa_pe_k, nsa_w_ck, nsa_pe_v, nsa_w_cv, nsa_q_g, nsa_k_g,
                                   fox_q_g, fox_k_g, mem_q_g)
    proj2d = proj.reshape(b * t, PROJ_COLS)
    o_nsa = _nsa_attention(proj, small, kvc, rel_bias)

    cum_col, cum_row = _fox_cumsum(small, fox_b_f)
    o_fox = _fox_attention(proj, cum_col, cum_row)

    mm = mem.shape[1]
    mem_gain = jnp.concatenate([tile(mem_k_g, MEM_HEADS), ones(MEM_HEADS)])
    mem_flag = jnp.concatenate([ones(MEM_HEADS), zeros(MEM_HEADS)])
    memkv = _rms_project(mem.reshape(b * mm, d), norm_mem_g, w_mem_kv.astype(_BF16), mem_gain, mem_flag,
                         0, None, mm, MEM_W).reshape(b, mm, 2 * MEM_W)
    o_mem = _mem_attention(proj, memkv)
    return proj2d, o_nsa.reshape(b * t, NSA_Q_W), o_fox.reshape(b * t, FOX_W), o_mem.reshape(b * t, MEM_W)


def _router_kernel(x_ref, g_ref, whi_ref, wlo_ref, b_ref, h_ref, idx_ref, gate_ref, cnt_ref):
    x = x_ref[...]
    ms = jnp.mean(x * x, axis=-1, keepdims=True)
    h = x * lax.rsqrt(ms + RMS_EPS) * g_ref[...]
    h_hi = h.astype(_BF16)
    h_ref[...] = _to_token_major(h_hi)
    h_lo = (h - h_hi.astype(_F32)).astype(_BF16)
    logits = _dot_nt(whi_ref[...], h_hi) + _dot_nt(whi_ref[...], h_lo) + _dot_nt(wlo_ref[...], h_hi)
    scores = jax.nn.sigmoid(logits)
    sb = scores + b_ref[...]
    eidx = lax.broadcasted_iota(jnp.int32, sb.shape, 0)
    idxs, vals = [], []
    for _ in range(TOP_K):
        m = jnp.max(sb, axis=0, keepdims=True)
        idx = jnp.min(jnp.where(sb == m, eidx, N_EXPERTS), axis=0, keepdims=True)
        hit = eidx == idx
        vals.append(jnp.sum(jnp.where(hit, scores, 0.0), axis=0, keepdims=True))
        idxs.append(idx)
        sb = jnp.where(hit, NEG_INF, sb)
    top_s = jnp.concatenate(vals, axis=0)
    idx_ref[...] = jnp.concatenate(idxs, axis=0)
    gate_ref[...] = top_s / jnp.sum(top_s, axis=0, keepdims=True) * ROUTED_SCALE

    @pl.when(pl.program_id(0) == 0)
    def _():
        cnt_ref[...] = jnp.zeros(cnt_ref.shape, cnt_ref.dtype)

    picked = jnp.where(sb < 0.5 * NEG_INF, 1.0, 0.0)
    cnt_ref[...] += jnp.sum(picked, axis=1, keepdims=True)


def _router(x1, norm_g, w_router, router_bias):
    n, d = x1.shape
    tm = ROUTER_TM
    wt = w_router.astype(_F32).T
    w_hi = wt.astype(_BF16)
    w_lo = (wt - w_hi.astype(_F32)).astype(_BF16)
    const = lambda r, c: pl.BlockSpec((r, c), lambda i: (0, 0))
    return pl.pallas_call(
        _router_kernel,
        grid=(n // tm,),
        in_specs=[pl.BlockSpec((tm, d), lambda i: (i, 0)), const(1, d), const(N_EXPERTS, d), const(N_EXPERTS, d),
                  const(N_EXPERTS, 1)],
        out_specs=[pl.BlockSpec((tm, d // LANES, LANES), lambda i: (i, 0, 0)),
                   pl.BlockSpec((TOP_K, tm), lambda i: (0, i)),
                   pl.BlockSpec((TOP_K, tm), lambda i: (0, i)),
                   const(N_EXPERTS, 1)],
        out_shape=[jax.ShapeDtypeStruct((n, d // LANES, LANES), _BF16), jax.ShapeDtypeStruct((TOP_K, n), jnp.int32),
                   jax.ShapeDtypeStruct((TOP_K, n), _F32), jax.ShapeDtypeStruct((N_EXPERTS, 1), _F32)],
        compiler_params=_cparams(("arbitrary",)),
        name="moe_router",
    )(x1, norm_g.reshape(1, d).astype(_F32), w_hi, w_lo, router_bias.reshape(N_EXPERTS, 1).astype(_F32))


def _moe_kernel(be_ref, nbr_ref, src_hbm, dst_hbm, h_hbm, sg_ref, wg_ref, wu_ref, wd_ref, y_hbm,
                xbuf, ybuf, xmat, sidx, didx, wg_bf, wu_bf, wd_bf, gsem, ssem, isem):
    n = pl.program_id(0)
    nbr = nbr_ref[0]
    last = nbr - 1
    slot = n % 2
    other = 1 - slot
    blk = MOE_BLOCK
    dump_block = dst_hbm.shape[0] - 1
    ring = didx.shape[0]

    def src_copy(block, s):
        return pltpu.make_async_copy(src_hbm.at[block], sidx.at[s], isem.at[0, s])

    def dst_copy(block, s3):
        return pltpu.make_async_copy(dst_hbm.at[block], didx.at[s3], isem.at[1, s3])

    def issue_gather(s):
        for r in range(blk):
            tok = sidx[s, 0, r]
            pltpu.make_async_copy(h_hbm.at[tok], xbuf.at[s, r], gsem.at[s]).start(priority=r % 2)

    def issue_scatter(s, s3):
        for r in range(blk):
            row = didx[s3, 0, r]
            pltpu.make_async_copy(ybuf.at[s, r], y_hbm.at[row], ssem.at[s]).start(priority=r % 2)

    def wait_rows(buf, sem, s):
        pltpu.make_async_copy(buf.at[s], buf.at[s], sem.at[s]).wait()

    @pl.when(n < nbr)
    def _():
        @pl.when(n == 0)
        def _():
            src_copy(0, 0).start()
            dst_copy(0, 0).start()
            dst_copy(dump_block, ring - 1).start()
            ybuf[1] = jnp.zeros(ybuf.shape[1:], ybuf.dtype)
            src_copy(0, 0).wait()
            dst_copy(0, 0).wait()
            dst_copy(dump_block, ring - 1).wait()
            issue_gather(0)
            nxt0 = jnp.minimum(1, last)
            src_copy(nxt0, 1).start()
            dst_copy(nxt0, 1).start()

        wait_rows(xbuf, gsem, slot)

        @pl.when(n < last)
        def _():
            nn = jnp.minimum(n + 2, last)
            src_copy(nn, slot).start()
            dst_copy(nn, (n + 2) % ring).start()

        changed = (n == 0) | (be_ref[n] != be_ref[jnp.maximum(n - 1, 0)])

        @pl.when(changed)
        def _():
            wg_bf[...] = wg_ref[0].astype(_BF16)
            wu_bf[...] = wu_ref[0].astype(_BF16)
            wd_bf[...] = wd_ref[0].astype(_BF16)

        @pl.when(n >= 1)
        def _():
            wait_rows(ybuf, ssem, slot)

        nxt = jnp.minimum(n + 1, last)
        src_copy(nxt, other).wait()
        dst_copy(nxt, (n + 1) % ring).wait()

        issue_scatter(other, (n + ring - 1) % ring)
        xmat[...] = _from_token_major(xbuf[slot])
        issue_gather(other)
        gcol = jnp.broadcast_to(sg_ref[0], (LANES, blk)).T
        a = jax.nn.silu(_dot(xmat[...], wg_bf[...])) * _dot(xmat[...], wu_bf[...])
        a = (a * _lanes([gcol] * (a.shape[1] // LANES))).astype(_BF16)
        ybuf[slot] = _to_token_major(_dot(a, wd_bf[...]).astype(_BF16))

        @pl.when(n == last)
        def _():
            issue_scatter(slot, n % ring)
            wait_rows(xbuf, gsem, other)
            wait_rows(ybuf, ssem, other)
            wait_rows(ybuf, ssem, slot)


def _moe_experts(h3, block_expert, nb_real, slot_src, slot_dst, slot_gate, we_gate, we_up, we_down):
    n, c, _ = h3.shape
    d = c * LANES
    nb = block_expert.shape[0]
    blk = MOE_BLOCK
    e, _, de = we_gate.shape
    grid_spec = pltpu.PrefetchScalarGridSpec(
        num_scalar_prefetch=2,
        grid=(nb,),
        in_specs=[
            pl.BlockSpec(memory_space=pl.ANY),
            pl.BlockSpec(memory_space=pl.ANY),
            pl.BlockSpec(memory_space=pl.ANY),
            pl.BlockSpec((1, 1, blk), lambda i, be, nbr: (i, 0, 0)),
            pl.BlockSpec((1, d, de), lambda i, be, nbr: (be[i], 0, 0)),
            pl.BlockSpec((1, d, de), lambda i, be, nbr: (be[i], 0, 0)),
            pl.BlockSpec((1, de, d), lambda i, be, nbr: (be[i], 0, 0)),
        ],
        out_specs=pl.BlockSpec(memory_space=pl.ANY),
        scratch_shapes=[
            pltpu.VMEM((2, blk, c, LANES), _BF16), pltpu.VMEM((2, blk, c, LANES), _BF16),
            pltpu.VMEM((blk, d), _BF16),
            pltpu.SMEM((2, 1, blk), jnp.int32), pltpu.SMEM((4, 1, blk), jnp.int32),
            pltpu.VMEM((d, de), _BF16), pltpu.VMEM((d, de), _BF16), pltpu.VMEM((de, d), _BF16),
            pltpu.SemaphoreType.DMA((2,)), pltpu.SemaphoreType.DMA((2,)), pltpu.SemaphoreType.DMA((2, 4)),
        ],
    )
    return pl.pallas_call(
        _moe_kernel,
        grid_spec=grid_spec,
        out_shape=jax.ShapeDtypeStruct((TOP_K * n + blk, c, LANES), _BF16),
        compiler_params=_cparams(("arbitrary",)),
        name="moe_experts",
    )(block_expert, nb_real, slot_src, slot_dst, h3, slot_gate, we_gate, we_up, we_down)


def _dispatch_plan(top_idx, gates, counts, n):
    blk = MOE_BLOCK
    a = n * TOP_K
    nb = (a + N_EXPERTS * (blk - 1)) // blk
    order = jnp.argsort(top_idx.reshape(a)).astype(jnp.int32)
    counts = counts.astype(jnp.int32)
    padded = (counts + blk - 1) // blk * blk
    start = jnp.cumsum(counts) - counts
    pstart = jnp.cumsum(padded) - padded
    block_end = jnp.cumsum(padded) // blk
    blocks = jnp.arange(nb, dtype=jnp.int32)
    block_expert = jnp.minimum(jnp.sum((block_end[None, :] <= blocks[:, None]).astype(jnp.int32), axis=1),
                               N_EXPERTS - 1)
    off = blocks * blk - pstart[block_expert]
    base = start[block_expert] + off
    lane = jnp.arange(blk, dtype=jnp.int32)[None, :]
    rows = jnp.take(order, base[:, None] + lane, mode="clip")
    real = (off[:, None] + lane) < counts[block_expert][:, None]
    tok = rows // TOP_K
    kk = rows - tok * TOP_K
    slot_src = jnp.where(real, tok, 0)
    slot_dst = jnp.where(real, kk * n + tok, TOP_K * n + lane)
    slot_dst = jnp.concatenate([slot_dst, TOP_K * n + lane], axis=0)
    slot_gate = jnp.where(real, jnp.take(gates.reshape(a), rows, mode="clip"), 0.0).reshape(nb, 1, blk)
    nb_real = (jnp.sum(padded) // blk).reshape(1).astype(jnp.int32)
    return block_expert, nb_real, slot_src.reshape(nb, 1, blk), slot_dst.reshape(nb + 1, 1, blk), slot_gate


def _combine_kernel(x_ref, h_ref, wsg_ref, wsu_ref, wsd_ref, *rest):
    y_refs, o_ref = rest[:TOP_K], rest[TOP_K]
    h = _from_token_major(h_ref[...])
    a = (jax.nn.silu(_dot(h, wsg_ref[...])) * _dot(h, wsu_ref[...])).astype(_BF16)
    routed = y_refs[0][...].astype(_F32)
    for k in range(1, TOP_K):
        routed = routed + y_refs[k][...].astype(_F32)
    o_ref[...] = x_ref[...] + _dot(a, wsd_ref[...]) + _from_token_major(routed)


def _combine(x1, h3, y3, ws_gate, ws_up, ws_down):
    n, d = x1.shape
    c = d // LANES
    tm = COMBINE_TM
    nt = n // tm
    de = ws_gate.shape[1]
    const = lambda r, cc: pl.BlockSpec((r, cc), lambda i: (0, 0))
    row = pl.BlockSpec((tm, d), lambda i: (i, 0))
    y_specs = [pl.BlockSpec((tm, c, LANES), functools.partial(lambda i, k: (k * nt + i, 0, 0), k=k))
               for k in range(TOP_K)]
    return pl.pallas_call(
        _combine_kernel,
        grid=(nt,),
        in_specs=[row, pl.BlockSpec((tm, c, LANES), lambda i: (i, 0, 0)), const(d, de), const(d, de), const(de, d)]
        + y_specs,
        out_specs=row,
        out_shape=jax.ShapeDtypeStruct((n, d), _F32),
        compiler_params=_cparams(("arbitrary",)),
        name="moe_combine",
    )(x1, h3, ws_gate.astype(_BF16), ws_up.astype(_BF16), ws_down.astype(_BF16), *([y3] * TOP_K))


def _moe_block(x1, norm_ffn_g, w_router, router_bias, we_gate, we_up, we_down, ws_gate, ws_up, ws_down):
    n = x1.shape[0]
    h3, top_idx_t, gates_t, counts = _router(x1, norm_ffn_g, w_router, router_bias)
    block_expert, nb_real, slot_src, slot_dst, slot_gate = _dispatch_plan(
        top_idx_t.T, gates_t.T, counts.reshape(N_EXPERTS), n)
    y3 = _moe_experts(h3, block_expert, nb_real, slot_src, slot_dst, slot_gate, we_gate, we_up, we_down)
    return _combine(x1, h3, y3, ws_gate, ws_up, ws_down)


def kernel(x, mem, norm_attn_g, w_in, nsa_pe_k, nsa_w_ck, nsa_pe_v, nsa_w_cv, nsa_q_g, nsa_k_g, rel_bias, fox_b_f, fox_q_g, fox_k_g, norm_mem_g, w_mem_kv, mem_q_g, mem_k_g, w_o_nsa, w_o_fox, w_o_mem, w_out, norm_ffn_g, w_router, router_bias, we_gate, we_up, we_down, ws_gate, ws_up, ws_down):
    b, t, d = x.shape
    assert norm_attn_g.shape[0] == 1, "single-layer problem"
    l = 0
    proj2d, o_nsa, o_fox, o_mem = _attention_mixers(
        x, mem, norm_attn_g[l], w_in[l], nsa_pe_k[l], nsa_w_ck[l], nsa_pe_v[l], nsa_w_cv[l], nsa_q_g[l],
        nsa_k_g[l], rel_bias, fox_b_f[l], fox_q_g[l], fox_k_g[l], norm_mem_g[l], w_mem_kv[l], mem_q_g[l],
        mem_k_g[l])
    x1 = _merge(x.reshape(b * t, d), o_nsa, o_fox, o_mem, proj2d, w_o_nsa[l], w_o_fox[l], w_o_mem[l], w_out[l])
    out = _moe_block(x1, norm_ffn_g[l], w_router[l], router_bias[l], we_gate[l], we_up[l], we_down[l],
                     ws_gate[l], ws_up[l], ws_down[l])
    return out.reshape(b, t, d)
```

```python
import functools
import math

import jax
import jax.numpy as jnp
import numpy as np
from jax import lax
from jax.experimental import pallas as pl
from jax.experimental.pallas import tpu as pltpu

D_MODEL = 2048
HEAD_DIM = 128
NSA_HEADS = 8
NSA_KV_HEADS = 2
NSA_GROUP = NSA_HEADS // NSA_KV_HEADS
FOX_HEADS = 4
MEM_HEADS = 4
CMP_LEN = 32
CMP_STRIDE = 16
SLC_LEN = 64
N_SELECT = 16
WINDOW = 512
NUM_BUCKETS = 32
MAX_DISTANCE = 128
N_BRANCHES = 3
N_EXPERTS = 64
TOP_K = 8
D_EXPERT = 512
ROUTED_SCALE = 2.5
ATTN_SCALE = HEAD_DIM ** -0.5
NEG_INF = -1e30
FORCE_SCORE = 1e4
RMS_EPS = 1e-6
LOG2E = math.log2(math.e)

NSA_Q_W = NSA_HEADS * HEAD_DIM
NSA_KV_W = NSA_KV_HEADS * HEAD_DIM
NSA_GATE_W = 3 * NSA_HEADS
FOX_W = FOX_HEADS * HEAD_DIM
MEM_W = MEM_HEADS * HEAD_DIM
MERGE_W = N_BRANCHES * D_MODEL

LANES = 128
VMEM_LIMIT_BYTES = 56 * 1024 * 1024

PROJ_TM = 1024
PROJ_TN = 1536
ATT_TQ = 128
FAR_GROUP = 4
FOX_TQ = 256
MEM_TQ = 512
MERGE_TM = 256
ROUTER_TM = 512
MOE_BLOCK = 256
COMBINE_TM = 256

CB_MERGE = 0
CB_ATT = N_BRANCHES * D_MODEL // LANES
CB_QNSA = CB_ATT
CB_KCMP = CB_ATT + 8
CB_VCMP = CB_ATT + 10
CB_KSLC = CB_ATT + 12
CB_VSLC = CB_ATT + 14
CB_KWIN = CB_ATT + 16
CB_VWIN = CB_ATT + 18
CB_FOXQ = CB_ATT + 20
CB_FOXK = CB_ATT + 24
CB_FOXV = CB_ATT + 28
CB_MEMQ = CB_ATT + 32
PROJ_COLS = (CB_ATT + 36) * LANES
SMALL_W = 2 * LANES
FOXF_LANE = 12

_BF16 = jnp.bfloat16
_F32 = jnp.float32


def _cparams(sem):
    return pltpu.CompilerParams(dimension_semantics=sem, vmem_limit_bytes=VMEM_LIMIT_BYTES)


def _dot(a, b):
    return jnp.dot(a, b, preferred_element_type=_F32)


def _dot_nt(a, b):
    return lax.dot_general(a, b, (((1,), (1,)), ((), ())), preferred_element_type=_F32)


def _dot_tn(a, b):
    return lax.dot_general(a, b, (((0,), (0,)), ((), ())), preferred_element_type=_F32)


def _lanes(parts):
    return jnp.concatenate(parts, axis=1)


def _to_token_major(x):
    c = x.shape[1] // LANES
    chunks = jnp.stack([x[:, j * LANES:(j + 1) * LANES] for j in range(c)], axis=0)
    return pltpu.einshape("ctl->tcl", chunks)


def _from_token_major(x3):
    xt = pltpu.einshape("tcl->ctl", x3)
    return _lanes([xt[j] for j in range(x3.shape[1])])


REORDER_TR = 128
REORDER_TC = 512


def _reorder_kernel(pieces, narrow, w_ref, o_ref, s_ref):
    off = 0
    for src, width in pieces:
        for c in range(0, width, REORDER_TC):
            o_ref[:, off + c:off + c + REORDER_TC] = w_ref[:, src + c:src + c + REORDER_TC].astype(o_ref.dtype)
        off += width
    s_ref[...] = jnp.zeros(s_ref.shape, s_ref.dtype)
    for src, width, dst in narrow:
        s_ref[:, dst:dst + width] = w_ref[:, src:src + width].astype(s_ref.dtype)


def _reorder_cast(w, pieces, narrow, narrow_cols):
    rows, cols = w.shape
    total = sum(width for _, width in pieces)
    assert rows % REORDER_TR == 0 and all(width % REORDER_TC == 0 for _, width in pieces)
    return pl.pallas_call(
        functools.partial(_reorder_kernel, pieces, narrow),
        grid=(rows // REORDER_TR,),
        in_specs=[pl.BlockSpec((REORDER_TR, cols), lambda i: (i, 0))],
        out_specs=[pl.BlockSpec((REORDER_TR, total), lambda i: (i, 0)),
                   pl.BlockSpec((REORDER_TR, narrow_cols), lambda i: (i, 0))],
        out_shape=[jax.ShapeDtypeStruct((rows, total), _BF16), jax.ShapeDtypeStruct((rows, narrow_cols), _BF16)],
        compiler_params=_cparams(("arbitrary",)),
        name="reorder_cast",
    )(w)


def _proj_kernel(norm_j0, has_small, x_ref, g_ref, w_ref, cg_ref, cf_ref, *rest):
    if has_small:
        ws_ref, o_ref, os_ref, h_ref = rest
    else:
        o_ref, h_ref = rest
    j = pl.program_id(1)

    @pl.when(j == 0)
    def _():
        x = x_ref[...]
        ms = jnp.mean(x * x, axis=-1, keepdims=True)
        h = (x * lax.rsqrt(ms + RMS_EPS) * g_ref[...]).astype(_BF16)
        h_ref[...] = h
        if has_small:
            os_ref[...] = _dot(h, ws_ref[...])

    y = _dot(h_ref[...], w_ref[...])
    tn = y.shape[1]

    @pl.when(j >= norm_j0)
    def _():
        for c in range(tn // LANES):
            sl = slice(c * LANES, (c + 1) * LANES)
            yh = y[:, sl]
            ms = jnp.mean(yh * yh, axis=-1, keepdims=True)
            scale = jnp.where(cf_ref[:, sl] > 0.0, lax.rsqrt(ms + RMS_EPS), 1.0)
            o_ref[:, sl] = (yh * scale * cg_ref[:, sl]).astype(o_ref.dtype)

    @pl.when(j < norm_j0)
    def _():
        o_ref[...] = y.astype(o_ref.dtype)


def _rms_project(x2d, g, w, col_gain, col_flag, n_plain_cols, w_small, tm, tn):
    n, d = x2d.shape
    c = w.shape[1]
    has_small = w_small is not None
    in_specs = [
        pl.BlockSpec((tm, d), lambda i, j: (i, 0)),
        pl.BlockSpec((1, d), lambda i, j: (0, 0)),
        pl.BlockSpec((d, tn), lambda i, j: (0, j)),
        pl.BlockSpec((1, tn), lambda i, j: (0, j)),
        pl.BlockSpec((1, tn), lambda i, j: (0, j)),
    ]
    args = [x2d, g.reshape(1, d), w, col_gain.reshape(1, c), col_flag.reshape(1, c)]
    out_shape = [jax.ShapeDtypeStruct((n, c), _BF16)]
    out_specs = [pl.BlockSpec((tm, tn), lambda i, j: (i, j))]
    if has_small:
        ws = w_small.shape[1]
        in_specs.append(pl.BlockSpec((d, ws), lambda i, j: (0, 0)))
        args.append(w_small)
        out_shape.append(jax.ShapeDtypeStruct((n, ws), _F32))
        out_specs.append(pl.BlockSpec((tm, ws), lambda i, j: (i, 0)))
    assert n % tm == 0 and c % tn == 0 and n_plain_cols % tn == 0
    res = pl.pallas_call(
        functools.partial(_proj_kernel, n_plain_cols // tn, has_small),
        grid=(n // tm, c // tn),
        in_specs=in_specs,
        out_specs=out_specs,
        out_shape=out_shape,
        scratch_shapes=[pltpu.VMEM((tm, d), _BF16)],
        compiler_params=_cparams(("arbitrary", "arbitrary")),
        name="rms_project",
    )(*args)
    return res if has_small else res[0]


def _cmp_kernel(x_ref, wlo_ref, whi_ref, pelo_ref, pehi_ref, kg_ref, o_ref):
    j = pl.program_id(1)
    x = x_ref[0, 0]
    nchunk = x.shape[0]
    ylo = _dot(x, wlo_ref[0])
    yhi = _dot(x, whi_ref[0])
    pe = _dot(pelo_ref[0], wlo_ref[0]) + _dot(pehi_ref[0], whi_ref[0])
    y = ylo + pltpu.roll(yhi, nchunk - 1, 0) + pe[0:1, :]

    @pl.when(j < NSA_KV_HEADS)
    def _():
        ms = jnp.mean(y * y, axis=-1, keepdims=True)
        o_ref[0, 0] = (y * lax.rsqrt(ms + RMS_EPS) * kg_ref[...]).astype(o_ref.dtype)

    @pl.when(j >= NSA_KV_HEADS)
    def _():
        o_ref[0, 0] = y.T.astype(o_ref.dtype)


def _nsa_compress(cmp_in, w_ck, w_cv, pe_k, pe_v, k_g):
    b, nj, t, dk = cmp_in.shape
    nchunk = t // CMP_STRIDE
    half = CMP_LEN // 2
    assert nchunk == dk
    x = cmp_in.reshape(b, nj, nchunk, CMP_STRIDE * dk)

    def halves(w):
        return (w[:half].reshape(half * dk, dk).astype(_BF16),
                w[half:].reshape(half * dk, dk).astype(_BF16))

    klo, khi = halves(w_ck)
    vlo, vhi = halves(w_cv)
    wlo = jnp.stack([klo, vlo])
    whi = jnp.stack([khi, vhi])

    def pe_halves(pe):
        lo = jnp.broadcast_to(pe[:half].reshape(1, half * dk), (8, half * dk)).astype(_BF16)
        hi = jnp.broadcast_to(pe[half:].reshape(1, half * dk), (8, half * dk)).astype(_BF16)
        return lo, hi

    pklo, pkhi = pe_halves(pe_k)
    pvlo, pvhi = pe_halves(pe_v)
    pelo = jnp.stack([pklo, pvlo])
    pehi = jnp.stack([pkhi, pvhi])
    kv = lambda bb, j: (j // NSA_KV_HEADS, 0, 0)
    return pl.pallas_call(
        _cmp_kernel,
        grid=(b, nj),
        in_specs=[
            pl.BlockSpec((1, 1, nchunk, CMP_STRIDE * dk), lambda bb, j: (bb, j, 0, 0)),
            pl.BlockSpec((1, half * dk, dk), kv),
            pl.BlockSpec((1, half * dk, dk), kv),
            pl.BlockSpec((1, 8, half * dk), kv),
            pl.BlockSpec((1, 8, half * dk), kv),
            pl.BlockSpec((1, dk), lambda bb, j: (0, 0)),
        ],
        out_specs=pl.BlockSpec((1, 1, nchunk, dk), lambda bb, j: (bb, j, 0, 0)),
        out_shape=jax.ShapeDtypeStruct((b, nj, nchunk, dk), _BF16),
        compiler_params=_cparams(("arbitrary", "arbitrary")),
        name="nsa_compress",
    )(x, wlo, whi, pelo, pehi, k_g.reshape(1, dk).astype(_F32))


def _t_update(state, tiles):
    m, l, acc = state
    masked = [s if mask is None else jnp.where(mask, s, NEG_INF) for s, mask, _ in tiles]
    m_new = m
    for sm in masked:
        m_new = jnp.maximum(m_new, jnp.max(sm, axis=0, keepdims=True))
    alpha = jnp.exp2(m - m_new)
    l_new = alpha * l
    acc_new = alpha * acc
    for sm, (_, _, v) in zip(masked, tiles):
        e = jnp.exp2(sm - m_new)
        l_new = l_new + jnp.sum(e, axis=0, keepdims=True)
        acc_new = acc_new + _dot_tn(v, e.astype(_BF16))
    return m_new, l_new, acc_new


def _t_init(nq):
    return (jnp.full((1, nq), NEG_INF, _F32), jnp.zeros((1, nq), _F32), jnp.zeros((HEAD_DIM, nq), _F32))


def _t_finish(state):
    _, l, acc = state
    return acc / l


def _nsa_kernel(q_ref, kc_ref, vct_ref, ks_ref, vs_ref, kw_ref, vw_ref, gl_ref,
                wt_ref, dt_ref, ovt_ref, ext_ref, o_ref, sel_ref):
    i = pl.program_id(2)
    tq = ATT_TQ
    hg = NSA_GROUP
    nq = hg * tq
    n_slc = ovt_ref.shape[0]

    q = q_ref[0]
    qt = _lanes([q[:, h * HEAD_DIM:(h + 1) * HEAD_DIM].astype(_F32).T for h in range(hg)]).astype(_BF16)

    key = lax.broadcasted_iota(jnp.int32, (LANES, nq), 0)
    qry = lax.broadcasted_iota(jnp.int32, (LANES, nq), 1) & (tq - 1)

    def ktile(ref, kt):
        return ref[0, pl.ds(pl.multiple_of(kt * LANES, LANES), LANES), :]

    def near_bias(d):
        return _lanes([dt_ref[h, d] for h in range(hg)])

    woff = pl.multiple_of(wt_ref.shape[1] - LANES - 8 - (LANES // CMP_STRIDE) * i, 8)
    s = _dot(kc_ref[0, 0], qt) + _lanes([wt_ref[h, pl.ds(woff, LANES), :] for h in range(hg)])
    n_win = WINDOW // tq
    win_tiles = []
    for d in range(n_win, -1, -1):
        ktd = jnp.maximum(i - d, 0)
        sc = _dot(ktile(kw_ref, ktd), qt)
        if d <= 1:
            sc = sc + near_bias(d)
        if d == n_win:
            mk = (qry < key) & (i >= d)
        elif d == 0:
            mk = key <= qry
        else:
            mk = jnp.broadcast_to(i >= d, (LANES, nq))
        win_tiles.append((sc, mk, ktile(vw_ref, ktd)))

    mask_c = (i * tq + qry) >= (CMP_STRIDE * key + CMP_LEN - 1)
    mx = jnp.max(jnp.where(mask_c, s, NEG_INF), axis=0, keepdims=True)
    e = jnp.where(mask_c, jnp.exp2(s - mx), 0.0)
    l = jnp.sum(e, axis=0, keepdims=True)
    p_c = e / jnp.where(l > 0.0, l, 1.0)
    o_c = _dot(vct_ref[0, 0], p_c.astype(_BF16))

    ps = p_c[:, 0:tq]
    for h in range(1, hg):
        ps = ps + p_c[:, h * tq:(h + 1) * tq]
    ps_hi = ps.astype(_BF16)
    ps_lo = (ps - ps_hi.astype(_F32)).astype(_BF16)
    imp = _dot(ovt_ref[...], ps_hi) + _dot(ovt_ref[...], ps_lo)
    blk = lax.broadcasted_iota(jnp.int32, (n_slc, tq), 0)
    tpos = i * tq + lax.broadcasted_iota(jnp.int32, (n_slc, tq), 1)
    cur = tpos // SLC_LEN
    valid = blk * SLC_LEN <= tpos
    forced = (blk == 0) | (blk == cur) | (blk == cur - 1)
    score = jnp.where(valid, jnp.where(forced, FORCE_SCORE, imp), -1.0)
    rank = jnp.zeros((n_slc, tq), _F32)
    for k in range(n_slc):
        ck = score[k:k + 1, :]
        beats = (ck > score) | ((ck == score) & (blk > k))
        rank = rank + jnp.where(beats, 1.0, 0.0)
    sel = jnp.where(rank < float(min(N_SELECT, n_slc)), 1.0, 0.0).astype(_BF16)
    selfull = _dot(ext_ref[...], sel)
    for kt in range(ext_ref.shape[0] // LANES):
        sel_ref[kt] = selfull[kt * LANES:(kt + 1) * LANES, :]

    def sel_mask(kt, ok=True):
        return _lanes([sel_ref[kt]] * hg) > jnp.where(ok, 0.5, 2.0)

    o_w = _t_finish(_t_update(_t_init(nq), win_tiles))

    n_far = jnp.maximum(i - 1, 0)

    def far_body(j, st):
        k0 = pl.multiple_of(j * (FAR_GROUP * LANES), FAR_GROUP * LANES)
        sg = _dot(ks_ref[0, pl.ds(k0, FAR_GROUP * LANES), :], qt)
        tiles = []
        for u in range(FAR_GROUP):
            kt = FAR_GROUP * j + u
            tiles.append((sg[u * LANES:(u + 1) * LANES], sel_mask(kt, kt < n_far), ktile(vs_ref, kt)))
        return _t_update(st, tiles)

    st = lax.fori_loop(0, (n_far + FAR_GROUP - 1) // FAR_GROUP, far_body, _t_init(nq))
    kt1 = jnp.maximum(i - 1, 0)
    st = _t_update(st, [
        (_dot(ktile(ks_ref, kt1), qt) + near_bias(1), sel_mask(kt1, i >= 1), ktile(vs_ref, kt1)),
        (_dot(ktile(ks_ref, i), qt) + near_bias(0), sel_mask(i) & (key <= qry), ktile(vs_ref, i))])
    o_s = _t_finish(st)

    glt = jax.nn.sigmoid(gl_ref[0]).T
    grow = lambda br: _lanes([glt[br * hg + h:br * hg + h + 1, :] for h in range(hg)])
    o = grow(0) * o_c + grow(1) * o_s + grow(2) * o_w
    for h in range(hg):
        o_ref[0, :, h * HEAD_DIM:(h + 1) * HEAD_DIM] = o[:, h * tq:(h + 1) * tq].T.astype(o_ref.dtype)


def _t5_bucket(rel):
    n = np.maximum(rel, 0)
    max_exact = NUM_BUCKETS // 2
    ratio = np.maximum(n, 1).astype(np.float32) / np.float32(max_exact)
    log_ratio = np.log(ratio) / np.float32(math.log(MAX_DISTANCE / max_exact))
    large = np.minimum(max_exact + (log_ratio * np.float32(NUM_BUCKETS - max_exact)).astype(np.int32),
                       NUM_BUCKETS - 1)
    return np.where(n < max_exact, n, large).astype(np.int32)


def _bias_lookup(rb, bucket):
    bk = jnp.asarray(bucket.astype(np.int8))[None]
    ex = (slice(None),) + (None,) * bucket.ndim
    tab = jnp.broadcast_to(rb[:, 0][ex], (rb.shape[0],) + bucket.shape)
    for k in range(1, NUM_BUCKETS):
        tab = jnp.where(bk == k, rb[:, k][ex], tab)
    return tab


def _nsa_tables(rel_bias):
    rb = rel_bias.astype(_F32).T * LOG2E
    r = np.arange(ATT_TQ)
    rel_d = (np.arange(2) * ATT_TQ)[:, None, None] + r[None, None, :] - r[None, :, None]
    assert _t5_bucket(np.array([ATT_TQ + 1]))[0] == NUM_BUCKETS - 1
    u = np.arange(2 * LANES)
    rel_w = r[None, :] - CMP_STRIDE * (u[:, None] - (LANES - 8)) - (CMP_LEN - 1)
    far = rb[:, NUM_BUCKETS - 1]
    dt = _bias_lookup(rb, _t5_bucket(rel_d)) - far[:, None, None, None]
    return dt, _bias_lookup(rb, _t5_bucket(rel_w))


def _nsa_attention(proj, small, kvc, rel_bias):
    b, t, _ = proj.shape
    g, hg, tq = NSA_KV_HEADS, NSA_GROUP, ATT_TQ
    n_slc = t // SLC_LEN
    n_cmp = (t - CMP_LEN) // CMP_STRIDE + 1
    nt = t // LANES
    assert n_cmp <= LANES and kvc.shape[2] == LANES and n_slc % 16 == 0 and tq == LANES
    assert nt % FAR_GROUP == 0
    assert LANES - 8 - (LANES // CMP_STRIDE) * (t // tq - 1) >= 0
    dt, wt = _nsa_tables(rel_bias)
    cstart = np.arange(LANES) * CMP_STRIDE
    sstart = np.arange(n_slc) * SLC_LEN
    overlap = np.clip(np.minimum(cstart[None, :] + CMP_LEN, sstart[:, None] + SLC_LEN)
                      - np.maximum(cstart[None, :], sstart[:, None]), 0, None).astype(np.float32) / CMP_LEN
    overlap[:, n_cmp:] = 0.0
    ovt = jnp.asarray(overlap, _BF16)
    ext = jnp.asarray(np.arange(t)[:, None] // SLC_LEN == np.arange(n_slc)[None, :], _BF16)

    seq = lambda cb: pl.BlockSpec((1, t, LANES), lambda bb, gg, ii: (bb, 0, cb + gg))
    return pl.pallas_call(
        _nsa_kernel,
        grid=(b, g, t // tq),
        in_specs=[
            pl.BlockSpec((1, tq, hg * HEAD_DIM), lambda bb, gg, ii: (bb, ii, CB_QNSA * LANES // (hg * HEAD_DIM) + gg)),
            pl.BlockSpec((1, 1, LANES, HEAD_DIM), lambda bb, gg, ii: (bb, gg, 0, 0)),
            pl.BlockSpec((1, 1, HEAD_DIM, LANES), lambda bb, gg, ii: (bb, NSA_KV_HEADS + gg, 0, 0)),
            seq(CB_KSLC), seq(CB_VSLC), seq(CB_KWIN), seq(CB_VWIN),
            pl.BlockSpec((1, tq, LANES), lambda bb, gg, ii: (bb, ii, gg)),
            pl.BlockSpec((hg, 2 * LANES, tq), lambda bb, gg, ii: (gg, 0, 0)),
            pl.BlockSpec((hg, 2, LANES, tq), lambda bb, gg, ii: (gg, 0, 0, 0)),
            pl.BlockSpec((n_slc, LANES), lambda bb, gg, ii: (0, 0)),
            pl.BlockSpec((t, n_slc), lambda bb, gg, ii: (0, 0)),
        ],
        out_specs=pl.BlockSpec((1, tq, hg * HEAD_DIM), lambda bb, gg, ii: (bb, ii, gg)),
        out_shape=jax.ShapeDtypeStruct((b, t, NSA_Q_W), _BF16),
        scratch_shapes=[pltpu.VMEM((nt, LANES, tq), _F32)],
        compiler_params=_cparams(("arbitrary", "arbitrary", "arbitrary")),
        name="nsa_attention",
    )(proj, kvc, kvc, proj, proj, proj, proj, small, wt, dt, ovt, ext)


def _fox_cum_kernel(s_ref, b_ref, col_ref, row_ref):
    z = s_ref[0] + b_ref[...]
    lf = (jnp.minimum(z, 0.0) - jnp.log1p(jnp.exp(-jnp.abs(z)))) * LOG2E
    x = lf.T
    t = x.shape[1]
    lane = lax.broadcasted_iota(jnp.int32, x.shape, 1)
    sh = 1
    while sh < t:
        x = x + jnp.where(lane >= sh, pltpu.roll(x, sh, 1), 0.0)
        sh *= 2
    row_ref[0] = x[8:16, :]
    col_ref[0] = x.T


def _fox_cumsum(small, b_f):
    b, t, _ = small.shape
    bvec = jnp.zeros((1, LANES), _F32).at[0, FOXF_LANE:FOXF_LANE + FOX_HEADS].set(b_f.astype(_F32))
    return pl.pallas_call(
        _fox_cum_kernel,
        grid=(b,),
        in_specs=[pl.BlockSpec((1, t, LANES), lambda bb: (bb, 0, 1)),
                  pl.BlockSpec((1, LANES), lambda bb: (0, 0))],
        out_specs=[pl.BlockSpec((1, t, LANES), lambda bb: (bb, 0, 0)),
                   pl.BlockSpec((1, 8, t), lambda bb: (bb, 0, 0))],
        out_shape=[jax.ShapeDtypeStruct((b, t, LANES), _F32), jax.ShapeDtypeStruct((b, 8, t), _F32)],
        compiler_params=_cparams(("arbitrary",)),
        name="fox_cumsum",
    )(small, bvec)


def _fox_kernel(q_ref, k_ref, v_ref, cc_ref, cr_ref, o_ref):
    i = pl.program_id(1)
    tq = FOX_TQ
    key = lax.broadcasted_iota(jnp.int32, (tq, tq), 0)
    qry = lax.broadcasted_iota(jnp.int32, (tq, tq), 1)
    heads = range(FOX_HEADS)
    hsl = [slice(h * HEAD_DIM, (h + 1) * HEAD_DIM) for h in heads]
    qts = [q_ref[0, :, hsl[h]].astype(_F32).T.astype(_BF16) for h in heads]
    cqs = [cr_ref[0, FOXF_LANE - 8 + h:FOXF_LANE - 8 + h + 1, :] for h in heads]

    def keys(kt):
        return pl.ds(pl.multiple_of(kt * tq, tq), tq)

    def logits(h, kt):
        ck = cc_ref[0, keys(kt), FOXF_LANE + h:FOXF_LANE + h + 1]
        return _dot(k_ref[0, keys(kt), hsl[h]], qts[h]) + (cqs[h] - ck)

    def body(kt, sts):
        ss = [logits(h, kt) for h in heads]
        return tuple(_t_update(sts[h], [(ss[h], None, v_ref[0, keys(kt), hsl[h]])]) for h in heads)

    sts = lax.fori_loop(0, i, body, tuple(_t_init(tq) for _ in heads))
    ss = [logits(h, i) for h in heads]
    for h in heads:
        st = _t_update(sts[h], [(ss[h], key <= qry, v_ref[0, keys(i), hsl[h]])])
        o_ref[0, :, hsl[h]] = _t_finish(st).T.astype(o_ref.dtype)


def _fox_attention(proj, cum_col, cum_row):
    b, t, _ = proj.shape
    tq = FOX_TQ
    w = FOX_W
    return pl.pallas_call(
        _fox_kernel,
        grid=(b, t // tq),
        in_specs=[
            pl.BlockSpec((1, tq, w), lambda bb, ii: (bb, ii, CB_FOXQ * LANES // w)),
            pl.BlockSpec((1, t, w), lambda bb, ii: (bb, 0, CB_FOXK * LANES // w)),
            pl.BlockSpec((1, t, w), lambda bb, ii: (bb, 0, CB_FOXV * LANES // w)),
            pl.BlockSpec((1, t, LANES), lambda bb, ii: (bb, 0, 0)),
            pl.BlockSpec((1, 8, tq), lambda bb, ii: (bb, 0, ii)),
        ],
        out_specs=pl.BlockSpec((1, tq, w), lambda bb, ii: (bb, ii, 0)),
        out_shape=jax.ShapeDtypeStruct((b, t, w), _BF16),
        compiler_params=_cparams(("arbitrary", "arbitrary")),
        name="fox_attention",
    )(proj, proj, proj, cum_col, cum_row)


def _mem_kernel(q_ref, kv_ref, o_ref):
    for h in range(MEM_HEADS):
        hs = slice(h * HEAD_DIM, (h + 1) * HEAD_DIM)
        vs = slice(MEM_W + h * HEAD_DIM, MEM_W + (h + 1) * HEAD_DIM)
        s = _dot_nt(q_ref[0, :, hs], kv_ref[0, :, hs])
        e = jnp.exp2(s - jnp.max(s, axis=1, keepdims=True))
        p = e / jnp.sum(e, axis=1, keepdims=True)
        o_ref[0, :, hs] = _dot(p.astype(_BF16), kv_ref[0, :, vs]).astype(o_ref.dtype)


def _mem_attention(proj, memkv):
    b, t, _ = proj.shape
    m = memkv.shape[1]
    tq = MEM_TQ
    return pl.pallas_call(
        _mem_kernel,
        grid=(b, t // tq),
        in_specs=[pl.BlockSpec((1, tq, MEM_W), lambda bb, ii: (bb, ii, CB_MEMQ * LANES // MEM_W)),
                  pl.BlockSpec((1, m, 2 * MEM_W), lambda bb, ii: (bb, 0, 0))],
        out_specs=pl.BlockSpec((1, tq, MEM_W), lambda bb, ii: (bb, ii, 0)),
        out_shape=jax.ShapeDtypeStruct((b, t, MEM_W), _BF16),
        compiler_params=_cparams(("arbitrary", "arbitrary")),
        name="mem_attention",
    )(proj, memkv)


def _merge_kernel(x_ref, on_ref, of_ref, om_ref, g0_ref, g1_ref, g2_ref, wn_ref, wf_ref, wm_ref, wo_ref,
                  o_ref, z_ref):
    tn = 512
    for c in range(D_MODEL // tn):
        cs = slice(c * tn, (c + 1) * tn)
        z = (jax.nn.sigmoid(g0_ref[:, cs].astype(_F32)) * _dot(on_ref[...], wn_ref[:, cs])
             + jax.nn.sigmoid(g1_ref[:, cs].astype(_F32)) * _dot(of_ref[...], wf_ref[:, cs])
             + jax.nn.sigmoid(g2_ref[:, cs].astype(_F32)) * _dot(om_ref[...], wm_ref[:, cs]))
        z_ref[:, cs] = z.astype(_BF16)
    o_ref[...] = x_ref[...] + _dot(z_ref[...], wo_ref[...])


def _merge(x2d, o_nsa, o_fox, o_mem, proj2d, w_o_nsa, w_o_fox, w_o_mem, w_out):
    n, d = x2d.shape
    tm = MERGE_TM
    gcb = CB_MERGE * LANES // d
    const = lambda r, c: pl.BlockSpec((r, c), lambda i: (0, 0))
    return pl.pallas_call(
        _merge_kernel,
        grid=(n // tm,),
        in_specs=[
            pl.BlockSpec((tm, d), lambda i: (i, 0)),
            pl.BlockSpec((tm, NSA_Q_W), lambda i: (i, 0)),
            pl.BlockSpec((tm, FOX_W), lambda i: (i, 0)),
            pl.BlockSpec((tm, MEM_W), lambda i: (i, 0)),
            pl.BlockSpec((tm, d), lambda i: (i, gcb)),
            pl.BlockSpec((tm, d), lambda i: (i, gcb + 1)),
            pl.BlockSpec((tm, d), lambda i: (i, gcb + 2)),
            const(NSA_Q_W, d), const(FOX_W, d), const(MEM_W, d), const(d, d),
        ],
        out_specs=pl.BlockSpec((tm, d), lambda i: (i, 0)),
        out_shape=jax.ShapeDtypeStruct((n, d), _F32),
        scratch_shapes=[pltpu.VMEM((tm, d), _BF16)],
        compiler_params=_cparams(("arbitrary",)),
        name="merge_out",
    )(x2d, o_nsa, o_fox, o_mem, proj2d, proj2d, proj2d,
      w_o_nsa.astype(_BF16), w_o_fox.astype(_BF16), w_o_mem.astype(_BF16), w_out.astype(_BF16))


def _nsa_inputs(x, norm_attn_g, w_in, nsa_pe_k, nsa_w_ck, nsa_pe_v, nsa_w_cv, nsa_q_g, nsa_k_g,
                fox_q_g, fox_k_g, mem_q_g):
    b, t, d = x.shape
    ones = lambda k: jnp.ones((k * HEAD_DIM,), _F32)
    zeros = lambda k: jnp.zeros((k * HEAD_DIM,), _F32)
    tile = lambda gv, k: jnp.tile(gv.astype(_F32), k)
    qs = ATTN_SCALE * LOG2E

    g_end = NSA_Q_W + 6 * NSA_KV_W
    f_off = g_end + NSA_GATE_W + 3 * FOX_W
    m_off = f_off + FOX_HEADS + MEM_W
    narrow = tuple((g_end + (br * NSA_KV_HEADS + gg) * NSA_GROUP, NSA_GROUP, gg * LANES + br * NSA_GROUP)
                   for br in range(3) for gg in range(NSA_KV_HEADS))
    narrow += ((f_off, FOX_HEADS, LANES + FOXF_LANE),)
    w_all, w_small = _reorder_cast(
        w_in, ((m_off, MERGE_W), (0, g_end), (g_end + NSA_GATE_W, 3 * FOX_W), (f_off + FOX_HEADS, MEM_W)),
        narrow, SMALL_W)
    col_gain = jnp.concatenate([
        jnp.ones((MERGE_W,), _F32),
        tile(nsa_q_g, NSA_HEADS) * qs, ones(2), ones(2), tile(nsa_k_g, 2), ones(2), tile(nsa_k_g, 2), ones(2),
        tile(fox_q_g, FOX_HEADS) * qs, tile(fox_k_g, FOX_HEADS), ones(FOX_HEADS),
        tile(mem_q_g, MEM_HEADS) * qs])
    col_flag = jnp.concatenate([
        jnp.zeros((MERGE_W,), _F32),
        ones(NSA_HEADS), zeros(2), zeros(2), ones(2), zeros(2), ones(2), zeros(2),
        ones(FOX_HEADS), ones(FOX_HEADS), zeros(FOX_HEADS), ones(MEM_HEADS)])

    x2d = x.reshape(b * t, d)
    proj2d, small2d = _rms_project(x2d, norm_attn_g, w_all, col_gain, col_flag, CB_ATT * LANES, w_small,
                                   PROJ_TM, PROJ_TN)
    proj = proj2d.reshape(b, t, PROJ_COLS)
    small = small2d.reshape(b, t, SMALL_W)

    cmp_in = proj[:, :, CB_KCMP * LANES:(CB_KCMP + 4) * LANES].reshape(b, t, 4, HEAD_DIM).transpose(0, 2, 1, 3)
    kvc = _nsa_compress(cmp_in, nsa_w_ck, nsa_w_cv, nsa_pe_k, nsa_pe_v, nsa_k_g)
    return proj, small, kvc


def _attention_mixers(x, mem, norm_attn_g, w_in, nsa_pe_k, nsa_w_ck, nsa_pe_v, nsa_w_cv, nsa_q_g, nsa_k_g,
                      rel_bias, fox_b_f, fox_q_g, fox_k_g, norm_mem_g, w_mem_kv, mem_q_g, mem_k_g):
    b, t, d = x.shape
    ones = lambda k: jnp.ones((k * HEAD_DIM,), _F32)
    zeros = lambda k: jnp.zeros((k * HEAD_DIM,), _F32)
    tile = lambda gv, k: jnp.tile(gv.astype(_F32), k)
    proj, small, kvc = _nsa_inputs(x, norm_attn_g, w_in, nsa_pe_k, nsa_w_ck, nsa_pe_v, nsa_w_cv, nsa_q_g, nsa_k_g,
                                   fox_q_g, fox_k_g, mem_q_g)
    proj2d = proj.reshape(b * t, PROJ_COLS)
    o_nsa = _nsa_attention(proj, small, kvc, rel_bias)

    cum_col, cum_row = _fox_cumsum(small, fox_b_f)
    o_fox = _fox_attention(proj, cum_col, cum_row)

    mm = mem.shape[1]
    mem_gain = jnp.concatenate([tile(mem_k_g, MEM_HEADS), ones(MEM_HEADS)])
    mem_flag = jnp.concatenate([ones(MEM_HEADS), zeros(MEM_HEADS)])
    memkv = _rms_project(mem.reshape(b * mm, d), norm_mem_g, w_mem_kv.astype(_BF16), mem_gain, mem_flag,
                         0, None, mm, MEM_W).reshape(b, mm, 2 * MEM_W)
    o_mem = _mem_attention(proj, memkv)
    return proj2d, o_nsa.reshape(b * t, NSA_Q_W), o_fox.reshape(b * t, FOX_W), o_mem.reshape(b * t, MEM_W)


def _router_kernel(x_ref, g_ref, whi_ref, wlo_ref, b_ref, h_ref, idx_ref, gate_ref, cnt_ref):
    x = x_ref[...]
    ms = jnp.mean(x * x, axis=-1, keepdims=True)
    h = x * lax.rsqrt(ms + RMS_EPS) * g_ref[...]
    h_hi = h.astype(_BF16)
    h_ref[...] = _to_token_major(h_hi)
    h_lo = (h - h_hi.astype(_F32)).astype(_BF16)
    logits = _dot_nt(whi_ref[...], h_hi) + _dot_nt(whi_ref[...], h_lo) + _dot_nt(wlo_ref[...], h_hi)
    scores = jax.nn.sigmoid(logits)
    sb = scores + b_ref[...]
    eidx = lax.broadcasted_iota(jnp.int32, sb.shape, 0)
    idxs, vals = [], []
    for _ in range(TOP_K):
        m = jnp.max(sb, axis=0, keepdims=True)
        idx = jnp.min(jnp.where(sb == m, eidx, N_EXPERTS), axis=0, keepdims=True)
        hit = eidx == idx
        vals.append(jnp.sum(jnp.where(hit, scores, 0.0), axis=0, keepdims=True))
        idxs.append(idx)
        sb = jnp.where(hit, NEG_INF, sb)
    top_s = jnp.concatenate(vals, axis=0)
    idx_ref[...] = jnp.concatenate(idxs, axis=0)
    gate_ref[...] = top_s / jnp.sum(top_s, axis=0, keepdims=True) * ROUTED_SCALE

    @pl.when(pl.program_id(0) == 0)
    def _():
        cnt_ref[...] = jnp.zeros(cnt_ref.shape, cnt_ref.dtype)

    picked = jnp.where(sb < 0.5 * NEG_INF, 1.0, 0.0)
    cnt_ref[...] += jnp.sum(picked, axis=1, keepdims=True)


def _router(x1, norm_g, w_router, router_bias):
    n, d = x1.shape
    tm = ROUTER_TM
    wt = w_router.astype(_F32).T
    w_hi = wt.astype(_BF16)
    w_lo = (wt - w_hi.astype(_F32)).astype(_BF16)
    const = lambda r, c: pl.BlockSpec((r, c), lambda i: (0, 0))
    return pl.pallas_call(
        _router_kernel,
        grid=(n // tm,),
        in_specs=[pl.BlockSpec((tm, d), lambda i: (i, 0)), const(1, d), const(N_EXPERTS, d), const(N_EXPERTS, d),
                  const(N_EXPERTS, 1)],
        out_specs=[pl.BlockSpec((tm, d // LANES, LANES), lambda i: (i, 0, 0)),
                   pl.BlockSpec((TOP_K, tm), lambda i: (0, i)),
                   pl.BlockSpec((TOP_K, tm), lambda i: (0, i)),
                   const(N_EXPERTS, 1)],
        out_shape=[jax.ShapeDtypeStruct((n, d // LANES, LANES), _BF16), jax.ShapeDtypeStruct((TOP_K, n), jnp.int32),
                   jax.ShapeDtypeStruct((TOP_K, n), _F32), jax.ShapeDtypeStruct((N_EXPERTS, 1), _F32)],
        compiler_params=_cparams(("arbitrary",)),
        name="moe_router",
    )(x1, norm_g.reshape(1, d).astype(_F32), w_hi, w_lo, router_bias.reshape(N_EXPERTS, 1).astype(_F32))


def _moe_kernel(be_ref, nbr_ref, src_hbm, dst_hbm, h_hbm, sg_ref, wg_ref, wu_ref, wd_ref, y_hbm,
                xbuf, ybuf, xmat, sidx, didx, wg_bf, wu_bf, wd_bf, gsem, ssem, isem):
    n = pl.program_id(0)
    nbr = nbr_ref[0]
    last = nbr - 1
    slot = n % 2
    other = 1 - slot
    blk = MOE_BLOCK
    dump_block = dst_hbm.shape[0] - 1
    ring = didx.shape[0]

    def src_copy(block, s):
        return pltpu.make_async_copy(src_hbm.at[block], sidx.at[s], isem.at[0, s])

    def dst_copy(block, s3):
        return pltpu.make_async_copy(dst_hbm.at[block], didx.at[s3], isem.at[1, s3])

    def issue_gather(s):
        for r in range(blk):
            tok = sidx[s, 0, r]
            pltpu.make_async_copy(h_hbm.at[tok], xbuf.at[s, r], gsem.at[s]).start(priority=r % 2)

    def issue_scatter(s, s3):
        for r in range(blk):
            row = didx[s3, 0, r]
            pltpu.make_async_copy(ybuf.at[s, r], y_hbm.at[row], ssem.at[s]).start(priority=r % 2)

    def wait_rows(buf, sem, s):
        pltpu.make_async_copy(buf.at[s], buf.at[s], sem.at[s]).wait()

    @pl.when(n < nbr)
    def _():
        @pl.when(n == 0)
        def _():
            src_copy(0, 0).start()
            dst_copy(0, 0).start()
            dst_copy(dump_block, ring - 1).start()
            ybuf[1] = jnp.zeros(ybuf.shape[1:], ybuf.dtype)
            src_copy(0, 0).wait()
            dst_copy(0, 0).wait()
            dst_copy(dump_block, ring - 1).wait()
            issue_gather(0)
            nxt0 = jnp.minimum(1, last)
            src_copy(nxt0, 1).start()
            dst_copy(nxt0, 1).start()

        wait_rows(xbuf, gsem, slot)

        @pl.when(n < last)
        def _():
            nn = jnp.minimum(n + 2, last)
            src_copy(nn, slot).start()
            dst_copy(nn, (n + 2) % ring).start()

        changed = (n == 0) | (be_ref[n] != be_ref[jnp.maximum(n - 1, 0)])

        @pl.when(changed)
        def _():
            wg_bf[...] = wg_ref[0].astype(_BF16)
            wu_bf[...] = wu_ref[0].astype(_BF16)
            wd_bf[...] = wd_ref[0].astype(_BF16)

        @pl.when(n >= 1)
        def _():
            wait_rows(ybuf, ssem, slot)

        nxt = jnp.minimum(n + 1, last)
        src_copy(nxt, other).wait()
        dst_copy(nxt, (n + 1) % ring).wait()

        issue_scatter(other, (n + ring - 1) % ring)
        xmat[...] = _from_token_major(xbuf[slot])
        issue_gather(other)
        gcol = jnp.broadcast_to(sg_ref[0], (LANES, blk)).T
        a = jax.nn.silu(_dot(xmat[...], wg_bf[...])) * _dot(xmat[...], wu_bf[...])
        a = (a * _lanes([gcol] * (a.shape[1] // LANES))).astype(_BF16)
        ybuf[slot] = _to_token_major(_dot(a, wd_bf[...]).astype(_BF16))

        @pl.when(n == last)
        def _():
            issue_scatter(slot, n % ring)
            wait_rows(xbuf, gsem, other)
            wait_rows(ybuf, ssem, other)
            wait_rows(ybuf, ssem, slot)


def _moe_experts(h3, block_expert, nb_real, slot_src, slot_dst, slot_gate, we_gate, we_up, we_down):
    n, c, _ = h3.shape
    d = c * LANES
    nb = block_expert.shape[0]
    blk = MOE_BLOCK
    e, _, de = we_gate.shape
    grid_spec = pltpu.PrefetchScalarGridSpec(
        num_scalar_prefetch=2,
        grid=(nb,),
        in_specs=[
            pl.BlockSpec(memory_space=pl.ANY),
            pl.BlockSpec(memory_space=pl.ANY),
            pl.BlockSpec(memory_space=pl.ANY),
            pl.BlockSpec((1, 1, blk), lambda i, be, nbr: (i, 0, 0)),
            pl.BlockSpec((1, d, de), lambda i, be, nbr: (be[i], 0, 0)),
            pl.BlockSpec((1, d, de), lambda i, be, nbr: (be[i], 0, 0)),
            pl.BlockSpec((1, de, d), lambda i, be, nbr: (be[i], 0, 0)),
        ],
        out_specs=pl.BlockSpec(memory_space=pl.ANY),
        scratch_shapes=[
            pltpu.VMEM((2, blk, c, LANES), _BF16), pltpu.VMEM((2, blk, c, LANES), _BF16),
            pltpu.VMEM((blk, d), _BF16),
            pltpu.SMEM((2, 1, blk), jnp.int32), pltpu.SMEM((4, 1, blk), jnp.int32),
            pltpu.VMEM((d, de), _BF16), pltpu.VMEM((d, de), _BF16), pltpu.VMEM((de, d), _BF16),
            pltpu.SemaphoreType.DMA((2,)), pltpu.SemaphoreType.DMA((2,)), pltpu.SemaphoreType.DMA((2, 4)),
        ],
    )
    return pl.pallas_call(
        _moe_kernel,
        grid_spec=grid_spec,
        out_shape=jax.ShapeDtypeStruct((TOP_K * n + blk, c, LANES), _BF16),
        compiler_params=_cparams(("arbitrary",)),
        name="moe_experts",
    )(block_expert, nb_real, slot_src, slot_dst, h3, slot_gate, we_gate, we_up, we_down)


def _dispatch_plan(top_idx, gates, counts, n):
    blk = MOE_BLOCK
    a = n * TOP_K
    nb = (a + N_EXPERTS * (blk - 1)) // blk
    order = jnp.argsort(top_idx.reshape(a)).astype(jnp.int32)
    counts = counts.astype(jnp.int32)
    padded = (counts + blk - 1) // blk * blk
    start = jnp.cumsum(counts) - counts
    pstart = jnp.cumsum(padded) - padded
    block_end = jnp.cumsum(padded) // blk
    blocks = jnp.arange(nb, dtype=jnp.int32)
    block_expert = jnp.minimum(jnp.sum((block_end[None, :] <= blocks[:, None]).astype(jnp.int32), axis=1),
                               N_EXPERTS - 1)
    off = blocks * blk - pstart[block_expert]
    base = start[block_expert] + off
    lane = jnp.arange(blk, dtype=jnp.int32)[None, :]
    rows = jnp.take(order, base[:, None] + lane, mode="clip")
    real = (off[:, None] + lane) < counts[block_expert][:, None]
    tok = rows // TOP_K
    kk = rows - tok * TOP_K
    slot_src = jnp.where(real, tok, 0)
    slot_dst = jnp.where(real, kk * n + tok, TOP_K * n + lane)
    slot_dst = jnp.concatenate([slot_dst, TOP_K * n + lane], axis=0)
    slot_gate = jnp.where(real, jnp.take(gates.reshape(a), rows, mode="clip"), 0.0).reshape(nb, 1, blk)
    nb_real = (jnp.sum(padded) // blk).reshape(1).astype(jnp.int32)
    return block_expert, nb_real, slot_src.reshape(nb, 1, blk), slot_dst.reshape(nb + 1, 1, blk), slot_gate


def _combine_kernel(x_ref, h_ref, wsg_ref, wsu_ref, wsd_ref, *rest):
    y_refs, o_ref = rest[:TOP_K], rest[TOP_K]
    h = _from_token_major(h_ref[...])
    a = (jax.nn.silu(_dot(h, wsg_ref[...])) * _dot(h, wsu_ref[...])).astype(_BF16)
    routed = y_refs[0][...].astype(_F32)
    for k in range(1, TOP_K):
        routed = routed + y_refs[k][...].astype(_F32)
    o_ref[...] = x_ref[...] + _dot(a, wsd_ref[...]) + _from_token_major(routed)


def _combine(x1, h3, y3, ws_gate, ws_up, ws_down):
    n, d = x1.shape
    c = d // LANES
    tm = COMBINE_TM
    nt = n // tm
    de = ws_gate.shape[1]
    const = lambda r, cc: pl.BlockSpec((r, cc), lambda i: (0, 0))
    row = pl.BlockSpec((tm, d), lambda i: (i, 0))
    y_specs = [pl.BlockSpec((tm, c, LANES), functools.partial(lambda i, k: (k * nt + i, 0, 0), k=k))
               for k in range(TOP_K)]
    return pl.pallas_call(
        _combine_kernel,
        grid=(nt,),
        in_specs=[row, pl.BlockSpec((tm, c, LANES), lambda i: (i, 0, 0)), const(d, de), const(d, de), const(de, d)]
        + y_specs,
        out_specs=row,
        out_shape=jax.ShapeDtypeStruct((n, d), _F32),
        compiler_params=_cparams(("arbitrary",)),
        name="moe_combine",
    )(x1, h3, ws_gate.astype(_BF16), ws_up.astype(_BF16), ws_down.astype(_BF16), *([y3] * TOP_K))


def _moe_block(x1, norm_ffn_g, w_router, router_bias, we_gate, we_up, we_down, ws_gate, ws_up, ws_down):
    n = x1.shape[0]
    h3, top_idx_t, gates_t, counts = _router(x1, norm_ffn_g, w_router, router_bias)
    block_expert, nb_real, slot_src, slot_dst, slot_gate = _dispatch_plan(
        top_idx_t.T, gates_t.T, counts.reshape(N_EXPERTS), n)
    y3 = _moe_experts(h3, block_expert, nb_real, slot_src, slot_dst, slot_gate, we_gate, we_up, we_down)
    return _combine(x1, h3, y3, ws_gate, ws_up, ws_down)


def kernel(x, mem, norm_attn_g, w_in, nsa_pe_k, nsa_w_ck, nsa_pe_v, nsa_w_cv, nsa_q_g, nsa_k_g, rel_bias, fox_b_f, fox_q_g, fox_k_g, norm_mem_g, w_mem_kv, mem_q_g, mem_k_g, w_o_nsa, w_o_fox, w_o_mem, w_out, norm_ffn_g, w_router, router_bias, we_gate, we_up, we_down, ws_gate, ws_up, ws_down):
    b, t, d = x.shape
    assert norm_attn_g.shape[0] == 1, "single-layer problem"
    l = 0
    proj2d, o_nsa, o_fox, o_mem = _attention_mixers(
        x, mem, norm_attn_g[l], w_in[l], nsa_pe_k[l], nsa_w_ck[l], nsa_pe_v[l], nsa_w_cv[l], nsa_q_g[l],
        nsa_k_g[l], rel_bias, fox_b_f[l], fox_q_g[l], fox_k_g[l], norm_mem_g[l], w_mem_kv[l], mem_q_g[l],
        mem_k_g[l])
    x1 = _merge(x.reshape(b * t, d), o_nsa, o_fox, o_mem, proj2d, w_o_nsa[l], w_o_fox[l], w_o_mem[l], w_out[l])
    out = _moe_block(x1, norm_ffn_g[l], w_router[l], router_bias[l], we_gate[l], we_up[l], we_down[l],
                     ws_gate[l], ws_up[l], ws_down[l])
    return out.reshape(b, t, d)
```

```python
import functools
import math

import jax
import jax.numpy as jnp
import numpy as np
from jax import lax
from jax.experimental import pallas as pl
from jax.experimental.pallas import tpu as pltpu

D_MODEL = 2048
HEAD_DIM = 128
NSA_HEADS = 8
NSA_KV_HEADS = 2
NSA_GROUP = NSA_HEADS // NSA_KV_HEADS
FOX_HEADS = 4
MEM_HEADS = 4
CMP_LEN = 32
CMP_STRIDE = 16
SLC_LEN = 64
N_SELECT = 16
WINDOW = 512
NUM_BUCKETS = 32
MAX_DISTANCE = 128
N_BRANCHES = 3
N_EXPERTS = 64
TOP_K = 8
D_EXPERT = 512
ROUTED_SCALE = 2.5
ATTN_SCALE = HEAD_DIM ** -0.5
NEG_INF = -1e30
FORCE_SCORE = 1e4
RMS_EPS = 1e-6
LOG2E = math.log2(math.e)

NSA_Q_W = NSA_HEADS * HEAD_DIM
NSA_KV_W = NSA_KV_HEADS * HEAD_DIM
NSA_GATE_W = 3 * NSA_HEADS
FOX_W = FOX_HEADS * HEAD_DIM
MEM_W = MEM_HEADS * HEAD_DIM
MERGE_W = N_BRANCHES * D_MODEL

LANES = 128
VMEM_LIMIT_BYTES = 56 * 1024 * 1024

PROJ_TM = 1024
PROJ_TN = 1536
ATT_TQ = 128
FAR_GROUP = 4
FOX_TQ = 256
MEM_TQ = 512
MERGE_TM = 256
ROUTER_TM = 512
MOE_BLOCK = 256
GATHER_PRIORITY = 1
SCATTER_PRIORITY = 0
COMBINE_TM = 256

CB_MERGE = 0
CB_ATT = N_BRANCHES * D_MODEL // LANES
CB_QNSA = CB_ATT
CB_KCMP = CB_ATT + 8
CB_VCMP = CB_ATT + 10
CB_KSLC = CB_ATT + 12
CB_VSLC = CB_ATT + 14
CB_KWIN = CB_ATT + 16
CB_VWIN = CB_ATT + 18
CB_FOXQ = CB_ATT + 20
CB_FOXK = CB_ATT + 24
CB_FOXV = CB_ATT + 28
CB_MEMQ = CB_ATT + 32
PROJ_COLS = (CB_ATT + 36) * LANES
SMALL_W = 2 * LANES
FOXF_LANE = 12

_BF16 = jnp.bfloat16
_F32 = jnp.float32


def _cparams(sem):
    return pltpu.CompilerParams(dimension_semantics=sem, vmem_limit_bytes=VMEM_LIMIT_BYTES)


def _dot(a, b):
    return jnp.dot(a, b, preferred_element_type=_F32)


def _dot_nt(a, b):
    return lax.dot_general(a, b, (((1,), (1,)), ((), ())), preferred_element_type=_F32)


def _dot_tn(a, b):
    return lax.dot_general(a, b, (((0,), (0,)), ((), ())), preferred_element_type=_F32)


def _lanes(parts):
    return jnp.concatenate(parts, axis=1)


def _to_token_major(x):
    c = x.shape[1] // LANES
    chunks = jnp.stack([x[:, j * LANES:(j + 1) * LANES] for j in range(c)], axis=0)
    return pltpu.einshape("ctl->tcl", chunks)


def _from_token_major(x3):
    xt = pltpu.einshape("tcl->ctl", x3)
    return _lanes([xt[j] for j in range(x3.shape[1])])


REORDER_TR = 128
REORDER_TC = 512


def _reorder_kernel(pieces, narrow, w_ref, o_ref, s_ref):
    off = 0
    for src, width in pieces:
        for c in range(0, width, REORDER_TC):
            o_ref[:, off + c:off + c + REORDER_TC] = w_ref[:, src + c:src + c + REORDER_TC].astype(o_ref.dtype)
        off += width
    s_ref[...] = jnp.zeros(s_ref.shape, s_ref.dtype)
    for src, width, dst in narrow:
        s_ref[:, dst:dst + width] = w_ref[:, src:src + width].astype(s_ref.dtype)


def _reorder_cast(w, pieces, narrow, narrow_cols):
    rows, cols = w.shape
    total = sum(width for _, width in pieces)
    assert rows % REORDER_TR == 0 and all(width % REORDER_TC == 0 for _, width in pieces)
    return pl.pallas_call(
        functools.partial(_reorder_kernel, pieces, narrow),
        grid=(rows // REORDER_TR,),
        in_specs=[pl.BlockSpec((REORDER_TR, cols), lambda i: (i, 0))],
        out_specs=[pl.BlockSpec((REORDER_TR, total), lambda i: (i, 0)),
                   pl.BlockSpec((REORDER_TR, narrow_cols), lambda i: (i, 0))],
        out_shape=[jax.ShapeDtypeStruct((rows, total), _BF16), jax.ShapeDtypeStruct((rows, narrow_cols), _BF16)],
        compiler_params=_cparams(("arbitrary",)),
        name="reorder_cast",
    )(w)


def _proj_kernel(norm_j0, has_small, x_ref, g_ref, w_ref, cg_ref, cf_ref, *rest):
    if has_small:
        ws_ref, o_ref, os_ref, h_ref = rest
    else:
        o_ref, h_ref = rest
    j = pl.program_id(1)

    @pl.when(j == 0)
    def _():
        x = x_ref[...]
        ms = jnp.mean(x * x, axis=-1, keepdims=True)
        h = (x * lax.rsqrt(ms + RMS_EPS) * g_ref[...]).astype(_BF16)
        h_ref[...] = h
        if has_small:
            os_ref[...] = _dot(h, ws_ref[...])

    y = _dot(h_ref[...], w_ref[...])
    tn = y.shape[1]

    @pl.when(j >= norm_j0)
    def _():
        for c in range(tn // LANES):
            sl = slice(c * LANES, (c + 1) * LANES)
            yh = y[:, sl]
            ms = jnp.mean(yh * yh, axis=-1, keepdims=True)
            scale = jnp.where(cf_ref[:, sl] > 0.0, lax.rsqrt(ms + RMS_EPS), 1.0)
            o_ref[:, sl] = (yh * scale * cg_ref[:, sl]).astype(o_ref.dtype)

    @pl.when(j < norm_j0)
    def _():
        o_ref[...] = y.astype(o_ref.dtype)


def _rms_project(x2d, g, w, col_gain, col_flag, n_plain_cols, w_small, tm, tn):
    n, d = x2d.shape
    c = w.shape[1]
    has_small = w_small is not None
    in_specs = [
        pl.BlockSpec((tm, d), lambda i, j: (i, 0)),
        pl.BlockSpec((1, d), lambda i, j: (0, 0)),
        pl.BlockSpec((d, tn), lambda i, j: (0, j)),
        pl.BlockSpec((1, tn), lambda i, j: (0, j)),
        pl.BlockSpec((1, tn), lambda i, j: (0, j)),
    ]
    args = [x2d, g.reshape(1, d), w, col_gain.reshape(1, c), col_flag.reshape(1, c)]
    out_shape = [jax.ShapeDtypeStruct((n, c), _BF16)]
    out_specs = [pl.BlockSpec((tm, tn), lambda i, j: (i, j))]
    if has_small:
        ws = w_small.shape[1]
        in_specs.append(pl.BlockSpec((d, ws), lambda i, j: (0, 0)))
        args.append(w_small)
        out_shape.append(jax.ShapeDtypeStruct((n, ws), _F32))
        out_specs.append(pl.BlockSpec((tm, ws), lambda i, j: (i, 0)))
    assert n % tm == 0 and c % tn == 0 and n_plain_cols % tn == 0
    res = pl.pallas_call(
        functools.partial(_proj_kernel, n_plain_cols // tn, has_small),
        grid=(n // tm, c // tn),
        in_specs=in_specs,
        out_specs=out_specs,
        out_shape=out_shape,
        scratch_shapes=[pltpu.VMEM((tm, d), _BF16)],
        compiler_params=_cparams(("arbitrary", "arbitrary")),
        name="rms_project",
    )(*args)
    return res if has_small else res[0]


def _cmp_kernel(x_ref, wlo_ref, whi_ref, pelo_ref, pehi_ref, kg_ref, o_ref):
    j = pl.program_id(1)
    x = x_ref[0, 0]
    nchunk = x.shape[0]
    ylo = _dot(x, wlo_ref[0])
    yhi = _dot(x, whi_ref[0])
    pe = _dot(pelo_ref[0], wlo_ref[0]) + _dot(pehi_ref[0], whi_ref[0])
    y = ylo + pltpu.roll(yhi, nchunk - 1, 0) + pe[0:1, :]

    @pl.when(j < NSA_KV_HEADS)
    def _():
        ms = jnp.mean(y * y, axis=-1, keepdims=True)
        o_ref[0, 0] = (y * lax.rsqrt(ms + RMS_EPS) * kg_ref[...]).astype(o_ref.dtype)

    @pl.when(j >= NSA_KV_HEADS)
    def _():
        o_ref[0, 0] = y.T.astype(o_ref.dtype)


def _nsa_compress(cmp_in, w_ck, w_cv, pe_k, pe_v, k_g):
    b, nj, t, dk = cmp_in.shape
    nchunk = t // CMP_STRIDE
    half = CMP_LEN // 2
    assert nchunk == dk
    x = cmp_in.reshape(b, nj, nchunk, CMP_STRIDE * dk)

    def halves(w):
        return (w[:half].reshape(half * dk, dk).astype(_BF16),
                w[half:].reshape(half * dk, dk).astype(_BF16))

    klo, khi = halves(w_ck)
    vlo, vhi = halves(w_cv)
    wlo = jnp.stack([klo, vlo])
    whi = jnp.stack([khi, vhi])

    def pe_halves(pe):
        lo = jnp.broadcast_to(pe[:half].reshape(1, half * dk), (8, half * dk)).astype(_BF16)
        hi = jnp.broadcast_to(pe[half:].reshape(1, half * dk), (8, half * dk)).astype(_BF16)
        return lo, hi

    pklo, pkhi = pe_halves(pe_k)
    pvlo, pvhi = pe_halves(pe_v)
    pelo = jnp.stack([pklo, pvlo])
    pehi = jnp.stack([pkhi, pvhi])
    kv = lambda bb, j: (j // NSA_KV_HEADS, 0, 0)
    return pl.pallas_call(
        _cmp_kernel,
        grid=(b, nj),
        in_specs=[
            pl.BlockSpec((1, 1, nchunk, CMP_STRIDE * dk), lambda bb, j: (bb, j, 0, 0)),
            pl.BlockSpec((1, half * dk, dk), kv),
            pl.BlockSpec((1, half * dk, dk), kv),
            pl.BlockSpec((1, 8, half * dk), kv),
            pl.BlockSpec((1, 8, half * dk), kv),
            pl.BlockSpec((1, dk), lambda bb, j: (0, 0)),
        ],
        out_specs=pl.BlockSpec((1, 1, nchunk, dk), lambda bb, j: (bb, j, 0, 0)),
        out_shape=jax.ShapeDtypeStruct((b, nj, nchunk, dk), _BF16),
        compiler_params=_cparams(("arbitrary", "arbitrary")),
        name="nsa_compress",
    )(x, wlo, whi, pelo, pehi, k_g.reshape(1, dk).astype(_F32))


def _t_update(state, tiles):
    m, l, acc = state
    masked = [s if mask is None else jnp.where(mask, s, NEG_INF) for s, mask, _ in tiles]
    m_new = m
    for sm in masked:
        m_new = jnp.maximum(m_new, jnp.max(sm, axis=0, keepdims=True))
    alpha = jnp.exp2(m - m_new)
    l_new = alpha * l
    acc_new = alpha * acc
    for sm, (_, _, v) in zip(masked, tiles):
        e = jnp.exp2(sm - m_new)
        l_new = l_new + jnp.sum(e, axis=0, keepdims=True)
        acc_new = acc_new + _dot_tn(v, e.astype(_BF16))
    return m_new, l_new, acc_new


def _t_init(nq):
    return (jnp.full((1, nq), NEG_INF, _F32), jnp.zeros((1, nq), _F32), jnp.zeros((HEAD_DIM, nq), _F32))


def _t_finish(state):
    _, l, acc = state
    return acc / l


def _nsa_kernel(q_ref, kc_ref, vct_ref, ks_ref, vs_ref, kw_ref, vw_ref, gl_ref,
                wt_ref, dt_ref, ovt_ref, ext_ref, o_ref, sel_ref):
    i = pl.program_id(2)
    tq = ATT_TQ
    hg = NSA_GROUP
    nq = hg * tq
    n_slc = ovt_ref.shape[0]

    q = q_ref[0]
    qt = _lanes([q[:, h * HEAD_DIM:(h + 1) * HEAD_DIM].astype(_F32).T for h in range(hg)]).astype(_BF16)

    key = lax.broadcasted_iota(jnp.int32, (LANES, nq), 0)
    qry = lax.broadcasted_iota(jnp.int32, (LANES, nq), 1) & (tq - 1)

    def ktile(ref, kt):
        return ref[0, pl.ds(pl.multiple_of(kt * LANES, LANES), LANES), :]

    def near_bias(d):
        return _lanes([dt_ref[h, d] for h in range(hg)])

    woff = pl.multiple_of(wt_ref.shape[1] - LANES - 8 - (LANES // CMP_STRIDE) * i, 8)
    s = _dot(kc_ref[0, 0], qt) + _lanes([wt_ref[h, pl.ds(woff, LANES), :] for h in range(hg)])
    n_win = WINDOW // tq
    win_tiles = []
    for d in range(n_win, -1, -1):
        ktd = jnp.maximum(i - d, 0)
        sc = _dot(ktile(kw_ref, ktd), qt)
        if d <= 1:
            sc = sc + near_bias(d)
        if d == n_win:
            mk = (qry < key) & (i >= d)
        elif d == 0:
            mk = key <= qry
        else:
            mk = jnp.broadcast_to(i >= d, (LANES, nq))
        win_tiles.append((sc, mk, ktile(vw_ref, ktd)))

    mask_c = (i * tq + qry) >= (CMP_STRIDE * key + CMP_LEN - 1)
    mx = jnp.max(jnp.where(mask_c, s, NEG_INF), axis=0, keepdims=True)
    e = jnp.where(mask_c, jnp.exp2(s - mx), 0.0)
    l = jnp.sum(e, axis=0, keepdims=True)
    p_c = e / jnp.where(l > 0.0, l, 1.0)
    o_c = _dot(vct_ref[0, 0], p_c.astype(_BF16))

    ps = p_c[:, 0:tq]
    for h in range(1, hg):
        ps = ps + p_c[:, h * tq:(h + 1) * tq]
    ps_hi = ps.astype(_BF16)
    ps_lo = (ps - ps_hi.astype(_F32)).astype(_BF16)
    imp = _dot(ovt_ref[...], ps_hi) + _dot(ovt_ref[...], ps_lo)
    blk = lax.broadcasted_iota(jnp.int32, (n_slc, tq), 0)
    tpos = i * tq + lax.broadcasted_iota(jnp.int32, (n_slc, tq), 1)
    cur = tpos // SLC_LEN
    valid = blk * SLC_LEN <= tpos
    forced = (blk == 0) | (blk == cur) | (blk == cur - 1)
    score = jnp.where(valid, jnp.where(forced, FORCE_SCORE, imp), -1.0)
    rank = jnp.zeros((n_slc, tq), _F32)
    for k in range(n_slc):
        ck = score[k:k + 1, :]
        beats = (ck > score) | ((ck == score) & (blk > k))
        rank = rank + jnp.where(beats, 1.0, 0.0)
    sel = jnp.where(rank < float(min(N_SELECT, n_slc)), 1.0, 0.0).astype(_BF16)
    selfull = _dot(ext_ref[...], sel)
    for kt in range(ext_ref.shape[0] // LANES):
        sel_ref[kt] = selfull[kt * LANES:(kt + 1) * LANES, :]

    def sel_mask(kt, ok=True):
        return _lanes([sel_ref[kt]] * hg) > jnp.where(ok, 0.5, 2.0)

    o_w = _t_finish(_t_update(_t_init(nq), win_tiles))

    n_far = jnp.maximum(i - 1, 0)

    def far_body(j, st):
        k0 = pl.multiple_of(j * (FAR_GROUP * LANES), FAR_GROUP * LANES)
        sg = _dot(ks_ref[0, pl.ds(k0, FAR_GROUP * LANES), :], qt)
        tiles = []
        for u in range(FAR_GROUP):
            kt = FAR_GROUP * j + u
            tiles.append((sg[u * LANES:(u + 1) * LANES], sel_mask(kt, kt < n_far), ktile(vs_ref, kt)))
        return _t_update(st, tiles)

    st = lax.fori_loop(0, (n_far + FAR_GROUP - 1) // FAR_GROUP, far_body, _t_init(nq))
    kt1 = jnp.maximum(i - 1, 0)
    st = _t_update(st, [
        (_dot(ktile(ks_ref, kt1), qt) + near_bias(1), sel_mask(kt1, i >= 1), ktile(vs_ref, kt1)),
        (_dot(ktile(ks_ref, i), qt) + near_bias(0), sel_mask(i) & (key <= qry), ktile(vs_ref, i))])
    o_s = _t_finish(st)

    glt = jax.nn.sigmoid(gl_ref[0]).T
    grow = lambda br: _lanes([glt[br * hg + h:br * hg + h + 1, :] for h in range(hg)])
    o = grow(0) * o_c + grow(1) * o_s + grow(2) * o_w
    for h in range(hg):
        o_ref[0, :, h * HEAD_DIM:(h + 1) * HEAD_DIM] = o[:, h * tq:(h + 1) * tq].T.astype(o_ref.dtype)


def _t5_bucket(rel):
    n = np.maximum(rel, 0)
    max_exact = NUM_BUCKETS // 2
    ratio = np.maximum(n, 1).astype(np.float32) / np.float32(max_exact)
    log_ratio = np.log(ratio) / np.float32(math.log(MAX_DISTANCE / max_exact))
    large = np.minimum(max_exact + (log_ratio * np.float32(NUM_BUCKETS - max_exact)).astype(np.int32),
                       NUM_BUCKETS - 1)
    return np.where(n < max_exact, n, large).astype(np.int32)


def _bias_lookup(rb, bucket):
    bk = jnp.asarray(bucket.astype(np.int8))[None]
    ex = (slice(None),) + (None,) * bucket.ndim
    tab = jnp.broadcast_to(rb[:, 0][ex], (rb.shape[0],) + bucket.shape)
    for k in range(1, NUM_BUCKETS):
        tab = jnp.where(bk == k, rb[:, k][ex], tab)
    return tab


def _nsa_tables(rel_bias):
    rb = rel_bias.astype(_F32).T * LOG2E
    r = np.arange(ATT_TQ)
    rel_d = (np.arange(2) * ATT_TQ)[:, None, None] + r[None, None, :] - r[None, :, None]
    assert _t5_bucket(np.array([ATT_TQ + 1]))[0] == NUM_BUCKETS - 1
    u = np.arange(2 * LANES)
    rel_w = r[None, :] - CMP_STRIDE * (u[:, None] - (LANES - 8)) - (CMP_LEN - 1)
    far = rb[:, NUM_BUCKETS - 1]
    dt = _bias_lookup(rb, _t5_bucket(rel_d)) - far[:, None, None, None]
    return dt, _bias_lookup(rb, _t5_bucket(rel_w))


def _nsa_attention(proj, small, kvc, rel_bias):
    b, t, _ = proj.shape
    g, hg, tq = NSA_KV_HEADS, NSA_GROUP, ATT_TQ
    n_slc = t // SLC_LEN
    n_cmp = (t - CMP_LEN) // CMP_STRIDE + 1
    nt = t // LANES
    assert n_cmp <= LANES and kvc.shape[2] == LANES and n_slc % 16 == 0 and tq == LANES
    assert nt % FAR_GROUP == 0
    assert LANES - 8 - (LANES // CMP_STRIDE) * (t // tq - 1) >= 0
    dt, wt = _nsa_tables(rel_bias)
    cstart = np.arange(LANES) * CMP_STRIDE
    sstart = np.arange(n_slc) * SLC_LEN
    overlap = np.clip(np.minimum(cstart[None, :] + CMP_LEN, sstart[:, None] + SLC_LEN)
                      - np.maximum(cstart[None, :], sstart[:, None]), 0, None).astype(np.float32) / CMP_LEN
    overlap[:, n_cmp:] = 0.0
    ovt = jnp.asarray(overlap, _BF16)
    ext = jnp.asarray(np.arange(t)[:, None] // SLC_LEN == np.arange(n_slc)[None, :], _BF16)

    seq = lambda cb: pl.BlockSpec((1, t, LANES), lambda bb, gg, ii: (bb, 0, cb + gg))
    return pl.pallas_call(
        _nsa_kernel,
        grid=(b, g, t // tq),
        in_specs=[
            pl.BlockSpec((1, tq, hg * HEAD_DIM), lambda bb, gg, ii: (bb, ii, CB_QNSA * LANES // (hg * HEAD_DIM) + gg)),
            pl.BlockSpec((1, 1, LANES, HEAD_DIM), lambda bb, gg, ii: (bb, gg, 0, 0)),
            pl.BlockSpec((1, 1, HEAD_DIM, LANES), lambda bb, gg, ii: (bb, NSA_KV_HEADS + gg, 0, 0)),
            seq(CB_KSLC), seq(CB_VSLC), seq(CB_KWIN), seq(CB_VWIN),
            pl.BlockSpec((1, tq, LANES), lambda bb, gg, ii: (bb, ii, gg)),
            pl.BlockSpec((hg, 2 * LANES, tq), lambda bb, gg, ii: (gg, 0, 0)),
            pl.BlockSpec((hg, 2, LANES, tq), lambda bb, gg, ii: (gg, 0, 0, 0)),
            pl.BlockSpec((n_slc, LANES), lambda bb, gg, ii: (0, 0)),
            pl.BlockSpec((t, n_slc), lambda bb, gg, ii: (0, 0)),
        ],
        out_specs=pl.BlockSpec((1, tq, hg * HEAD_DIM), lambda bb, gg, ii: (bb, ii, gg)),
        out_shape=jax.ShapeDtypeStruct((b, t, NSA_Q_W), _BF16),
        scratch_shapes=[pltpu.VMEM((nt, LANES, tq), _F32)],
        compiler_params=_cparams(("arbitrary", "arbitrary", "arbitrary")),
        name="nsa_attention",
    )(proj, kvc, kvc, proj, proj, proj, proj, small, wt, dt, ovt, ext)


def _fox_cum_kernel(s_ref, b_ref, col_ref, row_ref):
    z = s_ref[0] + b_ref[...]
    lf = (jnp.minimum(z, 0.0) - jnp.log1p(jnp.exp(-jnp.abs(z)))) * LOG2E
    x = lf.T
    t = x.shape[1]
    lane = lax.broadcasted_iota(jnp.int32, x.shape, 1)
    sh = 1
    while sh < t:
        x = x + jnp.where(lane >= sh, pltpu.roll(x, sh, 1), 0.0)
        sh *= 2
    row_ref[0] = x[8:16, :]
    col_ref[0] = x.T


def _fox_cumsum(small, b_f):
    b, t, _ = small.shape
    bvec = jnp.zeros((1, LANES), _F32).at[0, FOXF_LANE:FOXF_LANE + FOX_HEADS].set(b_f.astype(_F32))
    return pl.pallas_call(
        _fox_cum_kernel,
        grid=(b,),
        in_specs=[pl.BlockSpec((1, t, LANES), lambda bb: (bb, 0, 1)),
                  pl.BlockSpec((1, LANES), lambda bb: (0, 0))],
        out_specs=[pl.BlockSpec((1, t, LANES), lambda bb: (bb, 0, 0)),
                   pl.BlockSpec((1, 8, t), lambda bb: (bb, 0, 0))],
        out_shape=[jax.ShapeDtypeStruct((b, t, LANES), _F32), jax.ShapeDtypeStruct((b, 8, t), _F32)],
        compiler_params=_cparams(("arbitrary",)),
        name="fox_cumsum",
    )(small, bvec)


def _fox_kernel(q_ref, k_ref, v_ref, cc_ref, cr_ref, o_ref):
    i = pl.program_id(1)
    tq = FOX_TQ
    key = lax.broadcasted_iota(jnp.int32, (tq, tq), 0)
    qry = lax.broadcasted_iota(jnp.int32, (tq, tq), 1)
    heads = range(FOX_HEADS)
    hsl = [slice(h * HEAD_DIM, (h + 1) * HEAD_DIM) for h in heads]
    qts = [q_ref[0, :, hsl[h]].astype(_F32).T.astype(_BF16) for h in heads]
    cqs = [cr_ref[0, FOXF_LANE - 8 + h:FOXF_LANE - 8 + h + 1, :] for h in heads]

    def keys(kt):
        return pl.ds(pl.multiple_of(kt * tq, tq), tq)

    def logits(h, kt):
        ck = cc_ref[0, keys(kt), FOXF_LANE + h:FOXF_LANE + h + 1]
        return _dot(k_ref[0, keys(kt), hsl[h]], qts[h]) + (cqs[h] - ck)

    def body(kt, sts):
        ss = [logits(h, kt) for h in heads]
        return tuple(_t_update(sts[h], [(ss[h], None, v_ref[0, keys(kt), hsl[h]])]) for h in heads)

    sts = lax.fori_loop(0, i, body, tuple(_t_init(tq) for _ in heads))
    ss = [logits(h, i) for h in heads]
    for h in heads:
        st = _t_update(sts[h], [(ss[h], key <= qry, v_ref[0, keys(i), hsl[h]])])
        o_ref[0, :, hsl[h]] = _t_finish(st).T.astype(o_ref.dtype)


def _fox_attention(proj, cum_col, cum_row):
    b, t, _ = proj.shape
    tq = FOX_TQ
    w = FOX_W
    return pl.pallas_call(
        _fox_kernel,
        grid=(b, t // tq),
        in_specs=[
            pl.BlockSpec((1, tq, w), lambda bb, ii: (bb, ii, CB_FOXQ * LANES // w)),
            pl.BlockSpec((1, t, w), lambda bb, ii: (bb, 0, CB_FOXK * LANES // w)),
            pl.BlockSpec((1, t, w), lambda bb, ii: (bb, 0, CB_FOXV * LANES // w)),
            pl.BlockSpec((1, t, LANES), lambda bb, ii: (bb, 0, 0)),
            pl.BlockSpec((1, 8, tq), lambda bb, ii: (bb, 0, ii)),
        ],
        out_specs=pl.BlockSpec((1, tq, w), lambda bb, ii: (bb, ii, 0)),
        out_shape=jax.ShapeDtypeStruct((b, t, w), _BF16),
        compiler_params=_cparams(("arbitrary", "arbitrary")),
        name="fox_attention",
    )(proj, proj, proj, cum_col, cum_row)


def _mem_kernel(q_ref, kv_ref, o_ref):
    for h in range(MEM_HEADS):
        hs = slice(h * HEAD_DIM, (h + 1) * HEAD_DIM)
        vs = slice(MEM_W + h * HEAD_DIM, MEM_W + (h + 1) * HEAD_DIM)
        s = _dot_nt(q_ref[0, :, hs], kv_ref[0, :, hs])
        e = jnp.exp2(s - jnp.max(s, axis=1, keepdims=True))
        p = e / jnp.sum(e, axis=1, keepdims=True)
        o_ref[0, :, hs] = _dot(p.astype(_BF16), kv_ref[0, :, vs]).astype(o_ref.dtype)


def _mem_attention(proj, memkv):
    b, t, _ = proj.shape
    m = memkv.shape[1]
    tq = MEM_TQ
    return pl.pallas_call(
        _mem_kernel,
        grid=(b, t // tq),
        in_specs=[pl.BlockSpec((1, tq, MEM_W), lambda bb, ii: (bb, ii, CB_MEMQ * LANES // MEM_W)),
                  pl.BlockSpec((1, m, 2 * MEM_W), lambda bb, ii: (bb, 0, 0))],
        out_specs=pl.BlockSpec((1, tq, MEM_W), lambda bb, ii: (bb, ii, 0)),
        out_shape=jax.ShapeDtypeStruct((b, t, MEM_W), _BF16),
        compiler_params=_cparams(("arbitrary", "arbitrary")),
        name="mem_attention",
    )(proj, memkv)


def _merge_kernel(x_ref, on_ref, of_ref, om_ref, g0_ref, g1_ref, g2_ref, wn_ref, wf_ref, wm_ref, wo_ref,
                  o_ref, z_ref):
    tn = 512
    for c in range(D_MODEL // tn):
        cs = slice(c * tn, (c + 1) * tn)
        z = (jax.nn.sigmoid(g0_ref[:, cs].astype(_F32)) * _dot(on_ref[...], wn_ref[:, cs])
             + jax.nn.sigmoid(g1_ref[:, cs].astype(_F32)) * _dot(of_ref[...], wf_ref[:, cs])
             + jax.nn.sigmoid(g2_ref[:, cs].astype(_F32)) * _dot(om_ref[...], wm_ref[:, cs]))
        z_ref[:, cs] = z.astype(_BF16)
    o_ref[...] = x_ref[...] + _dot(z_ref[...], wo_ref[...])


def _merge(x2d, o_nsa, o_fox, o_mem, proj2d, w_o_nsa, w_o_fox, w_o_mem, w_out):
    n, d = x2d.shape
    tm = MERGE_TM
    gcb = CB_MERGE * LANES // d
    const = lambda r, c: pl.BlockSpec((r, c), lambda i: (0, 0))
    return pl.pallas_call(
        _merge_kernel,
        grid=(n // tm,),
        in_specs=[
            pl.BlockSpec((tm, d), lambda i: (i, 0)),
            pl.BlockSpec((tm, NSA_Q_W), lambda i: (i, 0)),
            pl.BlockSpec((tm, FOX_W), lambda i: (i, 0)),
            pl.BlockSpec((tm, MEM_W), lambda i: (i, 0)),
            pl.BlockSpec((tm, d), lambda i: (i, gcb)),
            pl.BlockSpec((tm, d), lambda i: (i, gcb + 1)),
            pl.BlockSpec((tm, d), lambda i: (i, gcb + 2)),
            const(NSA_Q_W, d), const(FOX_W, d), const(MEM_W, d), const(d, d),
        ],
        out_specs=pl.BlockSpec((tm, d), lambda i: (i, 0)),
        out_shape=jax.ShapeDtypeStruct((n, d), _F32),
        scratch_shapes=[pltpu.VMEM((tm, d), _BF16)],
        compiler_params=_cparams(("arbitrary",)),
        name="merge_out",
    )(x2d, o_nsa, o_fox, o_mem, proj2d, proj2d, proj2d,
      w_o_nsa.astype(_BF16), w_o_fox.astype(_BF16), w_o_mem.astype(_BF16), w_out.astype(_BF16))


def _nsa_inputs(x, norm_attn_g, w_in, nsa_pe_k, nsa_w_ck, nsa_pe_v, nsa_w_cv, nsa_q_g, nsa_k_g,
                fox_q_g, fox_k_g, mem_q_g):
    b, t, d = x.shape
    ones = lambda k: jnp.ones((k * HEAD_DIM,), _F32)
    zeros = lambda k: jnp.zeros((k * HEAD_DIM,), _F32)
    tile = lambda gv, k: jnp.tile(gv.astype(_F32), k)
    qs = ATTN_SCALE * LOG2E

    g_end = NSA_Q_W + 6 * NSA_KV_W
    f_off = g_end + NSA_GATE_W + 3 * FOX_W
    m_off = f_off + FOX_HEADS + MEM_W
    narrow = tuple((g_end + (br * NSA_KV_HEADS + gg) * NSA_GROUP, NSA_GROUP, gg * LANES + br * NSA_GROUP)
                   for br in range(3) for gg in range(NSA_KV_HEADS))
    narrow += ((f_off, FOX_HEADS, LANES + FOXF_LANE),)
    w_all, w_small = _reorder_cast(
        w_in, ((m_off, MERGE_W), (0, g_end), (g_end + NSA_GATE_W, 3 * FOX_W), (f_off + FOX_HEADS, MEM_W)),
        narrow, SMALL_W)
    col_gain = jnp.concatenate([
        jnp.ones((MERGE_W,), _F32),
        tile(nsa_q_g, NSA_HEADS) * qs, ones(2), ones(2), tile(nsa_k_g, 2), ones(2), tile(nsa_k_g, 2), ones(2),
        tile(fox_q_g, FOX_HEADS) * qs, tile(fox_k_g, FOX_HEADS), ones(FOX_HEADS),
        tile(mem_q_g, MEM_HEADS) * qs])
    col_flag = jnp.concatenate([
        jnp.zeros((MERGE_W,), _F32),
        ones(NSA_HEADS), zeros(2), zeros(2), ones(2), zeros(2), ones(2), zeros(2),
        ones(FOX_HEADS), ones(FOX_HEADS), zeros(FOX_HEADS), ones(MEM_HEADS)])

    x2d = x.reshape(b * t, d)
    proj2d, small2d = _rms_project(x2d, norm_attn_g, w_all, col_gain, col_flag, CB_ATT * LANES, w_small,
                                   PROJ_TM, PROJ_TN)
    proj = proj2d.reshape(b, t, PROJ_COLS)
    small = small2d.reshape(b, t, SMALL_W)

    cmp_in = proj[:, :, CB_KCMP * LANES:(CB_KCMP + 4) * LANES].reshape(b, t, 4, HEAD_DIM).transpose(0, 2, 1, 3)
    kvc = _nsa_compress(cmp_in, nsa_w_ck, nsa_w_cv, nsa_pe_k, nsa_pe_v, nsa_k_g)
    return proj, small, kvc


def _attention_mixers(x, mem, norm_attn_g, w_in, nsa_pe_k, nsa_w_ck, nsa_pe_v, nsa_w_cv, nsa_q_g, nsa_k_g,
                      rel_bias, fox_b_f, fox_q_g, fox_k_g, norm_mem_g, w_mem_kv, mem_q_g, mem_k_g):
    b, t, d = x.shape
    ones = lambda k: jnp.ones((k * HEAD_DIM,), _F32)
    zeros = lambda k: jnp.zeros((k * HEAD_DIM,), _F32)
    tile = lambda gv, k: jnp.tile(gv.astype(_F32), k)
    proj, small, kvc = _nsa_inputs(x, norm_attn_g, w_in, nsa_pe_k, nsa_w_ck, nsa_pe_v, nsa_w_cv, nsa_q_g, nsa_k_g,
                                   fox_q_g, fox_k_g, mem_q_g)
    proj2d = proj.reshape(b * t, PROJ_COLS)
    o_nsa = _nsa_attention(proj, small, kvc, rel_bias)

    cum_col, cum_row = _fox_cumsum(small, fox_b_f)
    o_fox = _fox_attention(proj, cum_col, cum_row)

    mm = mem.shape[1]
    mem_gain = jnp.concatenate([tile(mem_k_g, MEM_HEADS), ones(MEM_HEADS)])
    mem_flag = jnp.concatenate([ones(MEM_HEADS), zeros(MEM_HEADS)])
    memkv = _rms_project(mem.reshape(b * mm, d), norm_mem_g, w_mem_kv.astype(_BF16), mem_gain, mem_flag,
                         0, None, mm, MEM_W).reshape(b, mm, 2 * MEM_W)
    o_mem = _mem_attention(proj, memkv)
    return proj2d, o_nsa.reshape(b * t, NSA_Q_W), o_fox.reshape(b * t, FOX_W), o_mem.reshape(b * t, MEM_W)


def _router_kernel(x_ref, g_ref, whi_ref, wlo_ref, b_ref, h_ref, idx_ref, gate_ref, cnt_ref):
    x = x_ref[...]
    ms = jnp.mean(x * x, axis=-1, keepdims=True)
    h = x * lax.rsqrt(ms + RMS_EPS) * g_ref[...]
    h_hi = h.astype(_BF16)
    h_ref[...] = _to_token_major(h_hi)
    h_lo = (h - h_hi.astype(_F32)).astype(_BF16)
    logits = _dot_nt(whi_ref[...], h_hi) + _dot_nt(whi_ref[...], h_lo) + _dot_nt(wlo_ref[...], h_hi)
    scores = jax.nn.sigmoid(logits)
    sb = scores + b_ref[...]
    eidx = lax.broadcasted_iota(jnp.int32, sb.shape, 0)
    idxs, vals = [], []
    for _ in range(TOP_K):
        m = jnp.max(sb, axis=0, keepdims=True)
        idx = jnp.min(jnp.where(sb == m, eidx, N_EXPERTS), axis=0, keepdims=True)
        hit = eidx == idx
        vals.append(jnp.sum(jnp.where(hit, scores, 0.0), axis=0, keepdims=True))
        idxs.append(idx)
        sb = jnp.where(hit, NEG_INF, sb)
    top_s = jnp.concatenate(vals, axis=0)
    idx_ref[...] = jnp.concatenate(idxs, axis=0)
    gate_ref[...] = top_s / jnp.sum(top_s, axis=0, keepdims=True) * ROUTED_SCALE

    @pl.when(pl.program_id(0) == 0)
    def _():
        cnt_ref[...] = jnp.zeros(cnt_ref.shape, cnt_ref.dtype)

    picked = jnp.where(sb < 0.5 * NEG_INF, 1.0, 0.0)
    cnt_ref[...] += jnp.sum(picked, axis=1, keepdims=True)


def _router(x1, norm_g, w_router, router_bias):
    n, d = x1.shape
    tm = ROUTER_TM
    wt = w_router.astype(_F32).T
    w_hi = wt.astype(_BF16)
    w_lo = (wt - w_hi.astype(_F32)).astype(_BF16)
    const = lambda r, c: pl.BlockSpec((r, c), lambda i: (0, 0))
    return pl.pallas_call(
        _router_kernel,
        grid=(n // tm,),
        in_specs=[pl.BlockSpec((tm, d), lambda i: (i, 0)), const(1, d), const(N_EXPERTS, d), const(N_EXPERTS, d),
                  const(N_EXPERTS, 1)],
        out_specs=[pl.BlockSpec((tm, d // LANES, LANES), lambda i: (i, 0, 0)),
                   pl.BlockSpec((TOP_K, tm), lambda i: (0, i)),
                   pl.BlockSpec((TOP_K, tm), lambda i: (0, i)),
                   const(N_EXPERTS, 1)],
        out_shape=[jax.ShapeDtypeStruct((n, d // LANES, LANES), _BF16), jax.ShapeDtypeStruct((TOP_K, n), jnp.int32),
                   jax.ShapeDtypeStruct((TOP_K, n), _F32), jax.ShapeDtypeStruct((N_EXPERTS, 1), _F32)],
        compiler_params=_cparams(("arbitrary",)),
        name="moe_router",
    )(x1, norm_g.reshape(1, d).astype(_F32), w_hi, w_lo, router_bias.reshape(N_EXPERTS, 1).astype(_F32))


def _moe_kernel(be_ref, nbr_ref, src_hbm, dst_hbm, h_hbm, sg_ref, wg_ref, wu_ref, wd_ref, y_hbm,
                xbuf, ybuf, xmat, sidx, didx, wg_bf, wu_bf, wd_bf, gsem, ssem, isem):
    n = pl.program_id(0)
    nbr = nbr_ref[0]
    last = nbr - 1
    slot = n % 2
    other = 1 - slot
    blk = MOE_BLOCK
    dump_block = dst_hbm.shape[0] - 1
    ring = didx.shape[0]

    def src_copy(block, s):
        return pltpu.make_async_copy(src_hbm.at[block], sidx.at[s], isem.at[0, s])

    def dst_copy(block, s3):
        return pltpu.make_async_copy(dst_hbm.at[block], didx.at[s3], isem.at[1, s3])

    def issue_gather(s):
        for r in range(blk):
            tok = sidx[s, 0, r]
            pltpu.make_async_copy(h_hbm.at[tok], xbuf.at[s, r], gsem.at[s]).start(priority=GATHER_PRIORITY)

    def issue_scatter(s, s3):
        for r in range(blk):
            row = didx[s3, 0, r]
            pltpu.make_async_copy(ybuf.at[s, r], y_hbm.at[row], ssem.at[s]).start(priority=SCATTER_PRIORITY)

    def wait_rows(buf, sem, s):
        pltpu.make_async_copy(buf.at[s], buf.at[s], sem.at[s]).wait()

    @pl.when(n < nbr)
    def _():
        @pl.when(n == 0)
        def _():
            src_copy(0, 0).start()
            dst_copy(0, 0).start()
            dst_copy(dump_block, ring - 1).start()
            ybuf[1] = jnp.zeros(ybuf.shape[1:], ybuf.dtype)
            src_copy(0, 0).wait()
            dst_copy(0, 0).wait()
            dst_copy(dump_block, ring - 1).wait()
            issue_gather(0)
            nxt0 = jnp.minimum(1, last)
            src_copy(nxt0, 1).start()
            dst_copy(nxt0, 1).start()

        wait_rows(xbuf, gsem, slot)

        @pl.when(n < last)
        def _():
            nn = jnp.minimum(n + 2, last)
            src_copy(nn, slot).start()
            dst_copy(nn, (n + 2) % ring).start()

        changed = (n == 0) | (be_ref[n] != be_ref[jnp.maximum(n - 1, 0)])

        @pl.when(changed)
        def _():
            wg_bf[...] = wg_ref[0].astype(_BF16)
            wu_bf[...] = wu_ref[0].astype(_BF16)
            wd_bf[...] = wd_ref[0].astype(_BF16)

        @pl.when(n >= 1)
        def _():
            wait_rows(ybuf, ssem, slot)

        nxt = jnp.minimum(n + 1, last)
        src_copy(nxt, other).wait()
        dst_copy(nxt, (n + 1) % ring).wait()

        issue_scatter(other, (n + ring - 1) % ring)
        xmat[...] = _from_token_major(xbuf[slot])
        issue_gather(other)
        gcol = jnp.broadcast_to(sg_ref[0], (LANES, blk)).T
        a = jax.nn.silu(_dot(xmat[...], wg_bf[...])) * _dot(xmat[...], wu_bf[...])
        a = (a * _lanes([gcol] * (a.shape[1] // LANES))).astype(_BF16)
        ybuf[slot] = _to_token_major(_dot(a, wd_bf[...]).astype(_BF16))

        @pl.when(n == last)
        def _():
            issue_scatter(slot, n % ring)
            wait_rows(xbuf, gsem, other)
            wait_rows(ybuf, ssem, other)
            wait_rows(ybuf, ssem, slot)


def _moe_experts(h3, block_expert, nb_real, slot_src, slot_dst, slot_gate, we_gate, we_up, we_down):
    n, c, _ = h3.shape
    d = c * LANES
    nb = block_expert.shape[0]
    blk = MOE_BLOCK
    e, _, de = we_gate.shape
    grid_spec = pltpu.PrefetchScalarGridSpec(
        num_scalar_prefetch=2,
        grid=(nb,),
        in_specs=[
            pl.BlockSpec(memory_space=pl.ANY),
            pl.BlockSpec(memory_space=pl.ANY),
            pl.BlockSpec(memory_space=pl.ANY),
            pl.BlockSpec((1, 1, blk), lambda i, be, nbr: (i, 0, 0)),
            pl.BlockSpec((1, d, de), lambda i, be, nbr: (be[i], 0, 0)),
            pl.BlockSpec((1, d, de), lambda i, be, nbr: (be[i], 0, 0)),
            pl.BlockSpec((1, de, d), lambda i, be, nbr: (be[i], 0, 0)),
        ],
        out_specs=pl.BlockSpec(memory_space=pl.ANY),
        scratch_shapes=[
            pltpu.VMEM((2, blk, c, LANES), _BF16), pltpu.VMEM((2, blk, c, LANES), _BF16),
            pltpu.VMEM((blk, d), _BF16),
            pltpu.SMEM((2, 1, blk), jnp.int32), pltpu.SMEM((4, 1, blk), jnp.int32),
            pltpu.VMEM((d, de), _BF16), pltpu.VMEM((d, de), _BF16), pltpu.VMEM((de, d), _BF16),
            pltpu.SemaphoreType.DMA((2,)), pltpu.SemaphoreType.DMA((2,)), pltpu.SemaphoreType.DMA((2, 4)),
        ],
    )
    return pl.pallas_call(
        _moe_kernel,
        grid_spec=grid_spec,
        out_shape=jax.ShapeDtypeStruct((TOP_K * n + blk, c, LANES), _BF16),
        compiler_params=_cparams(("arbitrary",)),
        name="moe_experts",
    )(block_expert, nb_real, slot_src, slot_dst, h3, slot_gate, we_gate, we_up, we_down)


def _dispatch_plan(top_idx, gates, counts, n):
    blk = MOE_BLOCK
    a = n * TOP_K
    nb = (a + N_EXPERTS * (blk - 1)) // blk
    order = jnp.argsort(top_idx.reshape(a)).astype(jnp.int32)
    counts = counts.astype(jnp.int32)
    padded = (counts + blk - 1) // blk * blk
    start = jnp.cumsum(counts) - counts
    pstart = jnp.cumsum(padded) - padded
    block_end = jnp.cumsum(padded) // blk
    blocks = jnp.arange(nb, dtype=jnp.int32)
    block_expert = jnp.minimum(jnp.sum((block_end[None, :] <= blocks[:, None]).astype(jnp.int32), axis=1),
                               N_EXPERTS - 1)
    off = blocks * blk - pstart[block_expert]
    base = start[block_expert] + off
    lane = jnp.arange(blk, dtype=jnp.int32)[None, :]
    rows = jnp.take(order, base[:, None] + lane, mode="clip")
    real = (off[:, None] + lane) < counts[block_expert][:, None]
    tok = rows // TOP_K
    kk = rows - tok * TOP_K
    slot_src = jnp.where(real, tok, 0)
    slot_dst = jnp.where(real, kk * n + tok, TOP_K * n + lane)
    slot_dst = jnp.concatenate([slot_dst, TOP_K * n + lane], axis=0)
    slot_gate = jnp.where(real, jnp.take(gates.reshape(a), rows, mode="clip"), 0.0).reshape(nb, 1, blk)
    nb_real = (jnp.sum(padded) // blk).reshape(1).astype(jnp.int32)
    return block_expert, nb_real, slot_src.reshape(nb, 1, blk), slot_dst.reshape(nb + 1, 1, blk), slot_gate


def _combine_kernel(x_ref, h_ref, wsg_ref, wsu_ref, wsd_ref, *rest):
    y_refs, o_ref = rest[:TOP_K], rest[TOP_K]
    h = _from_token_major(h_ref[...])
    a = (jax.nn.silu(_dot(h, wsg_ref[...])) * _dot(h, wsu_ref[...])).astype(_BF16)
    routed = y_refs[0][...].astype(_F32)
    for k in range(1, TOP_K):
        routed = routed + y_refs[k][...].astype(_F32)
    o_ref[...] = x_ref[...] + _dot(a, wsd_ref[...]) + _from_token_major(routed)


def _combine(x1, h3, y3, ws_gate, ws_up, ws_down):
    n, d = x1.shape
    c = d // LANES
    tm = COMBINE_TM
    nt = n // tm
    de = ws_gate.shape[1]
    const = lambda r, cc: pl.BlockSpec((r, cc), lambda i: (0, 0))
    row = pl.BlockSpec((tm, d), lambda i: (i, 0))
    y_specs = [pl.BlockSpec((tm, c, LANES), functools.partial(lambda i, k: (k * nt + i, 0, 0), k=k))
               for k in range(TOP_K)]
    return pl.pallas_call(
        _combine_kernel,
        grid=(nt,),
        in_specs=[row, pl.BlockSpec((tm, c, LANES), lambda i: (i, 0, 0)), const(d, de), const(d, de), const(de, d)]
        + y_specs,
        out_specs=row,
        out_shape=jax.ShapeDtypeStruct((n, d), _F32),
        compiler_params=_cparams(("arbitrary",)),
        name="moe_combine",
    )(x1, h3, ws_gate.astype(_BF16), ws_up.astype(_BF16), ws_down.astype(_BF16), *([y3] * TOP_K))


def _moe_block(x1, norm_ffn_g, w_router, router_bias, we_gate, we_up, we_down, ws_gate, ws_up, ws_down):
    n = x1.shape[0]
    h3, top_idx_t, gates_t, counts = _router(x1, norm_ffn_g, w_router, router_bias)
    block_expert, nb_real, slot_src, slot_dst, slot_gate = _dispatch_plan(
        top_idx_t.T, gates_t.T, counts.reshape(N_EXPERTS), n)
    y3 = _moe_experts(h3, block_expert, nb_real, slot_src, slot_dst, slot_gate, we_gate, we_up, we_down)
    return _combine(x1, h3, y3, ws_gate, ws_up, ws_down)


def kernel(x, mem, norm_attn_g, w_in, nsa_pe_k, nsa_w_ck, nsa_pe_v, nsa_w_cv, nsa_q_g, nsa_k_g, rel_bias, fox_b_f, fox_q_g, fox_k_g, norm_mem_g, w_mem_kv, mem_q_g, mem_k_g, w_o_nsa, w_o_fox, w_o_mem, w_out, norm_ffn_g, w_router, router_bias, we_gate, we_up, we_down, ws_gate, ws_up, ws_down):
    b, t, d = x.shape
    assert norm_attn_g.shape[0] == 1, "single-layer problem"
    l = 0
    proj2d, o_nsa, o_fox, o_mem = _attention_mixers(
        x, mem, norm_attn_g[l], w_in[l], nsa_pe_k[l], nsa_w_ck[l], nsa_pe_v[l], nsa_w_cv[l], nsa_q_g[l],
        nsa_k_g[l], rel_bias, fox_b_f[l], fox_q_g[l], fox_k_g[l], norm_mem_g[l], w_mem_kv[l], mem_q_g[l],
        mem_k_g[l])
    x1 = _merge(x.reshape(b * t, d), o_nsa, o_fox, o_mem, proj2d, w_o_nsa[l], w_o_fox[l], w_o_mem[l], w_out[l])
    out = _moe_block(x1, norm_ffn_g[l], w_router[l], router_bias[l], we_gate[l], we_up[l], we_down[l],
                     ws_gate[l], ws_up[l], ws_down[l])
    return out.reshape(b, t, d)
```

```python
import functools
import math

import jax
import jax.numpy as jnp
import numpy as np
from jax import lax
from jax.experimental import pallas as pl
from jax.experimental.pallas import tpu as pltpu

D_MODEL = 2048
HEAD_DIM = 128
NSA_HEADS = 8
NSA_KV_HEADS = 2
NSA_GROUP = NSA_HEADS // NSA_KV_HEADS
FOX_HEADS = 4
MEM_HEADS = 4
CMP_LEN = 32
CMP_STRIDE = 16
SLC_LEN = 64
N_SELECT = 16
WINDOW = 512
NUM_BUCKETS = 32
MAX_DISTANCE = 128
N_BRANCHES = 3
N_EXPERTS = 64
TOP_K = 8
D_EXPERT = 512
ROUTED_SCALE = 2.5
ATTN_SCALE = HEAD_DIM ** -0.5
NEG_INF = -1e30
FORCE_SCORE = 1e4
RMS_EPS = 1e-6
LOG2E = math.log2(math.e)

NSA_Q_W = NSA_HEADS * HEAD_DIM
NSA_KV_W = NSA_KV_HEADS * HEAD_DIM
NSA_GATE_W = 3 * NSA_HEADS
FOX_W = FOX_HEADS * HEAD_DIM
MEM_W = MEM_HEADS * HEAD_DIM
MERGE_W = N_BRANCHES * D_MODEL

LANES = 128
VMEM_LIMIT_BYTES = 56 * 1024 * 1024

PROJ_TM = 1024
PROJ_TN = 1536
ATT_TQ = 128
FAR_GROUP = 4
FOX_TQ = 256
MEM_TQ = 512
MERGE_TM = 256
ROUTER_TM = 512
MOE_BLOCK = 256
COMBINE_TM = 256

CB_MERGE = 0
CB_ATT = N_BRANCHES * D_MODEL // LANES
CB_QNSA = CB_ATT
CB_KCMP = CB_ATT + 8
CB_VCMP = CB_ATT + 10
CB_KSLC = CB_ATT + 12
CB_VSLC = CB_ATT + 14
CB_KWIN = CB_ATT + 16
CB_VWIN = CB_ATT + 18
CB_FOXQ = CB_ATT + 20
CB_FOXK = CB_ATT + 24
CB_FOXV = CB_ATT + 28
CB_MEMQ = CB_ATT + 32
PROJ_COLS = (CB_ATT + 36) * LANES
SMALL_W = 2 * LANES
FOXF_LANE = 12

_BF16 = jnp.bfloat16
_F32 = jnp.float32


def _cparams(sem):
    return pltpu.CompilerParams(dimension_semantics=sem, vmem_limit_bytes=VMEM_LIMIT_BYTES)


def _dot(a, b):
    return jnp.dot(a, b, preferred_element_type=_F32)


def _dot_nt(a, b):
    return lax.dot_general(a, b, (((1,), (1,)), ((), ())), preferred_element_type=_F32)


def _dot_tn(a, b):
    return lax.dot_general(a, b, (((0,), (0,)), ((), ())), preferred_element_type=_F32)


def _lanes(parts):
    return jnp.concatenate(parts, axis=1)


def _to_token_major(x):
    c = x.shape[1] // LANES
    chunks = jnp.stack([x[:, j * LANES:(j + 1) * LANES] for j in range(c)], axis=0)
    return pltpu.einshape("ctl->tcl", chunks)


def _from_token_major(x3):
    xt = pltpu.einshape("tcl->ctl", x3)
    return _lanes([xt[j] for j in range(x3.shape[1])])


REORDER_TR = 128
REORDER_TC = 512


def _reorder_kernel(pieces, narrow, w_ref, o_ref, s_ref):
    off = 0
    for src, width in pieces:
        for c in range(0, width, REORDER_TC):
            o_ref[:, off + c:off + c + REORDER_TC] = w_ref[:, src + c:src + c + REORDER_TC].astype(o_ref.dtype)
        off += width
    s_ref[...] = jnp.zeros(s_ref.shape, s_ref.dtype)
    for src, width, dst in narrow:
        s_ref[:, dst:dst + width] = w_ref[:, src:src + width].astype(s_ref.dtype)


def _reorder_cast(w, pieces, narrow, narrow_cols):
    rows, cols = w.shape
    total = sum(width for _, width in pieces)
    assert rows % REORDER_TR == 0 and all(width % REORDER_TC == 0 for _, width in pieces)
    return pl.pallas_call(
        functools.partial(_reorder_kernel, pieces, narrow),
        grid=(rows // REORDER_TR,),
        in_specs=[pl.BlockSpec((REORDER_TR, cols), lambda i: (i, 0))],
        out_specs=[pl.BlockSpec((REORDER_TR, total), lambda i: (i, 0)),
                   pl.BlockSpec((REORDER_TR, narrow_cols), lambda i: (i, 0))],
        out_shape=[jax.ShapeDtypeStruct((rows, total), _BF16), jax.ShapeDtypeStruct((rows, narrow_cols), _BF16)],
        compiler_params=_cparams(("arbitrary",)),
        name="reorder_cast",
    )(w)


def _proj_kernel(norm_j0, has_small, x_ref, g_ref, w_ref, cg_ref, cf_ref, *rest):
    if has_small:
        ws_ref, o_ref, os_ref, h_ref = rest
    else:
        o_ref, h_ref = rest
    j = pl.program_id(1)

    @pl.when(j == 0)
    def _():
        x = x_ref[...]
        ms = jnp.mean(x * x, axis=-1, keepdims=True)
        h = (x * lax.rsqrt(ms + RMS_EPS) * g_ref[...]).astype(_BF16)
        h_ref[...] = h
        if has_small:
            os_ref[...] = _dot(h, ws_ref[...])

    y = _dot(h_ref[...], w_ref[...])
    tn = y.shape[1]

    @pl.when(j >= norm_j0)
    def _():
        for c in range(tn // LANES):
            sl = slice(c * LANES, (c + 1) * LANES)
            yh = y[:, sl]
            ms = jnp.mean(yh * yh, axis=-1, keepdims=True)
            scale = jnp.where(cf_ref[:, sl] > 0.0, lax.rsqrt(ms + RMS_EPS), 1.0)
            o_ref[:, sl] = (yh * scale * cg_ref[:, sl]).astype(o_ref.dtype)

    @pl.when(j < norm_j0)
    def _():
        o_ref[...] = y.astype(o_ref.dtype)


def _rms_project(x2d, g, w, col_gain, col_flag, n_plain_cols, w_small, tm, tn):
    n, d = x2d.shape
    c = w.shape[1]
    has_small = w_small is not None
    in_specs = [
        pl.BlockSpec((tm, d), lambda i, j: (i, 0)),
        pl.BlockSpec((1, d), lambda i, j: (0, 0)),
        pl.BlockSpec((d, tn), lambda i, j: (0, j)),
        pl.BlockSpec((1, tn), lambda i, j: (0, j)),
        pl.BlockSpec((1, tn), lambda i, j: (0, j)),
    ]
    args = [x2d, g.reshape(1, d), w, col_gain.reshape(1, c), col_flag.reshape(1, c)]
    out_shape = [jax.ShapeDtypeStruct((n, c), _BF16)]
    out_specs = [pl.BlockSpec((tm, tn), lambda i, j: (i, j))]
    if has_small:
        ws = w_small.shape[1]
        in_specs.append(pl.BlockSpec((d, ws), lambda i, j: (0, 0)))
        args.append(w_small)
        out_shape.append(jax.ShapeDtypeStruct((n, ws), _F32))
        out_specs.append(pl.BlockSpec((tm, ws), lambda i, j: (i, 0)))
    assert n % tm == 0 and c % tn == 0 and n_plain_cols % tn == 0
    res = pl.pallas_call(
        functools.partial(_proj_kernel, n_plain_cols // tn, has_small),
        grid=(n // tm, c // tn),
        in_specs=in_specs,
        out_specs=out_specs,
        out_shape=out_shape,
        scratch_shapes=[pltpu.VMEM((tm, d), _BF16)],
        compiler_params=_cparams(("arbitrary", "arbitrary")),
        name="rms_project",
    )(*args)
    return res if has_small else res[0]


def _cmp_kernel(x_ref, wlo_ref, whi_ref, pelo_ref, pehi_ref, kg_ref, o_ref):
    j = pl.program_id(1)
    x = x_ref[0, 0]
    nchunk = x.shape[0]
    ylo = _dot(x, wlo_ref[0])
    yhi = _dot(x, whi_ref[0])
    pe = _dot(pelo_ref[0], wlo_ref[0]) + _dot(pehi_ref[0], whi_ref[0])
    y = ylo + pltpu.roll(yhi, nchunk - 1, 0) + pe[0:1, :]

    @pl.when(j < NSA_KV_HEADS)
    def _():
        ms = jnp.mean(y * y, axis=-1, keepdims=True)
        o_ref[0, 0] = (y * lax.rsqrt(ms + RMS_EPS) * kg_ref[...]).astype(o_ref.dtype)

    @pl.when(j >= NSA_KV_HEADS)
    def _():
        o_ref[0, 0] = y.T.astype(o_ref.dtype)


def _nsa_compress(cmp_in, w_ck, w_cv, pe_k, pe_v, k_g):
    b, nj, t, dk = cmp_in.shape
    nchunk = t // CMP_STRIDE
    half = CMP_LEN // 2
    assert nchunk == dk
    x = cmp_in.reshape(b, nj, nchunk, CMP_STRIDE * dk)

    def halves(w):
        return (w[:half].reshape(half * dk, dk).astype(_BF16),
                w[half:].reshape(half * dk, dk).astype(_BF16))

    klo, khi = halves(w_ck)
    vlo, vhi = halves(w_cv)
    wlo = jnp.stack([klo, vlo])
    whi = jnp.stack([khi, vhi])

    def pe_halves(pe):
        lo = jnp.broadcast_to(pe[:half].reshape(1, half * dk), (8, half * dk)).astype(_BF16)
        hi = jnp.broadcast_to(pe[half:].reshape(1, half * dk), (8, half * dk)).astype(_BF16)
        return lo, hi

    pklo, pkhi = pe_halves(pe_k)
    pvlo, pvhi = pe_halves(pe_v)
    pelo = jnp.stack([pklo, pvlo])
    pehi = jnp.stack([pkhi, pvhi])
    kv = lambda bb, j: (j // NSA_KV_HEADS, 0, 0)
    return pl.pallas_call(
        _cmp_kernel,
        grid=(b, nj),
        in_specs=[
            pl.BlockSpec((1, 1, nchunk, CMP_STRIDE * dk), lambda bb, j: (bb, j, 0, 0)),
            pl.BlockSpec((1, half * dk, dk), kv),
            pl.BlockSpec((1, half * dk, dk), kv),
            pl.BlockSpec((1, 8, half * dk), kv),
            pl.BlockSpec((1, 8, half * dk), kv),
            pl.BlockSpec((1, dk), lambda bb, j: (0, 0)),
        ],
        out_specs=pl.BlockSpec((1, 1, nchunk, dk), lambda bb, j: (bb, j, 0, 0)),
        out_shape=jax.ShapeDtypeStruct((b, nj, nchunk, dk), _BF16),
        compiler_params=_cparams(("arbitrary", "arbitrary")),
        name="nsa_compress",
    )(x, wlo, whi, pelo, pehi, k_g.reshape(1, dk).astype(_F32))


def _t_update(state, tiles):
    m, l, acc = state
    masked = [s if mask is None else jnp.where(mask, s, NEG_INF) for s, mask, _ in tiles]
    m_new = m
    for sm in masked:
        m_new = jnp.maximum(m_new, jnp.max(sm, axis=0, keepdims=True))
    alpha = jnp.exp2(m - m_new)
    l_new = alpha * l
    acc_new = alpha * acc
    for sm, (_, _, v) in zip(masked, tiles):
        e = jnp.exp2(sm - m_new)
        l_new = l_new + jnp.sum(e, axis=0, keepdims=True)
        acc_new = acc_new + _dot_tn(v, e.astype(_BF16))
    return m_new, l_new, acc_new


def _t_init(nq):
    return (jnp.full((1, nq), NEG_INF, _F32), jnp.zeros((1, nq), _F32), jnp.zeros((HEAD_DIM, nq), _F32))


def _t_finish(state):
    _, l, acc = state
    return acc / l


def _nsa_kernel(q_ref, kc_ref, vct_ref, ks_ref, vs_ref, kw_ref, vw_ref, gl_ref,
                wt_ref, dt_ref, ovt_ref, ext_ref, o_ref, sel_ref):
    i = pl.program_id(2)
    tq = ATT_TQ
    hg = NSA_GROUP
    nq = hg * tq
    n_slc = ovt_ref.shape[0]

    q = q_ref[0]
    qt = _lanes([q[:, h * HEAD_DIM:(h + 1) * HEAD_DIM].astype(_F32).T for h in range(hg)]).astype(_BF16)

    key = lax.broadcasted_iota(jnp.int32, (LANES, nq), 0)
    qry = lax.broadcasted_iota(jnp.int32, (LANES, nq), 1) & (tq - 1)

    def ktile(ref, kt):
        return ref[0, pl.ds(pl.multiple_of(kt * LANES, LANES), LANES), :]

    def near_bias(d):
        return _lanes([dt_ref[h, d] for h in range(hg)])

    woff = pl.multiple_of(wt_ref.shape[1] - LANES - 8 - (LANES // CMP_STRIDE) * i, 8)
    s = _dot(kc_ref[0, 0], qt) + _lanes([wt_ref[h, pl.ds(woff, LANES), :] for h in range(hg)])
    n_win = WINDOW // tq
    win_tiles = []
    for d in range(n_win, -1, -1):
        ktd = jnp.maximum(i - d, 0)
        sc = _dot(ktile(kw_ref, ktd), qt)
        if d <= 1:
            sc = sc + near_bias(d)
        if d == n_win:
            mk = (qry < key) & (i >= d)
        elif d == 0:
            mk = key <= qry
        else:
            mk = jnp.broadcast_to(i >= d, (LANES, nq))
        win_tiles.append((sc, mk, ktile(vw_ref, ktd)))

    mask_c = (i * tq + qry) >= (CMP_STRIDE * key + CMP_LEN - 1)
    mx = jnp.max(jnp.where(mask_c, s, NEG_INF), axis=0, keepdims=True)
    e = jnp.where(mask_c, jnp.exp2(s - mx), 0.0)
    l = jnp.sum(e, axis=0, keepdims=True)
    p_c = e / jnp.where(l > 0.0, l, 1.0)
    o_c = _dot(vct_ref[0, 0], p_c.astype(_BF16))

    ps = p_c[:, 0:tq]
    for h in range(1, hg):
        ps = ps + p_c[:, h * tq:(h + 1) * tq]
    ps_hi = ps.astype(_BF16)
    ps_lo = (ps - ps_hi.astype(_F32)).astype(_BF16)
    imp = _dot(ovt_ref[...], ps_hi) + _dot(ovt_ref[...], ps_lo)
    blk = lax.broadcasted_iota(jnp.int32, (n_slc, tq), 0)
    tpos = i * tq + lax.broadcasted_iota(jnp.int32, (n_slc, tq), 1)
    cur = tpos // SLC_LEN
    valid = blk * SLC_LEN <= tpos
    forced = (blk == 0) | (blk == cur) | (blk == cur - 1)
    score = jnp.where(valid, jnp.where(forced, FORCE_SCORE, imp), -1.0)
    rank = jnp.zeros((n_slc, tq), _F32)
    for k in range(n_slc):
        ck = score[k:k + 1, :]
        beats = (ck > score) | ((ck == score) & (blk > k))
        rank = rank + jnp.where(beats, 1.0, 0.0)
    sel = jnp.where(rank < float(min(N_SELECT, n_slc)), 1.0, 0.0).astype(_BF16)
    selfull = _dot(ext_ref[...], sel)
    for kt in range(ext_ref.shape[0] // LANES):
        sel_ref[kt] = selfull[kt * LANES:(kt + 1) * LANES, :]

    def sel_mask(kt, ok=True):
        return _lanes([sel_ref[kt]] * hg) > jnp.where(ok, 0.5, 2.0)

    o_w = _t_finish(_t_update(_t_init(nq), win_tiles))

    n_far = jnp.maximum(i - 1, 0)

    def far_body(j, st):
        k0 = pl.multiple_of(j * (FAR_GROUP * LANES), FAR_GROUP * LANES)
        sg = _dot(ks_ref[0, pl.ds(k0, FAR_GROUP * LANES), :], qt)
        tiles = []
        for u in range(FAR_GROUP):
            kt = FAR_GROUP * j + u
            tiles.append((sg[u * LANES:(u + 1) * LANES], sel_mask(kt, kt < n_far), ktile(vs_ref, kt)))
        return _t_update(st, tiles)

    st = lax.fori_loop(0, (n_far + FAR_GROUP - 1) // FAR_GROUP, far_body, _t_init(nq))
    kt1 = jnp.maximum(i - 1, 0)
    st = _t_update(st, [
        (_dot(ktile(ks_ref, kt1), qt) + near_bias(1), sel_mask(kt1, i >= 1), ktile(vs_ref, kt1)),
        (_dot(ktile(ks_ref, i), qt) + near_bias(0), sel_mask(i) & (key <= qry), ktile(vs_ref, i))])
    o_s = _t_finish(st)

    glt = jax.nn.sigmoid(gl_ref[0]).T
    grow = lambda br: _lanes([glt[br * hg + h:br * hg + h + 1, :] for h in range(hg)])
    o = grow(0) * o_c + grow(1) * o_s + grow(2) * o_w
    for h in range(hg):
        o_ref[0, :, h * HEAD_DIM:(h + 1) * HEAD_DIM] = o[:, h * tq:(h + 1) * tq].T.astype(o_ref.dtype)


def _t5_bucket(rel):
    n = np.maximum(rel, 0)
    max_exact = NUM_BUCKETS // 2
    ratio = np.maximum(n, 1).astype(np.float32) / np.float32(max_exact)
    log_ratio = np.log(ratio) / np.float32(math.log(MAX_DISTANCE / max_exact))
    large = np.minimum(max_exact + (log_ratio * np.float32(NUM_BUCKETS - max_exact)).astype(np.int32),
                       NUM_BUCKETS - 1)
    return np.where(n < max_exact, n, large).astype(np.int32)


def _bias_lookup(rb, bucket):
    bk = jnp.asarray(bucket.astype(np.int8))[None]
    ex = (slice(None),) + (None,) * bucket.ndim
    tab = jnp.broadcast_to(rb[:, 0][ex], (rb.shape[0],) + bucket.shape)
    for k in range(1, NUM_BUCKETS):
        tab = jnp.where(bk == k, rb[:, k][ex], tab)
    return tab


def _nsa_tables(rel_bias):
    rb = rel_bias.astype(_F32).T * LOG2E
    r = np.arange(ATT_TQ)
    rel_d = (np.arange(2) * ATT_TQ)[:, None, None] + r[None, None, :] - r[None, :, None]
    assert _t5_bucket(np.array([ATT_TQ + 1]))[0] == NUM_BUCKETS - 1
    u = np.arange(2 * LANES)
    rel_w = r[None, :] - CMP_STRIDE * (u[:, None] - (LANES - 8)) - (CMP_LEN - 1)
    far = rb[:, NUM_BUCKETS - 1]
    dt = _bias_lookup(rb, _t5_bucket(rel_d)) - far[:, None, None, None]
    return dt, _bias_lookup(rb, _t5_bucket(rel_w))


def _nsa_attention(proj, small, kvc, rel_bias):
    b, t, _ = proj.shape
    g, hg, tq = NSA_KV_HEADS, NSA_GROUP, ATT_TQ
    n_slc = t // SLC_LEN
    n_cmp = (t - CMP_LEN) // CMP_STRIDE + 1
    nt = t // LANES
    assert n_cmp <= LANES and kvc.shape[2] == LANES and n_slc % 16 == 0 and tq == LANES
    assert nt % FAR_GROUP == 0
    assert LANES - 8 - (LANES // CMP_STRIDE) * (t // tq - 1) >= 0
    dt, wt = _nsa_tables(rel_bias)
    cstart = np.arange(LANES) * CMP_STRIDE
    sstart = np.arange(n_slc) * SLC_LEN
    overlap = np.clip(np.minimum(cstart[None, :] + CMP_LEN, sstart[:, None] + SLC_LEN)
                      - np.maximum(cstart[None, :], sstart[:, None]), 0, None).astype(np.float32) / CMP_LEN
    overlap[:, n_cmp:] = 0.0
    ovt = jnp.asarray(overlap, _BF16)
    ext = jnp.asarray(np.arange(t)[:, None] // SLC_LEN == np.arange(n_slc)[None, :], _BF16)

    seq = lambda cb: pl.BlockSpec((1, t, LANES), lambda bb, gg, ii: (bb, 0, cb + gg))
    return pl.pallas_call(
        _nsa_kernel,
        grid=(b, g, t // tq),
        in_specs=[
            pl.BlockSpec((1, tq, hg * HEAD_DIM), lambda bb, gg, ii: (bb, ii, CB_QNSA * LANES // (hg * HEAD_DIM) + gg)),
            pl.BlockSpec((1, 1, LANES, HEAD_DIM), lambda bb, gg, ii: (bb, gg, 0, 0)),
            pl.BlockSpec((1, 1, HEAD_DIM, LANES), lambda bb, gg, ii: (bb, NSA_KV_HEADS + gg, 0, 0)),
            seq(CB_KSLC), seq(CB_VSLC), seq(CB_KWIN), seq(CB_VWIN),
            pl.BlockSpec((1, tq, LANES), lambda bb, gg, ii: (bb, ii, gg)),
            pl.BlockSpec((hg, 2 * LANES, tq), lambda bb, gg, ii: (gg, 0, 0)),
            pl.BlockSpec((hg, 2, LANES, tq), lambda bb, gg, ii: (gg, 0, 0, 0)),
            pl.BlockSpec((n_slc, LANES), lambda bb, gg, ii: (0, 0)),
            pl.BlockSpec((t, n_slc), lambda bb, gg, ii: (0, 0)),
        ],
        out_specs=pl.BlockSpec((1, tq, hg * HEAD_DIM), lambda bb, gg, ii: (bb, ii, gg)),
        out_shape=jax.ShapeDtypeStruct((b, t, NSA_Q_W), _BF16),
        scratch_shapes=[pltpu.VMEM((nt, LANES, tq), _F32)],
        compiler_params=_cparams(("arbitrary", "arbitrary", "arbitrary")),
        name="nsa_attention",
    )(proj, kvc, kvc, proj, proj, proj, proj, small, wt, dt, ovt, ext)


def _fox_cum_kernel(s_ref, b_ref, col_ref, row_ref):
    z = s_ref[0] + b_ref[...]
    lf = (jnp.minimum(z, 0.0) - jnp.log1p(jnp.exp(-jnp.abs(z)))) * LOG2E
    x = lf.T
    t = x.shape[1]
    lane = lax.broadcasted_iota(jnp.int32, x.shape, 1)
    sh = 1
    while sh < t:
        x = x + jnp.where(lane >= sh, pltpu.roll(x, sh, 1), 0.0)
        sh *= 2
    row_ref[0] = x[8:16, :]
    col_ref[0] = x.T


def _fox_cumsum(small, b_f):
    b, t, _ = small.shape
    bvec = jnp.zeros((1, LANES), _F32).at[0, FOXF_LANE:FOXF_LANE + FOX_HEADS].set(b_f.astype(_F32))
    return pl.pallas_call(
        _fox_cum_kernel,
        grid=(b,),
        in_specs=[pl.BlockSpec((1, t, LANES), lambda bb: (bb, 0, 1)),
                  pl.BlockSpec((1, LANES), lambda bb: (0, 0))],
        out_specs=[pl.BlockSpec((1, t, LANES), lambda bb: (bb, 0, 0)),
                   pl.BlockSpec((1, 8, t), lambda bb: (bb, 0, 0))],
        out_shape=[jax.ShapeDtypeStruct((b, t, LANES), _F32), jax.ShapeDtypeStruct((b, 8, t), _F32)],
        compiler_params=_cparams(("arbitrary",)),
        name="fox_cumsum",
    )(small, bvec)


def _fox_kernel(q_ref, k_ref, v_ref, cc_ref, cr_ref, o_ref):
    i = pl.program_id(1)
    tq = FOX_TQ
    key = lax.broadcasted_iota(jnp.int32, (tq, tq), 0)
    qry = lax.broadcasted_iota(jnp.int32, (tq, tq), 1)
    heads = range(FOX_HEADS)
    hsl = [slice(h * HEAD_DIM, (h + 1) * HEAD_DIM) for h in heads]
    qts = [q_ref[0, :, hsl[h]].astype(_F32).T.astype(_BF16) for h in heads]
    cqs = [cr_ref[0, FOXF_LANE - 8 + h:FOXF_LANE - 8 + h + 1, :] for h in heads]

    def keys(kt):
        return pl.ds(pl.multiple_of(kt * tq, tq), tq)

    def logits(h, kt):
        ck = cc_ref[0, keys(kt), FOXF_LANE + h:FOXF_LANE + h + 1]
        return _dot(k_ref[0, keys(kt), hsl[h]], qts[h]) + (cqs[h] - ck)

    def body(kt, sts):
        ss = [logits(h, kt) for h in heads]
        return tuple(_t_update(sts[h], [(ss[h], None, v_ref[0, keys(kt), hsl[h]])]) for h in heads)

    sts = lax.fori_loop(0, i, body, tuple(_t_init(tq) for _ in heads))
    ss = [logits(h, i) for h in heads]
    for h in heads:
        st = _t_update(sts[h], [(ss[h], key <= qry, v_ref[0, keys(i), hsl[h]])])
        o_ref[0, :, hsl[h]] = _t_finish(st).T.astype(o_ref.dtype)


def _fox_attention(proj, cum_col, cum_row):
    b, t, _ = proj.shape
    tq = FOX_TQ
    w = FOX_W
    return pl.pallas_call(
        _fox_kernel,
        grid=(b, t // tq),
        in_specs=[
            pl.BlockSpec((1, tq, w), lambda bb, ii: (bb, ii, CB_FOXQ * LANES // w)),
            pl.BlockSpec((1, t, w), lambda bb, ii: (bb, 0, CB_FOXK * LANES // w)),
            pl.BlockSpec((1, t, w), lambda bb, ii: (bb, 0, CB_FOXV * LANES // w)),
            pl.BlockSpec((1, t, LANES), lambda bb, ii: (bb, 0, 0)),
            pl.BlockSpec((1, 8, tq), lambda bb, ii: (bb, 0, ii)),
        ],
        out_specs=pl.BlockSpec((1, tq, w), lambda bb, ii: (bb, ii, 0)),
        out_shape=jax.ShapeDtypeStruct((b, t, w), _BF16),
        compiler_params=_cparams(("arbitrary", "arbitrary")),
        name="fox_attention",
    )(proj, proj, proj, cum_col, cum_row)


def _mem_kernel(q_ref, kv_ref, o_ref):
    for h in range(MEM_HEADS):
        hs = slice(h * HEAD_DIM, (h + 1) * HEAD_DIM)
        vs = slice(MEM_W + h * HEAD_DIM, MEM_W + (h + 1) * HEAD_DIM)
        s = _dot_nt(q_ref[0, :, hs], kv_ref[0, :, hs])
        e = jnp.exp2(s - jnp.max(s, axis=1, keepdims=True))
        p = e / jnp.sum(e, axis=1, keepdims=True)
        o_ref[0, :, hs] = _dot(p.astype(_BF16), kv_ref[0, :, vs]).astype(o_ref.dtype)


def _mem_attention(proj, memkv):
    b, t, _ = proj.shape
    m = memkv.shape[1]
    tq = MEM_TQ
    return pl.pallas_call(
        _mem_kernel,
        grid=(b, t // tq),
        in_specs=[pl.BlockSpec((1, tq, MEM_W), lambda bb, ii: (bb, ii, CB_MEMQ * LANES // MEM_W)),
                  pl.BlockSpec((1, m, 2 * MEM_W), lambda bb, ii: (bb, 0, 0))],
        out_specs=pl.BlockSpec((1, tq, MEM_W), lambda bb, ii: (bb, ii, 0)),
        out_shape=jax.ShapeDtypeStruct((b, t, MEM_W), _BF16),
        compiler_params=_cparams(("arbitrary", "arbitrary")),
        name="mem_attention",
    )(proj, memkv)


def _merge_kernel(x_ref, on_ref, of_ref, om_ref, g0_ref, g1_ref, g2_ref, wn_ref, wf_ref, wm_ref, wo_ref,
                  o_ref, z_ref):
    tn = 512
    for c in range(D_MODEL // tn):
        cs = slice(c * tn, (c + 1) * tn)
        z = (jax.nn.sigmoid(g0_ref[:, cs].astype(_F32)) * _dot(on_ref[...], wn_ref[:, cs])
             + jax.nn.sigmoid(g1_ref[:, cs].astype(_F32)) * _dot(of_ref[...], wf_ref[:, cs])
             + jax.nn.sigmoid(g2_ref[:, cs].astype(_F32)) * _dot(om_ref[...], wm_ref[:, cs]))
        z_ref[:, cs] = z.astype(_BF16)
    o_ref[...] = x_ref[...] + _dot(z_ref[...], wo_ref[...])


def _merge(x2d, o_nsa, o_fox, o_mem, proj2d, w_o_nsa, w_o_fox, w_o_mem, w_out):
    n, d = x2d.shape
    tm = MERGE_TM
    gcb = CB_MERGE * LANES // d
    const = lambda r, c: pl.BlockSpec((r, c), lambda i: (0, 0))
    return pl.pallas_call(
        _merge_kernel,
        grid=(n // tm,),
        in_specs=[
            pl.BlockSpec((tm, d), lambda i: (i, 0)),
            pl.BlockSpec((tm, NSA_Q_W), lambda i: (i, 0)),
            pl.BlockSpec((tm, FOX_W), lambda i: (i, 0)),
            pl.BlockSpec((tm, MEM_W), lambda i: (i, 0)),
            pl.BlockSpec((tm, d), lambda i: (i, gcb)),
            pl.BlockSpec((tm, d), lambda i: (i, gcb + 1)),
            pl.BlockSpec((tm, d), lambda i: (i, gcb + 2)),
            const(NSA_Q_W, d), const(FOX_W, d), const(MEM_W, d), const(d, d),
        ],
        out_specs=pl.BlockSpec((tm, d), lambda i: (i, 0)),
        out_shape=jax.ShapeDtypeStruct((n, d), _F32),
        scratch_shapes=[pltpu.VMEM((tm, d), _BF16)],
        compiler_params=_cparams(("arbitrary",)),
        name="merge_out",
    )(x2d, o_nsa, o_fox, o_mem, proj2d, proj2d, proj2d,
      w_o_nsa.astype(_BF16), w_o_fox.astype(_BF16), w_o_mem.astype(_BF16), w_out.astype(_BF16))


def _nsa_inputs(x, norm_attn_g, w_in, nsa_pe_k, nsa_w_ck, nsa_pe_v, nsa_w_cv, nsa_q_g, nsa_k_g,
                fox_q_g, fox_k_g, mem_q_g):
    b, t, d = x.shape
    ones = lambda k: jnp.ones((k * HEAD_DIM,), _F32)
    zeros = lambda k: jnp.zeros((k * HEAD_DIM,), _F32)
    tile = lambda gv, k: jnp.tile(gv.astype(_F32), k)
    qs = ATTN_SCALE * LOG2E

    g_end = NSA_Q_W + 6 * NSA_KV_W
    f_off = g_end + NSA_GATE_W + 3 * FOX_W
    m_off = f_off + FOX_HEADS + MEM_W
    narrow = tuple((g_end + (br * NSA_KV_HEADS + gg) * NSA_GROUP, NSA_GROUP, gg * LANES + br * NSA_GROUP)
                   for br in range(3) for gg in range(NSA_KV_HEADS))
    narrow += ((f_off, FOX_HEADS, LANES + FOXF_LANE),)
    w_all, w_small = _reorder_cast(
        w_in, ((m_off, MERGE_W), (0, g_end), (g_end + NSA_GATE_W, 3 * FOX_W), (f_off + FOX_HEADS, MEM_W)),
        narrow, SMALL_W)
    col_gain = jnp.concatenate([
        jnp.ones((MERGE_W,), _F32),
        tile(nsa_q_g, NSA_HEADS) * qs, ones(2), ones(2), tile(nsa_k_g, 2), ones(2), tile(nsa_k_g, 2), ones(2),
        tile(fox_q_g, FOX_HEADS) * qs, tile(fox_k_g, FOX_HEADS), ones(FOX_HEADS),
        tile(mem_q_g, MEM_HEADS) * qs])
    col_flag = jnp.concatenate([
        jnp.zeros((MERGE_W,), _F32),
        ones(NSA_HEADS), zeros(2), zeros(2), ones(2), zeros(2), ones(2), zeros(2),
        ones(FOX_HEADS), ones(FOX_HEADS), zeros(FOX_HEADS), ones(MEM_HEADS)])

    x2d = x.reshape(b * t, d)
    proj2d, small2d = _rms_project(x2d, norm_attn_g, w_all, col_gain, col_flag, CB_ATT * LANES, w_small,
                                   PROJ_TM, PROJ_TN)
    proj = proj2d.reshape(b, t, PROJ_COLS)
    small = small2d.reshape(b, t, SMALL_W)

    cmp_in = proj[:, :, CB_KCMP * LANES:(CB_KCMP + 4) * LANES].reshape(b, t, 4, HEAD_DIM).transpose(0, 2, 1, 3)
    kvc = _nsa_compress(cmp_in, nsa_w_ck, nsa_w_cv, nsa_pe_k, nsa_pe_v, nsa_k_g)
    return proj, small, kvc


def _attention_mixers(x, mem, norm_attn_g, w_in, nsa_pe_k, nsa_w_ck, nsa_pe_v, nsa_w_cv, nsa_q_g, nsa_k_g,
                      rel_bias, fox_b_f, fox_q_g, fox_k_g, norm_mem_g, w_mem_kv, mem_q_g, mem_k_g):
    b, t, d = x.shape
    ones = lambda k: jnp.ones((k * HEAD_DIM,), _F32)
    zeros = lambda k: jnp.zeros((k * HEAD_DIM,), _F32)
    tile = lambda gv, k: jnp.tile(gv.astype(_F32), k)
    proj, small, kvc = _nsa_inputs(x, norm_attn_g, w_in, nsa_pe_k, nsa_w_ck, nsa_pe_v, nsa_w_cv, nsa_q_g, nsa_k_g,
                                   fox_q_g, fox_k_g, mem_q_g)
    proj2d = proj.reshape(b * t, PROJ_COLS)
    o_nsa = _nsa_attention(proj, small, kvc, rel_bias)

    cum_col, cum_row = _fox_cumsum(small, fox_b_f)
    o_fox = _fox_attention(proj, cum_col, cum_row)

    mm = mem.shape[1]
    mem_gain = jnp.concatenate([tile(mem_k_g, MEM_HEADS), ones(MEM_HEADS)])
    mem_flag = jnp.concatenate([ones(MEM_HEADS), zeros(MEM_HEADS)])
    memkv = _rms_project(mem.reshape(b * mm, d), norm_mem_g, w_mem_kv.astype(_BF16), mem_gain, mem_flag,
                         0, None, mm, MEM_W).reshape(b, mm, 2 * MEM_W)
    o_mem = _mem_attention(proj, memkv)
    return proj2d, o_nsa.reshape(b * t, NSA_Q_W), o_fox.reshape(b * t, FOX_W), o_mem.reshape(b * t, MEM_W)


def _router_kernel(x_ref, g_ref, whi_ref, wlo_ref, b_ref, h_ref, idx_ref, gate_ref, cnt_ref):
    x = x_ref[...]
    ms = jnp.mean(x * x, axis=-1, keepdims=True)
    h = x * lax.rsqrt(ms + RMS_EPS) * g_ref[...]
    h_hi = h.astype(_BF16)
    h_ref[...] = _to_token_major(h_hi)
    h_lo = (h - h_hi.astype(_F32)).astype(_BF16)
    logits = _dot_nt(whi_ref[...], h_hi) + _dot_nt(whi_ref[...], h_lo) + _dot_nt(wlo_ref[...], h_hi)
    scores = jax.nn.sigmoid(logits)
    sb = scores + b_ref[...]
    eidx = lax.broadcasted_iota(jnp.int32, sb.shape, 0)
    idxs, vals = [], []
    for _ in range(TOP_K):
        m = jnp.max(sb, axis=0, keepdims=True)
        idx = jnp.min(jnp.where(sb == m, eidx, N_EXPERTS), axis=0, keepdims=True)
        hit = eidx == idx
        vals.append(jnp.sum(jnp.where(hit, scores, 0.0), axis=0, keepdims=True))
        idxs.append(idx)
        sb = jnp.where(hit, NEG_INF, sb)
    top_s = jnp.concatenate(vals, axis=0)
    idx_ref[...] = jnp.concatenate(idxs, axis=0)
    gate_ref[...] = top_s / jnp.sum(top_s, axis=0, keepdims=True) * ROUTED_SCALE

    @pl.when(pl.program_id(0) == 0)
    def _():
        cnt_ref[...] = jnp.zeros(cnt_ref.shape, cnt_ref.dtype)

    picked = jnp.where(sb < 0.5 * NEG_INF, 1.0, 0.0)
    cnt_ref[...] += jnp.sum(picked, axis=1, keepdims=True)


def _router(x1, norm_g, w_router, router_bias):
    n, d = x1.shape
    tm = ROUTER_TM
    wt = w_router.astype(_F32).T
    w_hi = wt.astype(_BF16)
    w_lo = (wt - w_hi.astype(_F32)).astype(_BF16)
    const = lambda r, c: pl.BlockSpec((r, c), lambda i: (0, 0))
    return pl.pallas_call(
        _router_kernel,
        grid=(n // tm,),
        in_specs=[pl.BlockSpec((tm, d), lambda i: (i, 0)), const(1, d), const(N_EXPERTS, d), const(N_EXPERTS, d),
                  const(N_EXPERTS, 1)],
        out_specs=[pl.BlockSpec((tm, d // LANES, LANES), lambda i: (i, 0, 0)),
                   pl.BlockSpec((TOP_K, tm), lambda i: (0, i)),
                   pl.BlockSpec((TOP_K, tm), lambda i: (0, i)),
                   const(N_EXPERTS, 1)],
        out_shape=[jax.ShapeDtypeStruct((n, d // LANES, LANES), _BF16), jax.ShapeDtypeStruct((TOP_K, n), jnp.int32),
                   jax.ShapeDtypeStruct((TOP_K, n), _F32), jax.ShapeDtypeStruct((N_EXPERTS, 1), _F32)],
        compiler_params=_cparams(("arbitrary",)),
        name="moe_router",
    )(x1, norm_g.reshape(1, d).astype(_F32), w_hi, w_lo, router_bias.reshape(N_EXPERTS, 1).astype(_F32))


def _moe_kernel(be_ref, nbr_ref, src_hbm, dst_hbm, h_hbm, sg_ref, wg_ref, wu_ref, wd_ref, y_hbm,
                xbuf, ybuf, xmat, sidx, didx, wg_bf, wu_bf, wd_bf, gsem, ssem, isem):
    n = pl.program_id(0)
    nbr = nbr_ref[0]
    last = nbr - 1
    slot = n % 2
    other = 1 - slot
    blk = MOE_BLOCK
    dump_block = dst_hbm.shape[0] - 1
    ring = didx.shape[0]

    def src_copy(block, s):
        return pltpu.make_async_copy(src_hbm.at[block], sidx.at[s], isem.at[0, s])

    def dst_copy(block, s3):
        return pltpu.make_async_copy(dst_hbm.at[block], didx.at[s3], isem.at[1, s3])

    def gather_row(s, r, prio):
        pltpu.make_async_copy(h_hbm.at[sidx[s, 0, r]], xbuf.at[s, r], gsem.at[s]).start(priority=prio)

    def scatter_row(s, s3, r, prio):
        pltpu.make_async_copy(ybuf.at[s, r], y_hbm.at[didx[s3, 0, r]], ssem.at[s]).start(priority=prio)

    def issue_gather(s):
        for r in range(blk):
            gather_row(s, r, r % 2)

    def issue_scatter(s, s3):
        for r in range(blk):
            scatter_row(s, s3, r, r % 2)

    def issue_both(s, s3):
        for r in range(blk):
            scatter_row(s, s3, r, r % 2)
            gather_row(s, r, (r + 1) % 2)

    def wait_rows(buf, sem, s):
        pltpu.make_async_copy(buf.at[s], buf.at[s], sem.at[s]).wait()

    @pl.when(n < nbr)
    def _():
        @pl.when(n == 0)
        def _():
            src_copy(0, 0).start()
            dst_copy(0, 0).start()
            dst_copy(dump_block, ring - 1).start()
            ybuf[1] = jnp.zeros(ybuf.shape[1:], ybuf.dtype)
            src_copy(0, 0).wait()
            dst_copy(0, 0).wait()
            dst_copy(dump_block, ring - 1).wait()
            issue_gather(0)
            nxt0 = jnp.minimum(1, last)
            src_copy(nxt0, 1).start()
            dst_copy(nxt0, 1).start()

        wait_rows(xbuf, gsem, slot)

        @pl.when(n < last)
        def _():
            nn = jnp.minimum(n + 2, last)
            src_copy(nn, slot).start()
            dst_copy(nn, (n + 2) % ring).start()

        changed = (n == 0) | (be_ref[n] != be_ref[jnp.maximum(n - 1, 0)])

        @pl.when(changed)
        def _():
            wg_bf[...] = wg_ref[0].astype(_BF16)
            wu_bf[...] = wu_ref[0].astype(_BF16)
            wd_bf[...] = wd_ref[0].astype(_BF16)

        @pl.when(n >= 1)
        def _():
            wait_rows(ybuf, ssem, slot)

        nxt = jnp.minimum(n + 1, last)
        src_copy(nxt, other).wait()
        dst_copy(nxt, (n + 1) % ring).wait()

        xmat[...] = _from_token_major(xbuf[slot])
        issue_both(other, (n + ring - 1) % ring)
        gcol = jnp.broadcast_to(sg_ref[0], (LANES, blk)).T
        a = jax.nn.silu(_dot(xmat[...], wg_bf[...])) * _dot(xmat[...], wu_bf[...])
        a = (a * _lanes([gcol] * (a.shape[1] // LANES))).astype(_BF16)
        ybuf[slot] = _to_token_major(_dot(a, wd_bf[...]).astype(_BF16))

        @pl.when(n == last)
        def _():
            issue_scatter(slot, n % ring)
            wait_rows(xbuf, gsem, other)
            wait_rows(ybuf, ssem, other)
            wait_rows(ybuf, ssem, slot)


def _moe_experts(h3, block_expert, nb_real, slot_src, slot_dst, slot_gate, we_gate, we_up, we_down):
    n, c, _ = h3.shape
    d = c * LANES
    nb = block_expert.shape[0]
    blk = MOE_BLOCK
    e, _, de = we_gate.shape
    grid_spec = pltpu.PrefetchScalarGridSpec(
        num_scalar_prefetch=2,
        grid=(nb,),
        in_specs=[
            pl.BlockSpec(memory_space=pl.ANY),
            pl.BlockSpec(memory_space=pl.ANY),
            pl.BlockSpec(memory_space=pl.ANY),
            pl.BlockSpec((1, 1, blk), lambda i, be, nbr: (i, 0, 0)),
            pl.BlockSpec((1, d, de), lambda i, be, nbr: (be[i], 0, 0)),
            pl.BlockSpec((1, d, de), lambda i, be, nbr: (be[i], 0, 0)),
            pl.BlockSpec((1, de, d), lambda i, be, nbr: (be[i], 0, 0)),
        ],
        out_specs=pl.BlockSpec(memory_space=pl.ANY),
        scratch_shapes=[
            pltpu.VMEM((2, blk, c, LANES), _BF16), pltpu.VMEM((2, blk, c, LANES), _BF16),
            pltpu.VMEM((blk, d), _BF16),
            pltpu.SMEM((2, 1, blk), jnp.int32), pltpu.SMEM((4, 1, blk), jnp.int32),
            pltpu.VMEM((d, de), _BF16), pltpu.VMEM((d, de), _BF16), pltpu.VMEM((de, d), _BF16),
            pltpu.SemaphoreType.DMA((2,)), pltpu.SemaphoreType.DMA((2,)), pltpu.SemaphoreType.DMA((2, 4)),
        ],
    )
    return pl.pallas_call(
        _moe_kernel,
        grid_spec=grid_spec,
        out_shape=jax.ShapeDtypeStruct((TOP_K * n + blk, c, LANES), _BF16),
        compiler_params=_cparams(("arbitrary",)),
        name="moe_experts",
    )(block_expert, nb_real, slot_src, slot_dst, h3, slot_gate, we_gate, we_up, we_down)


def _dispatch_plan(top_idx, gates, counts, n):
    blk = MOE_BLOCK
    a = n * TOP_K
    nb = (a + N_EXPERTS * (blk - 1)) // blk
    order = jnp.argsort(top_idx.reshape(a)).astype(jnp.int32)
    counts = counts.astype(jnp.int32)
    padded = (counts + blk - 1) // blk * blk
    start = jnp.cumsum(counts) - counts
    pstart = jnp.cumsum(padded) - padded
    block_end = jnp.cumsum(padded) // blk
    blocks = jnp.arange(nb, dtype=jnp.int32)
    block_expert = jnp.minimum(jnp.sum((block_end[None, :] <= blocks[:, None]).astype(jnp.int32), axis=1),
                               N_EXPERTS - 1)
    off = blocks * blk - pstart[block_expert]
    base = start[block_expert] + off
    lane = jnp.arange(blk, dtype=jnp.int32)[None, :]
    rows = jnp.take(order, base[:, None] + lane, mode="clip")
    real = (off[:, None] + lane) < counts[block_expert][:, None]
    tok = rows // TOP_K
    kk = rows - tok * TOP_K
    slot_src = jnp.where(real, tok, 0)
    slot_dst = jnp.where(real, kk * n + tok, TOP_K * n + lane)
    slot_dst = jnp.concatenate([slot_dst, TOP_K * n + lane], axis=0)
    slot_gate = jnp.where(real, jnp.take(gates.reshape(a), rows, mode="clip"), 0.0).reshape(nb, 1, blk)
    nb_real = (jnp.sum(padded) // blk).reshape(1).astype(jnp.int32)
    return block_expert, nb_real, slot_src.reshape(nb, 1, blk), slot_dst.reshape(nb + 1, 1, blk), slot_gate


def _combine_kernel(x_ref, h_ref, wsg_ref, wsu_ref, wsd_ref, *rest):
    y_refs, o_ref = rest[:TOP_K], rest[TOP_K]
    h = _from_token_major(h_ref[...])
    a = (jax.nn.silu(_dot(h, wsg_ref[...])) * _dot(h, wsu_ref[...])).astype(_BF16)
    routed = y_refs[0][...].astype(_F32)
    for k in range(1, TOP_K):
        routed = routed + y_refs[k][...].astype(_F32)
    o_ref[...] = x_ref[...] + _dot(a, wsd_ref[...]) + _from_token_major(routed)


def _combine(x1, h3, y3, ws_gate, ws_up, ws_down):
    n, d = x1.shape
    c = d // LANES
    tm = COMBINE_TM
    nt = n // tm
    de = ws_gate.shape[1]
    const = lambda r, cc: pl.BlockSpec((r, cc), lambda i: (0, 0))
    row = pl.BlockSpec((tm, d), lambda i: (i, 0))
    y_specs = [pl.BlockSpec((tm, c, LANES), functools.partial(lambda i, k: (k * nt + i, 0, 0), k=k))
               for k in range(TOP_K)]
    return pl.pallas_call(
        _combine_kernel,
        grid=(nt,),
        in_specs=[row, pl.BlockSpec((tm, c, LANES), lambda i: (i, 0, 0)), const(d, de), const(d, de), const(de, d)]
        + y_specs,
        out_specs=row,
        out_shape=jax.ShapeDtypeStruct((n, d), _F32),
        compiler_params=_cparams(("arbitrary",)),
        name="moe_combine",
    )(x1, h3, ws_gate.astype(_BF16), ws_up.astype(_BF16), ws_down.astype(_BF16), *([y3] * TOP_K))


def _moe_block(x1, norm_ffn_g, w_router, router_bias, we_gate, we_up, we_down, ws_gate, ws_up, ws_down):
    n = x1.shape[0]
    h3, top_idx_t, gates_t, counts = _router(x1, norm_ffn_g, w_router, router_bias)
    block_expert, nb_real, slot_src, slot_dst, slot_gate = _dispatch_plan(
        top_idx_t.T, gates_t.T, counts.reshape(N_EXPERTS), n)
    y3 = _moe_experts(h3, block_expert, nb_real, slot_src, slot_dst, slot_gate, we_gate, we_up, we_down)
    return _combine(x1, h3, y3, ws_gate, ws_up, ws_down)


def kernel(x, mem, norm_attn_g, w_in, nsa_pe_k, nsa_w_ck, nsa_pe_v, nsa_w_cv, nsa_q_g, nsa_k_g, rel_bias, fox_b_f, fox_q_g, fox_k_g, norm_mem_g, w_mem_kv, mem_q_g, mem_k_g, w_o_nsa, w_o_fox, w_o_mem, w_out, norm_ffn_g, w_router, router_bias, we_gate, we_up, we_down, ws_gate, ws_up, ws_down):
    b, t, d = x.shape
    assert norm_attn_g.shape[0] == 1, "single-layer problem"
    l = 0
    proj2d, o_nsa, o_fox, o_mem = _attention_mixers(
        x, mem, norm_attn_g[l], w_in[l], nsa_pe_k[l], nsa_w_ck[l], nsa_pe_v[l], nsa_w_cv[l], nsa_q_g[l],
        nsa_k_g[l], rel_bias, fox_b_f[l], fox_q_g[l], fox_k_g[l], norm_mem_g[l], w_mem_kv[l], mem_q_g[l],
        mem_k_g[l])
    x1 = _merge(x.reshape(b * t, d), o_nsa, o_fox, o_mem, proj2d, w_o_nsa[l], w_o_fox[l], w_o_mem[l], w_out[l])
    out = _moe_block(x1, norm_ffn_g[l], w_router[l], router_bias[l], we_gate[l], we_up[l], we_down[l],
                     ws_gate[l], ws_up[l], ws_down[l])
    return out.reshape(b, t, d)
```

```python
import functools
import math

import jax
import jax.numpy as jnp
import numpy as np
from jax import lax
from jax.experimental import pallas as pl
from jax.experimental.pallas import tpu as pltpu

D_MODEL = 2048
HEAD_DIM = 128
NSA_HEADS = 8
NSA_KV_HEADS = 2
NSA_GROUP = NSA_HEADS // NSA_KV_HEADS
FOX_HEADS = 4
MEM_HEADS = 4
CMP_LEN = 32
CMP_STRIDE = 16
SLC_LEN = 64
N_SELECT = 16
WINDOW = 512
NUM_BUCKETS = 32
MAX_DISTANCE = 128
N_BRANCHES = 3
N_EXPERTS = 64
TOP_K = 8
D_EXPERT = 512
ROUTED_SCALE = 2.5
ATTN_SCALE = HEAD_DIM ** -0.5
NEG_INF = -1e30
FORCE_SCORE = 1e4
RMS_EPS = 1e-6
LOG2E = math.log2(math.e)

NSA_Q_W = NSA_HEADS * HEAD_DIM
NSA_KV_W = NSA_KV_HEADS * HEAD_DIM
NSA_GATE_W = 3 * NSA_HEADS
FOX_W = FOX_HEADS * HEAD_DIM
MEM_W = MEM_HEADS * HEAD_DIM
MERGE_W = N_BRANCHES * D_MODEL

LANES = 128
VMEM_LIMIT_BYTES = 56 * 1024 * 1024

PROJ_TM = 1024
PROJ_TN = 1536
ATT_TQ = 128
FAR_GROUP = 4
FOX_TQ = 256
MEM_TQ = 512
MERGE_TM = 256
ROUTER_TM = 512
MOE_BLOCK = 256
COMBINE_TM = 256

CB_MERGE = 0
CB_ATT = N_BRANCHES * D_MODEL // LANES
CB_QNSA = CB_ATT
CB_KCMP = CB_ATT + 8
CB_VCMP = CB_ATT + 10
CB_KSLC = CB_ATT + 12
CB_VSLC = CB_ATT + 14
CB_KWIN = CB_ATT + 16
CB_VWIN = CB_ATT + 18
CB_FOXQ = CB_ATT + 20
CB_FOXK = CB_ATT + 24
CB_FOXV = CB_ATT + 28
CB_MEMQ = CB_ATT + 32
PROJ_COLS = (CB_ATT + 36) * LANES
SMALL_W = 2 * LANES
FOXF_LANE = 12

_BF16 = jnp.bfloat16
_F32 = jnp.float32


def _cparams(sem):
    return pltpu.CompilerParams(dimension_semantics=sem, vmem_limit_bytes=VMEM_LIMIT_BYTES)


def _dot(a, b):
    return jnp.dot(a, b, preferred_element_type=_F32)


def _dot_nt(a, b):
    return lax.dot_general(a, b, (((1,), (1,)), ((), ())), preferred_element_type=_F32)


def _dot_tn(a, b):
    return lax.dot_general(a, b, (((0,), (0,)), ((), ())), preferred_element_type=_F32)


def _lanes(parts):
    return jnp.concatenate(parts, axis=1)


def _to_token_major(x):
    c = x.shape[1] // LANES
    chunks = jnp.stack([x[:, j * LANES:(j + 1) * LANES] for j in range(c)], axis=0)
    return pltpu.einshape("ctl->tcl", chunks)


def _from_token_major(x3):
    xt = pltpu.einshape("tcl->ctl", x3)
    return _lanes([xt[j] for j in range(x3.shape[1])])


REORDER_TR = 128
REORDER_TC = 512


def _reorder_kernel(pieces, narrow, w_ref, o_ref, s_ref):
    off = 0
    for src, width in pieces:
        for c in range(0, width, REORDER_TC):
            o_ref[:, off + c:off + c + REORDER_TC] = w_ref[:, src + c:src + c + REORDER_TC].astype(o_ref.dtype)
        off += width
    s_ref[...] = jnp.zeros(s_ref.shape, s_ref.dtype)
    for src, width, dst in narrow:
        s_ref[:, dst:dst + width] = w_ref[:, src:src + width].astype(s_ref.dtype)


def _reorder_cast(w, pieces, narrow, narrow_cols):
    rows, cols = w.shape
    total = sum(width for _, width in pieces)
    assert rows % REORDER_TR == 0 and all(width % REORDER_TC == 0 for _, width in pieces)
    return pl.pallas_call(
        functools.partial(_reorder_kernel, pieces, narrow),
        grid=(rows // REORDER_TR,),
        in_specs=[pl.BlockSpec((REORDER_TR, cols), lambda i: (i, 0))],
        out_specs=[pl.BlockSpec((REORDER_TR, total), lambda i: (i, 0)),
                   pl.BlockSpec((REORDER_TR, narrow_cols), lambda i: (i, 0))],
        out_shape=[jax.ShapeDtypeStruct((rows, total), _BF16), jax.ShapeDtypeStruct((rows, narrow_cols), _BF16)],
        compiler_params=_cparams(("arbitrary",)),
        name="reorder_cast",
    )(w)


def _proj_kernel(norm_j0, has_small, x_ref, g_ref, w_ref, cg_ref, cf_ref, *rest):
    if has_small:
        ws_ref, o_ref, os_ref, h_ref = rest
    else:
        o_ref, h_ref = rest
    j = pl.program_id(1)

    @pl.when(j == 0)
    def _():
        x = x_ref[...]
        ms = jnp.mean(x * x, axis=-1, keepdims=True)
        h = (x * lax.rsqrt(ms + RMS_EPS) * g_ref[...]).astype(_BF16)
        h_ref[...] = h
        if has_small:
            os_ref[...] = _dot(h, ws_ref[...])

    y = _dot(h_ref[...], w_ref[...])
    tn = y.shape[1]

    @pl.when(j >= norm_j0)
    def _():
        for c in range(tn // LANES):
            sl = slice(c * LANES, (c + 1) * LANES)
            yh = y[:, sl]
            ms = jnp.mean(yh * yh, axis=-1, keepdims=True)
            scale = jnp.where(cf_ref[:, sl] > 0.0, lax.rsqrt(ms + RMS_EPS), 1.0)
            o_ref[:, sl] = (yh * scale * cg_ref[:, sl]).astype(o_ref.dtype)

    @pl.when(j < norm_j0)
    def _():
        o_ref[...] = y.astype(o_ref.dtype)


def _rms_project(x2d, g, w, col_gain, col_flag, n_plain_cols, w_small, tm, tn):
    n, d = x2d.shape
    c = w.shape[1]
    has_small = w_small is not None
    in_specs = [
        pl.BlockSpec((tm, d), lambda i, j: (i, 0)),
        pl.BlockSpec((1, d), lambda i, j: (0, 0)),
        pl.BlockSpec((d, tn), lambda i, j: (0, j)),
        pl.BlockSpec((1, tn), lambda i, j: (0, j)),
        pl.BlockSpec((1, tn), lambda i, j: (0, j)),
    ]
    args = [x2d, g.reshape(1, d), w, col_gain.reshape(1, c), col_flag.reshape(1, c)]
    out_shape = [jax.ShapeDtypeStruct((n, c), _BF16)]
    out_specs = [pl.BlockSpec((tm, tn), lambda i, j: (i, j))]
    if has_small:
        ws = w_small.shape[1]
        in_specs.append(pl.BlockSpec((d, ws), lambda i, j: (0, 0)))
        args.append(w_small)
        out_shape.append(jax.ShapeDtypeStruct((n, ws), _F32))
        out_specs.append(pl.BlockSpec((tm, ws), lambda i, j: (i, 0)))
    assert n % tm == 0 and c % tn == 0 and n_plain_cols % tn == 0
    res = pl.pallas_call(
        functools.partial(_proj_kernel, n_plain_cols // tn, has_small),
        grid=(n // tm, c // tn),
        in_specs=in_specs,
        out_specs=out_specs,
        out_shape=out_shape,
        scratch_shapes=[pltpu.VMEM((tm, d), _BF16)],
        compiler_params=_cparams(("arbitrary", "arbitrary")),
        name="rms_project",
    )(*args)
    return res if has_small else res[0]


def _cmp_kernel(x_ref, wlo_ref, whi_ref, pelo_ref, pehi_ref, kg_ref, o_ref):
    j = pl.program_id(1)
    x = x_ref[0, 0]
    nchunk = x.shape[0]
    ylo = _dot(x, wlo_ref[0])
    yhi = _dot(x, whi_ref[0])
    pe = _dot(pelo_ref[0], wlo_ref[0]) + _dot(pehi_ref[0], whi_ref[0])
    y = ylo + pltpu.roll(yhi, nchunk - 1, 0) + pe[0:1, :]

    @pl.when(j < NSA_KV_HEADS)
    def _():
        ms = jnp.mean(y * y, axis=-1, keepdims=True)
        o_ref[0, 0] = (y * lax.rsqrt(ms + RMS_EPS) * kg_ref[...]).astype(o_ref.dtype)

    @pl.when(j >= NSA_KV_HEADS)
    def _():
        o_ref[0, 0] = y.T.astype(o_ref.dtype)


def _nsa_compress(cmp_in, w_ck, w_cv, pe_k, pe_v, k_g):
    b, nj, t, dk = cmp_in.shape
    nchunk = t // CMP_STRIDE
    half = CMP_LEN // 2
    assert nchunk == dk
    x = cmp_in.reshape(b, nj, nchunk, CMP_STRIDE * dk)

    def halves(w):
        return (w[:half].reshape(half * dk, dk).astype(_BF16),
                w[half:].reshape(half * dk, dk).astype(_BF16))

    klo, khi = halves(w_ck)
    vlo, vhi = halves(w_cv)
    wlo = jnp.stack([klo, vlo])
    whi = jnp.stack([khi, vhi])

    def pe_halves(pe):
        lo = jnp.broadcast_to(pe[:half].reshape(1, half * dk), (8, half * dk)).astype(_BF16)
        hi = jnp.broadcast_to(pe[half:].reshape(1, half * dk), (8, half * dk)).astype(_BF16)
        return lo, hi

    pklo, pkhi = pe_halves(pe_k)
    pvlo, pvhi = pe_halves(pe_v)
    pelo = jnp.stack([pklo, pvlo])
    pehi = jnp.stack([pkhi, pvhi])
    kv = lambda bb, j: (j // NSA_KV_HEADS, 0, 0)
    return pl.pallas_call(
        _cmp_kernel,
        grid=(b, nj),
        in_specs=[
            pl.BlockSpec((1, 1, nchunk, CMP_STRIDE * dk), lambda bb, j: (bb, j, 0, 0)),
            pl.BlockSpec((1, half * dk, dk), kv),
            pl.BlockSpec((1, half * dk, dk), kv),
            pl.BlockSpec((1, 8, half * dk), kv),
            pl.BlockSpec((1, 8, half * dk), kv),
            pl.BlockSpec((1, dk), lambda bb, j: (0, 0)),
        ],
        out_specs=pl.BlockSpec((1, 1, nchunk, dk), lambda bb, j: (bb, j, 0, 0)),
        out_shape=jax.ShapeDtypeStruct((b, nj, nchunk, dk), _BF16),
        compiler_params=_cparams(("arbitrary", "arbitrary")),
        name="nsa_compress",
    )(x, wlo, whi, pelo, pehi, k_g.reshape(1, dk).astype(_F32))


def _t_update(state, tiles):
    m, l, acc = state
    masked = [s if mask is None else jnp.where(mask, s, NEG_INF) for s, mask, _ in tiles]
    m_new = m
    for sm in masked:
        m_new = jnp.maximum(m_new, jnp.max(sm, axis=0, keepdims=True))
    alpha = jnp.exp2(m - m_new)
    l_new = alpha * l
    acc_new = alpha * acc
    for sm, (_, _, v) in zip(masked, tiles):
        e = jnp.exp2(sm - m_new)
        l_new = l_new + jnp.sum(e, axis=0, keepdims=True)
        acc_new = acc_new + _dot_tn(v, e.astype(_BF16))
    return m_new, l_new, acc_new


def _t_init(nq):
    return (jnp.full((1, nq), NEG_INF, _F32), jnp.zeros((1, nq), _F32), jnp.zeros((HEAD_DIM, nq), _F32))


def _t_finish(state):
    _, l, acc = state
    return acc / l


def _nsa_kernel(q_ref, kvc_ref, ks_ref, vs_ref, kw_ref, vw_ref, gl_ref,
                wt_ref, dt_ref, ovt_ref, ext_ref, o_ref, sel_ref):
    i = pl.program_id(1)
    tq = ATT_TQ
    hg = NSA_GROUP
    nq = hg * tq
    n_slc = ovt_ref.shape[0]
    groups = range(NSA_KV_HEADS)

    key = lax.broadcasted_iota(jnp.int32, (LANES, nq), 0)
    qry = lax.broadcasted_iota(jnp.int32, (LANES, nq), 1) & (tq - 1)

    def gcols(g):
        return slice(g * LANES, (g + 1) * LANES)

    def ktile(ref, g, kt):
        return ref[0, pl.ds(pl.multiple_of(kt * LANES, LANES), LANES), gcols(g)]

    def near_bias(g, d):
        return _lanes([dt_ref[g * hg + h, d] for h in range(hg)])

    q = q_ref[0]
    qts = [_lanes([q[:, (g * hg + h) * HEAD_DIM:(g * hg + h + 1) * HEAD_DIM].astype(_F32).T
                   for h in range(hg)]).astype(_BF16) for g in groups]

    woff = pl.multiple_of(wt_ref.shape[1] - LANES - 8 - (LANES // CMP_STRIDE) * i, 8)
    s_c = [_dot(kvc_ref[0, g], qts[g]) + _lanes([wt_ref[g * hg + h, pl.ds(woff, LANES), :] for h in range(hg)])
           for g in groups]
    n_win = WINDOW // tq
    win_tiles = [[] for _ in groups]
    for g in groups:
        for d in range(n_win, -1, -1):
            ktd = jnp.maximum(i - d, 0)
            sc = _dot(ktile(kw_ref, g, ktd), qts[g])
            if d <= 1:
                sc = sc + near_bias(g, d)
            if d == n_win:
                mk = (qry < key) & (i >= d)
            elif d == 0:
                mk = key <= qry
            else:
                mk = jnp.broadcast_to(i >= d, (LANES, nq))
            win_tiles[g].append((sc, mk, ktile(vw_ref, g, ktd)))

    mask_c = (i * tq + qry) >= (CMP_STRIDE * key + CMP_LEN - 1)
    blk = lax.broadcasted_iota(jnp.int32, (n_slc, tq), 0)
    tpos = i * tq + lax.broadcasted_iota(jnp.int32, (n_slc, tq), 1)
    cur = tpos // SLC_LEN
    valid = blk * SLC_LEN <= tpos
    forced = (blk == 0) | (blk == cur) | (blk == cur - 1)
    o_c = []
    for g in groups:
        s = s_c[g]
        mx = jnp.max(jnp.where(mask_c, s, NEG_INF), axis=0, keepdims=True)
        e = jnp.where(mask_c, jnp.exp2(s - mx), 0.0)
        l = jnp.sum(e, axis=0, keepdims=True)
        p_c = e / jnp.where(l > 0.0, l, 1.0)
        o_c.append(_dot(kvc_ref[0, NSA_KV_HEADS + g], p_c.astype(_BF16)))
        ps = p_c[:, 0:tq]
        for h in range(1, hg):
            ps = ps + p_c[:, h * tq:(h + 1) * tq]
        ps_hi = ps.astype(_BF16)
        ps_lo = (ps - ps_hi.astype(_F32)).astype(_BF16)
        imp = _dot(ovt_ref[...], ps_hi) + _dot(ovt_ref[...], ps_lo)
        score = jnp.where(valid, jnp.where(forced, FORCE_SCORE, imp), -1.0)
        rank = jnp.zeros((n_slc, tq), _F32)
        for k in range(n_slc):
            ck = score[k:k + 1, :]
            beats = (ck > score) | ((ck == score) & (blk > k))
            rank = rank + jnp.where(beats, 1.0, 0.0)
        sel = jnp.where(rank < float(min(N_SELECT, n_slc)), 1.0, 0.0).astype(_BF16)
        selfull = _dot(ext_ref[...], sel)
        for kt in range(ext_ref.shape[0] // LANES):
            sel_ref[g, kt] = selfull[kt * LANES:(kt + 1) * LANES, :]

    def sel_mask(g, kt, ok=True):
        return _lanes([sel_ref[g, kt]] * hg) > jnp.where(ok, 0.5, 2.0)

    o_w = [_t_finish(_t_update(_t_init(nq), win_tiles[g])) for g in groups]

    n_far = jnp.maximum(i - 1, 0)

    def far_body(j, sts):
        k0 = pl.multiple_of(j * (FAR_GROUP * LANES), FAR_GROUP * LANES)
        sgs = [_dot(ks_ref[0, pl.ds(k0, FAR_GROUP * LANES), gcols(g)], qts[g]) for g in groups]
        out = []
        for g in groups:
            tiles = []
            for u in range(FAR_GROUP):
                kt = FAR_GROUP * j + u
                tiles.append((sgs[g][u * LANES:(u + 1) * LANES], sel_mask(g, kt, kt < n_far), ktile(vs_ref, g, kt)))
            out.append(_t_update(sts[g], tiles))
        return tuple(out)

    sts = lax.fori_loop(0, (n_far + FAR_GROUP - 1) // FAR_GROUP, far_body, tuple(_t_init(nq) for _ in groups))
    kt1 = jnp.maximum(i - 1, 0)
    near = [(_dot(ktile(ks_ref, g, kt1), qts[g]) + near_bias(g, 1), _dot(ktile(ks_ref, g, i), qts[g]) + near_bias(g, 0))
            for g in groups]
    o_s = [_t_finish(_t_update(sts[g], [
        (near[g][0], sel_mask(g, kt1, i >= 1), ktile(vs_ref, g, kt1)),
        (near[g][1], sel_mask(g, i) & (key <= qry), ktile(vs_ref, g, i))])) for g in groups]

    for g in groups:
        glt = jax.nn.sigmoid(gl_ref[0, :, gcols(g)]).T
        grow = lambda br: _lanes([glt[br * hg + h:br * hg + h + 1, :] for h in range(hg)])
        o = grow(0) * o_c[g] + grow(1) * o_s[g] + grow(2) * o_w[g]
        for h in range(hg):
            hd = (g * hg + h) * HEAD_DIM
            o_ref[0, :, hd:hd + HEAD_DIM] = o[:, h * tq:(h + 1) * tq].T.astype(o_ref.dtype)


def _t5_bucket(rel):
    n = np.maximum(rel, 0)
    max_exact = NUM_BUCKETS // 2
    ratio = np.maximum(n, 1).astype(np.float32) / np.float32(max_exact)
    log_ratio = np.log(ratio) / np.float32(math.log(MAX_DISTANCE / max_exact))
    large = np.minimum(max_exact + (log_ratio * np.float32(NUM_BUCKETS - max_exact)).astype(np.int32),
                       NUM_BUCKETS - 1)
    return np.where(n < max_exact, n, large).astype(np.int32)


def _bias_lookup(rb, bucket):
    bk = jnp.asarray(bucket.astype(np.int8))[None]
    ex = (slice(None),) + (None,) * bucket.ndim
    tab = jnp.broadcast_to(rb[:, 0][ex], (rb.shape[0],) + bucket.shape)
    for k in range(1, NUM_BUCKETS):
        tab = jnp.where(bk == k, rb[:, k][ex], tab)
    return tab


def _nsa_tables(rel_bias):
    rb = rel_bias.astype(_F32).T * LOG2E
    r = np.arange(ATT_TQ)
    rel_d = (np.arange(2) * ATT_TQ)[:, None, None] + r[None, None, :] - r[None, :, None]
    assert _t5_bucket(np.array([ATT_TQ + 1]))[0] == NUM_BUCKETS - 1
    u = np.arange(2 * LANES)
    rel_w = r[None, :] - CMP_STRIDE * (u[:, None] - (LANES - 8)) - (CMP_LEN - 1)
    far = rb[:, NUM_BUCKETS - 1]
    dt = _bias_lookup(rb, _t5_bucket(rel_d)) - far[:, None, None, None]
    return dt, _bias_lookup(rb, _t5_bucket(rel_w))


def _nsa_attention(proj, small, kvc, rel_bias):
    b, t, _ = proj.shape
    g, hg, tq = NSA_KV_HEADS, NSA_GROUP, ATT_TQ
    n_slc = t // SLC_LEN
    n_cmp = (t - CMP_LEN) // CMP_STRIDE + 1
    nt = t // LANES
    assert n_cmp <= LANES and kvc.shape[2] == LANES and n_slc % 16 == 0 and tq == LANES
    assert nt % FAR_GROUP == 0
    assert LANES - 8 - (LANES // CMP_STRIDE) * (t // tq - 1) >= 0
    dt, wt = _nsa_tables(rel_bias)
    cstart = np.arange(LANES) * CMP_STRIDE
    sstart = np.arange(n_slc) * SLC_LEN
    overlap = np.clip(np.minimum(cstart[None, :] + CMP_LEN, sstart[:, None] + SLC_LEN)
                      - np.maximum(cstart[None, :], sstart[:, None]), 0, None).astype(np.float32) / CMP_LEN
    overlap[:, n_cmp:] = 0.0
    ovt = jnp.asarray(overlap, _BF16)
    ext = jnp.asarray(np.arange(t)[:, None] // SLC_LEN == np.arange(n_slc)[None, :], _BF16)

    gw = g * LANES
    seq = lambda cb: pl.BlockSpec((1, t, gw), lambda bb, ii: (bb, 0, cb * LANES // gw))
    heads = g * hg
    return pl.pallas_call(
        _nsa_kernel,
        grid=(b, t // tq),
        in_specs=[
            pl.BlockSpec((1, tq, NSA_Q_W), lambda bb, ii: (bb, ii, CB_QNSA * LANES // NSA_Q_W)),
            pl.BlockSpec((1, 2 * g, LANES, HEAD_DIM), lambda bb, ii: (bb, 0, 0, 0)),
            seq(CB_KSLC), seq(CB_VSLC), seq(CB_KWIN), seq(CB_VWIN),
            pl.BlockSpec((1, tq, SMALL_W), lambda bb, ii: (bb, ii, 0)),
            pl.BlockSpec((heads, 2 * LANES, tq), lambda bb, ii: (0, 0, 0)),
            pl.BlockSpec((heads, 2, LANES, tq), lambda bb, ii: (0, 0, 0, 0)),
            pl.BlockSpec((n_slc, LANES), lambda bb, ii: (0, 0)),
            pl.BlockSpec((t, n_slc), lambda bb, ii: (0, 0)),
        ],
        out_specs=pl.BlockSpec((1, tq, NSA_Q_W), lambda bb, ii: (bb, ii, 0)),
        out_shape=jax.ShapeDtypeStruct((b, t, NSA_Q_W), _BF16),
        scratch_shapes=[pltpu.VMEM((g, nt, LANES, tq), _F32)],
        compiler_params=_cparams(("arbitrary", "arbitrary")),
        name="nsa_attention",
    )(proj, kvc, proj, proj, proj, proj, small, wt, dt, ovt, ext)


def _fox_cum_kernel(s_ref, b_ref, col_ref, row_ref):
    z = s_ref[0] + b_ref[...]
    lf = (jnp.minimum(z, 0.0) - jnp.log1p(jnp.exp(-jnp.abs(z)))) * LOG2E
    x = lf.T
    t = x.shape[1]
    lane = lax.broadcasted_iota(jnp.int32, x.shape, 1)
    sh = 1
    while sh < t:
        x = x + jnp.where(lane >= sh, pltpu.roll(x, sh, 1), 0.0)
        sh *= 2
    row_ref[0] = x[8:16, :]
    col_ref[0] = x.T


def _fox_cumsum(small, b_f):
    b, t, _ = small.shape
    bvec = jnp.zeros((1, LANES), _F32).at[0, FOXF_LANE:FOXF_LANE + FOX_HEADS].set(b_f.astype(_F32))
    return pl.pallas_call(
        _fox_cum_kernel,
        grid=(b,),
        in_specs=[pl.BlockSpec((1, t, LANES), lambda bb: (bb, 0, 1)),
                  pl.BlockSpec((1, LANES), lambda bb: (0, 0))],
        out_specs=[pl.BlockSpec((1, t, LANES), lambda bb: (bb, 0, 0)),
                   pl.BlockSpec((1, 8, t), lambda bb: (bb, 0, 0))],
        out_shape=[jax.ShapeDtypeStruct((b, t, LANES), _F32), jax.ShapeDtypeStruct((b, 8, t), _F32)],
        compiler_params=_cparams(("arbitrary",)),
        name="fox_cumsum",
    )(small, bvec)


def _fox_kernel(q_ref, k_ref, v_ref, cc_ref, cr_ref, o_ref):
    i = pl.program_id(1)
    tq = FOX_TQ
    key = lax.broadcasted_iota(jnp.int32, (tq, tq), 0)
    qry = lax.broadcasted_iota(jnp.int32, (tq, tq), 1)
    heads = range(FOX_HEADS)
    hsl = [slice(h * HEAD_DIM, (h + 1) * HEAD_DIM) for h in heads]
    qts = [q_ref[0, :, hsl[h]].astype(_F32).T.astype(_BF16) for h in heads]
    cqs = [cr_ref[0, FOXF_LANE - 8 + h:FOXF_LANE - 8 + h + 1, :] for h in heads]

    def keys(kt):
        return pl.ds(pl.multiple_of(kt * tq, tq), tq)

    def logits(h, kt):
        ck = cc_ref[0, keys(kt), FOXF_LANE + h:FOXF_LANE + h + 1]
        return _dot(k_ref[0, keys(kt), hsl[h]], qts[h]) + (cqs[h] - ck)

    def body(kt, sts):
        ss = [logits(h, kt) for h in heads]
        return tuple(_t_update(sts[h], [(ss[h], None, v_ref[0, keys(kt), hsl[h]])]) for h in heads)

    sts = lax.fori_loop(0, i, body, tuple(_t_init(tq) for _ in heads))
    ss = [logits(h, i) for h in heads]
    for h in heads:
        st = _t_update(sts[h], [(ss[h], key <= qry, v_ref[0, keys(i), hsl[h]])])
        o_ref[0, :, hsl[h]] = _t_finish(st).T.astype(o_ref.dtype)


def _fox_attention(proj, cum_col, cum_row):
    b, t, _ = proj.shape
    tq = FOX_TQ
    w = FOX_W
    return pl.pallas_call(
        _fox_kernel,
        grid=(b, t // tq),
        in_specs=[
            pl.BlockSpec((1, tq, w), lambda bb, ii: (bb, ii, CB_FOXQ * LANES // w)),
            pl.BlockSpec((1, t, w), lambda bb, ii: (bb, 0, CB_FOXK * LANES // w)),
            pl.BlockSpec((1, t, w), lambda bb, ii: (bb, 0, CB_FOXV * LANES // w)),
            pl.BlockSpec((1, t, LANES), lambda bb, ii: (bb, 0, 0)),
            pl.BlockSpec((1, 8, tq), lambda bb, ii: (bb, 0, ii)),
        ],
        out_specs=pl.BlockSpec((1, tq, w), lambda bb, ii: (bb, ii, 0)),
        out_shape=jax.ShapeDtypeStruct((b, t, w), _BF16),
        compiler_params=_cparams(("arbitrary", "arbitrary")),
        name="fox_attention",
    )(proj, proj, proj, cum_col, cum_row)


def _mem_kernel(q_ref, kv_ref, o_ref):
    for h in range(MEM_HEADS):
        hs = slice(h * HEAD_DIM, (h + 1) * HEAD_DIM)
        vs = slice(MEM_W + h * HEAD_DIM, MEM_W + (h + 1) * HEAD_DIM)
        s = _dot_nt(q_ref[0, :, hs], kv_ref[0, :, hs])
        e = jnp.exp2(s - jnp.max(s, axis=1, keepdims=True))
        p = e / jnp.sum(e, axis=1, keepdims=True)
        o_ref[0, :, hs] = _dot(p.astype(_BF16), kv_ref[0, :, vs]).astype(o_ref.dtype)


def _mem_attention(proj, memkv):
    b, t, _ = proj.shape
    m = memkv.shape[1]
    tq = MEM_TQ
    return pl.pallas_call(
        _mem_kernel,
        grid=(b, t // tq),
        in_specs=[pl.BlockSpec((1, tq, MEM_W), lambda bb, ii: (bb, ii, CB_MEMQ * LANES // MEM_W)),
                  pl.BlockSpec((1, m, 2 * MEM_W), lambda bb, ii: (bb, 0, 0))],
        out_specs=pl.BlockSpec((1, tq, MEM_W), lambda bb, ii: (bb, ii, 0)),
        out_shape=jax.ShapeDtypeStruct((b, t, MEM_W), _BF16),
        compiler_params=_cparams(("arbitrary", "arbitrary")),
        name="mem_attention",
    )(proj, memkv)


def _merge_kernel(x_ref, on_ref, of_ref, om_ref, g0_ref, g1_ref, g2_ref, wn_ref, wf_ref, wm_ref, wo_ref,
                  o_ref, z_ref):
    tn = 512
    for c in range(D_MODEL // tn):
        cs = slice(c * tn, (c + 1) * tn)
        z = (jax.nn.sigmoid(g0_ref[:, cs].astype(_F32)) * _dot(on_ref[...], wn_ref[:, cs])
             + jax.nn.sigmoid(g1_ref[:, cs].astype(_F32)) * _dot(of_ref[...], wf_ref[:, cs])
             + jax.nn.sigmoid(g2_ref[:, cs].astype(_F32)) * _dot(om_ref[...], wm_ref[:, cs]))
        z_ref[:, cs] = z.astype(_BF16)
    o_ref[...] = x_ref[...] + _dot(z_ref[...], wo_ref[...])


def _merge(x2d, o_nsa, o_fox, o_mem, proj2d, w_o_nsa, w_o_fox, w_o_mem, w_out):
    n, d = x2d.shape
    tm = MERGE_TM
    gcb = CB_MERGE * LANES // d
    const = lambda r, c: pl.BlockSpec((r, c), lambda i: (0, 0))
    return pl.pallas_call(
        _merge_kernel,
        grid=(n // tm,),
        in_specs=[
            pl.BlockSpec((tm, d), lambda i: (i, 0)),
            pl.BlockSpec((tm, NSA_Q_W), lambda i: (i, 0)),
            pl.BlockSpec((tm, FOX_W), lambda i: (i, 0)),
            pl.BlockSpec((tm, MEM_W), lambda i: (i, 0)),
            pl.BlockSpec((tm, d), lambda i: (i, gcb)),
            pl.BlockSpec((tm, d), lambda i: (i, gcb + 1)),
            pl.BlockSpec((tm, d), lambda i: (i, gcb + 2)),
            const(NSA_Q_W, d), const(FOX_W, d), const(MEM_W, d), const(d, d),
        ],
        out_specs=pl.BlockSpec((tm, d), lambda i: (i, 0)),
        out_shape=jax.ShapeDtypeStruct((n, d), _F32),
        scratch_shapes=[pltpu.VMEM((tm, d), _BF16)],
        compiler_params=_cparams(("arbitrary",)),
        name="merge_out",
    )(x2d, o_nsa, o_fox, o_mem, proj2d, proj2d, proj2d,
      w_o_nsa.astype(_BF16), w_o_fox.astype(_BF16), w_o_mem.astype(_BF16), w_out.astype(_BF16))


def _nsa_inputs(x, norm_attn_g, w_in, nsa_pe_k, nsa_w_ck, nsa_pe_v, nsa_w_cv, nsa_q_g, nsa_k_g,
                fox_q_g, fox_k_g, mem_q_g):
    b, t, d = x.shape
    ones = lambda k: jnp.ones((k * HEAD_DIM,), _F32)
    zeros = lambda k: jnp.zeros((k * HEAD_DIM,), _F32)
    tile = lambda gv, k: jnp.tile(gv.astype(_F32), k)
    qs = ATTN_SCALE * LOG2E

    g_end = NSA_Q_W + 6 * NSA_KV_W
    f_off = g_end + NSA_GATE_W + 3 * FOX_W
    m_off = f_off + FOX_HEADS + MEM_W
    narrow = tuple((g_end + (br * NSA_KV_HEADS + gg) * NSA_GROUP, NSA_GROUP, gg * LANES + br * NSA_GROUP)
                   for br in range(3) for gg in range(NSA_KV_HEADS))
    narrow += ((f_off, FOX_HEADS, LANES + FOXF_LANE),)
    w_all, w_small = _reorder_cast(
        w_in, ((m_off, MERGE_W), (0, g_end), (g_end + NSA_GATE_W, 3 * FOX_W), (f_off + FOX_HEADS, MEM_W)),
        narrow, SMALL_W)
    col_gain = jnp.concatenate([
        jnp.ones((MERGE_W,), _F32),
        tile(nsa_q_g, NSA_HEADS) * qs, ones(2), ones(2), tile(nsa_k_g, 2), ones(2), tile(nsa_k_g, 2), ones(2),
        tile(fox_q_g, FOX_HEADS) * qs, tile(fox_k_g, FOX_HEADS), ones(FOX_HEADS),
        tile(mem_q_g, MEM_HEADS) * qs])
    col_flag = jnp.concatenate([
        jnp.zeros((MERGE_W,), _F32),
        ones(NSA_HEADS), zeros(2), zeros(2), ones(2), zeros(2), ones(2), zeros(2),
        ones(FOX_HEADS), ones(FOX_HEADS), zeros(FOX_HEADS), ones(MEM_HEADS)])

    x2d = x.reshape(b * t, d)
    proj2d, small2d = _rms_project(x2d, norm_attn_g, w_all, col_gain, col_flag, CB_ATT * LANES, w_small,
                                   PROJ_TM, PROJ_TN)
    proj = proj2d.reshape(b, t, PROJ_COLS)
    small = small2d.reshape(b, t, SMALL_W)

    cmp_in = proj[:, :, CB_KCMP * LANES:(CB_KCMP + 4) * LANES].reshape(b, t, 4, HEAD_DIM).transpose(0, 2, 1, 3)
    kvc = _nsa_compress(cmp_in, nsa_w_ck, nsa_w_cv, nsa_pe_k, nsa_pe_v, nsa_k_g)
    return proj, small, kvc


def _attention_mixers(x, mem, norm_attn_g, w_in, nsa_pe_k, nsa_w_ck, nsa_pe_v, nsa_w_cv, nsa_q_g, nsa_k_g,
                      rel_bias, fox_b_f, fox_q_g, fox_k_g, norm_mem_g, w_mem_kv, mem_q_g, mem_k_g):
    b, t, d = x.shape
    ones = lambda k: jnp.ones((k * HEAD_DIM,), _F32)
    zeros = lambda k: jnp.zeros((k * HEAD_DIM,), _F32)
    tile = lambda gv, k: jnp.tile(gv.astype(_F32), k)
    proj, small, kvc = _nsa_inputs(x, norm_attn_g, w_in, nsa_pe_k, nsa_w_ck, nsa_pe_v, nsa_w_cv, nsa_q_g, nsa_k_g,
                                   fox_q_g, fox_k_g, mem_q_g)
    proj2d = proj.reshape(b * t, PROJ_COLS)
    o_nsa = _nsa_attention(proj, small, kvc, rel_bias)

    cum_col, cum_row = _fox_cumsum(small, fox_b_f)
    o_fox = _fox_attention(proj, cum_col, cum_row)

    mm = mem.shape[1]
    mem_gain = jnp.concatenate([tile(mem_k_g, MEM_HEADS), ones(MEM_HEADS)])
    mem_flag = jnp.concatenate([ones(MEM_HEADS), zeros(MEM_HEADS)])
    memkv = _rms_project(mem.reshape(b * mm, d), norm_mem_g, w_mem_kv.astype(_BF16), mem_gain, mem_flag,
                         0, None, mm, MEM_W).reshape(b, mm, 2 * MEM_W)
    o_mem = _mem_attention(proj, memkv)
    return proj2d, o_nsa.reshape(b * t, NSA_Q_W), o_fox.reshape(b * t, FOX_W), o_mem.reshape(b * t, MEM_W)


def _router_kernel(x_ref, g_ref, whi_ref, wlo_ref, b_ref, h_ref, idx_ref, gate_ref, cnt_ref):
    x = x_ref[...]
    ms = jnp.mean(x * x, axis=-1, keepdims=True)
    h = x * lax.rsqrt(ms + RMS_EPS) * g_ref[...]
    h_hi = h.astype(_BF16)
    h_ref[...] = _to_token_major(h_hi)
    h_lo = (h - h_hi.astype(_F32)).astype(_BF16)
    logits = _dot_nt(whi_ref[...], h_hi) + _dot_nt(whi_ref[...], h_lo) + _dot_nt(wlo_ref[...], h_hi)
    scores = jax.nn.sigmoid(logits)
    sb = scores + b_ref[...]
    eidx = lax.broadcasted_iota(jnp.int32, sb.shape, 0)
    idxs, vals = [], []
    for _ in range(TOP_K):
        m = jnp.max(sb, axis=0, keepdims=True)
        idx = jnp.min(jnp.where(sb == m, eidx, N_EXPERTS), axis=0, keepdims=True)
        hit = eidx == idx
        vals.append(jnp.sum(jnp.where(hit, scores, 0.0), axis=0, keepdims=True))
        idxs.append(idx)
        sb = jnp.where(hit, NEG_INF, sb)
    top_s = jnp.concatenate(vals, axis=0)
    idx_ref[...] = jnp.concatenate(idxs, axis=0)
    gate_ref[...] = top_s / jnp.sum(top_s, axis=0, keepdims=True) * ROUTED_SCALE

    @pl.when(pl.program_id(0) == 0)
    def _():
        cnt_ref[...] = jnp.zeros(cnt_ref.shape, cnt_ref.dtype)

    picked = jnp.where(sb < 0.5 * NEG_INF, 1.0, 0.0)
    cnt_ref[...] += jnp.sum(picked, axis=1, keepdims=True)


def _router(x1, norm_g, w_router, router_bias):
    n, d = x1.shape
    tm = ROUTER_TM
    wt = w_router.astype(_F32).T
    w_hi = wt.astype(_BF16)
    w_lo = (wt - w_hi.astype(_F32)).astype(_BF16)
    const = lambda r, c: pl.BlockSpec((r, c), lambda i: (0, 0))
    return pl.pallas_call(
        _router_kernel,
        grid=(n // tm,),
        in_specs=[pl.BlockSpec((tm, d), lambda i: (i, 0)), const(1, d), const(N_EXPERTS, d), const(N_EXPERTS, d),
                  const(N_EXPERTS, 1)],
        out_specs=[pl.BlockSpec((tm, d // LANES, LANES), lambda i: (i, 0, 0)),
                   pl.BlockSpec((TOP_K, tm), lambda i: (0, i)),
                   pl.BlockSpec((TOP_K, tm), lambda i: (0, i)),
                   const(N_EXPERTS, 1)],
        out_shape=[jax.ShapeDtypeStruct((n, d // LANES, LANES), _BF16), jax.ShapeDtypeStruct((TOP_K, n), jnp.int32),
                   jax.ShapeDtypeStruct((TOP_K, n), _F32), jax.ShapeDtypeStruct((N_EXPERTS, 1), _F32)],
        compiler_params=_cparams(("arbitrary",)),
        name="moe_router",
    )(x1, norm_g.reshape(1, d).astype(_F32), w_hi, w_lo, router_bias.reshape(N_EXPERTS, 1).astype(_F32))


def _moe_kernel(be_ref, nbr_ref, src_hbm, dst_hbm, h_hbm, sg_ref, wg_ref, wu_ref, wd_ref, y_hbm,
                xbuf, ybuf, xmat, sidx, didx, wg_bf, wu_bf, wd_bf, gsem, ssem, isem):
    n = pl.program_id(0)
    nbr = nbr_ref[0]
    last = nbr - 1
    slot = n % 2
    other = 1 - slot
    blk = MOE_BLOCK
    dump_block = dst_hbm.shape[0] - 1
    ring = didx.shape[0]

    def src_copy(block, s):
        return pltpu.make_async_copy(src_hbm.at[block], sidx.at[s], isem.at[0, s])

    def dst_copy(block, s3):
        return pltpu.make_async_copy(dst_hbm.at[block], didx.at[s3], isem.at[1, s3])

    def issue_gather(s):
        for r in range(blk):
            tok = sidx[s, 0, r]
            pltpu.make_async_copy(h_hbm.at[tok], xbuf.at[s, r], gsem.at[s]).start(priority=r % 2)

    def issue_scatter(s, s3):
        for r in range(blk):
            row = didx[s3, 0, r]
            pltpu.make_async_copy(ybuf.at[s, r], y_hbm.at[row], ssem.at[s]).start(priority=r % 2)

    def wait_rows(buf, sem, s):
        pltpu.make_async_copy(buf.at[s], buf.at[s], sem.at[s]).wait()

    @pl.when(n < nbr)
    def _():
        @pl.when(n == 0)
        def _():
            src_copy(0, 0).start()
            dst_copy(0, 0).start()
            dst_copy(dump_block, ring - 1).start()
            ybuf[1] = jnp.zeros(ybuf.shape[1:], ybuf.dtype)
            src_copy(0, 0).wait()
            dst_copy(0, 0).wait()
            dst_copy(dump_block, ring - 1).wait()
            issue_gather(0)
            nxt0 = jnp.minimum(1, last)
            src_copy(nxt0, 1).start()
            dst_copy(nxt0, 1).start()

        wait_rows(xbuf, gsem, slot)

        @pl.when(n < last)
        def _():
            nn = jnp.minimum(n + 2, last)
            src_copy(nn, slot).start()
            dst_copy(nn, (n + 2) % ring).start()

        changed = (n == 0) | (be_ref[n] != be_ref[jnp.maximum(n - 1, 0)])

        @pl.when(changed)
        def _():
            wg_bf[...] = wg_ref[0].astype(_BF16)
            wu_bf[...] = wu_ref[0].astype(_BF16)
            wd_bf[...] = wd_ref[0].astype(_BF16)

        @pl.when(n >= 1)
        def _():
            wait_rows(ybuf, ssem, slot)

        nxt = jnp.minimum(n + 1, last)
        src_copy(nxt, other).wait()
        dst_copy(nxt, (n + 1) % ring).wait()

        issue_scatter(other, (n + ring - 1) % ring)
        xmat[...] = _from_token_major(xbuf[slot])
        issue_gather(other)
        gcol = jnp.broadcast_to(sg_ref[0], (LANES, blk)).T
        a = jax.nn.silu(_dot(xmat[...], wg_bf[...])) * _dot(xmat[...], wu_bf[...])
        a = (a * _lanes([gcol] * (a.shape[1] // LANES))).astype(_BF16)
        ybuf[slot] = _to_token_major(_dot(a, wd_bf[...]).astype(_BF16))

        @pl.when(n == last)
        def _():
            issue_scatter(slot, n % ring)
            wait_rows(xbuf, gsem, other)
            wait_rows(ybuf, ssem, other)
            wait_rows(ybuf, ssem, slot)


def _moe_experts(h3, block_expert, nb_real, slot_src, slot_dst, slot_gate, we_gate, we_up, we_down):
    n, c, _ = h3.shape
    d = c * LANES
    nb = block_expert.shape[0]
    blk = MOE_BLOCK
    e, _, de = we_gate.shape
    grid_spec = pltpu.PrefetchScalarGridSpec(
        num_scalar_prefetch=2,
        grid=(nb,),
        in_specs=[
            pl.BlockSpec(memory_space=pl.ANY),
            pl.BlockSpec(memory_space=pl.ANY),
            pl.BlockSpec(memory_space=pl.ANY),
            pl.BlockSpec((1, 1, blk), lambda i, be, nbr: (i, 0, 0)),
            pl.BlockSpec((1, d, de), lambda i, be, nbr: (be[i], 0, 0)),
            pl.BlockSpec((1, d, de), lambda i, be, nbr: (be[i], 0, 0)),
            pl.BlockSpec((1, de, d), lambda i, be, nbr: (be[i], 0, 0)),
        ],
        out_specs=pl.BlockSpec(memory_space=pl.ANY),
        scratch_shapes=[
            pltpu.VMEM((2, blk, c, LANES), _BF16), pltpu.VMEM((2, blk, c, LANES), _BF16),
            pltpu.VMEM((blk, d), _BF16),
            pltpu.SMEM((2, 1, blk), jnp.int32), pltpu.SMEM((4, 1, blk), jnp.int32),
            pltpu.VMEM((d, de), _BF16), pltpu.VMEM((d, de), _BF16), pltpu.VMEM((de, d), _BF16),
            pltpu.SemaphoreType.DMA((2,)), pltpu.SemaphoreType.DMA((2,)), pltpu.SemaphoreType.DMA((2, 4)),
        ],
    )
    return pl.pallas_call(
        _moe_kernel,
        grid_spec=grid_spec,
        out_shape=jax.ShapeDtypeStruct((TOP_K * n + blk, c, LANES), _BF16),
        compiler_params=_cparams(("arbitrary",)),
        name="moe_experts",
    )(block_expert, nb_real, slot_src, slot_dst, h3, slot_gate, we_gate, we_up, we_down)


def _dispatch_plan(top_idx, gates, counts, n):
    blk = MOE_BLOCK
    a = n * TOP_K
    nb = (a + N_EXPERTS * (blk - 1)) // blk
    order = jnp.argsort(top_idx.reshape(a)).astype(jnp.int32)
    counts = counts.astype(jnp.int32)
    padded = (counts + blk - 1) // blk * blk
    start = jnp.cumsum(counts) - counts
    pstart = jnp.cumsum(padded) - padded
    block_end = jnp.cumsum(padded) // blk
    blocks = jnp.arange(nb, dtype=jnp.int32)
    block_expert = jnp.minimum(jnp.sum((block_end[None, :] <= blocks[:, None]).astype(jnp.int32), axis=1),
                               N_EXPERTS - 1)
    off = blocks * blk - pstart[block_expert]
    base = start[block_expert] + off
    lane = jnp.arange(blk, dtype=jnp.int32)[None, :]
    rows = jnp.take(order, base[:, None] + lane, mode="clip")
    real = (off[:, None] + lane) < counts[block_expert][:, None]
    tok = rows // TOP_K
    kk = rows - tok * TOP_K
    slot_src = jnp.where(real, tok, 0)
    slot_dst = jnp.where(real, kk * n + tok, TOP_K * n + lane)
    slot_dst = jnp.concatenate([slot_dst, TOP_K * n + lane], axis=0)
    slot_gate = jnp.where(real, jnp.take(gates.reshape(a), rows, mode="clip"), 0.0).reshape(nb, 1, blk)
    nb_real = (jnp.sum(padded) // blk).reshape(1).astype(jnp.int32)
    return block_expert, nb_real, slot_src.reshape(nb, 1, blk), slot_dst.reshape(nb + 1, 1, blk), slot_gate


def _combine_kernel(x_ref, h_ref, wsg_ref, wsu_ref, wsd_ref, *rest):
    y_refs, o_ref = rest[:TOP_K], rest[TOP_K]
    h = _from_token_major(h_ref[...])
    a = (jax.nn.silu(_dot(h, wsg_ref[...])) * _dot(h, wsu_ref[...])).astype(_BF16)
    routed = y_refs[0][...].astype(_F32)
    for k in range(1, TOP_K):
        routed = routed + y_refs[k][...].astype(_F32)
    o_ref[...] = x_ref[...] + _dot(a, wsd_ref[...]) + _from_token_major(routed)


def _combine(x1, h3, y3, ws_gate, ws_up, ws_down):
    n, d = x1.shape
    c = d // LANES
    tm = COMBINE_TM
    nt = n // tm
    de = ws_gate.shape[1]
    const = lambda r, cc: pl.BlockSpec((r, cc), lambda i: (0, 0))
    row = pl.BlockSpec((tm, d), lambda i: (i, 0))
    y_specs = [pl.BlockSpec((tm, c, LANES), functools.partial(lambda i, k: (k * nt + i, 0, 0), k=k))
               for k in range(TOP_K)]
    return pl.pallas_call(
        _combine_kernel,
        grid=(nt,),
        in_specs=[row, pl.BlockSpec((tm, c, LANES), lambda i: (i, 0, 0)), const(d, de), const(d, de), const(de, d)]
        + y_specs,
        out_specs=row,
        out_shape=jax.ShapeDtypeStruct((n, d), _F32),
        compiler_params=_cparams(("arbitrary",)),
        name="moe_combine",
    )(x1, h3, ws_gate.astype(_BF16), ws_up.astype(_BF16), ws_down.astype(_BF16), *([y3] * TOP_K))


def _moe_block(x1, norm_ffn_g, w_router, router_bias, we_gate, we_up, we_down, ws_gate, ws_up, ws_down):
    n = x1.shape[0]
    h3, top_idx_t, gates_t, counts = _router(x1, norm_ffn_g, w_router, router_bias)
    block_expert, nb_real, slot_src, slot_dst, slot_gate = _dispatch_plan(
        top_idx_t.T, gates_t.T, counts.reshape(N_EXPERTS), n)
    y3 = _moe_experts(h3, block_expert, nb_real, slot_src, slot_dst, slot_gate, we_gate, we_up, we_down)
    return _combine(x1, h3, y3, ws_gate, ws_up, ws_down)


def kernel(x, mem, norm_attn_g, w_in, nsa_pe_k, nsa_w_ck, nsa_pe_v, nsa_w_cv, nsa_q_g, nsa_k_g, rel_bias, fox_b_f, fox_q_g, fox_k_g, norm_mem_g, w_mem_kv, mem_q_g, mem_k_g, w_o_nsa, w_o_fox, w_o_mem, w_out, norm_ffn_g, w_router, router_bias, we_gate, we_up, we_down, ws_gate, ws_up, ws_down):
    b, t, d = x.shape
    assert norm_attn_g.shape[0] == 1, "single-layer problem"
    l = 0
    proj2d, o_nsa, o_fox, o_mem = _attention_mixers(
        x, mem, norm_attn_g[l], w_in[l], nsa_pe_k[l], nsa_w_ck[l], nsa_pe_v[l], nsa_w_cv[l], nsa_q_g[l],
        nsa_k_g[l], rel_bias, fox_b_f[l], fox_q_g[l], fox_k_g[l], norm_mem_g[l], w_mem_kv[l], mem_q_g[l],
        mem_k_g[l])
    x1 = _merge(x.reshape(b * t, d), o_nsa, o_fox, o_mem, proj2d, w_o_nsa[l], w_o_fox[l], w_o_mem[l], w_out[l])
    out = _moe_block(x1, norm_ffn_g[l], w_router[l], router_bias[l], we_gate[l], we_up[l], we_down[l],
                     ws_gate[l], ws_up[l], ws_down[l])
    return out.reshape(b, t, d)
```

```python
import functools
import math

import jax
import jax.numpy as jnp
import numpy as np
from jax import lax
from jax.experimental import pallas as pl
from jax.experimental.pallas import tpu as pltpu

D_MODEL = 2048
HEAD_DIM = 128
NSA_HEADS = 8
NSA_KV_HEADS = 2
NSA_GROUP = NSA_HEADS // NSA_KV_HEADS
FOX_HEADS = 4
MEM_HEADS = 4
CMP_LEN = 32
CMP_STRIDE = 16
SLC_LEN = 64
N_SELECT = 16
WINDOW = 512
NUM_BUCKETS = 32
MAX_DISTANCE = 128
N_BRANCHES = 3
N_EXPERTS = 64
TOP_K = 8
D_EXPERT = 512
ROUTED_SCALE = 2.5
ATTN_SCALE = HEAD_DIM ** -0.5
NEG_INF = -1e30
FORCE_SCORE = 1e4
RMS_EPS = 1e-6
LOG2E = math.log2(math.e)

NSA_Q_W = NSA_HEADS * HEAD_DIM
NSA_KV_W = NSA_KV_HEADS * HEAD_DIM
NSA_GATE_W = 3 * NSA_HEADS
FOX_W = FOX_HEADS * HEAD_DIM
MEM_W = MEM_HEADS * HEAD_DIM
MERGE_W = N_BRANCHES * D_MODEL

LANES = 128
VMEM_LIMIT_BYTES = 56 * 1024 * 1024

PROJ_TM = 1024
PROJ_TN = 1536
ATT_TQ = 128
FAR_GROUP = 4
FOX_TQ = 256
MEM_TQ = 512
MERGE_TM = 256
ROUTER_TM = 512
MOE_BLOCK = 256
COMBINE_TM = 256

CB_MERGE = 0
CB_ATT = N_BRANCHES * D_MODEL // LANES
CB_QNSA = CB_ATT
CB_KCMP = CB_ATT + 8
CB_VCMP = CB_ATT + 10
CB_KSLC = CB_ATT + 12
CB_VSLC = CB_ATT + 14
CB_KWIN = CB_ATT + 16
CB_VWIN = CB_ATT + 18
CB_FOXQ = CB_ATT + 20
CB_FOXK = CB_ATT + 24
CB_FOXV = CB_ATT + 28
CB_MEMQ = CB_ATT + 32
PROJ_COLS = (CB_ATT + 36) * LANES
SMALL_W = 2 * LANES
FOXF_LANE = 12

_BF16 = jnp.bfloat16
_F32 = jnp.float32


def _cparams(sem):
    return pltpu.CompilerParams(dimension_semantics=sem, vmem_limit_bytes=VMEM_LIMIT_BYTES)


def _dot(a, b):
    return jnp.dot(a, b, preferred_element_type=_F32)


def _dot_nt(a, b):
    return lax.dot_general(a, b, (((1,), (1,)), ((), ())), preferred_element_type=_F32)


def _dot_tn(a, b):
    return lax.dot_general(a, b, (((0,), (0,)), ((), ())), preferred_element_type=_F32)


def _lanes(parts):
    return jnp.concatenate(parts, axis=1)


def _to_token_major(x):
    c = x.shape[1] // LANES
    chunks = jnp.stack([x[:, j * LANES:(j + 1) * LANES] for j in range(c)], axis=0)
    return pltpu.einshape("ctl->tcl", chunks)


def _from_token_major(x3):
    xt = pltpu.einshape("tcl->ctl", x3)
    return _lanes([xt[j] for j in range(x3.shape[1])])


REORDER_TR = 128
REORDER_TC = 512


def _reorder_kernel(pieces, narrow, w_ref, o_ref, s_ref):
    off = 0
    for src, width in pieces:
        for c in range(0, width, REORDER_TC):
            o_ref[:, off + c:off + c + REORDER_TC] = w_ref[:, src + c:src + c + REORDER_TC].astype(o_ref.dtype)
        off += width
    s_ref[...] = jnp.zeros(s_ref.shape, s_ref.dtype)
    for src, width, dst in narrow:
        s_ref[:, dst:dst + width] = w_ref[:, src:src + width].astype(s_ref.dtype)


def _reorder_cast(w, pieces, narrow, narrow_cols):
    rows, cols = w.shape
    total = sum(width for _, width in pieces)
    assert rows % REORDER_TR == 0 and all(width % REORDER_TC == 0 for _, width in pieces)
    return pl.pallas_call(
        functools.partial(_reorder_kernel, pieces, narrow),
        grid=(rows // REORDER_TR,),
        in_specs=[pl.BlockSpec((REORDER_TR, cols), lambda i: (i, 0))],
        out_specs=[pl.BlockSpec((REORDER_TR, total), lambda i: (i, 0)),
                   pl.BlockSpec((REORDER_TR, narrow_cols), lambda i: (i, 0))],
        out_shape=[jax.ShapeDtypeStruct((rows, total), _BF16), jax.ShapeDtypeStruct((rows, narrow_cols), _BF16)],
        compiler_params=_cparams(("arbitrary",)),
        name="reorder_cast",
    )(w)


def _proj_kernel(norm_j0, has_small, x_ref, g_ref, w_ref, cg_ref, cf_ref, *rest):
    if has_small:
        ws_ref, o_ref, os_ref, h_ref = rest
    else:
        o_ref, h_ref = rest
    j = pl.program_id(1)

    @pl.when(j == 0)
    def _():
        x = x_ref[...]
        ms = jnp.mean(x * x, axis=-1, keepdims=True)
        h = (x * lax.rsqrt(ms + RMS_EPS) * g_ref[...]).astype(_BF16)
        h_ref[...] = h
        if has_small:
            os_ref[...] = _dot(h, ws_ref[...])

    y = _dot(h_ref[...], w_ref[...])
    tn = y.shape[1]

    @pl.when(j >= norm_j0)
    def _():
        for c in range(tn // LANES):
            sl = slice(c * LANES, (c + 1) * LANES)
            yh = y[:, sl]
            ms = jnp.mean(yh * yh, axis=-1, keepdims=True)
            scale = jnp.where(cf_ref[:, sl] > 0.0, lax.rsqrt(ms + RMS_EPS), 1.0)
            o_ref[:, sl] = (yh * scale * cg_ref[:, sl]).astype(o_ref.dtype)

    @pl.when(j < norm_j0)
    def _():
        o_ref[...] = y.astype(o_ref.dtype)


def _rms_project(x2d, g, w, col_gain, col_flag, n_plain_cols, w_small, tm, tn):
    n, d = x2d.shape
    c = w.shape[1]
    has_small = w_small is not None
    in_specs = [
        pl.BlockSpec((tm, d), lambda i, j: (i, 0)),
        pl.BlockSpec((1, d), lambda i, j: (0, 0)),
        pl.BlockSpec((d, tn), lambda i, j: (0, j)),
        pl.BlockSpec((1, tn), lambda i, j: (0, j)),
        pl.BlockSpec((1, tn), lambda i, j: (0, j)),
    ]
    args = [x2d, g.reshape(1, d), w, col_gain.reshape(1, c), col_flag.reshape(1, c)]
    out_shape = [jax.ShapeDtypeStruct((n, c), _BF16)]
    out_specs = [pl.BlockSpec((tm, tn), lambda i, j: (i, j))]
    if has_small:
        ws = w_small.shape[1]
        in_specs.append(pl.BlockSpec((d, ws), lambda i, j: (0, 0)))
        args.append(w_small)
        out_shape.append(jax.ShapeDtypeStruct((n, ws), _F32))
        out_specs.append(pl.BlockSpec((tm, ws), lambda i, j: (i, 0)))
    assert n % tm == 0 and c % tn == 0 and n_plain_cols % tn == 0
    res = pl.pallas_call(
        functools.partial(_proj_kernel, n_plain_cols // tn, has_small),
        grid=(n // tm, c // tn),
        in_specs=in_specs,
        out_specs=out_specs,
        out_shape=out_shape,
        scratch_shapes=[pltpu.VMEM((tm, d), _BF16)],
        compiler_params=_cparams(("arbitrary", "arbitrary")),
        name="rms_project",
    )(*args)
    return res if has_small else res[0]


def _cmp_kernel(x_ref, wlo_ref, whi_ref, pelo_ref, pehi_ref, kg_ref, o_ref):
    j = pl.program_id(1)
    x = x_ref[0, 0]
    nchunk = x.shape[0]
    ylo = _dot(x, wlo_ref[0])
    yhi = _dot(x, whi_ref[0])
    pe = _dot(pelo_ref[0], wlo_ref[0]) + _dot(pehi_ref[0], whi_ref[0])
    y = ylo + pltpu.roll(yhi, nchunk - 1, 0) + pe[0:1, :]

    @pl.when(j < NSA_KV_HEADS)
    def _():
        ms = jnp.mean(y * y, axis=-1, keepdims=True)
        o_ref[0, 0] = (y * lax.rsqrt(ms + RMS_EPS) * kg_ref[...]).astype(o_ref.dtype)

    @pl.when(j >= NSA_KV_HEADS)
    def _():
        o_ref[0, 0] = y.T.astype(o_ref.dtype)


def _nsa_compress(cmp_in, w_ck, w_cv, pe_k, pe_v, k_g):
    b, nj, t, dk = cmp_in.shape
    nchunk = t // CMP_STRIDE
    half = CMP_LEN // 2
    assert nchunk == dk
    x = cmp_in.reshape(b, nj, nchunk, CMP_STRIDE * dk)

    def halves(w):
        return (w[:half].reshape(half * dk, dk).astype(_BF16),
                w[half:].reshape(half * dk, dk).astype(_BF16))

    klo, khi = halves(w_ck)
    vlo, vhi = halves(w_cv)
    wlo = jnp.stack([klo, vlo])
    whi = jnp.stack([khi, vhi])

    def pe_halves(pe):
        lo = jnp.broadcast_to(pe[:half].reshape(1, half * dk), (8, half * dk)).astype(_BF16)
        hi = jnp.broadcast_to(pe[half:].reshape(1, half * dk), (8, half * dk)).astype(_BF16)
        return lo, hi

    pklo, pkhi = pe_halves(pe_k)
    pvlo, pvhi = pe_halves(pe_v)
    pelo = jnp.stack([pklo, pvlo])
    pehi = jnp.stack([pkhi, pvhi])
    kv = lambda bb, j: (j // NSA_KV_HEADS, 0, 0)
    return pl.pallas_call(
        _cmp_kernel,
        grid=(b, nj),
        in_specs=[
            pl.BlockSpec((1, 1, nchunk, CMP_STRIDE * dk), lambda bb, j: (bb, j, 0, 0)),
            pl.BlockSpec((1, half * dk, dk), kv),
            pl.BlockSpec((1, half * dk, dk), kv),
            pl.BlockSpec((1, 8, half * dk), kv),
            pl.BlockSpec((1, 8, half * dk), kv),
            pl.BlockSpec((1, dk), lambda bb, j: (0, 0)),
        ],
        out_specs=pl.BlockSpec((1, 1, nchunk, dk), lambda bb, j: (bb, j, 0, 0)),
        out_shape=jax.ShapeDtypeStruct((b, nj, nchunk, dk), _BF16),
        compiler_params=_cparams(("arbitrary", "arbitrary")),
        name="nsa_compress",
    )(x, wlo, whi, pelo, pehi, k_g.reshape(1, dk).astype(_F32))


def _t_update(state, tiles):
    m, l, acc = state
    masked = [s if mask is None else jnp.where(mask, s, NEG_INF) for s, mask, _ in tiles]
    m_new = m
    for sm in masked:
        m_new = jnp.maximum(m_new, jnp.max(sm, axis=0, keepdims=True))
    alpha = jnp.exp2(m - m_new)
    l_new = alpha * l
    acc_new = alpha * acc
    for sm, (_, _, v) in zip(masked, tiles):
        e = jnp.exp2(sm - m_new)
        l_new = l_new + jnp.sum(e, axis=0, keepdims=True)
        acc_new = acc_new + _dot_tn(v, e.astype(_BF16))
    return m_new, l_new, acc_new


def _t_init(nq):
    return (jnp.full((1, nq), NEG_INF, _F32), jnp.zeros((1, nq), _F32), jnp.zeros((HEAD_DIM, nq), _F32))


def _t_finish(state):
    _, l, acc = state
    return acc / l


def _nsa_kernel(q_ref, kvc_ref, ks_ref, vs_ref, kw_ref, vw_ref, gl_ref,
                wt_ref, dt_ref, ovt_ref, o_ref, sel_ref):
    i = pl.program_id(1)
    tq = ATT_TQ
    hg = NSA_GROUP
    nq = hg * tq
    n_slc = ovt_ref.shape[0]
    groups = range(NSA_KV_HEADS)

    key = lax.broadcasted_iota(jnp.int32, (LANES, nq), 0)
    qry = lax.broadcasted_iota(jnp.int32, (LANES, nq), 1) & (tq - 1)

    def gcols(g):
        return slice(g * LANES, (g + 1) * LANES)

    def ktile(ref, g, kt):
        return ref[0, pl.ds(pl.multiple_of(kt * LANES, LANES), LANES), gcols(g)]

    def near_bias(g, d):
        return _lanes([dt_ref[g * hg + h, d] for h in range(hg)])

    q = q_ref[0]
    qts = [_lanes([q[:, (g * hg + h) * HEAD_DIM:(g * hg + h + 1) * HEAD_DIM].astype(_F32).T
                   for h in range(hg)]).astype(_BF16) for g in groups]

    woff = pl.multiple_of(wt_ref.shape[1] - LANES - 8 - (LANES // CMP_STRIDE) * i, 8)
    s_c = [_dot(kvc_ref[0, g], qts[g]) + _lanes([wt_ref[g * hg + h, pl.ds(woff, LANES), :] for h in range(hg)])
           for g in groups]
    n_win = WINDOW // tq
    win_tiles = [[] for _ in groups]
    for g in groups:
        for d in range(n_win, -1, -1):
            ktd = jnp.maximum(i - d, 0)
            sc = _dot(ktile(kw_ref, g, ktd), qts[g])
            if d <= 1:
                sc = sc + near_bias(g, d)
            if d == n_win:
                mk = (qry < key) & (i >= d)
            elif d == 0:
                mk = key <= qry
            else:
                mk = jnp.broadcast_to(i >= d, (LANES, nq))
            win_tiles[g].append((sc, mk, ktile(vw_ref, g, ktd)))

    mask_c = (i * tq + qry) >= (CMP_STRIDE * key + CMP_LEN - 1)
    blk = lax.broadcasted_iota(jnp.int32, (n_slc, tq), 0)
    tpos = i * tq + lax.broadcasted_iota(jnp.int32, (n_slc, tq), 1)
    cur = tpos // SLC_LEN
    valid = blk * SLC_LEN <= tpos
    forced = (blk == 0) | (blk == cur) | (blk == cur - 1)
    o_c = []
    for g in groups:
        s = s_c[g]
        mx = jnp.max(jnp.where(mask_c, s, NEG_INF), axis=0, keepdims=True)
        e = jnp.where(mask_c, jnp.exp2(s - mx), 0.0)
        l = jnp.sum(e, axis=0, keepdims=True)
        p_c = e / jnp.where(l > 0.0, l, 1.0)
        o_c.append(_dot(kvc_ref[0, NSA_KV_HEADS + g], p_c.astype(_BF16)))
        ps = p_c[:, 0:tq]
        for h in range(1, hg):
            ps = ps + p_c[:, h * tq:(h + 1) * tq]
        ps_hi = ps.astype(_BF16)
        ps_lo = (ps - ps_hi.astype(_F32)).astype(_BF16)
        imp = _dot(ovt_ref[...], ps_hi) + _dot(ovt_ref[...], ps_lo)
        score = jnp.where(valid, jnp.where(forced, FORCE_SCORE, imp), -1.0)
        rank = jnp.zeros((n_slc, tq), _F32)
        for k in range(n_slc):
            ck = score[k:k + 1, :]
            beats = (ck > score) | ((ck == score) & (blk > k))
            rank = rank + jnp.where(beats, 1.0, 0.0)
        selb = jnp.where(rank < float(min(N_SELECT, n_slc)), 0.0, NEG_INF)
        sel_ref[g] = _lanes([selb] * hg)

    def sel_bias(g, kt, ok=True):
        rows = []
        for r in range(LANES // SLC_LEN):
            row = sel_ref[g, pl.ds((LANES // SLC_LEN) * kt + r, 1), :]
            rows.append(jnp.broadcast_to(jnp.where(ok, row, NEG_INF), (SLC_LEN, nq)))
        return jnp.concatenate(rows, axis=0)

    o_w = [_t_finish(_t_update(_t_init(nq), win_tiles[g])) for g in groups]

    n_far = jnp.maximum(i - 1, 0)

    def far_body(j, sts):
        k0 = pl.multiple_of(j * (FAR_GROUP * LANES), FAR_GROUP * LANES)
        sgs = [_dot(ks_ref[0, pl.ds(k0, FAR_GROUP * LANES), gcols(g)], qts[g]) for g in groups]
        out = []
        for g in groups:
            tiles = []
            for u in range(FAR_GROUP):
                kt = FAR_GROUP * j + u
                tiles.append((sgs[g][u * LANES:(u + 1) * LANES] + sel_bias(g, kt, kt < n_far), None,
                              ktile(vs_ref, g, kt)))
            out.append(_t_update(sts[g], tiles))
        return tuple(out)

    sts = lax.fori_loop(0, (n_far + FAR_GROUP - 1) // FAR_GROUP, far_body, tuple(_t_init(nq) for _ in groups))
    kt1 = jnp.maximum(i - 1, 0)
    near = [(_dot(ktile(ks_ref, g, kt1), qts[g]) + near_bias(g, 1), _dot(ktile(ks_ref, g, i), qts[g]) + near_bias(g, 0))
            for g in groups]
    o_s = [_t_finish(_t_update(sts[g], [
        (near[g][0] + sel_bias(g, kt1, i >= 1), None, ktile(vs_ref, g, kt1)),
        (near[g][1] + sel_bias(g, i), key <= qry, ktile(vs_ref, g, i))])) for g in groups]

    for g in groups:
        glt = jax.nn.sigmoid(gl_ref[0, :, gcols(g)]).T
        grow = lambda br: _lanes([glt[br * hg + h:br * hg + h + 1, :] for h in range(hg)])
        o = grow(0) * o_c[g] + grow(1) * o_s[g] + grow(2) * o_w[g]
        for h in range(hg):
            hd = (g * hg + h) * HEAD_DIM
            o_ref[0, :, hd:hd + HEAD_DIM] = o[:, h * tq:(h + 1) * tq].T.astype(o_ref.dtype)


def _t5_bucket(rel):
    n = np.maximum(rel, 0)
    max_exact = NUM_BUCKETS // 2
    ratio = np.maximum(n, 1).astype(np.float32) / np.float32(max_exact)
    log_ratio = np.log(ratio) / np.float32(math.log(MAX_DISTANCE / max_exact))
    large = np.minimum(max_exact + (log_ratio * np.float32(NUM_BUCKETS - max_exact)).astype(np.int32),
                       NUM_BUCKETS - 1)
    return np.where(n < max_exact, n, large).astype(np.int32)


def _bias_lookup(rb, bucket):
    bk = jnp.asarray(bucket.astype(np.int8))[None]
    ex = (slice(None),) + (None,) * bucket.ndim
    tab = jnp.broadcast_to(rb[:, 0][ex], (rb.shape[0],) + bucket.shape)
    for k in range(1, NUM_BUCKETS):
        tab = jnp.where(bk == k, rb[:, k][ex], tab)
    return tab


def _nsa_tables(rel_bias):
    rb = rel_bias.astype(_F32).T * LOG2E
    r = np.arange(ATT_TQ)
    rel_d = (np.arange(2) * ATT_TQ)[:, None, None] + r[None, None, :] - r[None, :, None]
    assert _t5_bucket(np.array([ATT_TQ + 1]))[0] == NUM_BUCKETS - 1
    u = np.arange(2 * LANES)
    rel_w = r[None, :] - CMP_STRIDE * (u[:, None] - (LANES - 8)) - (CMP_LEN - 1)
    far = rb[:, NUM_BUCKETS - 1]
    dt = _bias_lookup(rb, _t5_bucket(rel_d)) - far[:, None, None, None]
    return dt, _bias_lookup(rb, _t5_bucket(rel_w))


def _nsa_attention(proj, small, kvc, rel_bias):
    b, t, _ = proj.shape
    g, hg, tq = NSA_KV_HEADS, NSA_GROUP, ATT_TQ
    n_slc = t // SLC_LEN
    n_cmp = (t - CMP_LEN) // CMP_STRIDE + 1
    nt = t // LANES
    assert n_cmp <= LANES and kvc.shape[2] == LANES and n_slc % 16 == 0 and tq == LANES
    assert nt % FAR_GROUP == 0
    assert LANES - 8 - (LANES // CMP_STRIDE) * (t // tq - 1) >= 0
    dt, wt = _nsa_tables(rel_bias)
    cstart = np.arange(LANES) * CMP_STRIDE
    sstart = np.arange(n_slc) * SLC_LEN
    overlap = np.clip(np.minimum(cstart[None, :] + CMP_LEN, sstart[:, None] + SLC_LEN)
                      - np.maximum(cstart[None, :], sstart[:, None]), 0, None).astype(np.float32) / CMP_LEN
    overlap[:, n_cmp:] = 0.0
    ovt = jnp.asarray(overlap, _BF16)

    gw = g * LANES
    seq = lambda cb: pl.BlockSpec((1, t, gw), lambda bb, ii: (bb, 0, cb * LANES // gw))
    heads = g * hg
    return pl.pallas_call(
        _nsa_kernel,
        grid=(b, t // tq),
        in_specs=[
            pl.BlockSpec((1, tq, NSA_Q_W), lambda bb, ii: (bb, ii, CB_QNSA * LANES // NSA_Q_W)),
            pl.BlockSpec((1, 2 * g, LANES, HEAD_DIM), lambda bb, ii: (bb, 0, 0, 0)),
            seq(CB_KSLC), seq(CB_VSLC), seq(CB_KWIN), seq(CB_VWIN),
            pl.BlockSpec((1, tq, SMALL_W), lambda bb, ii: (bb, ii, 0)),
            pl.BlockSpec((heads, 2 * LANES, tq), lambda bb, ii: (0, 0, 0)),
            pl.BlockSpec((heads, 2, LANES, tq), lambda bb, ii: (0, 0, 0, 0)),
            pl.BlockSpec((n_slc, LANES), lambda bb, ii: (0, 0)),
        ],
        out_specs=pl.BlockSpec((1, tq, NSA_Q_W), lambda bb, ii: (bb, ii, 0)),
        out_shape=jax.ShapeDtypeStruct((b, t, NSA_Q_W), _BF16),
        scratch_shapes=[pltpu.VMEM((g, n_slc, hg * tq), _F32)],
        compiler_params=_cparams(("arbitrary", "arbitrary")),
        name="nsa_attention",
    )(proj, kvc, proj, proj, proj, proj, small, wt, dt, ovt)


def _fox_cum_kernel(s_ref, b_ref, col_ref, row_ref):
    z = s_ref[0] + b_ref[...]
    lf = (jnp.minimum(z, 0.0) - jnp.log1p(jnp.exp(-jnp.abs(z)))) * LOG2E
    x = lf.T
    t = x.shape[1]
    lane = lax.broadcasted_iota(jnp.int32, x.shape, 1)
    sh = 1
    while sh < t:
        x = x + jnp.where(lane >= sh, pltpu.roll(x, sh, 1), 0.0)
        sh *= 2
    row_ref[0] = x[8:16, :]
    col_ref[0] = x.T


def _fox_cumsum(small, b_f):
    b, t, _ = small.shape
    bvec = jnp.zeros((1, LANES), _F32).at[0, FOXF_LANE:FOXF_LANE + FOX_HEADS].set(b_f.astype(_F32))
    return pl.pallas_call(
        _fox_cum_kernel,
        grid=(b,),
        in_specs=[pl.BlockSpec((1, t, LANES), lambda bb: (bb, 0, 1)),
                  pl.BlockSpec((1, LANES), lambda bb: (0, 0))],
        out_specs=[pl.BlockSpec((1, t, LANES), lambda bb: (bb, 0, 0)),
                   pl.BlockSpec((1, 8, t), lambda bb: (bb, 0, 0))],
        out_shape=[jax.ShapeDtypeStruct((b, t, LANES), _F32), jax.ShapeDtypeStruct((b, 8, t), _F32)],
        compiler_params=_cparams(("arbitrary",)),
        name="fox_cumsum",
    )(small, bvec)


def _fox_kernel(q_ref, k_ref, v_ref, cc_ref, cr_ref, o_ref):
    i = pl.program_id(1)
    tq = FOX_TQ
    key = lax.broadcasted_iota(jnp.int32, (tq, tq), 0)
    qry = lax.broadcasted_iota(jnp.int32, (tq, tq), 1)
    heads = range(FOX_HEADS)
    hsl = [slice(h * HEAD_DIM, (h + 1) * HEAD_DIM) for h in heads]
    qts = [q_ref[0, :, hsl[h]].astype(_F32).T.astype(_BF16) for h in heads]
    cqs = [cr_ref[0, FOXF_LANE - 8 + h:FOXF_LANE - 8 + h + 1, :] for h in heads]

    def keys(kt):
        return pl.ds(pl.multiple_of(kt * tq, tq), tq)

    def logits(h, kt):
        ck = cc_ref[0, keys(kt), FOXF_LANE + h:FOXF_LANE + h + 1]
        return _dot(k_ref[0, keys(kt), hsl[h]], qts[h]) + (cqs[h] - ck)

    def body(kt, sts):
        ss = [logits(h, kt) for h in heads]
        return tuple(_t_update(sts[h], [(ss[h], None, v_ref[0, keys(kt), hsl[h]])]) for h in heads)

    sts = lax.fori_loop(0, i, body, tuple(_t_init(tq) for _ in heads))
    ss = [logits(h, i) for h in heads]
    for h in heads:
        st = _t_update(sts[h], [(ss[h], key <= qry, v_ref[0, keys(i), hsl[h]])])
        o_ref[0, :, hsl[h]] = _t_finish(st).T.astype(o_ref.dtype)


def _fox_attention(proj, cum_col, cum_row):
    b, t, _ = proj.shape
    tq = FOX_TQ
    w = FOX_W
    return pl.pallas_call(
        _fox_kernel,
        grid=(b, t // tq),
        in_specs=[
            pl.BlockSpec((1, tq, w), lambda bb, ii: (bb, ii, CB_FOXQ * LANES // w)),
            pl.BlockSpec((1, t, w), lambda bb, ii: (bb, 0, CB_FOXK * LANES // w)),
            pl.BlockSpec((1, t, w), lambda bb, ii: (bb, 0, CB_FOXV * LANES // w)),
            pl.BlockSpec((1, t, LANES), lambda bb, ii: (bb, 0, 0)),
            pl.BlockSpec((1, 8, tq), lambda bb, ii: (bb, 0, ii)),
        ],
        out_specs=pl.BlockSpec((1, tq, w), lambda bb, ii: (bb, ii, 0)),
        out_shape=jax.ShapeDtypeStruct((b, t, w), _BF16),
        compiler_params=_cparams(("arbitrary", "arbitrary")),
        name="fox_attention",
    )(proj, proj, proj, cum_col, cum_row)


def _mem_kernel(q_ref, kv_ref, o_ref):
    for h in range(MEM_HEADS):
        hs = slice(h * HEAD_DIM, (h + 1) * HEAD_DIM)
        vs = slice(MEM_W + h * HEAD_DIM, MEM_W + (h + 1) * HEAD_DIM)
        s = _dot_nt(q_ref[0, :, hs], kv_ref[0, :, hs])
        e = jnp.exp2(s - jnp.max(s, axis=1, keepdims=True))
        p = e / jnp.sum(e, axis=1, keepdims=True)
        o_ref[0, :, hs] = _dot(p.astype(_BF16), kv_ref[0, :, vs]).astype(o_ref.dtype)


def _mem_attention(proj, memkv):
    b, t, _ = proj.shape
    m = memkv.shape[1]
    tq = MEM_TQ
    return pl.pallas_call(
        _mem_kernel,
        grid=(b, t // tq),
        in_specs=[pl.BlockSpec((1, tq, MEM_W), lambda bb, ii: (bb, ii, CB_MEMQ * LANES // MEM_W)),
                  pl.BlockSpec((1, m, 2 * MEM_W), lambda bb, ii: (bb, 0, 0))],
        out_specs=pl.BlockSpec((1, tq, MEM_W), lambda bb, ii: (bb, ii, 0)),
        out_shape=jax.ShapeDtypeStruct((b, t, MEM_W), _BF16),
        compiler_params=_cparams(("arbitrary", "arbitrary")),
        name="mem_attention",
    )(proj, memkv)


def _merge_kernel(x_ref, on_ref, of_ref, om_ref, g0_ref, g1_ref, g2_ref, wn_ref, wf_ref, wm_ref, wo_ref,
                  o_ref, z_ref):
    tn = 512
    for c in range(D_MODEL // tn):
        cs = slice(c * tn, (c + 1) * tn)
        z = (jax.nn.sigmoid(g0_ref[:, cs].astype(_F32)) * _dot(on_ref[...], wn_ref[:, cs])
             + jax.nn.sigmoid(g1_ref[:, cs].astype(_F32)) * _dot(of_ref[...], wf_ref[:, cs])
             + jax.nn.sigmoid(g2_ref[:, cs].astype(_F32)) * _dot(om_ref[...], wm_ref[:, cs]))
        z_ref[:, cs] = z.astype(_BF16)
    o_ref[...] = x_ref[...] + _dot(z_ref[...], wo_ref[...])


def _merge(x2d, o_nsa, o_fox, o_mem, proj2d, w_o_nsa, w_o_fox, w_o_mem, w_out):
    n, d = x2d.shape
    tm = MERGE_TM
    gcb = CB_MERGE * LANES // d
    const = lambda r, c: pl.BlockSpec((r, c), lambda i: (0, 0))
    return pl.pallas_call(
        _merge_kernel,
        grid=(n // tm,),
        in_specs=[
            pl.BlockSpec((tm, d), lambda i: (i, 0)),
            pl.BlockSpec((tm, NSA_Q_W), lambda i: (i, 0)),
            pl.BlockSpec((tm, FOX_W), lambda i: (i, 0)),
            pl.BlockSpec((tm, MEM_W), lambda i: (i, 0)),
            pl.BlockSpec((tm, d), lambda i: (i, gcb)),
            pl.BlockSpec((tm, d), lambda i: (i, gcb + 1)),
            pl.BlockSpec((tm, d), lambda i: (i, gcb + 2)),
            const(NSA_Q_W, d), const(FOX_W, d), const(MEM_W, d), const(d, d),
        ],
        out_specs=pl.BlockSpec((tm, d), lambda i: (i, 0)),
        out_shape=jax.ShapeDtypeStruct((n, d), _F32),
        scratch_shapes=[pltpu.VMEM((tm, d), _BF16)],
        compiler_params=_cparams(("arbitrary",)),
        name="merge_out",
    )(x2d, o_nsa, o_fox, o_mem, proj2d, proj2d, proj2d,
      w_o_nsa.astype(_BF16), w_o_fox.astype(_BF16), w_o_mem.astype(_BF16), w_out.astype(_BF16))


def _nsa_inputs(x, norm_attn_g, w_in, nsa_pe_k, nsa_w_ck, nsa_pe_v, nsa_w_cv, nsa_q_g, nsa_k_g,
                fox_q_g, fox_k_g, mem_q_g):
    b, t, d = x.shape
    ones = lambda k: jnp.ones((k * HEAD_DIM,), _F32)
    zeros = lambda k: jnp.zeros((k * HEAD_DIM,), _F32)
    tile = lambda gv, k: jnp.tile(gv.astype(_F32), k)
    qs = ATTN_SCALE * LOG2E

    g_end = NSA_Q_W + 6 * NSA_KV_W
    f_off = g_end + NSA_GATE_W + 3 * FOX_W
    m_off = f_off + FOX_HEADS + MEM_W
    narrow = tuple((g_end + (br * NSA_KV_HEADS + gg) * NSA_GROUP, NSA_GROUP, gg * LANES + br * NSA_GROUP)
                   for br in range(3) for gg in range(NSA_KV_HEADS))
    narrow += ((f_off, FOX_HEADS, LANES + FOXF_LANE),)
    w_all, w_small = _reorder_cast(
        w_in, ((m_off, MERGE_W), (0, g_end), (g_end + NSA_GATE_W, 3 * FOX_W), (f_off + FOX_HEADS, MEM_W)),
        narrow, SMALL_W)
    col_gain = jnp.concatenate([
        jnp.ones((MERGE_W,), _F32),
        tile(nsa_q_g, NSA_HEADS) * qs, ones(2), ones(2), tile(nsa_k_g, 2), ones(2), tile(nsa_k_g, 2), ones(2),
        tile(fox_q_g, FOX_HEADS) * qs, tile(fox_k_g, FOX_HEADS), ones(FOX_HEADS),
        tile(mem_q_g, MEM_HEADS) * qs])
    col_flag = jnp.concatenate([
        jnp.zeros((MERGE_W,), _F32),
        ones(NSA_HEADS), zeros(2), zeros(2), ones(2), zeros(2), ones(2), zeros(2),
        ones(FOX_HEADS), ones(FOX_HEADS), zeros(FOX_HEADS), ones(MEM_HEADS)])

    x2d = x.reshape(b * t, d)
    proj2d, small2d = _rms_project(x2d, norm_attn_g, w_all, col_gain, col_flag, CB_ATT * LANES, w_small,
                                   PROJ_TM, PROJ_TN)
    proj = proj2d.reshape(b, t, PROJ_COLS)
    small = small2d.reshape(b, t, SMALL_W)

    cmp_in = proj[:, :, CB_KCMP * LANES:(CB_KCMP + 4) * LANES].reshape(b, t, 4, HEAD_DIM).transpose(0, 2, 1, 3)
    kvc = _nsa_compress(cmp_in, nsa_w_ck, nsa_w_cv, nsa_pe_k, nsa_pe_v, nsa_k_g)
    return proj, small, kvc


def _attention_mixers(x, mem, norm_attn_g, w_in, nsa_pe_k, nsa_w_ck, nsa_pe_v, nsa_w_cv, nsa_q_g, nsa_k_g,
                      rel_bias, fox_b_f, fox_q_g, fox_k_g, norm_mem_g, w_mem_kv, mem_q_g, mem_k_g):
    b, t, d = x.shape
    ones = lambda k: jnp.ones((k * HEAD_DIM,), _F32)
    zeros = lambda k: jnp.zeros((k * HEAD_DIM,), _F32)
    tile = lambda gv, k: jnp.tile(gv.astype(_F32), k)
    proj, small, kvc = _nsa_inputs(x, norm_attn_g, w_in, nsa_pe_k, nsa_w_ck, nsa_pe_v, nsa_w_cv, nsa_q_g, nsa_k_g,
                                   fox_q_g, fox_k_g, mem_q_g)
    proj2d = proj.reshape(b * t, PROJ_COLS)
    o_nsa = _nsa_attention(proj, small, kvc, rel_bias)

    cum_col, cum_row = _fox_cumsum(small, fox_b_f)
    o_fox = _fox_attention(proj, cum_col, cum_row)

    mm = mem.shape[1]
    mem_gain = jnp.concatenate([tile(mem_k_g, MEM_HEADS), ones(MEM_HEADS)])
    mem_flag = jnp.concatenate([ones(MEM_HEADS), zeros(MEM_HEADS)])
    memkv = _rms_project(mem.reshape(b * mm, d), norm_mem_g, w_mem_kv.astype(_BF16), mem_gain, mem_flag,
                         0, None, mm, MEM_W).reshape(b, mm, 2 * MEM_W)
    o_mem = _mem_attention(proj, memkv)
    return proj2d, o_nsa.reshape(b * t, NSA_Q_W), o_fox.reshape(b * t, FOX_W), o_mem.reshape(b * t, MEM_W)


def _router_kernel(x_ref, g_ref, whi_ref, wlo_ref, b_ref, h_ref, idx_ref, gate_ref, cnt_ref):
    x = x_ref[...]
    ms = jnp.mean(x * x, axis=-1, keepdims=True)
    h = x * lax.rsqrt(ms + RMS_EPS) * g_ref[...]
    h_hi = h.astype(_BF16)
    h_ref[...] = _to_token_major(h_hi)
    h_lo = (h - h_hi.astype(_F32)).astype(_BF16)
    logits = _dot_nt(whi_ref[...], h_hi) + _dot_nt(whi_ref[...], h_lo) + _dot_nt(wlo_ref[...], h_hi)
    scores = jax.nn.sigmoid(logits)
    sb = scores + b_ref[...]
    eidx = lax.broadcasted_iota(jnp.int32, sb.shape, 0)
    idxs, vals = [], []
    for _ in range(TOP_K):
        m = jnp.max(sb, axis=0, keepdims=True)
        idx = jnp.min(jnp.where(sb == m, eidx, N_EXPERTS), axis=0, keepdims=True)
        hit = eidx == idx
        vals.append(jnp.sum(jnp.where(hit, scores, 0.0), axis=0, keepdims=True))
        idxs.append(idx)
        sb = jnp.where(hit, NEG_INF, sb)
    top_s = jnp.concatenate(vals, axis=0)
    idx_ref[...] = jnp.concatenate(idxs, axis=0)
    gate_ref[...] = top_s / jnp.sum(top_s, axis=0, keepdims=True) * ROUTED_SCALE

    @pl.when(pl.program_id(0) == 0)
    def _():
        cnt_ref[...] = jnp.zeros(cnt_ref.shape, cnt_ref.dtype)

    picked = jnp.where(sb < 0.5 * NEG_INF, 1.0, 0.0)
    cnt_ref[...] += jnp.sum(picked, axis=1, keepdims=True)


def _router(x1, norm_g, w_router, router_bias):
    n, d = x1.shape
    tm = ROUTER_TM
    wt = w_router.astype(_F32).T
    w_hi = wt.astype(_BF16)
    w_lo = (wt - w_hi.astype(_F32)).astype(_BF16)
    const = lambda r, c: pl.BlockSpec((r, c), lambda i: (0, 0))
    return pl.pallas_call(
        _router_kernel,
        grid=(n // tm,),
        in_specs=[pl.BlockSpec((tm, d), lambda i: (i, 0)), const(1, d), const(N_EXPERTS, d), const(N_EXPERTS, d),
                  const(N_EXPERTS, 1)],
        out_specs=[pl.BlockSpec((tm, d // LANES, LANES), lambda i: (i, 0, 0)),
                   pl.BlockSpec((TOP_K, tm), lambda i: (0, i)),
                   pl.BlockSpec((TOP_K, tm), lambda i: (0, i)),
                   const(N_EXPERTS, 1)],
        out_shape=[jax.ShapeDtypeStruct((n, d // LANES, LANES), _BF16), jax.ShapeDtypeStruct((TOP_K, n), jnp.int32),
                   jax.ShapeDtypeStruct((TOP_K, n), _F32), jax.ShapeDtypeStruct((N_EXPERTS, 1), _F32)],
        compiler_params=_cparams(("arbitrary",)),
        name="moe_router",
    )(x1, norm_g.reshape(1, d).astype(_F32), w_hi, w_lo, router_bias.reshape(N_EXPERTS, 1).astype(_F32))


def _moe_kernel(be_ref, nbr_ref, src_hbm, dst_hbm, h_hbm, sg_ref, wg_ref, wu_ref, wd_ref, y_hbm,
                xbuf, ybuf, xmat, sidx, didx, wg_bf, wu_bf, wd_bf, gsem, ssem, isem):
    n = pl.program_id(0)
    nbr = nbr_ref[0]
    last = nbr - 1
    slot = n % 2
    other = 1 - slot
    blk = MOE_BLOCK
    dump_block = dst_hbm.shape[0] - 1
    ring = didx.shape[0]

    def src_copy(block, s):
        return pltpu.make_async_copy(src_hbm.at[block], sidx.at[s], isem.at[0, s])

    def dst_copy(block, s3):
        return pltpu.make_async_copy(dst_hbm.at[block], didx.at[s3], isem.at[1, s3])

    def issue_gather(s):
        for r in range(blk):
            tok = sidx[s, 0, r]
            pltpu.make_async_copy(h_hbm.at[tok], xbuf.at[s, r], gsem.at[s]).start(priority=r % 2)

    def issue_scatter(s, s3):
        for r in range(blk):
            row = didx[s3, 0, r]
            pltpu.make_async_copy(ybuf.at[s, r], y_hbm.at[row], ssem.at[s]).start(priority=r % 2)

    def wait_rows(buf, sem, s):
        pltpu.make_async_copy(buf.at[s], buf.at[s], sem.at[s]).wait()

    @pl.when(n < nbr)
    def _():
        @pl.when(n == 0)
        def _():
            src_copy(0, 0).start()
            dst_copy(0, 0).start()
            dst_copy(dump_block, ring - 1).start()
            ybuf[1] = jnp.zeros(ybuf.shape[1:], ybuf.dtype)
            src_copy(0, 0).wait()
            dst_copy(0, 0).wait()
            dst_copy(dump_block, ring - 1).wait()
            issue_gather(0)
            nxt0 = jnp.minimum(1, last)
            src_copy(nxt0, 1).start()
            dst_copy(nxt0, 1).start()

        wait_rows(xbuf, gsem, slot)

        @pl.when(n < last)
        def _():
            nn = jnp.minimum(n + 2, last)
            src_copy(nn, slot).start()
            dst_copy(nn, (n + 2) % ring).start()

        changed = (n == 0) | (be_ref[n] != be_ref[jnp.maximum(n - 1, 0)])

        @pl.when(changed)
        def _():
            wg_bf[...] = wg_ref[0].astype(_BF16)
            wu_bf[...] = wu_ref[0].astype(_BF16)
            wd_bf[...] = wd_ref[0].astype(_BF16)

        @pl.when(n >= 1)
        def _():
            wait_rows(ybuf, ssem, slot)

        nxt = jnp.minimum(n + 1, last)
        src_copy(nxt, other).wait()
        dst_copy(nxt, (n + 1) % ring).wait()

        issue_scatter(other, (n + ring - 1) % ring)
        xmat[...] = _from_token_major(xbuf[slot])
        issue_gather(other)
        gcol = jnp.broadcast_to(sg_ref[0], (LANES, blk)).T
        a = jax.nn.silu(_dot(xmat[...], wg_bf[...])) * _dot(xmat[...], wu_bf[...])
        a = (a * _lanes([gcol] * (a.shape[1] // LANES))).astype(_BF16)
        ybuf[slot] = _to_token_major(_dot(a, wd_bf[...]).astype(_BF16))

        @pl.when(n == last)
        def _():
            issue_scatter(slot, n % ring)
            wait_rows(xbuf, gsem, other)
            wait_rows(ybuf, ssem, other)
            wait_rows(ybuf, ssem, slot)


def _moe_experts(h3, block_expert, nb_real, slot_src, slot_dst, slot_gate, we_gate, we_up, we_down):
    n, c, _ = h3.shape
    d = c * LANES
    nb = block_expert.shape[0]
    blk = MOE_BLOCK
    e, _, de = we_gate.shape
    grid_spec = pltpu.PrefetchScalarGridSpec(
        num_scalar_prefetch=2,
        grid=(nb,),
        in_specs=[
            pl.BlockSpec(memory_space=pl.ANY),
            pl.BlockSpec(memory_space=pl.ANY),
            pl.BlockSpec(memory_space=pl.ANY),
            pl.BlockSpec((1, 1, blk), lambda i, be, nbr: (i, 0, 0)),
            pl.BlockSpec((1, d, de), lambda i, be, nbr: (be[i], 0, 0)),
            pl.BlockSpec((1, d, de), lambda i, be, nbr: (be[i], 0, 0)),
            pl.BlockSpec((1, de, d), lambda i, be, nbr: (be[i], 0, 0)),
        ],
        out_specs=pl.BlockSpec(memory_space=pl.ANY),
        scratch_shapes=[
            pltpu.VMEM((2, blk, c, LANES), _BF16), pltpu.VMEM((2, blk, c, LANES), _BF16),
            pltpu.VMEM((blk, d), _BF16),
            pltpu.SMEM((2, 1, blk), jnp.int32), pltpu.SMEM((4, 1, blk), jnp.int32),
            pltpu.VMEM((d, de), _BF16), pltpu.VMEM((d, de), _BF16), pltpu.VMEM((de, d), _BF16),
            pltpu.SemaphoreType.DMA((2,)), pltpu.SemaphoreType.DMA((2,)), pltpu.SemaphoreType.DMA((2, 4)),
        ],
    )
    return pl.pallas_call(
        _moe_kernel,
        grid_spec=grid_spec,
        out_shape=jax.ShapeDtypeStruct((TOP_K * n + blk, c, LANES), _BF16),
        compiler_params=_cparams(("arbitrary",)),
        name="moe_experts",
    )(block_expert, nb_real, slot_src, slot_dst, h3, slot_gate, we_gate, we_up, we_down)


def _dispatch_plan(top_idx, gates, counts, n):
    blk = MOE_BLOCK
    a = n * TOP_K
    nb = (a + N_EXPERTS * (blk - 1)) // blk
    order = jnp.argsort(top_idx.reshape(a)).astype(jnp.int32)
    counts = counts.astype(jnp.int32)
    padded = (counts + blk - 1) // blk * blk
    start = jnp.cumsum(counts) - counts
    pstart = jnp.cumsum(padded) - padded
    block_end = jnp.cumsum(padded) // blk
    blocks = jnp.arange(nb, dtype=jnp.int32)
    block_expert = jnp.minimum(jnp.sum((block_end[None, :] <= blocks[:, None]).astype(jnp.int32), axis=1),
                               N_EXPERTS - 1)
    off = blocks * blk - pstart[block_expert]
    base = start[block_expert] + off
    lane = jnp.arange(blk, dtype=jnp.int32)[None, :]
    rows = jnp.take(order, base[:, None] + lane, mode="clip")
    real = (off[:, None] + lane) < counts[block_expert][:, None]
    tok = rows // TOP_K
    kk = rows - tok * TOP_K
    slot_src = jnp.where(real, tok, 0)
    slot_dst = jnp.where(real, kk * n + tok, TOP_K * n + lane)
    slot_dst = jnp.concatenate([slot_dst, TOP_K * n + lane], axis=0)
    slot_gate = jnp.where(real, jnp.take(gates.reshape(a), rows, mode="clip"), 0.0).reshape(nb, 1, blk)
    nb_real = (jnp.sum(padded) // blk).reshape(1).astype(jnp.int32)
    return block_expert, nb_real, slot_src.reshape(nb, 1, blk), slot_dst.reshape(nb + 1, 1, blk), slot_gate


def _combine_kernel(x_ref, h_ref, wsg_ref, wsu_ref, wsd_ref, *rest):
    y_refs, o_ref = rest[:TOP_K], rest[TOP_K]
    h = _from_token_major(h_ref[...])
    a = (jax.nn.silu(_dot(h, wsg_ref[...])) * _dot(h, wsu_ref[...])).astype(_BF16)
    routed = y_refs[0][...].astype(_F32)
    for k in range(1, TOP_K):
        routed = routed + y_refs[k][...].astype(_F32)
    o_ref[...] = x_ref[...] + _dot(a, wsd_ref[...]) + _from_token_major(routed)


def _combine(x1, h3, y3, ws_gate, ws_up, ws_down):
    n, d = x1.shape
    c = d // LANES
    tm = COMBINE_TM
    nt = n // tm
    de = ws_gate.shape[1]
    const = lambda r, cc: pl.BlockSpec((r, cc), lambda i: (0, 0))
    row = pl.BlockSpec((tm, d), lambda i: (i, 0))
    y_specs = [pl.BlockSpec((tm, c, LANES), functools.partial(lambda i, k: (k * nt + i, 0, 0), k=k))
               for k in range(TOP_K)]
    return pl.pallas_call(
        _combine_kernel,
        grid=(nt,),
        in_specs=[row, pl.BlockSpec((tm, c, LANES), lambda i: (i, 0, 0)), const(d, de), const(d, de), const(de, d)]
        + y_specs,
        out_specs=row,
        out_shape=jax.ShapeDtypeStruct((n, d), _F32),
        compiler_params=_cparams(("arbitrary",)),
        name="moe_combine",
    )(x1, h3, ws_gate.astype(_BF16), ws_up.astype(_BF16), ws_down.astype(_BF16), *([y3] * TOP_K))


def _moe_block(x1, norm_ffn_g, w_router, router_bias, we_gate, we_up, we_down, ws_gate, ws_up, ws_down):
    n = x1.shape[0]
    h3, top_idx_t, gates_t, counts = _router(x1, norm_ffn_g, w_router, router_bias)
    block_expert, nb_real, slot_src, slot_dst, slot_gate = _dispatch_plan(
        top_idx_t.T, gates_t.T, counts.reshape(N_EXPERTS), n)
    y3 = _moe_experts(h3, block_expert, nb_real, slot_src, slot_dst, slot_gate, we_gate, we_up, we_down)
    return _combine(x1, h3, y3, ws_gate, ws_up, ws_down)


def kernel(x, mem, norm_attn_g, w_in, nsa_pe_k, nsa_w_ck, nsa_pe_v, nsa_w_cv, nsa_q_g, nsa_k_g, rel_bias, fox_b_f, fox_q_g, fox_k_g, norm_mem_g, w_mem_kv, mem_q_g, mem_k_g, w_o_nsa, w_o_fox, w_o_mem, w_out, norm_ffn_g, w_router, router_bias, we_gate, we_up, we_down, ws_gate, ws_up, ws_down):
    b, t, d = x.shape
    assert norm_attn_g.shape[0] == 1, "single-layer problem"
    l = 0
    proj2d, o_nsa, o_fox, o_mem = _attention_mixers(
        x, mem, norm_attn_g[l], w_in[l], nsa_pe_k[l], nsa_w_ck[l], nsa_pe_v[l], nsa_w_cv[l], nsa_q_g[l],
        nsa_k_g[l], rel_bias, fox_b_f[l], fox_q_g[l], fox_k_g[l], norm_mem_g[l], w_mem_kv[l], mem_q_g[l],
        mem_k_g[l])
    x1 = _merge(x.reshape(b * t, d), o_nsa, o_fox, o_mem, proj2d, w_o_nsa[l], w_o_fox[l], w_o_mem[l], w_out[l])
    out = _moe_block(x1, norm_ffn_g[l], w_router[l], router_bias[l], we_gate[l], we_up[l], we_down[l],
                     ws_gate[l], ws_up[l], ws_down[l])
    return out.reshape(b, t, d)
```

```python
import functools
import math

import jax
import jax.numpy as jnp
import numpy as np
from jax import lax
from jax.experimental import pallas as pl
from jax.experimental.pallas import tpu as pltpu

D_MODEL = 2048
HEAD_DIM = 128
NSA_HEADS = 8
NSA_KV_HEADS = 2
NSA_GROUP = NSA_HEADS // NSA_KV_HEADS
FOX_HEADS = 4
MEM_HEADS = 4
CMP_LEN = 32
CMP_STRIDE = 16
SLC_LEN = 64
N_SELECT = 16
WINDOW = 512
NUM_BUCKETS = 32
MAX_DISTANCE = 128
N_BRANCHES = 3
N_EXPERTS = 64
TOP_K = 8
D_EXPERT = 512
ROUTED_SCALE = 2.5
ATTN_SCALE = HEAD_DIM ** -0.5
NEG_INF = -1e30
FORCE_SCORE = 1e4
RMS_EPS = 1e-6
LOG2E = math.log2(math.e)

NSA_Q_W = NSA_HEADS * HEAD_DIM
NSA_KV_W = NSA_KV_HEADS * HEAD_DIM
NSA_GATE_W = 3 * NSA_HEADS
FOX_W = FOX_HEADS * HEAD_DIM
MEM_W = MEM_HEADS * HEAD_DIM
MERGE_W = N_BRANCHES * D_MODEL

LANES = 128
VMEM_LIMIT_BYTES = 56 * 1024 * 1024

PROJ_TM = 1024
PROJ_TN = 1536
ATT_TQ = 128
FAR_GROUP = 4
FOX_TQ = 256
MEM_TQ = 512
MERGE_TM = 256
ROUTER_TM = 512
MOE_BLOCK = 256
COMBINE_TM = 256

CB_MERGE = 0
CB_ATT = N_BRANCHES * D_MODEL // LANES
CB_QNSA = CB_ATT
CB_KCMP = CB_ATT + 8
CB_VCMP = CB_ATT + 10
CB_KSLC = CB_ATT + 12
CB_VSLC = CB_ATT + 14
CB_KWIN = CB_ATT + 16
CB_VWIN = CB_ATT + 18
CB_FOXQ = CB_ATT + 20
CB_FOXK = CB_ATT + 24
CB_FOXV = CB_ATT + 28
CB_MEMQ = CB_ATT + 32
PROJ_COLS = (CB_ATT + 36) * LANES
SMALL_W = 2 * LANES
FOXF_LANE = 12

_BF16 = jnp.bfloat16
_F32 = jnp.float32


def _cparams(sem):
    return pltpu.CompilerParams(dimension_semantics=sem, vmem_limit_bytes=VMEM_LIMIT_BYTES)


def _dot(a, b):
    return jnp.dot(a, b, preferred_element_type=_F32)


def _dot_nt(a, b):
    return lax.dot_general(a, b, (((1,), (1,)), ((), ())), preferred_element_type=_F32)


def _dot_tn(a, b):
    return lax.dot_general(a, b, (((0,), (0,)), ((), ())), preferred_element_type=_F32)


def _lanes(parts):
    return jnp.concatenate(parts, axis=1)


def _to_token_major(x):
    c = x.shape[1] // LANES
    chunks = jnp.stack([x[:, j * LANES:(j + 1) * LANES] for j in range(c)], axis=0)
    return pltpu.einshape("ctl->tcl", chunks)


def _from_token_major(x3):
    xt = pltpu.einshape("tcl->ctl", x3)
    return _lanes([xt[j] for j in range(x3.shape[1])])


REORDER_TR = 128
REORDER_TC = 512


def _reorder_kernel(pieces, narrow, w_ref, o_ref, s_ref):
    off = 0
    for src, width in pieces:
        for c in range(0, width, REORDER_TC):
            o_ref[:, off + c:off + c + REORDER_TC] = w_ref[:, src + c:src + c + REORDER_TC].astype(o_ref.dtype)
        off += width
    s_ref[...] = jnp.zeros(s_ref.shape, s_ref.dtype)
    for src, width, dst in narrow:
        s_ref[:, dst:dst + width] = w_ref[:, src:src + width].astype(s_ref.dtype)


def _reorder_cast(w, pieces, narrow, narrow_cols):
    rows, cols = w.shape
    total = sum(width for _, width in pieces)
    assert rows % REORDER_TR == 0 and all(width % REORDER_TC == 0 for _, width in pieces)
    return pl.pallas_call(
        functools.partial(_reorder_kernel, pieces, narrow),
        grid=(rows // REORDER_TR,),
        in_specs=[pl.BlockSpec((REORDER_TR, cols), lambda i: (i, 0))],
        out_specs=[pl.BlockSpec((REORDER_TR, total), lambda i: (i, 0)),
                   pl.BlockSpec((REORDER_TR, narrow_cols), lambda i: (i, 0))],
        out_shape=[jax.ShapeDtypeStruct((rows, total), _BF16), jax.ShapeDtypeStruct((rows, narrow_cols), _BF16)],
        compiler_params=_cparams(("arbitrary",)),
        name="reorder_cast",
    )(w)


def _proj_kernel(norm_j0, has_small, x_ref, g_ref, w_ref, cg_ref, cf_ref, *rest):
    if has_small:
        ws_ref, o_ref, os_ref, h_ref = rest
    else:
        o_ref, h_ref = rest
    j = pl.program_id(1)

    @pl.when(j == 0)
    def _():
        x = x_ref[...]
        ms = jnp.mean(x * x, axis=-1, keepdims=True)
        h = (x * lax.rsqrt(ms + RMS_EPS) * g_ref[...]).astype(_BF16)
        h_ref[...] = h
        if has_small:
            os_ref[...] = _dot(h, ws_ref[...])

    y = _dot(h_ref[...], w_ref[...])
    tn = y.shape[1]

    @pl.when(j >= norm_j0)
    def _():
        for c in range(tn // LANES):
            sl = slice(c * LANES, (c + 1) * LANES)
            yh = y[:, sl]
            ms = jnp.mean(yh * yh, axis=-1, keepdims=True)
            scale = jnp.where(cf_ref[:, sl] > 0.0, lax.rsqrt(ms + RMS_EPS), 1.0)
            o_ref[:, sl] = (yh * scale * cg_ref[:, sl]).astype(o_ref.dtype)

    @pl.when(j < norm_j0)
    def _():
        o_ref[...] = y.astype(o_ref.dtype)


def _rms_project(x2d, g, w, col_gain, col_flag, n_plain_cols, w_small, tm, tn):
    n, d = x2d.shape
    c = w.shape[1]
    has_small = w_small is not None
    in_specs = [
        pl.BlockSpec((tm, d), lambda i, j: (i, 0)),
        pl.BlockSpec((1, d), lambda i, j: (0, 0)),
        pl.BlockSpec((d, tn), lambda i, j: (0, j)),
        pl.BlockSpec((1, tn), lambda i, j: (0, j)),
        pl.BlockSpec((1, tn), lambda i, j: (0, j)),
    ]
    args = [x2d, g.reshape(1, d), w, col_gain.reshape(1, c), col_flag.reshape(1, c)]
    out_shape = [jax.ShapeDtypeStruct((n, c), _BF16)]
    out_specs = [pl.BlockSpec((tm, tn), lambda i, j: (i, j))]
    if has_small:
        ws = w_small.shape[1]
        in_specs.append(pl.BlockSpec((d, ws), lambda i, j: (0, 0)))
        args.append(w_small)
        out_shape.append(jax.ShapeDtypeStruct((n, ws), _F32))
        out_specs.append(pl.BlockSpec((tm, ws), lambda i, j: (i, 0)))
    assert n % tm == 0 and c % tn == 0 and n_plain_cols % tn == 0
    res = pl.pallas_call(
        functools.partial(_proj_kernel, n_plain_cols // tn, has_small),
        grid=(n // tm, c // tn),
        in_specs=in_specs,
        out_specs=out_specs,
        out_shape=out_shape,
        scratch_shapes=[pltpu.VMEM((tm, d), _BF16)],
        compiler_params=_cparams(("arbitrary", "arbitrary")),
        name="rms_project",
    )(*args)
    return res if has_small else res[0]


def _cmp_kernel(x_ref, wlo_ref, whi_ref, pelo_ref, pehi_ref, kg_ref, o_ref):
    j = pl.program_id(1)
    x = x_ref[0, 0]
    nchunk = x.shape[0]
    ylo = _dot(x, wlo_ref[0])
    yhi = _dot(x, whi_ref[0])
    pe = _dot(pelo_ref[0], wlo_ref[0]) + _dot(pehi_ref[0], whi_ref[0])
    y = ylo + pltpu.roll(yhi, nchunk - 1, 0) + pe[0:1, :]

    @pl.when(j < NSA_KV_HEADS)
    def _():
        ms = jnp.mean(y * y, axis=-1, keepdims=True)
        o_ref[0, 0] = (y * lax.rsqrt(ms + RMS_EPS) * kg_ref[...]).astype(o_ref.dtype)

    @pl.when(j >= NSA_KV_HEADS)
    def _():
        o_ref[0, 0] = y.T.astype(o_ref.dtype)


def _nsa_compress(cmp_in, w_ck, w_cv, pe_k, pe_v, k_g):
    b, nj, t, dk = cmp_in.shape
    nchunk = t // CMP_STRIDE
    half = CMP_LEN // 2
    assert nchunk == dk
    x = cmp_in.reshape(b, nj, nchunk, CMP_STRIDE * dk)

    def halves(w):
        return (w[:half].reshape(half * dk, dk).astype(_BF16),
                w[half:].reshape(half * dk, dk).astype(_BF16))

    klo, khi = halves(w_ck)
    vlo, vhi = halves(w_cv)
    wlo = jnp.stack([klo, vlo])
    whi = jnp.stack([khi, vhi])

    def pe_halves(pe):
        lo = jnp.broadcast_to(pe[:half].reshape(1, half * dk), (8, half * dk)).astype(_BF16)
        hi = jnp.broadcast_to(pe[half:].reshape(1, half * dk), (8, half * dk)).astype(_BF16)
        return lo, hi

    pklo, pkhi = pe_halves(pe_k)
    pvlo, pvhi = pe_halves(pe_v)
    pelo = jnp.stack([pklo, pvlo])
    pehi = jnp.stack([pkhi, pvhi])
    kv = lambda bb, j: (j // NSA_KV_HEADS, 0, 0)
    return pl.pallas_call(
        _cmp_kernel,
        grid=(b, nj),
        in_specs=[
            pl.BlockSpec((1, 1, nchunk, CMP_STRIDE * dk), lambda bb, j: (bb, j, 0, 0)),
            pl.BlockSpec((1, half * dk, dk), kv),
            pl.BlockSpec((1, half * dk, dk), kv),
            pl.BlockSpec((1, 8, half * dk), kv),
            pl.BlockSpec((1, 8, half * dk), kv),
            pl.BlockSpec((1, dk), lambda bb, j: (0, 0)),
        ],
        out_specs=pl.BlockSpec((1, 1, nchunk, dk), lambda bb, j: (bb, j, 0, 0)),
        out_shape=jax.ShapeDtypeStruct((b, nj, nchunk, dk), _BF16),
        compiler_params=_cparams(("arbitrary", "arbitrary")),
        name="nsa_compress",
    )(x, wlo, whi, pelo, pehi, k_g.reshape(1, dk).astype(_F32))


def _t_update(state, tiles):
    m, l, acc = state
    masked = [s if mask is None else jnp.where(mask, s, NEG_INF) for s, mask, _ in tiles]
    m_new = m
    for sm in masked:
        m_new = jnp.maximum(m_new, jnp.max(sm, axis=0, keepdims=True))
    alpha = jnp.exp2(m - m_new)
    l_new = alpha * l
    acc_new = alpha * acc
    for sm, (_, _, v) in zip(masked, tiles):
        e = jnp.exp2(sm - m_new)
        l_new = l_new + jnp.sum(e, axis=0, keepdims=True)
        acc_new = acc_new + _dot_tn(v, e.astype(_BF16))
    return m_new, l_new, acc_new


def _t_init(nq):
    return (jnp.full((1, nq), NEG_INF, _F32), jnp.zeros((1, nq), _F32), jnp.zeros((HEAD_DIM, nq), _F32))


def _t_finish(state):
    _, l, acc = state
    return acc / l


def _nsa_kernel(q_ref, kvc_ref, ks_ref, vs_ref, kw_ref, vw_ref, gl_ref,
                wt_ref, dt_ref, ovt_ref, o_ref, sel_ref):
    i = pl.program_id(1)
    tq = ATT_TQ
    hg = NSA_GROUP
    nq = hg * tq
    n_slc = ovt_ref.shape[0]
    groups = range(NSA_KV_HEADS)

    key = lax.broadcasted_iota(jnp.int32, (LANES, nq), 0)
    qry = lax.broadcasted_iota(jnp.int32, (LANES, nq), 1) & (tq - 1)

    def gcols(g):
        return slice(g * LANES, (g + 1) * LANES)

    def ktile(ref, g, kt):
        return ref[0, pl.ds(pl.multiple_of(kt * LANES, LANES), LANES), gcols(g)]

    def near_bias(g, d):
        return _lanes([dt_ref[g * hg + h, d] for h in range(hg)])

    q = q_ref[0]
    qts = [_lanes([q[:, (g * hg + h) * HEAD_DIM:(g * hg + h + 1) * HEAD_DIM].astype(_F32).T
                   for h in range(hg)]).astype(_BF16) for g in groups]

    woff = pl.multiple_of(wt_ref.shape[1] - LANES - 8 - (LANES // CMP_STRIDE) * i, 8)
    s_c = [_dot(kvc_ref[0, g], qts[g]) + _lanes([wt_ref[g * hg + h, pl.ds(woff, LANES), :] for h in range(hg)])
           for g in groups]
    n_win = WINDOW // tq
    win_tiles = [[] for _ in groups]
    for g in groups:
        for d in range(n_win, -1, -1):
            ktd = jnp.maximum(i - d, 0)
            sc = _dot(ktile(kw_ref, g, ktd), qts[g])
            if d <= 1:
                sc = sc + near_bias(g, d)
            if d == n_win:
                mk = (qry < key) & (i >= d)
            elif d == 0:
                mk = key <= qry
            else:
                mk = jnp.broadcast_to(i >= d, (LANES, nq))
            win_tiles[g].append((sc, mk, ktile(vw_ref, g, ktd)))

    mask_c = (i * tq + qry) >= (CMP_STRIDE * key + CMP_LEN - 1)
    blk = lax.broadcasted_iota(jnp.int32, (n_slc, tq), 0)
    tpos = i * tq + lax.broadcasted_iota(jnp.int32, (n_slc, tq), 1)
    cur = tpos // SLC_LEN
    valid = blk * SLC_LEN <= tpos
    forced = (blk == 0) | (blk == cur) | (blk == cur - 1)
    o_c = []
    for g in groups:
        s = s_c[g]
        mx = jnp.max(jnp.where(mask_c, s, NEG_INF), axis=0, keepdims=True)
        e = jnp.where(mask_c, jnp.exp2(s - mx), 0.0)
        l = jnp.sum(e, axis=0, keepdims=True)
        p_c = e / jnp.where(l > 0.0, l, 1.0)
        o_c.append(_dot(kvc_ref[0, NSA_KV_HEADS + g], p_c.astype(_BF16)))
        ps = p_c[:, 0:tq]
        for h in range(1, hg):
            ps = ps + p_c[:, h * tq:(h + 1) * tq]
        ps_hi = ps.astype(_BF16)
        ps_lo = (ps - ps_hi.astype(_F32)).astype(_BF16)
        imp = _dot(ovt_ref[...], ps_hi) + _dot(ovt_ref[...], ps_lo)
        score = jnp.where(valid, jnp.where(forced, FORCE_SCORE, imp), -1.0)
        rank = jnp.zeros((n_slc, tq), _F32)
        for k in range(n_slc):
            ck = score[k:k + 1, :]
            beats = (ck > score) | ((ck == score) & (blk > k))
            rank = rank + jnp.where(beats, 1.0, 0.0)
        selb = jnp.where(rank < float(min(N_SELECT, n_slc)), 0.0, NEG_INF)
        sel_ref[g] = _lanes([selb] * hg)

    def sel_bias(g, kt, ok=True):
        rows = []
        for r in range(LANES // SLC_LEN):
            row = sel_ref[g, pl.ds((LANES // SLC_LEN) * kt + r, 1), :]
            rows.append(jnp.broadcast_to(jnp.where(ok, row, NEG_INF), (SLC_LEN, nq)))
        return jnp.concatenate(rows, axis=0)

    o_w = [_t_finish(_t_update(_t_init(nq), win_tiles[g])) for g in groups]

    n_far = jnp.maximum(i - 1, 0)

    def far_body(j, sts):
        k0 = pl.multiple_of(j * (FAR_GROUP * LANES), FAR_GROUP * LANES)
        sgs = [_dot(ks_ref[0, pl.ds(k0, FAR_GROUP * LANES), gcols(g)], qts[g]) for g in groups]
        out = []
        for g in groups:
            tiles = []
            for u in range(FAR_GROUP):
                kt = FAR_GROUP * j + u
                tiles.append((sgs[g][u * LANES:(u + 1) * LANES] + sel_bias(g, kt, kt < n_far), None,
                              ktile(vs_ref, g, kt)))
            out.append(_t_update(sts[g], tiles))
        return tuple(out)

    sts = lax.fori_loop(0, (n_far + FAR_GROUP - 1) // FAR_GROUP, far_body, tuple(_t_init(nq) for _ in groups))
    kt1 = jnp.maximum(i - 1, 0)
    near = [(_dot(ktile(ks_ref, g, kt1), qts[g]) + near_bias(g, 1), _dot(ktile(ks_ref, g, i), qts[g]) + near_bias(g, 0))
            for g in groups]
    o_s = [_t_finish(_t_update(sts[g], [
        (near[g][0] + sel_bias(g, kt1, i >= 1), None, ktile(vs_ref, g, kt1)),
        (near[g][1] + sel_bias(g, i), key <= qry, ktile(vs_ref, g, i))])) for g in groups]

    for g in groups:
        glt = jax.nn.sigmoid(gl_ref[0, :, gcols(g)]).T
        grow = lambda br: _lanes([glt[br * hg + h:br * hg + h + 1, :] for h in range(hg)])
        o = grow(0) * o_c[g] + grow(1) * o_s[g] + grow(2) * o_w[g]
        for h in range(hg):
            hd = (g * hg + h) * HEAD_DIM
            o_ref[0, :, hd:hd + HEAD_DIM] = o[:, h * tq:(h + 1) * tq].T.astype(o_ref.dtype)


def _t5_bucket(rel):
    n = np.maximum(rel, 0)
    max_exact = NUM_BUCKETS // 2
    ratio = np.maximum(n, 1).astype(np.float32) / np.float32(max_exact)
    log_ratio = np.log(ratio) / np.float32(math.log(MAX_DISTANCE / max_exact))
    large = np.minimum(max_exact + (log_ratio * np.float32(NUM_BUCKETS - max_exact)).astype(np.int32),
                       NUM_BUCKETS - 1)
    return np.where(n < max_exact, n, large).astype(np.int32)


def _bias_lookup(rb, bucket):
    bk = jnp.asarray(bucket.astype(np.int8))[None]
    ex = (slice(None),) + (None,) * bucket.ndim
    tab = jnp.broadcast_to(rb[:, 0][ex], (rb.shape[0],) + bucket.shape)
    for k in range(1, NUM_BUCKETS):
        tab = jnp.where(bk == k, rb[:, k][ex], tab)
    return tab


def _nsa_tables(rel_bias):
    rb = rel_bias.astype(_F32).T * LOG2E
    r = np.arange(ATT_TQ)
    rel_d = (np.arange(2) * ATT_TQ)[:, None, None] + r[None, None, :] - r[None, :, None]
    assert _t5_bucket(np.array([ATT_TQ + 1]))[0] == NUM_BUCKETS - 1
    u = np.arange(2 * LANES)
    rel_w = r[None, :] - CMP_STRIDE * (u[:, None] - (LANES - 8)) - (CMP_LEN - 1)
    far = rb[:, NUM_BUCKETS - 1]
    dt = _bias_lookup(rb, _t5_bucket(rel_d)) - far[:, None, None, None]
    return dt, _bias_lookup(rb, _t5_bucket(rel_w))


def _nsa_attention(proj, small, kvc, rel_bias):
    b, t, _ = proj.shape
    g, hg, tq = NSA_KV_HEADS, NSA_GROUP, ATT_TQ
    n_slc = t // SLC_LEN
    n_cmp = (t - CMP_LEN) // CMP_STRIDE + 1
    nt = t // LANES
    assert n_cmp <= LANES and kvc.shape[2] == LANES and n_slc % 16 == 0 and tq == LANES
    assert nt % FAR_GROUP == 0
    assert LANES - 8 - (LANES // CMP_STRIDE) * (t // tq - 1) >= 0
    dt, wt = _nsa_tables(rel_bias)
    cstart = np.arange(LANES) * CMP_STRIDE
    sstart = np.arange(n_slc) * SLC_LEN
    overlap = np.clip(np.minimum(cstart[None, :] + CMP_LEN, sstart[:, None] + SLC_LEN)
                      - np.maximum(cstart[None, :], sstart[:, None]), 0, None).astype(np.float32) / CMP_LEN
    overlap[:, n_cmp:] = 0.0
    ovt = jnp.asarray(overlap, _BF16)

    gw = g * LANES
    seq = lambda cb: pl.BlockSpec((1, t, gw), lambda bb, ii: (bb, 0, cb * LANES // gw))
    heads = g * hg
    return pl.pallas_call(
        _nsa_kernel,
        grid=(b, t // tq),
        in_specs=[
            pl.BlockSpec((1, tq, NSA_Q_W), lambda bb, ii: (bb, ii, CB_QNSA * LANES // NSA_Q_W)),
            pl.BlockSpec((1, 2 * g, LANES, HEAD_DIM), lambda bb, ii: (bb, 0, 0, 0)),
            seq(CB_KSLC), seq(CB_VSLC), seq(CB_KWIN), seq(CB_VWIN),
            pl.BlockSpec((1, tq, SMALL_W), lambda bb, ii: (bb, ii, 0)),
            pl.BlockSpec((heads, 2 * LANES, tq), lambda bb, ii: (0, 0, 0)),
            pl.BlockSpec((heads, 2, LANES, tq), lambda bb, ii: (0, 0, 0, 0)),
            pl.BlockSpec((n_slc, LANES), lambda bb, ii: (0, 0)),
        ],
        out_specs=pl.BlockSpec((1, tq, NSA_Q_W), lambda bb, ii: (bb, ii, 0)),
        out_shape=jax.ShapeDtypeStruct((b, t, NSA_Q_W), _BF16),
        scratch_shapes=[pltpu.VMEM((g, n_slc, hg * tq), _F32)],
        compiler_params=_cparams(("arbitrary", "arbitrary")),
        name="nsa_attention",
    )(proj, kvc, proj, proj, proj, proj, small, wt, dt, ovt)


def _fox_cum_kernel(s_ref, b_ref, col_ref, row_ref):
    z = s_ref[0] + b_ref[...]
    lf = (jnp.minimum(z, 0.0) - jnp.log1p(jnp.exp(-jnp.abs(z)))) * LOG2E
    x = lf.T
    t = x.shape[1]
    lane = lax.broadcasted_iota(jnp.int32, x.shape, 1)
    sh = 1
    while sh < t:
        x = x + jnp.where(lane >= sh, pltpu.roll(x, sh, 1), 0.0)
        sh *= 2
    row_ref[0] = x[8:16, :]
    col_ref[0] = x.T


def _fox_cumsum(small, b_f):
    b, t, _ = small.shape
    bvec = jnp.zeros((1, LANES), _F32).at[0, FOXF_LANE:FOXF_LANE + FOX_HEADS].set(b_f.astype(_F32))
    return pl.pallas_call(
        _fox_cum_kernel,
        grid=(b,),
        in_specs=[pl.BlockSpec((1, t, LANES), lambda bb: (bb, 0, 1)),
                  pl.BlockSpec((1, LANES), lambda bb: (0, 0))],
        out_specs=[pl.BlockSpec((1, t, LANES), lambda bb: (bb, 0, 0)),
                   pl.BlockSpec((1, 8, t), lambda bb: (bb, 0, 0))],
        out_shape=[jax.ShapeDtypeStruct((b, t, LANES), _F32), jax.ShapeDtypeStruct((b, 8, t), _F32)],
        compiler_params=_cparams(("arbitrary",)),
        name="fox_cumsum",
    )(small, bvec)


def _fox_kernel(q_ref, k_ref, v_ref, cc_ref, cr_ref, o_ref):
    i = pl.program_id(1)
    tq = FOX_TQ
    key = lax.broadcasted_iota(jnp.int32, (tq, tq), 0)
    qry = lax.broadcasted_iota(jnp.int32, (tq, tq), 1)
    heads = range(FOX_HEADS)
    hsl = [slice(h * HEAD_DIM, (h + 1) * HEAD_DIM) for h in heads]
    qts = [q_ref[0, :, hsl[h]].astype(_F32).T.astype(_BF16) for h in heads]
    cqs = [cr_ref[0, FOXF_LANE - 8 + h:FOXF_LANE - 8 + h + 1, :] for h in heads]

    def keys(kt):
        return pl.ds(pl.multiple_of(kt * tq, tq), tq)

    def logits(h, kt):
        ck = cc_ref[0, keys(kt), FOXF_LANE + h:FOXF_LANE + h + 1]
        return _dot(k_ref[0, keys(kt), hsl[h]], qts[h]) + (cqs[h] - ck)

    def body(kt, sts):
        ss = [logits(h, kt) for h in heads]
        return tuple(_t_update(sts[h], [(ss[h], None, v_ref[0, keys(kt), hsl[h]])]) for h in heads)

    sts = lax.fori_loop(0, i, body, tuple(_t_init(tq) for _ in heads))
    ss = [logits(h, i) for h in heads]
    for h in heads:
        st = _t_update(sts[h], [(ss[h], key <= qry, v_ref[0, keys(i), hsl[h]])])
        o_ref[0, :, hsl[h]] = _t_finish(st).T.astype(o_ref.dtype)


def _fox_attention(proj, cum_col, cum_row):
    b, t, _ = proj.shape
    tq = FOX_TQ
    w = FOX_W
    return pl.pallas_call(
        _fox_kernel,
        grid=(b, t // tq),
        in_specs=[
            pl.BlockSpec((1, tq, w), lambda bb, ii: (bb, ii, CB_FOXQ * LANES // w)),
            pl.BlockSpec((1, t, w), lambda bb, ii: (bb, 0, CB_FOXK * LANES // w)),
            pl.BlockSpec((1, t, w), lambda bb, ii: (bb, 0, CB_FOXV * LANES // w)),
            pl.BlockSpec((1, t, LANES), lambda bb, ii: (bb, 0, 0)),
            pl.BlockSpec((1, 8, tq), lambda bb, ii: (bb, 0, ii)),
        ],
        out_specs=pl.BlockSpec((1, tq, w), lambda bb, ii: (bb, ii, 0)),
        out_shape=jax.ShapeDtypeStruct((b, t, w), _BF16),
        compiler_params=_cparams(("arbitrary", "arbitrary")),
        name="fox_attention",
    )(proj, proj, proj, cum_col, cum_row)


def _mem_kernel(q_ref, kv_ref, o_ref):
    for h in range(MEM_HEADS):
        hs = slice(h * HEAD_DIM, (h + 1) * HEAD_DIM)
        vs = slice(MEM_W + h * HEAD_DIM, MEM_W + (h + 1) * HEAD_DIM)
        s = _dot_nt(q_ref[0, :, hs], kv_ref[0, :, hs])
        e = jnp.exp2(s - jnp.max(s, axis=1, keepdims=True))
        p = e / jnp.sum(e, axis=1, keepdims=True)
        o_ref[0, :, hs] = _dot(p.astype(_BF16), kv_ref[0, :, vs]).astype(o_ref.dtype)


def _mem_attention(proj, memkv):
    b, t, _ = proj.shape
    m = memkv.shape[1]
    tq = MEM_TQ
    return pl.pallas_call(
        _mem_kernel,
        grid=(b, t // tq),
        in_specs=[pl.BlockSpec((1, tq, MEM_W), lambda bb, ii: (bb, ii, CB_MEMQ * LANES // MEM_W)),
                  pl.BlockSpec((1, m, 2 * MEM_W), lambda bb, ii: (bb, 0, 0))],
        out_specs=pl.BlockSpec((1, tq, MEM_W), lambda bb, ii: (bb, ii, 0)),
        out_shape=jax.ShapeDtypeStruct((b, t, MEM_W), _BF16),
        compiler_params=_cparams(("arbitrary", "arbitrary")),
        name="mem_attention",
    )(proj, memkv)


def _merge_kernel(x_ref, on_ref, of_ref, om_ref, g0_ref, g1_ref, g2_ref, wn_ref, wf_ref, wm_ref, wo_ref,
                  o_ref, z_ref):
    tn = 512
    for c in range(D_MODEL // tn):
        cs = slice(c * tn, (c + 1) * tn)
        z = (jax.nn.sigmoid(g0_ref[:, cs].astype(_F32)) * _dot(on_ref[...], wn_ref[:, cs])
             + jax.nn.sigmoid(g1_ref[:, cs].astype(_F32)) * _dot(of_ref[...], wf_ref[:, cs])
             + jax.nn.sigmoid(g2_ref[:, cs].astype(_F32)) * _dot(om_ref[...], wm_ref[:, cs]))
        z_ref[:, cs] = z.astype(_BF16)
    o_ref[...] = x_ref[...] + _dot(z_ref[...], wo_ref[...])


def _merge(x2d, o_nsa, o_fox, o_mem, proj2d, w_o_nsa, w_o_fox, w_o_mem, w_out):
    n, d = x2d.shape
    tm = MERGE_TM
    gcb = CB_MERGE * LANES // d
    const = lambda r, c: pl.BlockSpec((r, c), lambda i: (0, 0))
    return pl.pallas_call(
        _merge_kernel,
        grid=(n // tm,),
        in_specs=[
            pl.BlockSpec((tm, d), lambda i: (i, 0)),
            pl.BlockSpec((tm, NSA_Q_W), lambda i: (i, 0)),
            pl.BlockSpec((tm, FOX_W), lambda i: (i, 0)),
            pl.BlockSpec((tm, MEM_W), lambda i: (i, 0)),
            pl.BlockSpec((tm, d), lambda i: (i, gcb)),
            pl.BlockSpec((tm, d), lambda i: (i, gcb + 1)),
            pl.BlockSpec((tm, d), lambda i: (i, gcb + 2)),
            const(NSA_Q_W, d), const(FOX_W, d), const(MEM_W, d), const(d, d),
        ],
        out_specs=pl.BlockSpec((tm, d), lambda i: (i, 0)),
        out_shape=jax.ShapeDtypeStruct((n, d), _F32),
        scratch_shapes=[pltpu.VMEM((tm, d), _BF16)],
        compiler_params=_cparams(("arbitrary",)),
        name="merge_out",
    )(x2d, o_nsa, o_fox, o_mem, proj2d, proj2d, proj2d,
      w_o_nsa.astype(_BF16), w_o_fox.astype(_BF16), w_o_mem.astype(_BF16), w_out.astype(_BF16))


def _nsa_inputs(x, norm_attn_g, w_in, nsa_pe_k, nsa_w_ck, nsa_pe_v, nsa_w_cv, nsa_q_g, nsa_k_g,
                fox_q_g, fox_k_g, mem_q_g):
    b, t, d = x.shape
    ones = lambda k: jnp.ones((k * HEAD_DIM,), _F32)
    zeros = lambda k: jnp.zeros((k * HEAD_DIM,), _F32)
    tile = lambda gv, k: jnp.tile(gv.astype(_F32), k)
    qs = ATTN_SCALE * LOG2E

    g_end = NSA_Q_W + 6 * NSA_KV_W
    f_off = g_end + NSA_GATE_W + 3 * FOX_W
    m_off = f_off + FOX_HEADS + MEM_W
    narrow = tuple((g_end + (br * NSA_KV_HEADS + gg) * NSA_GROUP, NSA_GROUP, gg * LANES + br * NSA_GROUP)
                   for br in range(3) for gg in range(NSA_KV_HEADS))
    narrow += ((f_off, FOX_HEADS, LANES + FOXF_LANE),)
    w_all, w_small = _reorder_cast(
        w_in, ((m_off, MERGE_W), (0, g_end), (g_end + NSA_GATE_W, 3 * FOX_W), (f_off + FOX_HEADS, MEM_W)),
        narrow, SMALL_W)
    col_gain = jnp.concatenate([
        jnp.ones((MERGE_W,), _F32),
        tile(nsa_q_g, NSA_HEADS) * qs, ones(2), ones(2), tile(nsa_k_g, 2), ones(2), tile(nsa_k_g, 2), ones(2),
        tile(fox_q_g, FOX_HEADS) * qs, tile(fox_k_g, FOX_HEADS), ones(FOX_HEADS),
        tile(mem_q_g, MEM_HEADS) * qs])
    col_flag = jnp.concatenate([
        jnp.zeros((MERGE_W,), _F32),
        ones(NSA_HEADS), zeros(2), zeros(2), ones(2), zeros(2), ones(2), zeros(2),
        ones(FOX_HEADS), ones(FOX_HEADS), zeros(FOX_HEADS), ones(MEM_HEADS)])

    x2d = x.reshape(b * t, d)
    proj2d, small2d = _rms_project(x2d, norm_attn_g, w_all, col_gain, col_flag, CB_ATT * LANES, w_small,
                                   PROJ_TM, PROJ_TN)
    proj = proj2d.reshape(b, t, PROJ_COLS)
    small = small2d.reshape(b, t, SMALL_W)

    cmp_in = proj[:, :, CB_KCMP * LANES:(CB_KCMP + 4) * LANES].reshape(b, t, 4, HEAD_DIM).transpose(0, 2, 1, 3)
    kvc = _nsa_compress(cmp_in, nsa_w_ck, nsa_w_cv, nsa_pe_k, nsa_pe_v, nsa_k_g)
    return proj, small, kvc


def _attention_mixers(x, mem, norm_attn_g, w_in, nsa_pe_k, nsa_w_ck, nsa_pe_v, nsa_w_cv, nsa_q_g, nsa_k_g,
                      rel_bias, fox_b_f, fox_q_g, fox_k_g, norm_mem_g, w_mem_kv, mem_q_g, mem_k_g):
    b, t, d = x.shape
    ones = lambda k: jnp.ones((k * HEAD_DIM,), _F32)
    zeros = lambda k: jnp.zeros((k * HEAD_DIM,), _F32)
    tile = lambda gv, k: jnp.tile(gv.astype(_F32), k)
    proj, small, kvc = _nsa_inputs(x, norm_attn_g, w_in, nsa_pe_k, nsa_w_ck, nsa_pe_v, nsa_w_cv, nsa_q_g, nsa_k_g,
                                   fox_q_g, fox_k_g, mem_q_g)
    proj2d = proj.reshape(b * t, PROJ_COLS)
    o_nsa = _nsa_attention(proj, small, kvc, rel_bias)

    cum_col, cum_row = _fox_cumsum(small, fox_b_f)
    o_fox = _fox_attention(proj, cum_col, cum_row)

    mm = mem.shape[1]
    mem_gain = jnp.concatenate([tile(mem_k_g, MEM_HEADS), ones(MEM_HEADS)])
    mem_flag = jnp.concatenate([ones(MEM_HEADS), zeros(MEM_HEADS)])
    memkv = _rms_project(mem.reshape(b * mm, d), norm_mem_g, w_mem_kv.astype(_BF16), mem_gain, mem_flag,
                         0, None, mm, MEM_W).reshape(b, mm, 2 * MEM_W)
    o_mem = _mem_attention(proj, memkv)
    return proj2d, o_nsa.reshape(b * t, NSA_Q_W), o_fox.reshape(b * t, FOX_W), o_mem.reshape(b * t, MEM_W)


def _router_kernel(x_ref, g_ref, whi_ref, wlo_ref, b_ref, h_ref, idx_ref, gate_ref, cnt_ref):
    x = x_ref[...]
    ms = jnp.mean(x * x, axis=-1, keepdims=True)
    h = x * lax.rsqrt(ms + RMS_EPS) * g_ref[...]
    h_hi = h.astype(_BF16)
    h_ref[...] = _to_token_major(h_hi)
    h_lo = (h - h_hi.astype(_F32)).astype(_BF16)
    logits = _dot_nt(whi_ref[...], h_hi) + _dot_nt(whi_ref[...], h_lo) + _dot_nt(wlo_ref[...], h_hi)
    scores = jax.nn.sigmoid(logits)
    sb = scores + b_ref[...]
    eidx = lax.broadcasted_iota(jnp.int32, sb.shape, 0)
    idxs, vals = [], []
    for _ in range(TOP_K):
        m = jnp.max(sb, axis=0, keepdims=True)
        idx = jnp.min(jnp.where(sb == m, eidx, N_EXPERTS), axis=0, keepdims=True)
        hit = eidx == idx
        vals.append(jnp.sum(jnp.where(hit, scores, 0.0), axis=0, keepdims=True))
        idxs.append(idx)
        sb = jnp.where(hit, NEG_INF, sb)
    top_s = jnp.concatenate(vals, axis=0)
    idx_ref[...] = jnp.concatenate(idxs, axis=0)
    gate_ref[...] = top_s / jnp.sum(top_s, axis=0, keepdims=True) * ROUTED_SCALE

    @pl.when(pl.program_id(0) == 0)
    def _():
        cnt_ref[...] = jnp.zeros(cnt_ref.shape, cnt_ref.dtype)

    picked = jnp.where(sb < 0.5 * NEG_INF, 1.0, 0.0)
    cnt_ref[...] += jnp.sum(picked, axis=1, keepdims=True)


def _router(x1, norm_g, w_router, router_bias):
    n, d = x1.shape
    tm = ROUTER_TM
    wt = w_router.astype(_F32).T
    w_hi = wt.astype(_BF16)
    w_lo = (wt - w_hi.astype(_F32)).astype(_BF16)
    const = lambda r, c: pl.BlockSpec((r, c), lambda i: (0, 0))
    return pl.pallas_call(
        _router_kernel,
        grid=(n // tm,),
        in_specs=[pl.BlockSpec((tm, d), lambda i: (i, 0)), const(1, d), const(N_EXPERTS, d), const(N_EXPERTS, d),
                  const(N_EXPERTS, 1)],
        out_specs=[pl.BlockSpec((tm, d // LANES, LANES), lambda i: (i, 0, 0)),
                   pl.BlockSpec((TOP_K, tm), lambda i: (0, i)),
                   pl.BlockSpec((TOP_K, tm), lambda i: (0, i)),
                   const(N_EXPERTS, 1)],
        out_shape=[jax.ShapeDtypeStruct((n, d // LANES, LANES), _BF16), jax.ShapeDtypeStruct((TOP_K, n), jnp.int32),
                   jax.ShapeDtypeStruct((TOP_K, n), _F32), jax.ShapeDtypeStruct((N_EXPERTS, 1), _F32)],
        compiler_params=_cparams(("arbitrary",)),
        name="moe_router",
    )(x1, norm_g.reshape(1, d).astype(_F32), w_hi, w_lo, router_bias.reshape(N_EXPERTS, 1).astype(_F32))


def _moe_kernel(be_ref, nbr_ref, src_hbm, dst_hbm, h_hbm, sg_ref, wg_ref, wu_ref, wd_ref, y_hbm,
                xbuf, ybuf, xmat, sidx, didx, wg_bf, wu_bf, wd_bf, gsem, ssem, isem):
    n = pl.program_id(0)
    nbr = nbr_ref[0]
    last = nbr - 1
    slot = n % 2
    other = 1 - slot
    blk = MOE_BLOCK
    dump_block = dst_hbm.shape[0] - 1
    ring = didx.shape[0]

    def src_copy(block, s):
        return pltpu.make_async_copy(src_hbm.at[block], sidx.at[s], isem.at[0, s])

    def dst_copy(block, s3):
        return pltpu.make_async_copy(dst_hbm.at[block], didx.at[s3], isem.at[1, s3])

    def issue_gather(s):
        for r in range(blk):
            tok = sidx[s, 0, r]
            pltpu.make_async_copy(h_hbm.at[tok], xbuf.at[s, r], gsem.at[s]).start(priority=r % 2)

    def issue_scatter(s, s3):
        for r in range(blk):
            row = didx[s3, 0, r]
            pltpu.make_async_copy(ybuf.at[s, r], y_hbm.at[row], ssem.at[s]).start(priority=r % 2)

    def wait_rows(buf, sem, s):
        pltpu.make_async_copy(buf.at[s], buf.at[s], sem.at[s]).wait()

    @pl.when(n < nbr)
    def _():
        @pl.when(n == 0)
        def _():
            src_copy(0, 0).start()
            dst_copy(0, 0).start()
            dst_copy(dump_block, ring - 1).start()
            ybuf[1] = jnp.zeros(ybuf.shape[1:], ybuf.dtype)
            src_copy(0, 0).wait()
            dst_copy(0, 0).wait()
            dst_copy(dump_block, ring - 1).wait()
            issue_gather(0)
            nxt0 = jnp.minimum(1, last)
            src_copy(nxt0, 1).start()
            dst_copy(nxt0, 1).start()

        wait_rows(xbuf, gsem, slot)

        @pl.when(n < last)
        def _():
            nn = jnp.minimum(n + 2, last)
            src_copy(nn, slot).start()
            dst_copy(nn, (n + 2) % ring).start()

        changed = (n == 0) | (be_ref[n] != be_ref[jnp.maximum(n - 1, 0)])

        @pl.when(changed)
        def _():
            wg_bf[...] = wg_ref[0].astype(_BF16)
            wu_bf[...] = wu_ref[0].astype(_BF16)
            wd_bf[...] = wd_ref[0].astype(_BF16)

        @pl.when(n >= 1)
        def _():
            wait_rows(ybuf, ssem, slot)

        nxt = jnp.minimum(n + 1, last)
        src_copy(nxt, other).wait()
        dst_copy(nxt, (n + 1) % ring).wait()

        issue_scatter(other, (n + ring - 1) % ring)
        xmat[...] = _from_token_major(xbuf[slot])
        issue_gather(other)
        gcol = jnp.broadcast_to(sg_ref[0], (LANES, blk)).T
        a = jax.nn.silu(_dot(xmat[...], wg_bf[...])) * _dot(xmat[...], wu_bf[...])
        a = (a * _lanes([gcol] * (a.shape[1] // LANES))).astype(_BF16)
        ybuf[slot] = _to_token_major(_dot(a, wd_bf[...]).astype(_BF16))

        @pl.when(n == last)
        def _():
            issue_scatter(slot, n % ring)
            wait_rows(xbuf, gsem, other)
            wait_rows(ybuf, ssem, other)
            wait_rows(ybuf, ssem, slot)


def _moe_experts(h3, block_expert, nb_real, slot_src, slot_dst, slot_gate, we_gate, we_up, we_down):
    n, c, _ = h3.shape
    d = c * LANES
    nb = block_expert.shape[0]
    blk = MOE_BLOCK
    e, _, de = we_gate.shape
    grid_spec = pltpu.PrefetchScalarGridSpec(
        num_scalar_prefetch=2,
        grid=(nb,),
        in_specs=[
            pl.BlockSpec(memory_space=pl.ANY),
            pl.BlockSpec(memory_space=pl.ANY),
            pl.BlockSpec(memory_space=pl.ANY),
            pl.BlockSpec((1, 1, blk), lambda i, be, nbr: (i, 0, 0)),
            pl.BlockSpec((1, d, de), lambda i, be, nbr: (be[i], 0, 0)),
            pl.BlockSpec((1, d, de), lambda i, be, nbr: (be[i], 0, 0)),
            pl.BlockSpec((1, de, d), lambda i, be, nbr: (be[i], 0, 0)),
        ],
        out_specs=pl.BlockSpec(memory_space=pl.ANY),
        scratch_shapes=[
            pltpu.VMEM((2, blk, c, LANES), _BF16), pltpu.VMEM((2, blk, c, LANES), _BF16),
            pltpu.VMEM((blk, d), _BF16),
            pltpu.SMEM((2, 1, blk), jnp.int32), pltpu.SMEM((4, 1, blk), jnp.int32),
            pltpu.VMEM((d, de), _BF16), pltpu.VMEM((d, de), _BF16), pltpu.VMEM((de, d), _BF16),
            pltpu.SemaphoreType.DMA((2,)), pltpu.SemaphoreType.DMA((2,)), pltpu.SemaphoreType.DMA((2, 4)),
        ],
    )
    return pl.pallas_call(
        _moe_kernel,
        grid_spec=grid_spec,
        out_shape=jax.ShapeDtypeStruct((TOP_K * n + blk, c, LANES), _BF16),
        compiler_params=_cparams(("arbitrary",)),
        name="moe_experts",
    )(block_expert, nb_real, slot_src, slot_dst, h3, slot_gate, we_gate, we_up, we_down)


def _dispatch_plan(top_idx, gates, counts, n):
    blk = MOE_BLOCK
    a = n * TOP_K
    nb = (a + N_EXPERTS * (blk - 1)) // blk
    order = jnp.argsort(top_idx.reshape(a)).astype(jnp.int32)
    counts = counts.astype(jnp.int32)
    padded = (counts + blk - 1) // blk * blk
    start = jnp.cumsum(counts) - counts
    pstart = jnp.cumsum(padded) - padded
    block_end = jnp.cumsum(padded) // blk
    blocks = jnp.arange(nb, dtype=jnp.int32)
    block_expert = jnp.minimum(jnp.sum((block_end[None, :] <= blocks[:, None]).astype(jnp.int32), axis=1),
                               N_EXPERTS - 1)
    off = blocks * blk - pstart[block_expert]
    base = start[block_expert] + off
    lane = jnp.arange(blk, dtype=jnp.int32)[None, :]
    rows = jnp.take(order, base[:, None] + lane, mode="clip")
    real = (off[:, None] + lane) < counts[block_expert][:, None]
    slot_src = jnp.where(real, rows % n, 0)
    slot_dst = jnp.where(real, rows, TOP_K * n + lane)
    slot_dst = jnp.concatenate([slot_dst, TOP_K * n + lane], axis=0)
    slot_gate = jnp.where(real, jnp.take(gates.reshape(a), rows, mode="clip"), 0.0).reshape(nb, 1, blk)
    nb_real = (jnp.sum(padded) // blk).reshape(1).astype(jnp.int32)
    return block_expert, nb_real, slot_src.reshape(nb, 1, blk), slot_dst.reshape(nb + 1, 1, blk), slot_gate


def _combine_kernel(x_ref, h_ref, wsg_ref, wsu_ref, wsd_ref, *rest):
    y_refs, o_ref = rest[:TOP_K], rest[TOP_K]
    h = _from_token_major(h_ref[...])
    a = (jax.nn.silu(_dot(h, wsg_ref[...])) * _dot(h, wsu_ref[...])).astype(_BF16)
    routed = y_refs[0][...].astype(_F32)
    for k in range(1, TOP_K):
        routed = routed + y_refs[k][...].astype(_F32)
    o_ref[...] = x_ref[...] + _dot(a, wsd_ref[...]) + _from_token_major(routed)


def _combine(x1, h3, y3, ws_gate, ws_up, ws_down):
    n, d = x1.shape
    c = d // LANES
    tm = COMBINE_TM
    nt = n // tm
    de = ws_gate.shape[1]
    const = lambda r, cc: pl.BlockSpec((r, cc), lambda i: (0, 0))
    row = pl.BlockSpec((tm, d), lambda i: (i, 0))
    y_specs = [pl.BlockSpec((tm, c, LANES), functools.partial(lambda i, k: (k * nt + i, 0, 0), k=k))
               for k in range(TOP_K)]
    return pl.pallas_call(
        _combine_kernel,
        grid=(nt,),
        in_specs=[row, pl.BlockSpec((tm, c, LANES), lambda i: (i, 0, 0)), const(d, de), const(d, de), const(de, d)]
        + y_specs,
        out_specs=row,
        out_shape=jax.ShapeDtypeStruct((n, d), _F32),
        compiler_params=_cparams(("arbitrary",)),
        name="moe_combine",
    )(x1, h3, ws_gate.astype(_BF16), ws_up.astype(_BF16), ws_down.astype(_BF16), *([y3] * TOP_K))


def _moe_block(x1, norm_ffn_g, w_router, router_bias, we_gate, we_up, we_down, ws_gate, ws_up, ws_down):
    n = x1.shape[0]
    h3, top_idx_t, gates_t, counts = _router(x1, norm_ffn_g, w_router, router_bias)
    block_expert, nb_real, slot_src, slot_dst, slot_gate = _dispatch_plan(
        top_idx_t, gates_t, counts.reshape(N_EXPERTS), n)
    y3 = _moe_experts(h3, block_expert, nb_real, slot_src, slot_dst, slot_gate, we_gate, we_up, we_down)
    return _combine(x1, h3, y3, ws_gate, ws_up, ws_down)


def kernel(x, mem, norm_attn_g, w_in, nsa_pe_k, nsa_w_ck, nsa_pe_v, nsa_w_cv, nsa_q_g, nsa_k_g, rel_bias, fox_b_f, fox_q_g, fox_k_g, norm_mem_g, w_mem_kv, mem_q_g, mem_k_g, w_o_nsa, w_o_fox, w_o_mem, w_out, norm_ffn_g, w_router, router_bias, we_gate, we_up, we_down, ws_gate, ws_up, ws_down):
    b, t, d = x.shape
    assert norm_attn_g.shape[0] == 1, "single-layer problem"
    l = 0
    proj2d, o_nsa, o_fox, o_mem = _attention_mixers(
        x, mem, norm_attn_g[l], w_in[l], nsa_pe_k[l], nsa_w_ck[l], nsa_pe_v[l], nsa_w_cv[l], nsa_q_g[l],
        nsa_k_g[l], rel_bias, fox_b_f[l], fox_q_g[l], fox_k_g[l], norm_mem_g[l], w_mem_kv[l], mem_q_g[l],
        mem_k_g[l])
    x1 = _merge(x.reshape(b * t, d), o_nsa, o_fox, o_mem, proj2d, w_o_nsa[l], w_o_fox[l], w_o_mem[l], w_out[l])
    out = _moe_block(x1, norm_ffn_g[l], w_router[l], router_bias[l], we_gate[l], we_up[l], we_down[l],
                     ws_gate[l], ws_up[l], ws_down[l])
    return out.reshape(b, t, d)
```

```python
import functools
import math

import jax
import jax.numpy as jnp
import numpy as np
from jax import lax
from jax.experimental import pallas as pl
from jax.experimental.pallas import tpu as pltpu

D_MODEL = 2048
HEAD_DIM = 128
NSA_HEADS = 8
NSA_KV_HEADS = 2
NSA_GROUP = NSA_HEADS // NSA_KV_HEADS
FOX_HEADS = 4
MEM_HEADS = 4
CMP_LEN = 32
CMP_STRIDE = 16
SLC_LEN = 64
N_SELECT = 16
WINDOW = 512
NUM_BUCKETS = 32
MAX_DISTANCE = 128
N_BRANCHES = 3
N_EXPERTS = 64
TOP_K = 8
D_EXPERT = 512
ROUTED_SCALE = 2.5
ATTN_SCALE = HEAD_DIM ** -0.5
NEG_INF = -1e30
FORCE_SCORE = 1e4
RMS_EPS = 1e-6
LOG2E = math.log2(math.e)

NSA_Q_W = NSA_HEADS * HEAD_DIM
NSA_KV_W = NSA_KV_HEADS * HEAD_DIM
NSA_GATE_W = 3 * NSA_HEADS
FOX_W = FOX_HEADS * HEAD_DIM
MEM_W = MEM_HEADS * HEAD_DIM
MERGE_W = N_BRANCHES * D_MODEL

LANES = 128
VMEM_LIMIT_BYTES = 56 * 1024 * 1024

PROJ_TM = 1024
PROJ_TN = 1536
ATT_TQ = 128
FAR_GROUP = 4
FOX_TQ = 256
MEM_TQ = 512
MERGE_TM = 256
ROUTER_TM = 512
MOE_BLOCK = 256
COMBINE_TM = 256

CB_MERGE = 0
CB_ATT = N_BRANCHES * D_MODEL // LANES
CB_QNSA = CB_ATT
CB_KCMP = CB_ATT + 8
CB_VCMP = CB_ATT + 10
CB_KSLC = CB_ATT + 12
CB_VSLC = CB_ATT + 14
CB_KWIN = CB_ATT + 16
CB_VWIN = CB_ATT + 18
CB_FOXQ = CB_ATT + 20
CB_FOXK = CB_ATT + 24
CB_FOXV = CB_ATT + 28
CB_MEMQ = CB_ATT + 32
PROJ_COLS = (CB_ATT + 36) * LANES
SMALL_W = 2 * LANES
FOXF_LANE = 12

_BF16 = jnp.bfloat16
_F32 = jnp.float32


def _cparams(sem):
    return pltpu.CompilerParams(dimension_semantics=sem, vmem_limit_bytes=VMEM_LIMIT_BYTES)


def _dot(a, b):
    return jnp.dot(a, b, preferred_element_type=_F32)


def _dot_nt(a, b):
    return lax.dot_general(a, b, (((1,), (1,)), ((), ())), preferred_element_type=_F32)


def _dot_tn(a, b):
    return lax.dot_general(a, b, (((0,), (0,)), ((), ())), preferred_element_type=_F32)


def _lanes(parts):
    return jnp.concatenate(parts, axis=1)


def _to_token_major(x):
    c = x.shape[1] // LANES
    chunks = jnp.stack([x[:, j * LANES:(j + 1) * LANES] for j in range(c)], axis=0)
    return pltpu.einshape("ctl->tcl", chunks)


def _from_token_major(x3):
    xt = pltpu.einshape("tcl->ctl", x3)
    return _lanes([xt[j] for j in range(x3.shape[1])])


REORDER_TR = 128
REORDER_TC = 512


def _reorder_kernel(pieces, narrow, w_ref, o_ref, s_ref):
    off = 0
    for src, width in pieces:
        for c in range(0, width, REORDER_TC):
            o_ref[:, off + c:off + c + REORDER_TC] = w_ref[:, src + c:src + c + REORDER_TC].astype(o_ref.dtype)
        off += width
    s_ref[...] = jnp.zeros(s_ref.shape, s_ref.dtype)
    for src, width, dst in narrow:
        s_ref[:, dst:dst + width] = w_ref[:, src:src + width].astype(s_ref.dtype)


def _reorder_cast(w, pieces, narrow, narrow_cols):
    rows, cols = w.shape
    total = sum(width for _, width in pieces)
    assert rows % REORDER_TR == 0 and all(width % REORDER_TC == 0 for _, width in pieces)
    return pl.pallas_call(
        functools.partial(_reorder_kernel, pieces, narrow),
        grid=(rows // REORDER_TR,),
        in_specs=[pl.BlockSpec((REORDER_TR, cols), lambda i: (i, 0))],
        out_specs=[pl.BlockSpec((REORDER_TR, total), lambda i: (i, 0)),
                   pl.BlockSpec((REORDER_TR, narrow_cols), lambda i: (i, 0))],
        out_shape=[jax.ShapeDtypeStruct((rows, total), _BF16), jax.ShapeDtypeStruct((rows, narrow_cols), _BF16)],
        compiler_params=_cparams(("arbitrary",)),
        name="reorder_cast",
    )(w)


def _proj_kernel(norm_j0, has_small, x_ref, g_ref, w_ref, cg_ref, cf_ref, *rest):
    if has_small:
        ws_ref, o_ref, os_ref, h_ref = rest
    else:
        o_ref, h_ref = rest
    j = pl.program_id(1)

    @pl.when(j == 0)
    def _():
        x = x_ref[...]
        ms = jnp.mean(x * x, axis=-1, keepdims=True)
        h = (x * lax.rsqrt(ms + RMS_EPS) * g_ref[...]).astype(_BF16)
        h_ref[...] = h
        if has_small:
            os_ref[...] = _dot(h, ws_ref[...])

    y = _dot(h_ref[...], w_ref[...])
    tn = y.shape[1]

    @pl.when(j >= norm_j0)
    def _():
        for c in range(tn // LANES):
            sl = slice(c * LANES, (c + 1) * LANES)
            yh = y[:, sl]
            ms = jnp.mean(yh * yh, axis=-1, keepdims=True)
            scale = jnp.where(cf_ref[:, sl] > 0.0, lax.rsqrt(ms + RMS_EPS), 1.0)
            o_ref[:, sl] = (yh * scale * cg_ref[:, sl]).astype(o_ref.dtype)

    @pl.when(j < norm_j0)
    def _():
        o_ref[...] = y.astype(o_ref.dtype)


def _rms_project(x2d, g, w, col_gain, col_flag, n_plain_cols, w_small, tm, tn):
    n, d = x2d.shape
    c = w.shape[1]
    has_small = w_small is not None
    in_specs = [
        pl.BlockSpec((tm, d), lambda i, j: (i, 0)),
        pl.BlockSpec((1, d), lambda i, j: (0, 0)),
        pl.BlockSpec((d, tn), lambda i, j: (0, j)),
        pl.BlockSpec((1, tn), lambda i, j: (0, j)),
        pl.BlockSpec((1, tn), lambda i, j: (0, j)),
    ]
    args = [x2d, g.reshape(1, d), w, col_gain.reshape(1, c), col_flag.reshape(1, c)]
    out_shape = [jax.ShapeDtypeStruct((n, c), _BF16)]
    out_specs = [pl.BlockSpec((tm, tn), lambda i, j: (i, j))]
    if has_small:
        ws = w_small.shape[1]
        in_specs.append(pl.BlockSpec((d, ws), lambda i, j: (0, 0)))
        args.append(w_small)
        out_shape.append(jax.ShapeDtypeStruct((n, ws), _F32))
        out_specs.append(pl.BlockSpec((tm, ws), lambda i, j: (i, 0)))
    assert n % tm == 0 and c % tn == 0 and n_plain_cols % tn == 0
    res = pl.pallas_call(
        functools.partial(_proj_kernel, n_plain_cols // tn, has_small),
        grid=(n // tm, c // tn),
        in_specs=in_specs,
        out_specs=out_specs,
        out_shape=out_shape,
        scratch_shapes=[pltpu.VMEM((tm, d), _BF16)],
        compiler_params=_cparams(("arbitrary", "arbitrary")),
        name="rms_project",
    )(*args)
    return res if has_small else res[0]


def _cmp_kernel(x_ref, wlo_ref, whi_ref, pelo_ref, pehi_ref, kg_ref, o_ref):
    j = pl.program_id(1)
    x = x_ref[0, 0]
    nchunk = x.shape[0]
    ylo = _dot(x, wlo_ref[0])
    yhi = _dot(x, whi_ref[0])
    pe = _dot(pelo_ref[0], wlo_ref[0]) + _dot(pehi_ref[0], whi_ref[0])
    y = ylo + pltpu.roll(yhi, nchunk - 1, 0) + pe[0:1, :]

    @pl.when(j < NSA_KV_HEADS)
    def _():
        ms = jnp.mean(y * y, axis=-1, keepdims=True)
        o_ref[0, 0] = (y * lax.rsqrt(ms + RMS_EPS) * kg_ref[...]).astype(o_ref.dtype)

    @pl.when(j >= NSA_KV_HEADS)
    def _():
        o_ref[0, 0] = y.T.astype(o_ref.dtype)


def _nsa_compress(cmp_in, w_ck, w_cv, pe_k, pe_v, k_g):
    b, nj, t, dk = cmp_in.shape
    nchunk = t // CMP_STRIDE
    half = CMP_LEN // 2
    assert nchunk == dk
    x = cmp_in.reshape(b, nj, nchunk, CMP_STRIDE * dk)

    def halves(w):
        return (w[:half].reshape(half * dk, dk).astype(_BF16),
                w[half:].reshape(half * dk, dk).astype(_BF16))

    klo, khi = halves(w_ck)
    vlo, vhi = halves(w_cv)
    wlo = jnp.stack([klo, vlo])
    whi = jnp.stack([khi, vhi])

    def pe_halves(pe):
        lo = jnp.broadcast_to(pe[:half].reshape(1, half * dk), (8, half * dk)).astype(_BF16)
        hi = jnp.broadcast_to(pe[half:].reshape(1, half * dk), (8, half * dk)).astype(_BF16)
        return lo, hi

    pklo, pkhi = pe_halves(pe_k)
    pvlo, pvhi = pe_halves(pe_v)
    pelo = jnp.stack([pklo, pvlo])
    pehi = jnp.stack([pkhi, pvhi])
    kv = lambda bb, j: (j // NSA_KV_HEADS, 0, 0)
    return pl.pallas_call(
        _cmp_kernel,
        grid=(b, nj),
        in_specs=[
            pl.BlockSpec((1, 1, nchunk, CMP_STRIDE * dk), lambda bb, j: (bb, j, 0, 0)),
            pl.BlockSpec((1, half * dk, dk), kv),
            pl.BlockSpec((1, half * dk, dk), kv),
            pl.BlockSpec((1, 8, half * dk), kv),
            pl.BlockSpec((1, 8, half * dk), kv),
            pl.BlockSpec((1, dk), lambda bb, j: (0, 0)),
        ],
        out_specs=pl.BlockSpec((1, 1, nchunk, dk), lambda bb, j: (bb, j, 0, 0)),
        out_shape=jax.ShapeDtypeStruct((b, nj, nchunk, dk), _BF16),
        compiler_params=_cparams(("arbitrary", "arbitrary")),
        name="nsa_compress",
    )(x, wlo, whi, pelo, pehi, k_g.reshape(1, dk).astype(_F32))


def _t_update(state, tiles):
    m, l, acc = state
    masked = [s if mask is None else jnp.where(mask, s, NEG_INF) for s, mask, _ in tiles]
    m_new = m
    for sm in masked:
        m_new = jnp.maximum(m_new, jnp.max(sm, axis=0, keepdims=True))
    alpha = jnp.exp2(m - m_new)
    l_new = alpha * l
    acc_new = alpha * acc
    for sm, (_, _, v) in zip(masked, tiles):
        e = jnp.exp2(sm - m_new)
        l_new = l_new + jnp.sum(e, axis=0, keepdims=True)
        acc_new = acc_new + _dot_tn(v, e.astype(_BF16))
    return m_new, l_new, acc_new


def _t_init(nq):
    return (jnp.full((1, nq), NEG_INF, _F32), jnp.zeros((1, nq), _F32), jnp.zeros((HEAD_DIM, nq), _F32))


def _t_finish(state):
    _, l, acc = state
    return acc / l


def _nsa_kernel(q_ref, kvc_ref, ks_ref, vs_ref, kw_ref, vw_ref, gl_ref,
                wt_ref, dt_ref, ovt_ref, o_ref, sel_ref):
    i = pl.program_id(1)
    tq = ATT_TQ
    hg = NSA_GROUP
    nq = hg * tq
    n_slc = ovt_ref.shape[0]
    groups = range(NSA_KV_HEADS)

    key = lax.broadcasted_iota(jnp.int32, (LANES, nq), 0)
    qry = lax.broadcasted_iota(jnp.int32, (LANES, nq), 1) & (tq - 1)

    def gcols(g):
        return slice(g * LANES, (g + 1) * LANES)

    def ktile(ref, g, kt):
        return ref[0, pl.ds(pl.multiple_of(kt * LANES, LANES), LANES), gcols(g)]

    def near_bias(g, d):
        return _lanes([dt_ref[g * hg + h, d] for h in range(hg)])

    q = q_ref[0]
    qts = [_lanes([q[:, (g * hg + h) * HEAD_DIM:(g * hg + h + 1) * HEAD_DIM].astype(_F32).T
                   for h in range(hg)]).astype(_BF16) for g in groups]

    woff = pl.multiple_of(wt_ref.shape[1] - LANES - 8 - (LANES // CMP_STRIDE) * i, 8)
    s_c = [_dot(kvc_ref[0, g], qts[g]) + _lanes([wt_ref[g * hg + h, pl.ds(woff, LANES), :] for h in range(hg)])
           for g in groups]
    n_win = WINDOW // tq
    win_tiles = [[] for _ in groups]
    for g in groups:
        for d in range(n_win, -1, -1):
            ktd = jnp.maximum(i - d, 0)
            sc = _dot(ktile(kw_ref, g, ktd), qts[g])
            if d <= 1:
                sc = sc + near_bias(g, d)
            if d == n_win:
                mk = (qry < key) & (i >= d)
            elif d == 0:
                mk = key <= qry
            else:
                mk = jnp.broadcast_to(i >= d, (LANES, nq))
            win_tiles[g].append((sc, mk, ktile(vw_ref, g, ktd)))

    mask_c = (i * tq + qry) >= (CMP_STRIDE * key + CMP_LEN - 1)
    blk = lax.broadcasted_iota(jnp.int32, (n_slc, tq), 0)
    tpos = i * tq + lax.broadcasted_iota(jnp.int32, (n_slc, tq), 1)
    cur = tpos // SLC_LEN
    valid = blk * SLC_LEN <= tpos
    forced = (blk == 0) | (blk == cur) | (blk == cur - 1)
    o_c = []
    for g in groups:
        s = s_c[g]
        mx = jnp.max(jnp.where(mask_c, s, NEG_INF), axis=0, keepdims=True)
        e = jnp.where(mask_c, jnp.exp2(s - mx), 0.0)
        l = jnp.sum(e, axis=0, keepdims=True)
        p_c = e / jnp.where(l > 0.0, l, 1.0)
        o_c.append(_dot(kvc_ref[0, NSA_KV_HEADS + g], p_c.astype(_BF16)))
        ps = p_c[:, 0:tq]
        for h in range(1, hg):
            ps = ps + p_c[:, h * tq:(h + 1) * tq]
        ps_hi = ps.astype(_BF16)
        ps_lo = (ps - ps_hi.astype(_F32)).astype(_BF16)
        imp = _dot(ovt_ref[...], ps_hi) + _dot(ovt_ref[...], ps_lo)
        score = jnp.where(valid, jnp.where(forced, FORCE_SCORE, imp), -1.0)
        rank = jnp.zeros((n_slc, tq), _F32)
        for k in range(n_slc):
            ck = score[k:k + 1, :]
            beats = (ck > score) | ((ck == score) & (blk > k))
            rank = rank + jnp.where(beats, 1.0, 0.0)
        selb = jnp.where(rank < float(min(N_SELECT, n_slc)), 0.0, NEG_INF)
        sel_ref[g] = _lanes([selb] * hg)

    def sel_bias(g, kt, ok=True):
        rows = []
        for r in range(LANES // SLC_LEN):
            row = sel_ref[g, pl.ds((LANES // SLC_LEN) * kt + r, 1), :]
            rows.append(jnp.broadcast_to(jnp.where(ok, row, NEG_INF), (SLC_LEN, nq)))
        return jnp.concatenate(rows, axis=0)

    o_w = [_t_finish(_t_update(_t_init(nq), win_tiles[g])) for g in groups]

    n_far = jnp.maximum(i - 1, 0)

    def far_body(j, sts):
        k0 = pl.multiple_of(j * (FAR_GROUP * LANES), FAR_GROUP * LANES)
        sgs = [_dot(ks_ref[0, pl.ds(k0, FAR_GROUP * LANES), gcols(g)], qts[g]) for g in groups]
        out = []
        for g in groups:
            tiles = []
            for u in range(FAR_GROUP):
                kt = FAR_GROUP * j + u
                tiles.append((sgs[g][u * LANES:(u + 1) * LANES] + sel_bias(g, kt, kt < n_far), None,
                              ktile(vs_ref, g, kt)))
            out.append(_t_update(sts[g], tiles))
        return tuple(out)

    sts = lax.fori_loop(0, (n_far + FAR_GROUP - 1) // FAR_GROUP, far_body, tuple(_t_init(nq) for _ in groups))
    kt1 = jnp.maximum(i - 1, 0)
    near = [(_dot(ktile(ks_ref, g, kt1), qts[g]) + near_bias(g, 1), _dot(ktile(ks_ref, g, i), qts[g]) + near_bias(g, 0))
            for g in groups]
    o_s = [_t_finish(_t_update(sts[g], [
        (near[g][0] + sel_bias(g, kt1, i >= 1), None, ktile(vs_ref, g, kt1)),
        (near[g][1] + sel_bias(g, i), key <= qry, ktile(vs_ref, g, i))])) for g in groups]

    for g in groups:
        glt = jax.nn.sigmoid(gl_ref[0, :, gcols(g)]).T
        grow = lambda br: _lanes([glt[br * hg + h:br * hg + h + 1, :] for h in range(hg)])
        o = grow(0) * o_c[g] + grow(1) * o_s[g] + grow(2) * o_w[g]
        for h in range(hg):
            hd = (g * hg + h) * HEAD_DIM
            o_ref[0, :, hd:hd + HEAD_DIM] = o[:, h * tq:(h + 1) * tq].T.astype(o_ref.dtype)


def _t5_bucket(rel):
    n = np.maximum(rel, 0)
    max_exact = NUM_BUCKETS // 2
    ratio = np.maximum(n, 1).astype(np.float32) / np.float32(max_exact)
    log_ratio = np.log(ratio) / np.float32(math.log(MAX_DISTANCE / max_exact))
    large = np.minimum(max_exact + (log_ratio * np.float32(NUM_BUCKETS - max_exact)).astype(np.int32),
                       NUM_BUCKETS - 1)
    return np.where(n < max_exact, n, large).astype(np.int32)


def _bias_lookup(rb, bucket):
    bk = jnp.asarray(bucket.astype(np.int32))[None]
    ex = (slice(None),) + (None,) * bucket.ndim
    tab = jnp.broadcast_to(rb[:, 0][ex], (rb.shape[0],) + bucket.shape)
    for k in range(1, NUM_BUCKETS):
        tab = jnp.where(bk == k, rb[:, k][ex], tab)
    return tab


def _nsa_tables(rel_bias):
    rb = rel_bias.astype(_F32).T * LOG2E
    r = np.arange(ATT_TQ)
    rel_d = (np.arange(2) * ATT_TQ)[:, None, None] + r[None, None, :] - r[None, :, None]
    assert _t5_bucket(np.array([ATT_TQ + 1]))[0] == NUM_BUCKETS - 1
    u = np.arange(2 * LANES)
    rel_w = r[None, :] - CMP_STRIDE * (u[:, None] - (LANES - 8)) - (CMP_LEN - 1)
    far = rb[:, NUM_BUCKETS - 1]
    dt = _bias_lookup(rb, _t5_bucket(rel_d)) - far[:, None, None, None]
    return dt, _bias_lookup(rb, _t5_bucket(rel_w))


def _nsa_attention(proj, small, kvc, rel_bias):
    b, t, _ = proj.shape
    g, hg, tq = NSA_KV_HEADS, NSA_GROUP, ATT_TQ
    n_slc = t // SLC_LEN
    n_cmp = (t - CMP_LEN) // CMP_STRIDE + 1
    nt = t // LANES
    assert n_cmp <= LANES and kvc.shape[2] == LANES and n_slc % 16 == 0 and tq == LANES
    assert nt % FAR_GROUP == 0
    assert LANES - 8 - (LANES // CMP_STRIDE) * (t // tq - 1) >= 0
    dt, wt = _nsa_tables(rel_bias)
    cstart = np.arange(LANES) * CMP_STRIDE
    sstart = np.arange(n_slc) * SLC_LEN
    overlap = np.clip(np.minimum(cstart[None, :] + CMP_LEN, sstart[:, None] + SLC_LEN)
                      - np.maximum(cstart[None, :], sstart[:, None]), 0, None).astype(np.float32) / CMP_LEN
    overlap[:, n_cmp:] = 0.0
    ovt = jnp.asarray(overlap, _BF16)

    gw = g * LANES
    seq = lambda cb: pl.BlockSpec((1, t, gw), lambda bb, ii: (bb, 0, cb * LANES // gw))
    heads = g * hg
    return pl.pallas_call(
        _nsa_kernel,
        grid=(b, t // tq),
        in_specs=[
            pl.BlockSpec((1, tq, NSA_Q_W), lambda bb, ii: (bb, ii, CB_QNSA * LANES // NSA_Q_W)),
            pl.BlockSpec((1, 2 * g, LANES, HEAD_DIM), lambda bb, ii: (bb, 0, 0, 0)),
            seq(CB_KSLC), seq(CB_VSLC), seq(CB_KWIN), seq(CB_VWIN),
            pl.BlockSpec((1, tq, SMALL_W), lambda bb, ii: (bb, ii, 0)),
            pl.BlockSpec((heads, 2 * LANES, tq), lambda bb, ii: (0, 0, 0)),
            pl.BlockSpec((heads, 2, LANES, tq), lambda bb, ii: (0, 0, 0, 0)),
            pl.BlockSpec((n_slc, LANES), lambda bb, ii: (0, 0)),
        ],
        out_specs=pl.BlockSpec((1, tq, NSA_Q_W), lambda bb, ii: (bb, ii, 0)),
        out_shape=jax.ShapeDtypeStruct((b, t, NSA_Q_W), _BF16),
        scratch_shapes=[pltpu.VMEM((g, n_slc, hg * tq), _F32)],
        compiler_params=_cparams(("arbitrary", "arbitrary")),
        name="nsa_attention",
    )(proj, kvc, proj, proj, proj, proj, small, wt, dt, ovt)


def _fox_cum_kernel(s_ref, b_ref, col_ref, row_ref):
    z = s_ref[0] + b_ref[...]
    lf = (jnp.minimum(z, 0.0) - jnp.log1p(jnp.exp(-jnp.abs(z)))) * LOG2E
    x = lf.T
    t = x.shape[1]
    lane = lax.broadcasted_iota(jnp.int32, x.shape, 1)
    sh = 1
    while sh < t:
        x = x + jnp.where(lane >= sh, pltpu.roll(x, sh, 1), 0.0)
        sh *= 2
    row_ref[0] = x[8:16, :]
    col_ref[0] = x.T


def _fox_cumsum(small, b_f):
    b, t, _ = small.shape
    bvec = jnp.zeros((1, LANES), _F32).at[0, FOXF_LANE:FOXF_LANE + FOX_HEADS].set(b_f.astype(_F32))
    return pl.pallas_call(
        _fox_cum_kernel,
        grid=(b,),
        in_specs=[pl.BlockSpec((1, t, LANES), lambda bb: (bb, 0, 1)),
                  pl.BlockSpec((1, LANES), lambda bb: (0, 0))],
        out_specs=[pl.BlockSpec((1, t, LANES), lambda bb: (bb, 0, 0)),
                   pl.BlockSpec((1, 8, t), lambda bb: (bb, 0, 0))],
        out_shape=[jax.ShapeDtypeStruct((b, t, LANES), _F32), jax.ShapeDtypeStruct((b, 8, t), _F32)],
        compiler_params=_cparams(("arbitrary",)),
        name="fox_cumsum",
    )(small, bvec)


def _fox_kernel(q_ref, k_ref, v_ref, cc_ref, cr_ref, o_ref):
    i = pl.program_id(1)
    tq = FOX_TQ
    key = lax.broadcasted_iota(jnp.int32, (tq, tq), 0)
    qry = lax.broadcasted_iota(jnp.int32, (tq, tq), 1)
    heads = range(FOX_HEADS)
    hsl = [slice(h * HEAD_DIM, (h + 1) * HEAD_DIM) for h in heads]
    qts = [q_ref[0, :, hsl[h]].astype(_F32).T.astype(_BF16) for h in heads]
    cqs = [cr_ref[0, FOXF_LANE - 8 + h:FOXF_LANE - 8 + h + 1, :] for h in heads]

    def keys(kt):
        return pl.ds(pl.multiple_of(kt * tq, tq), tq)

    def logits(h, kt):
        ck = cc_ref[0, keys(kt), FOXF_LANE + h:FOXF_LANE + h + 1]
        return _dot(k_ref[0, keys(kt), hsl[h]], qts[h]) + (cqs[h] - ck)

    def body(kt, sts):
        ss = [logits(h, kt) for h in heads]
        return tuple(_t_update(sts[h], [(ss[h], None, v_ref[0, keys(kt), hsl[h]])]) for h in heads)

    sts = lax.fori_loop(0, i, body, tuple(_t_init(tq) for _ in heads))
    ss = [logits(h, i) for h in heads]
    for h in heads:
        st = _t_update(sts[h], [(ss[h], key <= qry, v_ref[0, keys(i), hsl[h]])])
        o_ref[0, :, hsl[h]] = _t_finish(st).T.astype(o_ref.dtype)


def _fox_attention(proj, cum_col, cum_row):
    b, t, _ = proj.shape
    tq = FOX_TQ
    w = FOX_W
    return pl.pallas_call(
        _fox_kernel,
        grid=(b, t // tq),
        in_specs=[
            pl.BlockSpec((1, tq, w), lambda bb, ii: (bb, ii, CB_FOXQ * LANES // w)),
            pl.BlockSpec((1, t, w), lambda bb, ii: (bb, 0, CB_FOXK * LANES // w)),
            pl.BlockSpec((1, t, w), lambda bb, ii: (bb, 0, CB_FOXV * LANES // w)),
            pl.BlockSpec((1, t, LANES), lambda bb, ii: (bb, 0, 0)),
            pl.BlockSpec((1, 8, tq), lambda bb, ii: (bb, 0, ii)),
        ],
        out_specs=pl.BlockSpec((1, tq, w), lambda bb, ii: (bb, ii, 0)),
        out_shape=jax.ShapeDtypeStruct((b, t, w), _BF16),
        compiler_params=_cparams(("arbitrary", "arbitrary")),
        name="fox_attention",
    )(proj, proj, proj, cum_col, cum_row)


def _mem_kernel(q_ref, kv_ref, o_ref):
    for h in range(MEM_HEADS):
        hs = slice(h * HEAD_DIM, (h + 1) * HEAD_DIM)
        vs = slice(MEM_W + h * HEAD_DIM, MEM_W + (h + 1) * HEAD_DIM)
        s = _dot_nt(q_ref[0, :, hs], kv_ref[0, :, hs])
        e = jnp.exp2(s - jnp.max(s, axis=1, keepdims=True))
        p = e / jnp.sum(e, axis=1, keepdims=True)
        o_ref[0, :, hs] = _dot(p.astype(_BF16), kv_ref[0, :, vs]).astype(o_ref.dtype)


def _mem_attention(proj, memkv):
    b, t, _ = proj.shape
    m = memkv.shape[1]
    tq = MEM_TQ
    return pl.pallas_call(
        _mem_kernel,
        grid=(b, t // tq),
        in_specs=[pl.BlockSpec((1, tq, MEM_W), lambda bb, ii: (bb, ii, CB_MEMQ * LANES // MEM_W)),
                  pl.BlockSpec((1, m, 2 * MEM_W), lambda bb, ii: (bb, 0, 0))],
        out_specs=pl.BlockSpec((1, tq, MEM_W), lambda bb, ii: (bb, ii, 0)),
        out_shape=jax.ShapeDtypeStruct((b, t, MEM_W), _BF16),
        compiler_params=_cparams(("arbitrary", "arbitrary")),
        name="mem_attention",
    )(proj, memkv)


def _merge_kernel(x_ref, on_ref, of_ref, om_ref, g0_ref, g1_ref, g2_ref, wn_ref, wf_ref, wm_ref, wo_ref,
                  o_ref, z_ref):
    tn = 512
    for c in range(D_MODEL // tn):
        cs = slice(c * tn, (c + 1) * tn)
        z = (jax.nn.sigmoid(g0_ref[:, cs].astype(_F32)) * _dot(on_ref[...], wn_ref[:, cs])
             + jax.nn.sigmoid(g1_ref[:, cs].astype(_F32)) * _dot(of_ref[...], wf_ref[:, cs])
             + jax.nn.sigmoid(g2_ref[:, cs].astype(_F32)) * _dot(om_ref[...], wm_ref[:, cs]))
        z_ref[:, cs] = z.astype(_BF16)
    o_ref[...] = x_ref[...] + _dot(z_ref[...], wo_ref[...])


def _merge(x2d, o_nsa, o_fox, o_mem, proj2d, w_o_nsa, w_o_fox, w_o_mem, w_out):
    n, d = x2d.shape
    tm = MERGE_TM
    gcb = CB_MERGE * LANES // d
    const = lambda r, c: pl.BlockSpec((r, c), lambda i: (0, 0))
    return pl.pallas_call(
        _merge_kernel,
        grid=(n // tm,),
        in_specs=[
            pl.BlockSpec((tm, d), lambda i: (i, 0)),
            pl.BlockSpec((tm, NSA_Q_W), lambda i: (i, 0)),
            pl.BlockSpec((tm, FOX_W), lambda i: (i, 0)),
            pl.BlockSpec((tm, MEM_W), lambda i: (i, 0)),
            pl.BlockSpec((tm, d), lambda i: (i, gcb)),
            pl.BlockSpec((tm, d), lambda i: (i, gcb + 1)),
            pl.BlockSpec((tm, d), lambda i: (i, gcb + 2)),
            const(NSA_Q_W, d), const(FOX_W, d), const(MEM_W, d), const(d, d),
        ],
        out_specs=pl.BlockSpec((tm, d), lambda i: (i, 0)),
        out_shape=jax.ShapeDtypeStruct((n, d), _F32),
        scratch_shapes=[pltpu.VMEM((tm, d), _BF16)],
        compiler_params=_cparams(("arbitrary",)),
        name="merge_out",
    )(x2d, o_nsa, o_fox, o_mem, proj2d, proj2d, proj2d,
      w_o_nsa.astype(_BF16), w_o_fox.astype(_BF16), w_o_mem.astype(_BF16), w_out.astype(_BF16))


def _nsa_inputs(x, norm_attn_g, w_in, nsa_pe_k, nsa_w_ck, nsa_pe_v, nsa_w_cv, nsa_q_g, nsa_k_g,
                fox_q_g, fox_k_g, mem_q_g):
    b, t, d = x.shape
    ones = lambda k: jnp.ones((k * HEAD_DIM,), _F32)
    zeros = lambda k: jnp.zeros((k * HEAD_DIM,), _F32)
    tile = lambda gv, k: jnp.tile(gv.astype(_F32), k)
    qs = ATTN_SCALE * LOG2E

    g_end = NSA_Q_W + 6 * NSA_KV_W
    f_off = g_end + NSA_GATE_W + 3 * FOX_W
    m_off = f_off + FOX_HEADS + MEM_W
    narrow = tuple((g_end + (br * NSA_KV_HEADS + gg) * NSA_GROUP, NSA_GROUP, gg * LANES + br * NSA_GROUP)
                   for br in range(3) for gg in range(NSA_KV_HEADS))
    narrow += ((f_off, FOX_HEADS, LANES + FOXF_LANE),)
    w_all, w_small = _reorder_cast(
        w_in, ((m_off, MERGE_W), (0, g_end), (g_end + NSA_GATE_W, 3 * FOX_W), (f_off + FOX_HEADS, MEM_W)),
        narrow, SMALL_W)
    col_gain = jnp.concatenate([
        jnp.ones((MERGE_W,), _F32),
        tile(nsa_q_g, NSA_HEADS) * qs, ones(2), ones(2), tile(nsa_k_g, 2), ones(2), tile(nsa_k_g, 2), ones(2),
        tile(fox_q_g, FOX_HEADS) * qs, tile(fox_k_g, FOX_HEADS), ones(FOX_HEADS),
        tile(mem_q_g, MEM_HEADS) * qs])
    col_flag = jnp.concatenate([
        jnp.zeros((MERGE_W,), _F32),
        ones(NSA_HEADS), zeros(2), zeros(2), ones(2), zeros(2), ones(2), zeros(2),
        ones(FOX_HEADS), ones(FOX_HEADS), zeros(FOX_HEADS), ones(MEM_HEADS)])

    x2d = x.reshape(b * t, d)
    proj2d, small2d = _rms_project(x2d, norm_attn_g, w_all, col_gain, col_flag, CB_ATT * LANES, w_small,
                                   PROJ_TM, PROJ_TN)
    proj = proj2d.reshape(b, t, PROJ_COLS)
    small = small2d.reshape(b, t, SMALL_W)

    cmp_in = proj[:, :, CB_KCMP * LANES:(CB_KCMP + 4) * LANES].reshape(b, t, 4, HEAD_DIM).transpose(0, 2, 1, 3)
    kvc = _nsa_compress(cmp_in, nsa_w_ck, nsa_w_cv, nsa_pe_k, nsa_pe_v, nsa_k_g)
    return proj, small, kvc


def _attention_mixers(x, mem, norm_attn_g, w_in, nsa_pe_k, nsa_w_ck, nsa_pe_v, nsa_w_cv, nsa_q_g, nsa_k_g,
                      rel_bias, fox_b_f, fox_q_g, fox_k_g, norm_mem_g, w_mem_kv, mem_q_g, mem_k_g):
    b, t, d = x.shape
    ones = lambda k: jnp.ones((k * HEAD_DIM,), _F32)
    zeros = lambda k: jnp.zeros((k * HEAD_DIM,), _F32)
    tile = lambda gv, k: jnp.tile(gv.astype(_F32), k)
    proj, small, kvc = _nsa_inputs(x, norm_attn_g, w_in, nsa_pe_k, nsa_w_ck, nsa_pe_v, nsa_w_cv, nsa_q_g, nsa_k_g,
                                   fox_q_g, fox_k_g, mem_q_g)
    proj2d = proj.reshape(b * t, PROJ_COLS)
    o_nsa = _nsa_attention(proj, small, kvc, rel_bias)

    cum_col, cum_row = _fox_cumsum(small, fox_b_f)
    o_fox = _fox_attention(proj, cum_col, cum_row)

    mm = mem.shape[1]
    mem_gain = jnp.concatenate([tile(mem_k_g, MEM_HEADS), ones(MEM_HEADS)])
    mem_flag = jnp.concatenate([ones(MEM_HEADS), zeros(MEM_HEADS)])
    memkv = _rms_project(mem.reshape(b * mm, d), norm_mem_g, w_mem_kv.astype(_BF16), mem_gain, mem_flag,
                         0, None, mm, MEM_W).reshape(b, mm, 2 * MEM_W)
    o_mem = _mem_attention(proj, memkv)
    return proj2d, o_nsa.reshape(b * t, NSA_Q_W), o_fox.reshape(b * t, FOX_W), o_mem.reshape(b * t, MEM_W)


def _router_kernel(x_ref, g_ref, whi_ref, wlo_ref, b_ref, h_ref, idx_ref, gate_ref, cnt_ref):
    x = x_ref[...]
    ms = jnp.mean(x * x, axis=-1, keepdims=True)
    h = x * lax.rsqrt(ms + RMS_EPS) * g_ref[...]
    h_hi = h.astype(_BF16)
    h_ref[...] = _to_token_major(h_hi)
    h_lo = (h - h_hi.astype(_F32)).astype(_BF16)
    logits = _dot_nt(whi_ref[...], h_hi) + _dot_nt(whi_ref[...], h_lo) + _dot_nt(wlo_ref[...], h_hi)
    scores = jax.nn.sigmoid(logits)
    sb = scores + b_ref[...]
    eidx = lax.broadcasted_iota(jnp.int32, sb.shape, 0)
    idxs, vals = [], []
    for _ in range(TOP_K):
        m = jnp.max(sb, axis=0, keepdims=True)
        idx = jnp.min(jnp.where(sb == m, eidx, N_EXPERTS), axis=0, keepdims=True)
        hit = eidx == idx
        vals.append(jnp.sum(jnp.where(hit, scores, 0.0), axis=0, keepdims=True))
        idxs.append(idx)
        sb = jnp.where(hit, NEG_INF, sb)
    top_s = jnp.concatenate(vals, axis=0)
    idx_ref[...] = jnp.concatenate(idxs, axis=0)
    gate_ref[...] = top_s / jnp.sum(top_s, axis=0, keepdims=True) * ROUTED_SCALE

    @pl.when(pl.program_id(0) == 0)
    def _():
        cnt_ref[...] = jnp.zeros(cnt_ref.shape, cnt_ref.dtype)

    picked = jnp.where(sb < 0.5 * NEG_INF, 1.0, 0.0)
    cnt_ref[...] += jnp.sum(picked, axis=1, keepdims=True)


def _router(x1, norm_g, w_router, router_bias):
    n, d = x1.shape
    tm = ROUTER_TM
    wt = w_router.astype(_F32).T
    w_hi = wt.astype(_BF16)
    w_lo = (wt - w_hi.astype(_F32)).astype(_BF16)
    const = lambda r, c: pl.BlockSpec((r, c), lambda i: (0, 0))
    return pl.pallas_call(
        _router_kernel,
        grid=(n // tm,),
        in_specs=[pl.BlockSpec((tm, d), lambda i: (i, 0)), const(1, d), const(N_EXPERTS, d), const(N_EXPERTS, d),
                  const(N_EXPERTS, 1)],
        out_specs=[pl.BlockSpec((tm, d // LANES, LANES), lambda i: (i, 0, 0)),
                   pl.BlockSpec((TOP_K, tm), lambda i: (0, i)),
                   pl.BlockSpec((TOP_K, tm), lambda i: (0, i)),
                   const(N_EXPERTS, 1)],
        out_shape=[jax.ShapeDtypeStruct((n, d // LANES, LANES), _BF16), jax.ShapeDtypeStruct((TOP_K, n), jnp.int32),
                   jax.ShapeDtypeStruct((TOP_K, n), _F32), jax.ShapeDtypeStruct((N_EXPERTS, 1), _F32)],
        compiler_params=_cparams(("arbitrary",)),
        name="moe_router",
    )(x1, norm_g.reshape(1, d).astype(_F32), w_hi, w_lo, router_bias.reshape(N_EXPERTS, 1).astype(_F32))


def _moe_kernel(be_ref, nbr_ref, src_hbm, dst_hbm, h_hbm, sg_ref, wg_ref, wu_ref, wd_ref, y_hbm,
                xbuf, ybuf, xmat, sidx, didx, wg_bf, wu_bf, wd_bf, gsem, ssem, isem):
    n = pl.program_id(0)
    nbr = nbr_ref[0]
    last = nbr - 1
    slot = n % 2
    other = 1 - slot
    blk = MOE_BLOCK
    dump_block = dst_hbm.shape[0] - 1
    ring = didx.shape[0]

    def src_copy(block, s):
        return pltpu.make_async_copy(src_hbm.at[block], sidx.at[s], isem.at[0, s])

    def dst_copy(block, s3):
        return pltpu.make_async_copy(dst_hbm.at[block], didx.at[s3], isem.at[1, s3])

    def issue_gather(s):
        for r in range(blk):
            tok = sidx[s, 0, r]
            pltpu.make_async_copy(h_hbm.at[tok], xbuf.at[s, r], gsem.at[s]).start(priority=r % 2)

    def issue_scatter(s, s3):
        for r in range(blk):
            row = didx[s3, 0, r]
            pltpu.make_async_copy(ybuf.at[s, r], y_hbm.at[row], ssem.at[s]).start(priority=r % 2)

    def wait_rows(buf, sem, s):
        pltpu.make_async_copy(buf.at[s], buf.at[s], sem.at[s]).wait()

    @pl.when(n < nbr)
    def _():
        @pl.when(n == 0)
        def _():
            src_copy(0, 0).start()
            dst_copy(0, 0).start()
            dst_copy(dump_block, ring - 1).start()
            ybuf[1] = jnp.zeros(ybuf.shape[1:], ybuf.dtype)
            src_copy(0, 0).wait()
            dst_copy(0, 0).wait()
            dst_copy(dump_block, ring - 1).wait()
            issue_gather(0)
            nxt0 = jnp.minimum(1, last)
            src_copy(nxt0, 1).start()
            dst_copy(nxt0, 1).start()

        wait_rows(xbuf, gsem, slot)

        @pl.when(n < last)
        def _():
            nn = jnp.minimum(n + 2, last)
            src_copy(nn, slot).start()
            dst_copy(nn, (n + 2) % ring).start()

        changed = (n == 0) | (be_ref[n] != be_ref[jnp.maximum(n - 1, 0)])

        @pl.when(changed)
        def _():
            wg_bf[...] = wg_ref[0].astype(_BF16)
            wu_bf[...] = wu_ref[0].astype(_BF16)
            wd_bf[...] = wd_ref[0].astype(_BF16)

        @pl.when(n >= 1)
        def _():
            wait_rows(ybuf, ssem, slot)

        nxt = jnp.minimum(n + 1, last)
        src_copy(nxt, other).wait()
        dst_copy(nxt, (n + 1) % ring).wait()

        issue_scatter(other, (n + ring - 1) % ring)
        xmat[...] = _from_token_major(xbuf[slot])
        issue_gather(other)
        gcol = jnp.broadcast_to(sg_ref[0], (LANES, blk)).T
        a = jax.nn.silu(_dot(xmat[...], wg_bf[...])) * _dot(xmat[...], wu_bf[...])
        a = (a * _lanes([gcol] * (a.shape[1] // LANES))).astype(_BF16)
        ybuf[slot] = _to_token_major(_dot(a, wd_bf[...]).astype(_BF16))

        @pl.when(n == last)
        def _():
            issue_scatter(slot, n % ring)
            wait_rows(xbuf, gsem, other)
            wait_rows(ybuf, ssem, other)
            wait_rows(ybuf, ssem, slot)


def _moe_experts(h3, block_expert, nb_real, slot_src, slot_dst, slot_gate, we_gate, we_up, we_down):
    n, c, _ = h3.shape
    d = c * LANES
    nb = block_expert.shape[0]
    blk = MOE_BLOCK
    e, _, de = we_gate.shape
    grid_spec = pltpu.PrefetchScalarGridSpec(
        num_scalar_prefetch=2,
        grid=(nb,),
        in_specs=[
            pl.BlockSpec(memory_space=pl.ANY),
            pl.BlockSpec(memory_space=pl.ANY),
            pl.BlockSpec(memory_space=pl.ANY),
            pl.BlockSpec((1, 1, blk), lambda i, be, nbr: (i, 0, 0)),
            pl.BlockSpec((1, d, de), lambda i, be, nbr: (be[i], 0, 0)),
            pl.BlockSpec((1, d, de), lambda i, be, nbr: (be[i], 0, 0)),
            pl.BlockSpec((1, de, d), lambda i, be, nbr: (be[i], 0, 0)),
        ],
        out_specs=pl.BlockSpec(memory_space=pl.ANY),
        scratch_shapes=[
            pltpu.VMEM((2, blk, c, LANES), _BF16), pltpu.VMEM((2, blk, c, LANES), _BF16),
            pltpu.VMEM((blk, d), _BF16),
            pltpu.SMEM((2, 1, blk), jnp.int32), pltpu.SMEM((4, 1, blk), jnp.int32),
            pltpu.VMEM((d, de), _BF16), pltpu.VMEM((d, de), _BF16), pltpu.VMEM((de, d), _BF16),
            pltpu.SemaphoreType.DMA((2,)), pltpu.SemaphoreType.DMA((2,)), pltpu.SemaphoreType.DMA((2, 4)),
        ],
    )
    return pl.pallas_call(
        _moe_kernel,
        grid_spec=grid_spec,
        out_shape=jax.ShapeDtypeStruct((TOP_K * n + blk, c, LANES), _BF16),
        compiler_params=_cparams(("arbitrary",)),
        name="moe_experts",
    )(block_expert, nb_real, slot_src, slot_dst, h3, slot_gate, we_gate, we_up, we_down)


def _dispatch_plan(top_idx, gates, counts, n):
    blk = MOE_BLOCK
    a = n * TOP_K
    nb = (a + N_EXPERTS * (blk - 1)) // blk
    order = jnp.argsort(top_idx.reshape(a)).astype(jnp.int32)
    counts = counts.astype(jnp.int32)
    padded = (counts + blk - 1) // blk * blk
    start = jnp.cumsum(counts) - counts
    pstart = jnp.cumsum(padded) - padded
    block_end = jnp.cumsum(padded) // blk
    blocks = jnp.arange(nb, dtype=jnp.int32)
    block_expert = jnp.minimum(jnp.sum((block_end[None, :] <= blocks[:, None]).astype(jnp.int32), axis=1),
                               N_EXPERTS - 1)
    off = blocks * blk - pstart[block_expert]
    base = start[block_expert] + off
    lane = jnp.arange(blk, dtype=jnp.int32)[None, :]
    rows = jnp.take(order, base[:, None] + lane, mode="clip")
    real = (off[:, None] + lane) < counts[block_expert][:, None]
    slot_src = jnp.where(real, rows % n, 0)
    slot_dst = jnp.where(real, rows, TOP_K * n + lane)
    slot_dst = jnp.concatenate([slot_dst, TOP_K * n + lane], axis=0)
    slot_gate = jnp.where(real, jnp.take(gates.reshape(a), rows, mode="clip"), 0.0).reshape(nb, 1, blk)
    nb_real = (jnp.sum(padded) // blk).reshape(1).astype(jnp.int32)
    return block_expert, nb_real, slot_src.reshape(nb, 1, blk), slot_dst.reshape(nb + 1, 1, blk), slot_gate


def _combine_kernel(x_ref, h_ref, wsg_ref, wsu_ref, wsd_ref, *rest):
    y_refs, o_ref = rest[:TOP_K], rest[TOP_K]
    h = _from_token_major(h_ref[...])
    a = (jax.nn.silu(_dot(h, wsg_ref[...])) * _dot(h, wsu_ref[...])).astype(_BF16)
    routed = y_refs[0][...].astype(_F32)
    for k in range(1, TOP_K):
        routed = routed + y_refs[k][...].astype(_F32)
    o_ref[...] = x_ref[...] + _dot(a, wsd_ref[...]) + _from_token_major(routed)


def _combine(x1, h3, y3, ws_gate, ws_up, ws_down):
    n, d = x1.shape
    c = d // LANES
    tm = COMBINE_TM
    nt = n // tm
    de = ws_gate.shape[1]
    const = lambda r, cc: pl.BlockSpec((r, cc), lambda i: (0, 0))
    row = pl.BlockSpec((tm, d), lambda i: (i, 0))
    y_specs = [pl.BlockSpec((tm, c, LANES), functools.partial(lambda i, k: (k * nt + i, 0, 0), k=k))
               for k in range(TOP_K)]
    return pl.pallas_call(
        _combine_kernel,
        grid=(nt,),
        in_specs=[row, pl.BlockSpec((tm, c, LANES), lambda i: (i, 0, 0)), const(d, de), const(d, de), const(de, d)]
        + y_specs,
        out_specs=row,
        out_shape=jax.ShapeDtypeStruct((n, d), _F32),
        compiler_params=_cparams(("arbitrary",)),
        name="moe_combine",
    )(x1, h3, ws_gate.astype(_BF16), ws_up.astype(_BF16), ws_down.astype(_BF16), *([y3] * TOP_K))


def _moe_block(x1, norm_ffn_g, w_router, router_bias, we_gate, we_up, we_down, ws_gate, ws_up, ws_down):
    n = x1.shape[0]
    h3, top_idx_t, gates_t, counts = _router(x1, norm_ffn_g, w_router, router_bias)
    block_expert, nb_real, slot_src, slot_dst, slot_gate = _dispatch_plan(
        top_idx_t, gates_t, counts.reshape(N_EXPERTS), n)
    y3 = _moe_experts(h3, block_expert, nb_real, slot_src, slot_dst, slot_gate, we_gate, we_up, we_down)
    return _combine(x1, h3, y3, ws_gate, ws_up, ws_down)


def kernel(x, mem, norm_attn_g, w_in, nsa_pe_k, nsa_w_ck, nsa_pe_v, nsa_w_cv, nsa_q_g, nsa_k_g, rel_bias, fox_b_f, fox_q_g, fox_k_g, norm_mem_g, w_mem_kv, mem_q_g, mem_k_g, w_o_nsa, w_o_fox, w_o_mem, w_out, norm_ffn_g, w_router, router_bias, we_gate, we_up, we_down, ws_gate, ws_up, ws_down):
    b, t, d = x.shape
    assert norm_attn_g.shape[0] == 1, "single-layer problem"
    l = 0
    proj2d, o_nsa, o_fox, o_mem = _attention_mixers(
        x, mem, norm_attn_g[l], w_in[l], nsa_pe_k[l], nsa_w_ck[l], nsa_pe_v[l], nsa_w_cv[l], nsa_q_g[l],
        nsa_k_g[l], rel_bias, fox_b_f[l], fox_q_g[l], fox_k_g[l], norm_mem_g[l], w_mem_kv[l], mem_q_g[l],
        mem_k_g[l])
    x1 = _merge(x.reshape(b * t, d), o_nsa, o_fox, o_mem, proj2d, w_o_nsa[l], w_o_fox[l], w_o_mem[l], w_out[l])
    out = _moe_block(x1, norm_ffn_g[l], w_router[l], router_bias[l], we_gate[l], we_up[l], we_down[l],
                     ws_gate[l], ws_up[l], ws_down[l])
    return out.reshape(b, t, d)
```
